```python
import jax, jax.numpy as jnp
from jax import lax
import numpy as np

D_MODEL = 1024
BATCH = 8
SEQ = 8192
DEPTH = 1

MEM_LEN = 256
FOX_HEADS = 16
FOX_HEAD_DIM = 64
FOX_WIDTH = FOX_HEADS * FOX_HEAD_DIM
Q_BLOCK = 128
CONV_CH = D_MODEL
CONV_WIDTH = 31
MEM_HEADS = 4
MEM_HEAD_DIM = D_MODEL // MEM_HEADS
MEM_WIDTH = MEM_HEADS * MEM_HEAD_DIM
N_BRANCHES = 3
FFN_HIDDEN = -(-8 * D_MODEL // (3 * 256)) * 256
RMS_EPS = 1e-6
LN_EPS = 1e-5
IN_COLS = (FOX_WIDTH, FOX_WIDTH, FOX_WIDTH, FOX_HEADS, 2 * CONV_CH, MEM_WIDTH, N_BRANCHES * D_MODEL)
IN_SPLITS = tuple(int(s) for s in np.cumsum(IN_COLS)[:-1])
IN_WIDTH = int(sum(IN_COLS))

kernel_name = "hybrid_fox_conformer_memxattn_swiglu"


def rmsnorm(x, g):
    xf = x.astype(jnp.float32)
    r = xf * lax.rsqrt(jnp.mean(xf * xf, axis=-1, keepdims=True) + RMS_EPS)
    return (r * g.astype(jnp.float32)).astype(x.dtype)


def layernorm(x, g, b):
    xf = x.astype(jnp.float32)
    mu = jnp.mean(xf, axis=-1, keepdims=True)
    xc = xf - mu
    r = xc * lax.rsqrt(jnp.mean(xc * xc, axis=-1, keepdims=True) + LN_EPS)
    return (r * g.astype(jnp.float32) + b.astype(jnp.float32)).astype(x.dtype)


def forgetting_attention(q, k, v, log_f):
    S = q.shape[2]
    scale = FOX_HEAD_DIM ** -0.5
    c = jnp.cumsum(log_f, axis=-1)
    outs = []
    for i in range(S // Q_BLOCK):
        start, end = i * Q_BLOCK, (i + 1) * Q_BLOCK
        qb = q[:, :, start:end]
        kb = k[:, :, :end]
        vb = v[:, :, :end]
        s = jnp.einsum('bhqd,bhkd->bhqk', qb, kb).astype(jnp.float32) * scale
        s = s + (c[:, :, start:end, None] - c[:, :, None, :end])
        q_pos = start + jnp.arange(Q_BLOCK)
        k_pos = jnp.arange(end)
        mask = k_pos[None, :] <= q_pos[:, None]
        s = jnp.where(mask, s, -jnp.inf)
        p = jax.nn.softmax(s, axis=-1)
        outs.append(jnp.einsum('bhqk,bhkd->bhqd', p.astype(vb.dtype), vb))
    return jnp.concatenate(outs, axis=2)


def conformer_conv(u, conv_w, conv_b, ln_g, ln_b):
    a, gate = jnp.split(u, 2, axis=-1)
    g = a * jax.nn.sigmoid(gate)
    w = conv_w.astype(g.dtype).reshape(CONV_WIDTH, 1, CONV_CH)
    y = lax.conv_general_dilated(
        g, w, window_strides=(1,), padding=[(CONV_WIDTH - 1, 0)],
        dimension_numbers=('NWC', 'WIO', 'NWC'), feature_group_count=CONV_CH)
    y = y + conv_b.astype(y.dtype)
    y = layernorm(y, ln_g, ln_b)
    return jax.nn.silu(y)


def memory_attention(q, mem_n, w_kv):
    B, S = q.shape[0], q.shape[1]
    kv = mem_n @ w_kv
    k, v = jnp.split(kv, 2, axis=-1)
    k = k.reshape(B, -1, MEM_HEADS, MEM_HEAD_DIM)
    v = v.reshape(B, -1, MEM_HEADS, MEM_HEAD_DIM)
    s = jnp.einsum('bshd,bmhd->bhsm', q, k).astype(jnp.float32) * (MEM_HEAD_DIM ** -0.5)
    p = jax.nn.softmax(s, axis=-1)
    o = jnp.einsum('bhsm,bmhd->bshd', p.astype(v.dtype), v)
    return o.reshape(B, S, MEM_WIDTH)


def _fwd_setup_inputs(seed: int = 0) -> dict:
    key = jax.random.key(seed)
    ks = jax.random.split(key, 24)
    f32 = jnp.float32
    L = DEPTH

    def nrm(k, shape, scale):
        return jax.random.normal(k, shape, f32) * scale

    def gain(k, shape):
        return 1.0 + 0.05 * jax.random.normal(k, shape, f32)

    return {
        "x": jax.random.normal(ks[0], (BATCH, SEQ, D_MODEL), f32),
        "mem": jax.random.normal(ks[1], (BATCH, MEM_LEN, D_MODEL), f32),
        "norm_mix_pre": gain(ks[2], (L, D_MODEL)),
        "norm_mix_post": gain(ks[3], (L, D_MODEL)),
        "norm_mem": gain(ks[4], (L, D_MODEL)),
        "w_in": nrm(ks[5], (L, D_MODEL, IN_WIDTH), D_MODEL ** -0.5),
        "b_forget": 3.0 + 0.5 * jax.random.normal(ks[6], (L, FOX_HEADS), f32),
        "conv_w": nrm(ks[7], (L, CONV_WIDTH, CONV_CH), CONV_WIDTH ** -0.5),
        "conv_b": nrm(ks[8], (L, CONV_CH), 0.02),
        "conv_ln_g": gain(ks[9], (L, CONV_CH)),
        "conv_ln_b": nrm(ks[10], (L, CONV_CH), 0.02),
        "w_kv_mem": nrm(ks[11], (L, D_MODEL, 2 * MEM_WIDTH), D_MODEL ** -0.5),
        "w_proj_attn": nrm(ks[12], (L, FOX_WIDTH, D_MODEL), FOX_WIDTH ** -0.5),
        "w_proj_conv": nrm(ks[13], (L, CONV_CH, D_MODEL), CONV_CH ** -0.5),
        "w_proj_mem": nrm(ks[14], (L, MEM_WIDTH, D_MODEL), MEM_WIDTH ** -0.5),
        "w_out": nrm(ks[15], (L, D_MODEL, D_MODEL), D_MODEL ** -0.5),
        "norm_ffn_pre": gain(ks[16], (L, D_MODEL)),
        "norm_ffn_post": gain(ks[17], (L, D_MODEL)),
        "w_gate_up": nrm(ks[18], (L, D_MODEL, 2 * FFN_HIDDEN), D_MODEL ** -0.5),
        "w_down": nrm(ks[19], (L, FFN_HIDDEN, D_MODEL), FFN_HIDDEN ** -0.5),
    }


def _fwd_reference(x, mem, norm_mix_pre, norm_mix_post, norm_mem, w_in, b_forget, conv_w, conv_b,
              conv_ln_g, conv_ln_b, w_kv_mem, w_proj_attn, w_proj_conv, w_proj_mem, w_out,
              norm_ffn_pre, norm_ffn_post, w_gate_up, w_down):
    B, S, D = x.shape
    for l in range(DEPTH):
        h = rmsnorm(x, norm_mix_pre[l])
        proj = h @ w_in[l]
        q, k, v, f_logit, glu_in, q_mem, gate_logit = jnp.split(proj, IN_SPLITS, axis=-1)

        def heads(t):
            return t.reshape(B, S, FOX_HEADS, FOX_HEAD_DIM).transpose(0, 2, 1, 3)
        log_f = jax.nn.log_sigmoid(
            f_logit.astype(jnp.float32) + b_forget[l].astype(jnp.float32)).transpose(0, 2, 1)
        o_attn = forgetting_attention(heads(q), heads(k), heads(v), log_f)
        o_attn = o_attn.transpose(0, 2, 1, 3).reshape(B, S, FOX_WIDTH)

        o_conv = conformer_conv(glu_in, conv_w[l], conv_b[l], conv_ln_g[l], conv_ln_b[l])

        mem_n = rmsnorm(mem, norm_mem[l])
        o_mem = memory_attention(q_mem.reshape(B, S, MEM_HEADS, MEM_HEAD_DIM), mem_n, w_kv_mem[l])

        gates = jax.nn.sigmoid(gate_logit).reshape(B, S, N_BRANCHES, D)
        merged = (gates[:, :, 0] * (o_attn @ w_proj_attn[l])
                  + gates[:, :, 1] * (o_conv @ w_proj_conv[l])
                  + gates[:, :, 2] * (o_mem @ w_proj_mem[l]))
        x = x + rmsnorm(merged @ w_out[l], norm_mix_post[l])

        h2 = rmsnorm(x, norm_ffn_pre[l])
        gu = h2 @ w_gate_up[l]
        g_ffn, u_ffn = jnp.split(gu, 2, axis=-1)
        ffn = (jax.nn.silu(g_ffn) * u_ffn) @ w_down[l]
        x = x + rmsnorm(ffn, norm_ffn_post[l])
    return x


import jax as _jax
import jax.numpy as _jnp

TWIN_FORMAT = 'train_step'
FWD_PARAMS = ['x', 'mem', 'norm_mix_pre', 'norm_mix_post', 'norm_mem', 'w_in', 'b_forget', 'conv_w', 'conv_b', 'conv_ln_g', 'conv_ln_b', 'w_kv_mem', 'w_proj_attn', 'w_proj_conv', 'w_proj_mem', 'w_out', 'norm_ffn_pre', 'norm_ffn_post', 'w_gate_up', 'w_down']
TWIN_WEIGHTS = ['norm_mix_pre', 'norm_mix_post', 'norm_mem', 'w_in', 'b_forget', 'conv_w', 'conv_b', 'conv_ln_g', 'conv_ln_b', 'w_kv_mem', 'w_proj_attn', 'w_proj_conv', 'w_proj_mem', 'w_out', 'norm_ffn_pre', 'norm_ffn_post', 'w_gate_up', 'w_down']
TWIN_DIFF_INPUT = 'x'
TWIN_INPUTS = ['x', 'mem', 'norm_mix_pre', 'norm_mix_post', 'norm_mem', 'w_in', 'b_forget', 'conv_w', 'conv_b', 'conv_ln_g', 'conv_ln_b', 'w_kv_mem', 'w_proj_attn', 'w_proj_conv', 'w_proj_mem', 'w_out', 'norm_ffn_pre', 'norm_ffn_post', 'w_gate_up', 'w_down', 'loss_target', 'm_norm_mix_pre', 'm_norm_mix_post', 'm_norm_mem', 'm_w_in', 'm_b_forget', 'm_conv_w', 'm_conv_b', 'm_conv_ln_g', 'm_conv_ln_b', 'm_w_kv_mem', 'm_w_proj_attn', 'm_w_proj_conv', 'm_w_proj_mem', 'm_w_out', 'm_norm_ffn_pre', 'm_norm_ffn_post', 'm_w_gate_up', 'm_w_down', 'v_norm_mix_pre', 'v_norm_mix_post', 'v_norm_mem', 'v_w_in', 'v_b_forget', 'v_conv_w', 'v_conv_b', 'v_conv_ln_g', 'v_conv_ln_b', 'v_w_kv_mem', 'v_w_proj_attn', 'v_w_proj_conv', 'v_w_proj_mem', 'v_w_out', 'v_norm_ffn_pre', 'v_norm_ffn_post', 'v_w_gate_up', 'v_w_down']
TWIN_OUTPUTS = ['loss', 'grad_x', 'grad_norm_mix_pre', 'grad_norm_mix_post', 'grad_norm_mem', 'grad_w_in', 'grad_b_forget', 'grad_conv_w', 'grad_conv_b', 'grad_conv_ln_g', 'grad_conv_ln_b', 'grad_w_kv_mem', 'grad_w_proj_attn', 'grad_w_proj_conv', 'grad_w_proj_mem', 'grad_w_out', 'grad_norm_ffn_pre', 'grad_norm_ffn_post', 'grad_w_gate_up', 'grad_w_down', 'delta_norm_mix_pre', 'delta_norm_mix_post', 'delta_norm_mem', 'delta_w_in', 'delta_b_forget', 'delta_conv_w', 'delta_conv_b', 'delta_conv_ln_g', 'delta_conv_ln_b', 'delta_w_kv_mem', 'delta_w_proj_attn', 'delta_w_proj_conv', 'delta_w_proj_mem', 'delta_w_out', 'delta_norm_ffn_pre', 'delta_norm_ffn_post', 'delta_w_gate_up', 'delta_w_down', 'new_m_norm_mix_pre', 'new_m_norm_mix_post', 'new_m_norm_mem', 'new_m_w_in', 'new_m_b_forget', 'new_m_conv_w', 'new_m_conv_b', 'new_m_conv_ln_g', 'new_m_conv_ln_b', 'new_m_w_kv_mem', 'new_m_w_proj_attn', 'new_m_w_proj_conv', 'new_m_w_proj_mem', 'new_m_w_out', 'new_m_norm_ffn_pre', 'new_m_norm_ffn_post', 'new_m_w_gate_up', 'new_m_w_down', 'new_v_norm_mix_pre', 'new_v_norm_mix_post', 'new_v_norm_mem', 'new_v_w_in', 'new_v_b_forget', 'new_v_conv_w', 'new_v_conv_b', 'new_v_conv_ln_g', 'new_v_conv_ln_b', 'new_v_w_kv_mem', 'new_v_w_proj_attn', 'new_v_w_proj_conv', 'new_v_w_proj_mem', 'new_v_w_out', 'new_v_norm_ffn_pre', 'new_v_norm_ffn_post', 'new_v_w_gate_up', 'new_v_w_down']
TWIN_LEAF_KINDS = {'loss': 'loss', 'grad_x': 'grad_x', 'grad_norm_mix_pre': 'grad_w', 'grad_norm_mix_post': 'grad_w', 'grad_norm_mem': 'grad_w', 'grad_w_in': 'grad_w', 'grad_b_forget': 'grad_w', 'grad_conv_w': 'grad_w', 'grad_conv_b': 'grad_w', 'grad_conv_ln_g': 'grad_w', 'grad_conv_ln_b': 'grad_w', 'grad_w_kv_mem': 'grad_w', 'grad_w_proj_attn': 'grad_w', 'grad_w_proj_conv': 'grad_w', 'grad_w_proj_mem': 'grad_w', 'grad_w_out': 'grad_w', 'grad_norm_ffn_pre': 'grad_w', 'grad_norm_ffn_post': 'grad_w', 'grad_w_gate_up': 'grad_w', 'grad_w_down': 'grad_w', 'delta_norm_mix_pre': 'delta_w', 'delta_norm_mix_post': 'delta_w', 'delta_norm_mem': 'delta_w', 'delta_w_in': 'delta_w', 'delta_b_forget': 'delta_w', 'delta_conv_w': 'delta_w', 'delta_conv_b': 'delta_w', 'delta_conv_ln_g': 'delta_w', 'delta_conv_ln_b': 'delta_w', 'delta_w_kv_mem': 'delta_w', 'delta_w_proj_attn': 'delta_w', 'delta_w_proj_conv': 'delta_w', 'delta_w_proj_mem': 'delta_w', 'delta_w_out': 'delta_w', 'delta_norm_ffn_pre': 'delta_w', 'delta_norm_ffn_post': 'delta_w', 'delta_w_gate_up': 'delta_w', 'delta_w_down': 'delta_w', 'new_m_norm_mix_pre': 'new_m', 'new_m_norm_mix_post': 'new_m', 'new_m_norm_mem': 'new_m', 'new_m_w_in': 'new_m', 'new_m_b_forget': 'new_m', 'new_m_conv_w': 'new_m', 'new_m_conv_b': 'new_m', 'new_m_conv_ln_g': 'new_m', 'new_m_conv_ln_b': 'new_m', 'new_m_w_kv_mem': 'new_m', 'new_m_w_proj_attn': 'new_m', 'new_m_w_proj_conv': 'new_m', 'new_m_w_proj_mem': 'new_m', 'new_m_w_out': 'new_m', 'new_m_norm_ffn_pre': 'new_m', 'new_m_norm_ffn_post': 'new_m', 'new_m_w_gate_up': 'new_m', 'new_m_w_down': 'new_m', 'new_v_norm_mix_pre': 'new_v', 'new_v_norm_mix_post': 'new_v', 'new_v_norm_mem': 'new_v', 'new_v_w_in': 'new_v', 'new_v_b_forget': 'new_v', 'new_v_conv_w': 'new_v', 'new_v_conv_b': 'new_v', 'new_v_conv_ln_g': 'new_v', 'new_v_conv_ln_b': 'new_v', 'new_v_w_kv_mem': 'new_v', 'new_v_w_proj_attn': 'new_v', 'new_v_w_proj_conv': 'new_v', 'new_v_w_proj_mem': 'new_v', 'new_v_w_out': 'new_v', 'new_v_norm_ffn_pre': 'new_v', 'new_v_norm_ffn_post': 'new_v', 'new_v_w_gate_up': 'new_v', 'new_v_w_down': 'new_v'}


def _forward(args):
    return _fwd_reference(*[args[k] for k in FWD_PARAMS])


def _output_shape():
    def fwd():
        inp = _fwd_setup_inputs(0)
        return _fwd_reference(*[inp[k] for k in FWD_PARAMS])
    out = _jax.eval_shape(fwd)
    return out.shape, out.dtype

N_MICROBATCH = 1
ADAM_LR = 0.001
ADAM_B1 = 0.9
ADAM_B2 = 0.999
ADAM_EPS = 1e-08
ADAM_WD = 0.01
ADAM_STEP = 10
PER_EXAMPLE_BATCH_AXIS = {'x': 0, 'mem': 0, 'loss_target': 0}
SHARED_INPUTS = []
_WEIGHT_DTYPES = {'norm_mix_pre': _jnp.float32, 'norm_mix_post': _jnp.float32, 'norm_mem': _jnp.float32, 'w_in': _jnp.float32, 'b_forget': _jnp.float32, 'conv_w': _jnp.float32, 'conv_b': _jnp.float32, 'conv_ln_g': _jnp.float32, 'conv_ln_b': _jnp.float32, 'w_kv_mem': _jnp.float32, 'w_proj_attn': _jnp.float32, 'w_proj_conv': _jnp.float32, 'w_proj_mem': _jnp.float32, 'w_out': _jnp.float32, 'norm_ffn_pre': _jnp.float32, 'norm_ffn_post': _jnp.float32, 'w_gate_up': _jnp.float32, 'w_down': _jnp.float32}
MOMENT_SCALE = {'norm_mix_pre': 8.421138e-01, 'norm_mix_post': 6.456616e+01, 'norm_mem': 2.755303e-01, 'w_in': 2.767122e-01, 'b_forget': 1.514930e+00, 'conv_w': 7.317966e-01, 'conv_b': 9.381064e+00, 'conv_ln_g': 3.451121e+00, 'conv_ln_b': 5.307216e+00, 'w_kv_mem': 1.703669e-01, 'w_proj_attn': 2.868324e-01, 'w_proj_conv': 2.024217e+00, 'w_proj_mem': 2.199096e-01, 'w_out': 2.189289e+00, 'norm_ffn_pre': 2.104871e+00, 'norm_ffn_post': 6.407389e+01, 'w_gate_up': 7.871072e-01, 'w_down': 1.722746e+00}


def _to_microbatches(a, axis):
    t = _jnp.moveaxis(a, axis, 0)
    t = t.reshape((N_MICROBATCH, t.shape[0] // N_MICROBATCH) + t.shape[1:])
    return _jnp.moveaxis(t, 1, axis + 1)


def setup_inputs(seed: int = 0) -> dict:
    inp = _fwd_setup_inputs(seed)
    key = _jax.random.fold_in(_jax.random.key(seed), 7919)
    shape, _ = _output_shape()
    out = dict(inp)
    out["loss_target"] = _jax.random.normal(_jax.random.fold_in(key, 0), shape, _jnp.float32)
    for i, name in enumerate(TWIN_WEIGHTS):
        w = inp[name].astype(_jnp.float32)
        if MOMENT_SCALE is None:
            s = _jnp.sqrt(_jnp.mean(_jnp.square(w)) + 1e-30)
        else:
            s = MOMENT_SCALE[name]
        km, kv = _jax.random.split(_jax.random.fold_in(key, i + 1))
        out[name] = w
        out["m_" + name] = s * _jax.random.normal(km, w.shape, _jnp.float32)
        out["v_" + name] = (s * s) * _jax.random.uniform(kv, w.shape, _jnp.float32, 0.5, 1.5)
    if N_MICROBATCH > 1:
        for name, axis in PER_EXAMPLE_BATCH_AXIS.items():
            out[name] = _to_microbatches(out[name], axis)
    return {'x': out['x'], 'mem': out['mem'], 'norm_mix_pre': out['norm_mix_pre'], 'norm_mix_post': out['norm_mix_post'], 'norm_mem': out['norm_mem'], 'w_in': out['w_in'], 'b_forget': out['b_forget'], 'conv_w': out['conv_w'], 'conv_b': out['conv_b'], 'conv_ln_g': out['conv_ln_g'], 'conv_ln_b': out['conv_ln_b'], 'w_kv_mem': out['w_kv_mem'], 'w_proj_attn': out['w_proj_attn'], 'w_proj_conv': out['w_proj_conv'], 'w_proj_mem': out['w_proj_mem'], 'w_out': out['w_out'], 'norm_ffn_pre': out['norm_ffn_pre'], 'norm_ffn_post': out['norm_ffn_post'], 'w_gate_up': out['w_gate_up'], 'w_down': out['w_down'], 'loss_target': out['loss_target'], 'm_norm_mix_pre': out['m_norm_mix_pre'], 'm_norm_mix_post': out['m_norm_mix_post'], 'm_norm_mem': out['m_norm_mem'], 'm_w_in': out['m_w_in'], 'm_b_forget': out['m_b_forget'], 'm_conv_w': out['m_conv_w'], 'm_conv_b': out['m_conv_b'], 'm_conv_ln_g': out['m_conv_ln_g'], 'm_conv_ln_b': out['m_conv_ln_b'], 'm_w_kv_mem': out['m_w_kv_mem'], 'm_w_proj_attn': out['m_w_proj_attn'], 'm_w_proj_conv': out['m_w_proj_conv'], 'm_w_proj_mem': out['m_w_proj_mem'], 'm_w_out': out['m_w_out'], 'm_norm_ffn_pre': out['m_norm_ffn_pre'], 'm_norm_ffn_post': out['m_norm_ffn_post'], 'm_w_gate_up': out['m_w_gate_up'], 'm_w_down': out['m_w_down'], 'v_norm_mix_pre': out['v_norm_mix_pre'], 'v_norm_mix_post': out['v_norm_mix_post'], 'v_norm_mem': out['v_norm_mem'], 'v_w_in': out['v_w_in'], 'v_b_forget': out['v_b_forget'], 'v_conv_w': out['v_conv_w'], 'v_conv_b': out['v_conv_b'], 'v_conv_ln_g': out['v_conv_ln_g'], 'v_conv_ln_b': out['v_conv_ln_b'], 'v_w_kv_mem': out['v_w_kv_mem'], 'v_w_proj_attn': out['v_w_proj_attn'], 'v_w_proj_conv': out['v_w_proj_conv'], 'v_w_proj_mem': out['v_w_proj_mem'], 'v_w_out': out['v_w_out'], 'v_norm_ffn_pre': out['v_norm_ffn_pre'], 'v_norm_ffn_post': out['v_norm_ffn_post'], 'v_w_gate_up': out['v_w_gate_up'], 'v_w_down': out['v_w_down']}


def _loss(weights, diff, rest, loss_target):
    with _jax.named_scope("forward"):
        args = {**rest, TWIN_DIFF_INPUT: diff, **{k: w.astype(_WEIGHT_DTYPES[k]) for k, w in weights.items()}}
        y = _forward(args)
    with _jax.named_scope("loss_head"):
        err = _jnp.square(y.astype(_jnp.float32) - loss_target)
        return 0.5 * _jnp.sum(_jnp.mean(err, axis=-1)) if err.ndim else 0.5 * err


def _adamw(w, g, m, v):
    m = ADAM_B1 * m + (1.0 - ADAM_B1) * g
    v = ADAM_B2 * v + (1.0 - ADAM_B2) * _jnp.square(g)
    m_hat = m / (1.0 - ADAM_B1 ** ADAM_STEP)
    v_hat = v / (1.0 - ADAM_B2 ** ADAM_STEP)
    delta = -ADAM_LR * (m_hat / (_jnp.sqrt(v_hat) + ADAM_EPS) + ADAM_WD * w)
    return delta, m, v


def reference(x, mem, norm_mix_pre, norm_mix_post, norm_mem, w_in, b_forget, conv_w, conv_b, conv_ln_g, conv_ln_b, w_kv_mem, w_proj_attn, w_proj_conv, w_proj_mem, w_out, norm_ffn_pre, norm_ffn_post, w_gate_up, w_down, loss_target, m_norm_mix_pre, m_norm_mix_post, m_norm_mem, m_w_in, m_b_forget, m_conv_w, m_conv_b, m_conv_ln_g, m_conv_ln_b, m_w_kv_mem, m_w_proj_attn, m_w_proj_conv, m_w_proj_mem, m_w_out, m_norm_ffn_pre, m_norm_ffn_post, m_w_gate_up, m_w_down, v_norm_mix_pre, v_norm_mix_post, v_norm_mem, v_w_in, v_b_forget, v_conv_w, v_conv_b, v_conv_ln_g, v_conv_ln_b, v_w_kv_mem, v_w_proj_attn, v_w_proj_conv, v_w_proj_mem, v_w_out, v_norm_ffn_pre, v_norm_ffn_post, v_w_gate_up, v_w_down):
    given = dict(x=x, mem=mem, norm_mix_pre=norm_mix_pre, norm_mix_post=norm_mix_post, norm_mem=norm_mem, w_in=w_in, b_forget=b_forget, conv_w=conv_w, conv_b=conv_b, conv_ln_g=conv_ln_g, conv_ln_b=conv_ln_b, w_kv_mem=w_kv_mem, w_proj_attn=w_proj_attn, w_proj_conv=w_proj_conv, w_proj_mem=w_proj_mem, w_out=w_out, norm_ffn_pre=norm_ffn_pre, norm_ffn_post=norm_ffn_post, w_gate_up=w_gate_up, w_down=w_down, loss_target=loss_target, m_norm_mix_pre=m_norm_mix_pre, m_norm_mix_post=m_norm_mix_post, m_norm_mem=m_norm_mem, m_w_in=m_w_in, m_b_forget=m_b_forget, m_conv_w=m_conv_w, m_conv_b=m_conv_b, m_conv_ln_g=m_conv_ln_g, m_conv_ln_b=m_conv_ln_b, m_w_kv_mem=m_w_kv_mem, m_w_proj_attn=m_w_proj_attn, m_w_proj_conv=m_w_proj_conv, m_w_proj_mem=m_w_proj_mem, m_w_out=m_w_out, m_norm_ffn_pre=m_norm_ffn_pre, m_norm_ffn_post=m_norm_ffn_post, m_w_gate_up=m_w_gate_up, m_w_down=m_w_down, v_norm_mix_pre=v_norm_mix_pre, v_norm_mix_post=v_norm_mix_post, v_norm_mem=v_norm_mem, v_w_in=v_w_in, v_b_forget=v_b_forget, v_conv_w=v_conv_w, v_conv_b=v_conv_b, v_conv_ln_g=v_conv_ln_g, v_conv_ln_b=v_conv_ln_b, v_w_kv_mem=v_w_kv_mem, v_w_proj_attn=v_w_proj_attn, v_w_proj_conv=v_w_proj_conv, v_w_proj_mem=v_w_proj_mem, v_w_out=v_w_out, v_norm_ffn_pre=v_norm_ffn_pre, v_norm_ffn_post=v_norm_ffn_post, v_w_gate_up=v_w_gate_up, v_w_down=v_w_down)
    weights = {n: given[n] for n in TWIN_WEIGHTS}
    shared = {n: given[n] for n in SHARED_INPUTS}
    per_example = {n: given[n] for n in ['x', 'mem']}
    grad_fn = _jax.value_and_grad(_loss, argnums=(0, 1))

    def one_microbatch(ex, loss_target):
        ex = dict(ex)
        diff = ex.pop(TWIN_DIFF_INPUT)
        return grad_fn(weights, diff, {**shared, **ex}, loss_target)

    if N_MICROBATCH == 1:
        loss, (grad_w, grad_x) = one_microbatch(per_example, given["loss_target"])
    else:
        def body(carry, xs):
            loss_sum, grad_sum = carry
            l_k, (gw_k, gx_k) = one_microbatch(xs[0], xs[1])
            with _jax.named_scope("update"):
                return (loss_sum + l_k, _jax.tree.map(_jnp.add, grad_sum, gw_k)), gx_k

        init = (_jnp.zeros((), _jnp.float32), _jax.tree.map(_jnp.zeros_like, weights))
        (loss, grad_w), grad_x = _jax.lax.scan(body, init, (per_example, given["loss_target"]))
    with _jax.named_scope("update"):
        delta_w, new_m, new_v = {}, {}, {}
        for n in TWIN_WEIGHTS:
            delta_w[n], new_m[n], new_v[n] = _adamw(weights[n], grad_w[n], given["m_" + n], given["v_" + n])
    return (loss, grad_x, *[grad_w[n] for n in TWIN_WEIGHTS], *[delta_w[n] for n in TWIN_WEIGHTS],
            *[new_m[n] for n in TWIN_WEIGHTS], *[new_v[n] for n in TWIN_WEIGHTS])
```

```python
import functools

import jax
import jax.numpy as jnp
from jax import lax
from jax.experimental import pallas as pl
from jax.experimental.pallas import tpu as pltpu

F32 = jnp.float32
MXU = jnp.bfloat16
WIRE = jnp.bfloat16

D = 1024
HF = 16
DH = 64
NP = D // 128
MEM_H = 4
MEM_DH = D // MEM_H
FFN = 2816
CW = 31
CWP = 32
HALO = 32
RMS_EPS = 1e-6
LN_EPS = 1e-5
LR, B1, B2, ADAM_EPS, WD, STEP = 0.001, 0.9, 0.999, 1e-8, 0.01, 10

N_CHIPS = 4
N_DEV = 8
LANES = 128
VMEM_LIMIT = 56 * 1024 * 1024

TM_PROJ = 512
NB_PROJ = 3
TQ = 512
TM_CONV = 256
TM_ROW = 256
TS_WG = 1024
WG_CAP = 1408
SMALL_ROWS = 16


def _pc(body, **kw):
    return pl.pallas_call(body, **kw)


def _cp(n_axes):
    return pltpu.CompilerParams(dimension_semantics=("arbitrary",) * n_axes, vmem_limit_bytes=VMEM_LIMIT)


def _sds(shape, dtype):
    return jax.ShapeDtypeStruct(shape, dtype)


def _dot(a, b):
    return jnp.dot(a, b, preferred_element_type=F32)


def _dot_nt(a, b):
    return lax.dot_general(a, b, (((1,), (1,)), ((), ())), preferred_element_type=F32)


def _dot_tn(a, b):
    return lax.dot_general(a, b, (((0,), (0,)), ((), ())), preferred_element_type=F32)


def _rms(u):
    return lax.rsqrt(jnp.mean(u * u, axis=-1, keepdims=True) + RMS_EPS)


def _rms_bwd(u, r, g, dn):
    w = dn * g
    return r * w - u * (r * r * r) * jnp.mean(u * w, axis=-1, keepdims=True)


def _sigmoid(z):
    return 1.0 / (1.0 + jnp.exp(-z))


def _tile(n, cap):
    if n <= cap:
        return n
    best = None
    for t in range(LANES, cap + 1, LANES):
        if n % t == 0:
            best = t
    assert best is not None, (n, cap)
    return best


def _rowtile(rows, cols, cap_bytes=1 << 20):
    best = None
    for t in range(8, rows + 1, 8):
        if rows % t == 0 and t * cols * 4 <= cap_bytes:
            best = t
    return best if best is not None else rows


def _split3(v):
    hi = v.astype(jnp.bfloat16)
    r1 = v - hi.astype(F32)
    mid = r1.astype(jnp.bfloat16)
    lo = (r1 - mid.astype(F32)).astype(jnp.bfloat16)
    return hi, mid, lo


def _dot_exact_rhs(a01, v):
    hi, mid, lo = _split3(v)
    return _dot(a01, hi) + _dot(a01, mid) + _dot(a01, lo)


def _in_proj(x, g_pre, w_main, w_f):
    s, d = x.shape
    n = w_main.shape[1]
    tm = min(TM_PROJ, s)
    tn = n // NB_PROJ

    def body(x_ref, g_ref, w_ref, wf_ref, proj_ref, h_ref, flog_ref, hs):
        @pl.when(pl.program_id(1) == 0)
        def _():
            xv = x_ref[...]
            h = (xv * _rms(xv) * g_ref[...]).astype(MXU)
            hs[...] = h
            h_ref[...] = h
            flog_ref[...] = _dot(h, wf_ref[...])

        proj_ref[...] = _dot(hs[...], w_ref[...]).astype(MXU)

    return _pc(
        body, name="in_proj", grid=(s // tm, NB_PROJ),
        in_specs=[pl.BlockSpec((tm, d), lambda i, j: (i, 0)), pl.BlockSpec((1, d), lambda i, j: (0, 0)),
                  pl.BlockSpec((d, tn), lambda i, j: (0, j)), pl.BlockSpec((d, LANES), lambda i, j: (0, 0))],
        out_specs=[pl.BlockSpec((tm, tn), lambda i, j: (i, j)), pl.BlockSpec((tm, d), lambda i, j: (i, 0)),
                   pl.BlockSpec((tm, LANES), lambda i, j: (i, 0))],
        out_shape=[_sds((s, n), MXU), _sds((s, d), MXU), _sds((s, LANES), F32)],
        scratch_shapes=[pltpu.VMEM((tm, d), MXU)], compiler_params=_cp(2),
    )(x, g_pre, w_main, w_f)


def _log_sigmoid(z):
    e = jnp.exp(-jnp.abs(z))
    log1p_e = jnp.where(e < 1e-3, e * (1.0 - 0.5 * e), jnp.log(1.0 + e))
    return jnp.minimum(z, 0.0) - log1p_e


def _logf_cumsum(flog, b_f):
    s = flog.shape[0]
    ch = LANES

    def body(f_ref, b_ref, c_ref):
        r = lax.broadcasted_iota(jnp.int32, (ch, ch), 0)
        q = lax.broadcasted_iota(jnp.int32, (ch, ch), 1)
        tri = jnp.where(r >= q, 1.0, 0.0).astype(jnp.bfloat16)

        def step(i, carry):
            rows = pl.ds(pl.multiple_of(i * ch, ch), ch)
            lf = _log_sigmoid(f_ref[rows, :] + b_ref[...])
            c_ref[rows, :] = _dot_exact_rhs(tri, lf) + carry
            return carry + jnp.sum(lf, axis=0, keepdims=True)

        lax.fori_loop(0, s // ch, step, jnp.zeros((1, LANES), F32))

    return _pc(body, name="logf_cumsum", out_shape=_sds((s, LANES), F32),
               compiler_params=pltpu.CompilerParams(vmem_limit_bytes=VMEM_LIMIT))(flog, b_f)


def _logf_cumsum_bwd(flog, b_f, dc):
    s = flog.shape[0]
    ch = LANES

    def body(f_ref, b_ref, dc_ref, df_ref, db_ref):
        r = lax.broadcasted_iota(jnp.int32, (ch, ch), 0)
        q = lax.broadcasted_iota(jnp.int32, (ch, ch), 1)
        tri = jnp.where(r <= q, 1.0, 0.0).astype(jnp.bfloat16)
        nch = s // ch

        def step(t, carry):
            tail, dbsum = carry
            i = nch - 1 - t
            rows = pl.ds(pl.multiple_of(i * ch, ch), ch)
            dcv = dc_ref[rows, :]
            dlf = _dot_exact_rhs(tri, dcv) + tail
            z = f_ref[rows, :] + b_ref[...]
            df = dlf * _sigmoid(-z)
            df_ref[rows, :] = df.astype(MXU)
            return tail + jnp.sum(dcv, axis=0, keepdims=True), dbsum + jnp.sum(df, axis=0, keepdims=True)

        zero = jnp.zeros((1, LANES), F32)
        _, dbsum = lax.fori_loop(0, nch, step, (zero, zero))
        db_ref[...] = jnp.broadcast_to(dbsum, db_ref.shape)

    return _pc(body, name="logf_cumsum_bwd", out_shape=[_sds((s, LANES), MXU), _sds((8, LANES), F32)],
               compiler_params=pltpu.CompilerParams(vmem_limit_bytes=VMEM_LIMIT))(flog, b_f, dc)


def _head_masks(rows):
    lane = lax.broadcasted_iota(jnp.int32, (rows, LANES), 1)
    return lane < DH, lane >= DH


def _fox_fwd(proj, crow):
    s = proj.shape[0]
    tq = min(TQ, s)
    nq = s // tq

    def body(q_ref, k_ref, v_ref, crow_ref, o_ref, lse_ref):
        i = pl.program_id(1)
        q2 = q_ref[...] * 0.125
        masks = _head_masks(tq)
        row = lax.broadcasted_iota(jnp.int32, (tq, tq), 0)
        col = lax.broadcasted_iota(jnp.int32, (tq, tq), 1)
        outs, lses = [], []
        for a in range(2):
            qa = jnp.where(masks[a], q2, jnp.zeros_like(q2))

            def blk(j, carry, diag, a=a, qa=qa):
                m, l, acc = carry
                rows = pl.ds(pl.multiple_of(j * tq, tq), tq)
                sc = _dot_nt(qa, k_ref[rows, :]) - crow_ref[0, j, a:a + 1, :]
                if diag:
                    sc = jnp.where(row >= col, sc, -jnp.inf)
                m_new = jnp.maximum(m, jnp.max(sc, axis=-1, keepdims=True))
                alpha = jnp.exp(m - m_new)
                p = jnp.exp(sc - m_new)
                l = alpha * l + jnp.sum(p, axis=-1, keepdims=True)
                acc = alpha * acc + _dot(p.astype(MXU), v_ref[rows, :])
                return m_new, l, acc

            init = (jnp.full((tq, 1), -jnp.inf, F32), jnp.zeros((tq, 1), F32), jnp.zeros((tq, LANES), F32))
            carry = lax.fori_loop(0, i, functools.partial(blk, diag=False), init)
            m, l, acc = blk(i, carry, True)
            outs.append(acc / l)
            lses.append(m + jnp.log(l))
        o_ref[...] = jnp.where(masks[0], outs[0], outs[1]).astype(MXU)
        lane2 = lax.broadcasted_iota(jnp.int32, (tq, 2), 1)
        lse_ref[0] = jnp.where(lane2 == 0, lses[0], lses[1])

    return _pc(
        body, name="fox_fwd", grid=(NP, nq),
        in_specs=[pl.BlockSpec((tq, LANES), lambda p, i: (i, p)),
                  pl.BlockSpec((s, LANES), lambda p, i: (0, NP + p)),
                  pl.BlockSpec((s, LANES), lambda p, i: (0, 2 * NP + p)),
                  pl.BlockSpec((1, nq, 2, tq), lambda p, i: (p, 0, 0, 0))],
        out_specs=[pl.BlockSpec((tq, LANES), lambda p, i: (i, p)), pl.BlockSpec((1, tq, 2), lambda p, i: (p, i, 0))],
        out_shape=[_sds((s, D), MXU), _sds((NP, s, 2), F32)],
        compiler_params=_cp(2),
    )(proj, proj, proj, crow)


def _fox_bwd(proj, do, lse_row, delta_row, ccol):
    s = proj.shape[0]
    tq = min(TQ, s)
    nq = s // tq

    def body(k_ref, v_ref, q_ref, do_ref, lse_ref, dl_ref, cc_ref, dq_ref, dk_ref, dv_ref, dc_ref, dr_ref, dq_acc):
        j = pl.program_id(1)

        @pl.when(j == 0)
        def _():
            dq_acc[...] = jnp.zeros_like(dq_acc)
            dr_ref[...] = jnp.zeros_like(dr_ref)

        k2 = k_ref[...]
        v2 = v_ref[...]
        ks = k2 * 0.125
        masks = _head_masks(tq)
        row = lax.broadcasted_iota(jnp.int32, (tq, tq), 0)
        col = lax.broadcasted_iota(jnp.int32, (tq, tq), 1)
        dk_acc = jnp.zeros((tq, LANES), F32)
        dv_acc = jnp.zeros((tq, LANES), F32)
        dcs = []
        for a in range(2):
            ksa = jnp.where(masks[a], ks, jnp.zeros_like(ks))
            va = jnp.where(masks[a], v2, jnp.zeros_like(v2))
            cc = cc_ref[0, :, a:a + 1]

            def blk(i, carry, diag, a=a, ksa=ksa, va=va, cc=cc):
                dk_a, dv_a, dc_a = carry
                rows = pl.ds(pl.multiple_of(i * tq, tq), tq)
                qi = q_ref[rows, :]
                doi = do_ref[rows, :]
                qa = jnp.where(masks[a], qi, jnp.zeros_like(qi))
                doa = jnp.where(masks[a], doi, jnp.zeros_like(doi))
                st = _dot_nt(ksa, qi) - cc
                if diag:
                    st = jnp.where(col >= row, st, -jnp.inf)
                pt = jnp.exp(st - lse_ref[0, i, a:a + 1, :])
                dv_a = dv_a + _dot(pt.astype(MXU), doa)
                dpt = _dot_nt(va, doi)
                dst = pt * (dpt - dl_ref[0, i, a:a + 1, :])
                dc_a = dc_a - jnp.sum(dst, axis=-1, keepdims=True)
                dr_ref[0, i, a:a + 1, :] += jnp.sum(dst, axis=0, keepdims=True)
                dsb = dst.astype(MXU)
                dk_a = dk_a + _dot(dsb, qa)
                dq_acc[rows, :] += _dot_tn(dsb, ksa)
                return dk_a, dv_a, dc_a

            carry = blk(j, (dk_acc, dv_acc, jnp.zeros((tq, 1), F32)), True)
            dk_acc, dv_acc, dc_a = lax.fori_loop(j + 1, nq, functools.partial(blk, diag=False), carry)
            dcs.append(dc_a)
        dk_ref[...] = (dk_acc * 0.125).astype(MXU)
        dv_ref[...] = dv_acc.astype(MXU)
        lane2 = lax.broadcasted_iota(jnp.int32, (tq, 2), 1)
        dc_ref[0] = jnp.where(lane2 == 0, dcs[0], dcs[1])

        @pl.when(j == nq - 1)
        def _():
            dq_ref[...] = dq_acc[...].astype(MXU)

    stat = pl.BlockSpec((1, nq, 2, tq), lambda p, j: (p, 0, 0, 0))
    return _pc(
        body, name="fox_bwd", grid=(NP, nq),
        in_specs=[pl.BlockSpec((tq, LANES), lambda p, j: (j, NP + p)),
                  pl.BlockSpec((tq, LANES), lambda p, j: (j, 2 * NP + p)),
                  pl.BlockSpec((s, LANES), lambda p, j: (0, p)),
                  pl.BlockSpec((s, LANES), lambda p, j: (0, p)),
                  stat, stat, pl.BlockSpec((1, tq, 2), lambda p, j: (p, j, 0))],
        out_specs=[pl.BlockSpec((s, LANES), lambda p, j: (0, p)),
                   pl.BlockSpec((tq, LANES), lambda p, j: (j, p)),
                   pl.BlockSpec((tq, LANES), lambda p, j: (j, p)),
                   pl.BlockSpec((1, tq, 2), lambda p, j: (p, j, 0)), stat],
        out_shape=[_sds((s, D), MXU), _sds((s, D), MXU), _sds((s, D), MXU), _sds((NP, s, 2), F32),
                   _sds((NP, nq, 2, tq), F32)],
        scratch_shapes=[pltpu.VMEM((s, LANES), F32)], compiler_params=_cp(2),
    )(proj, proj, proj, do, lse_row, delta_row, ccol)


def _glu(a, gate):
    return a.astype(F32) * _sigmoid(gate.astype(F32))


def _conv_fwd(proj, cw, cb, lg, lb):
    s = proj.shape[0]
    tm = min(TM_CONV, s)
    hb = tm // HALO

    def body(a_ref, g_ref, ah_ref, gh_ref, w_ref, cb_ref, lg_ref, lb_ref, y_ref, o_ref, gbuf):
        i = pl.program_id(0)
        gbuf[pl.ds(0, HALO), :] = jnp.where(i > 0, _glu(ah_ref[...], gh_ref[...]), 0.0)
        gbuf[pl.ds(HALO, tm), :] = _glu(a_ref[...], g_ref[...])
        acc = jnp.broadcast_to(cb_ref[...], (tm, D))
        for t in range(CW):
            acc = acc + w_ref[t:t + 1, :] * gbuf[pl.ds(HALO - (CW - 1) + t, tm), :]
        y_ref[...] = acc
        mu = jnp.mean(acc, axis=-1, keepdims=True)
        xc = acc - mu
        r = lax.rsqrt(jnp.mean(xc * xc, axis=-1, keepdims=True) + LN_EPS)
        nrm = xc * r * lg_ref[...] + lb_ref[...]
        o_ref[...] = (nrm * _sigmoid(nrm)).astype(MXU)

    vec = pl.BlockSpec((1, D), lambda i: (0, 0))
    return _pc(
        body, name="conv_fwd", grid=(s // tm,),
        in_specs=[pl.BlockSpec((tm, D), lambda i: (i, 3)), pl.BlockSpec((tm, D), lambda i: (i, 4)),
                  pl.BlockSpec((HALO, D), lambda i: (jnp.maximum(i * hb - 1, 0), 3)),
                  pl.BlockSpec((HALO, D), lambda i: (jnp.maximum(i * hb - 1, 0), 4)),
                  pl.BlockSpec((CWP, D), lambda i: (0, 0)), vec, vec, vec],
        out_specs=[pl.BlockSpec((tm, D), lambda i: (i, 0)), pl.BlockSpec((tm, D), lambda i: (i, 0))],
        out_shape=[_sds((s, D), F32), _sds((s, D), MXU)],
        scratch_shapes=[pltpu.VMEM((tm + HALO, D), F32)], compiler_params=_cp(1),
    )(proj, proj, proj, proj, cw, cb, lg, lb)


def _conv_bwd(proj, y, do, cw, lg, lb):
    s = proj.shape[0]
    tm = min(TM_CONV, s)
    hb = tm // HALO
    nt = s // tm
    last_hblk = s // HALO - 1

    def ln_bwd(yv, dov, lgv, lbv):
        mu = jnp.mean(yv, axis=-1, keepdims=True)
        xc = yv - mu
        r = lax.rsqrt(jnp.mean(xc * xc, axis=-1, keepdims=True) + LN_EPS)
        xh = xc * r
        nrm = xh * lgv + lbv
        sg = _sigmoid(nrm)
        dn = dov.astype(F32) * (sg * (1.0 + nrm * (1.0 - sg)))
        wv = dn * lgv
        dy = r * (wv - jnp.mean(wv, axis=-1, keepdims=True) - xh * jnp.mean(wv * xh, axis=-1, keepdims=True))
        return dy, dn, xh

    def body(a_ref, g_ref, ah_ref, gh_ref, y_ref, yn_ref, do_ref, don_ref, w_ref, lg_ref, lb_ref,
             da_ref, dg_ref, dw_ref, vec_ref, gbuf, dybuf):
        i = pl.program_id(0)

        @pl.when(i == 0)
        def _():
            dw_ref[...] = jnp.zeros_like(dw_ref)
            vec_ref[...] = jnp.zeros_like(vec_ref)

        lgv, lbv = lg_ref[...], lb_ref[...]
        av = a_ref[...].astype(F32)
        sgate = _sigmoid(g_ref[...].astype(F32))
        gbuf[pl.ds(0, HALO), :] = jnp.where(i > 0, _glu(ah_ref[...], gh_ref[...]), 0.0)
        gbuf[pl.ds(HALO, tm), :] = av * sgate
        dy, dn, xh = ln_bwd(y_ref[...], do_ref[...], lgv, lbv)
        dyn, _, _ = ln_bwd(yn_ref[...], don_ref[...], lgv, lbv)
        dybuf[pl.ds(0, tm), :] = dy
        dybuf[pl.ds(tm, HALO), :] = jnp.where(i < nt - 1, dyn, 0.0)
        vec_ref[0:1, :] += jnp.sum(dn * xh, axis=0, keepdims=True)
        vec_ref[1:2, :] += jnp.sum(dn, axis=0, keepdims=True)
        vec_ref[2:3, :] += jnp.sum(dy, axis=0, keepdims=True)
        dgl = jnp.zeros((tm, D), F32)
        for t in range(CW):
            dgl = dgl + w_ref[t:t + 1, :] * dybuf[pl.ds(CW - 1 - t, tm), :]
            dw_ref[t:t + 1, :] += jnp.sum(dy * gbuf[pl.ds(HALO - (CW - 1) + t, tm), :], axis=0, keepdims=True)
        da_ref[...] = (dgl * sgate).astype(MXU)
        dg_ref[...] = (dgl * av * sgate * (1.0 - sgate)).astype(MXU)

    vec = pl.BlockSpec((1, D), lambda i: (0, 0))
    cur = lambda c: pl.BlockSpec((tm, D), lambda i: (i, c))
    prv = lambda c: pl.BlockSpec((HALO, D), lambda i: (jnp.maximum(i * hb - 1, 0), c))
    nxt = pl.BlockSpec((HALO, D), lambda i: (jnp.minimum((i + 1) * hb, last_hblk), 0))
    return _pc(
        body, name="conv_bwd", grid=(nt,),
        in_specs=[cur(3), cur(4), prv(3), prv(4), cur(0), nxt, cur(0), nxt,
                  pl.BlockSpec((CWP, D), lambda i: (0, 0)), vec, vec],
        out_specs=[cur(0), cur(0), pl.BlockSpec((CWP, D), lambda i: (0, 0)), pl.BlockSpec((8, D), lambda i: (0, 0))],
        out_shape=[_sds((s, D), MXU), _sds((s, D), MXU), _sds((CWP, D), F32), _sds((8, D), F32)],
        scratch_shapes=[pltpu.VMEM((tm + HALO, D), F32), pltpu.VMEM((tm + HALO, D), F32)], compiler_params=_cp(1),
    )(proj, proj, proj, proj, y, y, do, do, cw, lg, lb)


def _mem_kv(mem, g_mem, w_kv):
    mm = mem.shape[0]

    def body(m_ref, g_ref, w_ref, mn_ref, kv_ref):
        mv = m_ref[...]
        mn = (mv * _rms(mv) * g_ref[...]).astype(MXU)
        mn_ref[...] = mn
        kv_ref[...] = _dot(mn, w_ref[...]).astype(MXU)

    return _pc(body, name="mem_kv", out_shape=[_sds((mm, D), MXU), _sds((mm, 2 * D), MXU)],
               compiler_params=pltpu.CompilerParams(vmem_limit_bytes=VMEM_LIMIT))(mem, g_mem, w_kv)


def _mem_kv_bwd(mem, g_mem, w_kv, dkv):
    mm = mem.shape[0]

    def body(m_ref, w_ref, dkv_ref, o_ref):
        mv = m_ref[...]
        dmn = _dot_nt(dkv_ref[...], w_ref[...])
        o_ref[...] = jnp.broadcast_to(jnp.sum(dmn * mv * _rms(mv), axis=0, keepdims=True), o_ref.shape)

    return _pc(body, name="mem_kv_bwd", out_shape=_sds((8, D), F32),
               compiler_params=pltpu.CompilerParams(vmem_limit_bytes=VMEM_LIMIT))(mem, w_kv, dkv)


def _mem_attn_fwd(proj, kv):
    s = proj.shape[0]
    mm = kv.shape[0]
    tm = min(TM_ROW, s)
    scale = MEM_DH ** -0.5

    def body(q_ref, kv_ref, o_ref):
        for h in range(MEM_H):
            cols = pl.ds(h * MEM_DH, MEM_DH)
            qh = q_ref[:, cols] * scale
            sc = _dot_nt(qh, kv_ref[:, cols])
            m = jnp.max(sc, axis=-1, keepdims=True)
            e = jnp.exp(sc - m)
            p = e / jnp.sum(e, axis=-1, keepdims=True)
            o_ref[:, cols] = _dot(p.astype(MXU), kv_ref[:, pl.ds(D + h * MEM_DH, MEM_DH)]).astype(MXU)

    return _pc(
        body, name="mem_attn_fwd", grid=(s // tm,),
        in_specs=[pl.BlockSpec((tm, D), lambda i: (i, 5)), pl.BlockSpec((mm, 2 * D), lambda i: (0, 0))],
        out_specs=pl.BlockSpec((tm, D), lambda i: (i, 0)), out_shape=_sds((s, D), MXU), compiler_params=_cp(1),
    )(proj, kv)


def _mem_attn_bwd(proj, kv, do):
    s = proj.shape[0]
    mm = kv.shape[0]
    tm = min(TM_ROW, s)
    scale = MEM_DH ** -0.5

    def body(q_ref, kv_ref, do_ref, dq_ref, dkv_ref):
        @pl.when(pl.program_id(0) == 0)
        def _():
            dkv_ref[...] = jnp.zeros_like(dkv_ref)

        for h in range(MEM_H):
            cols = pl.ds(h * MEM_DH, MEM_DH)
            vcols = pl.ds(D + h * MEM_DH, MEM_DH)
            qh = q_ref[:, cols]
            kh = kv_ref[:, cols] * scale
            doh = do_ref[:, cols]
            st = _dot_nt(kh, qh)
            m = jnp.max(st, axis=0, keepdims=True)
            e = jnp.exp(st - m)
            pt = e / jnp.sum(e, axis=0, keepdims=True)
            dpt = _dot_nt(kv_ref[:, vcols], doh)
            dst = pt * (dpt - jnp.sum(pt * dpt, axis=0, keepdims=True))
            dsb = dst.astype(MXU)
            dkv_ref[:, vcols] += _dot(pt.astype(MXU), doh)
            dkv_ref[:, cols] += _dot(dsb, qh) * scale
            dq_ref[:, cols] = _dot_tn(dsb, kh).astype(MXU)

    return _pc(
        body, name="mem_attn_bwd", grid=(s // tm,),
        in_specs=[pl.BlockSpec((tm, D), lambda i: (i, 5)), pl.BlockSpec((mm, 2 * D), lambda i: (0, 0)),
                  pl.BlockSpec((tm, D), lambda i: (i, 0))],
        out_specs=[pl.BlockSpec((tm, D), lambda i: (i, 0)), pl.BlockSpec((mm, 2 * D), lambda i: (0, 0))],
        out_shape=[_sds((s, D), MXU), _sds((mm, 2 * D), F32)], compiler_params=_cp(1),
    )(proj, kv, do)


def _resident(n):
    return [pltpu.VMEM((n, D, D), MXU), pltpu.SemaphoreType.DMA((n,))]


def _load_resident(hbm_refs, wbuf, sems):
    @pl.when(pl.program_id(0) == 0)
    def _():
        cps = [pltpu.make_async_copy(r, wbuf.at[k], sems.at[k]) for k, r in enumerate(hbm_refs)]
        for cp in cps:
            cp.start()
        for cp in cps:
            cp.wait()


def _merge_out(oa, oc, om, proj, x, wpa, wpc, wpm, wout, g_post, g_fpre):
    s = x.shape[0]
    tm = min(TM_ROW, s)

    def body(oa_ref, oc_ref, om_ref, gl_ref, x_ref, gp_ref, gf_ref, wpa_h, wpc_h, wpm_h, wout_h,
             pa_ref, pc_ref, pm_ref, mg_ref, z_ref, x1_ref, h2_ref, wbuf, sems):
        _load_resident([wpa_h, wpc_h, wpm_h, wout_h], wbuf, sems)
        merged = jnp.zeros((tm, D), F32)
        for b, (o_ref, p_ref) in enumerate(((oa_ref, pa_ref), (oc_ref, pc_ref), (om_ref, pm_ref))):
            pb = _dot(o_ref[...], wbuf[b])
            p_ref[...] = pb.astype(MXU)
            merged = merged + _sigmoid(gl_ref[:, pl.ds(b * D, D)].astype(F32)) * pb
        mg = merged.astype(MXU)
        mg_ref[...] = mg
        z = _dot(mg, wbuf[3])
        z_ref[...] = z
        x1 = x_ref[...] + z * _rms(z) * gp_ref[...]
        x1_ref[...] = x1
        h2_ref[...] = (x1 * _rms(x1) * gf_ref[...]).astype(MXU)

    rows = pl.BlockSpec((tm, D), lambda i: (i, 0))
    vec = pl.BlockSpec((1, D), lambda i: (0, 0))
    anyspec = pl.BlockSpec(memory_space=pl.ANY)
    return _pc(
        body, name="merge_out", grid=(s // tm,),
        in_specs=[rows, rows, rows, pl.BlockSpec((tm, 3 * D), lambda i: (i, 2)), rows, vec, vec,
                  anyspec, anyspec, anyspec, anyspec],
        out_specs=[rows] * 7,
        out_shape=[_sds((s, D), MXU)] * 4 + [_sds((s, D), F32)] * 2 + [_sds((s, D), MXU)],
        scratch_shapes=_resident(4), compiler_params=_cp(1),
    )(oa, oc, om, proj, x, g_post, g_fpre, wpa, wpc, wpm, wout)


def _ffn_up(h2, w_gu):
    s = h2.shape[0]
    tm = min(TM_PROJ, s)
    nb = 2
    bw = FFN // nb

    def body(h_ref, wg_ref, wu_ref, gf_ref, uf_ref, act_ref):
        hv = h_ref[...]
        gf = _dot(hv, wg_ref[...])
        uf = _dot(hv, wu_ref[...])
        gf_ref[...] = gf.astype(MXU)
        uf_ref[...] = uf.astype(MXU)
        act_ref[...] = (gf * _sigmoid(gf) * uf).astype(MXU)

    out = pl.BlockSpec((tm, bw), lambda i, j: (i, j))
    return _pc(
        body, name="ffn_up", grid=(s // tm, nb),
        in_specs=[pl.BlockSpec((tm, D), lambda i, j: (i, 0)), pl.BlockSpec((D, bw), lambda i, j: (0, j)),
                  pl.BlockSpec((D, bw), lambda i, j: (0, nb + j))],
        out_specs=[out, out, out], out_shape=[_sds((s, FFN), MXU)] * 3, compiler_params=_cp(2),
    )(h2, w_gu, w_gu)


def _ffn_down_loss(act, w_d, x1, tgt, g_fpost):
    s = act.shape[0]
    tm = min(TM_ROW, s)

    def body(a_ref, w_ref, x1_ref, t_ref, g_ref, dffn_ref, dy_ref, vec_ref, loss_ref):
        @pl.when(pl.program_id(0) == 0)
        def _():
            vec_ref[...] = jnp.zeros_like(vec_ref)
            loss_ref[...] = jnp.zeros_like(loss_ref)

        ffn = _dot(a_ref[...], w_ref[...])
        r = _rms(ffn)
        gv = g_ref[...]
        e = x1_ref[...] + ffn * r * gv - t_ref[...]
        loss_ref[...] += jnp.sum(e * e) * (0.5 / D)
        dy = e * (1.0 / D)
        dy_ref[...] = dy
        vec_ref[0:1, :] += jnp.sum(dy * ffn * r, axis=0, keepdims=True)
        dffn_ref[...] = _rms_bwd(ffn, r, gv, dy).astype(MXU)

    rows = pl.BlockSpec((tm, D), lambda i: (i, 0))
    return _pc(
        body, name="ffn_down_loss", grid=(s // tm,),
        in_specs=[pl.BlockSpec((tm, FFN), lambda i: (i, 0)), pl.BlockSpec((FFN, D), lambda i: (0, 0)), rows, rows,
                  pl.BlockSpec((1, D), lambda i: (0, 0))],
        out_specs=[rows, rows, pl.BlockSpec((8, D), lambda i: (0, 0)), pl.BlockSpec((8, LANES), lambda i: (0, 0))],
        out_shape=[_sds((s, D), MXU), _sds((s, D), F32), _sds((8, D), F32), _sds((8, LANES), F32)],
        compiler_params=_cp(1),
    )(act, w_d, x1, tgt, g_fpost)


def _ffn_down_bwd(dffn, w_d, gf, uf):
    s = dffn.shape[0]
    tm = min(TM_ROW, s)

    def body(d_ref, w_ref, gf_ref, uf_ref, dgf_ref, duf_ref):
        da = _dot_nt(d_ref[...], w_ref[...])
        gf = gf_ref[...].astype(F32)
        sg = _sigmoid(gf)
        duf_ref[...] = (da * gf * sg).astype(MXU)
        dgf_ref[...] = (da * uf_ref[...].astype(F32) * (sg * (1.0 + gf * (1.0 - sg)))).astype(MXU)

    wide = pl.BlockSpec((tm, FFN), lambda i: (i, 0))
    return _pc(
        body, name="ffn_down_bwd", grid=(s // tm,),
        in_specs=[pl.BlockSpec((tm, D), lambda i: (i, 0)), pl.BlockSpec((FFN, D), lambda i: (0, 0)), wide, wide],
        out_specs=[wide, wide], out_shape=[_sds((s, FFN), MXU)] * 2, compiler_params=_cp(1),
    )(dffn, w_d, gf, uf)


def _ffn_up_bwd(dgf, duf, w_gu, x1, dy, z, g_fpre, g_post):
    s = x1.shape[0]
    tm = min(TM_ROW, s)

    def body(dgf_ref, duf_ref, w_ref, x1_ref, dy_ref, z_ref, gf_ref, gp_ref, dx1_ref, dz_ref, vec_ref):
        @pl.when(pl.program_id(0) == 0)
        def _():
            vec_ref[...] = jnp.zeros_like(vec_ref)

        dh2 = _dot_nt(dgf_ref[...], w_ref[:, pl.ds(0, FFN)]) + _dot_nt(duf_ref[...], w_ref[:, pl.ds(FFN, FFN)])
        x1 = x1_ref[...]
        r2 = _rms(x1)
        vec_ref[0:1, :] += jnp.sum(dh2 * x1 * r2, axis=0, keepdims=True)
        dx1 = dy_ref[...] + _rms_bwd(x1, r2, gf_ref[...], dh2)
        dx1_ref[...] = dx1
        z = z_ref[...]
        rz = _rms(z)
        vec_ref[1:2, :] += jnp.sum(dx1 * z * rz, axis=0, keepdims=True)
        dz_ref[...] = _rms_bwd(z, rz, gp_ref[...], dx1).astype(MXU)

    rows = pl.BlockSpec((tm, D), lambda i: (i, 0))
    wide = pl.BlockSpec((tm, FFN), lambda i: (i, 0))
    vec = pl.BlockSpec((1, D), lambda i: (0, 0))
    return _pc(
        body, name="ffn_up_bwd", grid=(s // tm,),
        in_specs=[wide, wide, pl.BlockSpec((D, 2 * FFN), lambda i: (0, 0)), rows, rows, rows, vec, vec],
        out_specs=[rows, rows, pl.BlockSpec((8, D), lambda i: (0, 0))],
        out_shape=[_sds((s, D), F32), _sds((s, D), MXU), _sds((8, D), F32)], compiler_params=_cp(1),
    )(dgf, duf, w_gu, x1, dy, z, g_fpre, g_post)


def _merge_bwd(dz, proj, pa, pc, pm, oa, wpa, wpc, wpm, wout):
    s = dz.shape[0]
    tm = min(TM_ROW, s)

    def body(dz_ref, gl_ref, pa_ref, pc_ref, pm_ref, oa_ref, wpa_h, wpc_h, wpm_h, wout_h,
             dpa_ref, dpc_ref, dpm_ref, dgl_ref, doa_ref, doc_ref, dom_ref, dl_ref, wbuf, sems):
        _load_resident([wpa_h, wpc_h, wpm_h, wout_h], wbuf, sems)
        dm = _dot_nt(dz_ref[...], wbuf[3])
        quads = ((pa_ref, dpa_ref, doa_ref), (pc_ref, dpc_ref, doc_ref), (pm_ref, dpm_ref, dom_ref))
        for b, (p_ref, dp_ref, do_ref) in enumerate(quads):
            cols = pl.ds(b * D, D)
            gt = _sigmoid(gl_ref[:, cols].astype(F32))
            dp = (dm * gt).astype(MXU)
            dp_ref[...] = dp
            dgl_ref[:, cols] = (dm * p_ref[...].astype(F32) * gt * (1.0 - gt)).astype(MXU)
            dob = _dot_nt(dp, wbuf[b]).astype(MXU)
            do_ref[...] = dob
            if b == 0:
                prod = dob.astype(F32) * oa_ref[...].astype(F32)
                d_i = lax.broadcasted_iota(jnp.int32, (D, LANES), 0)
                h_i = lax.broadcasted_iota(jnp.int32, (D, LANES), 1)
                sel = jnp.where(lax.shift_right_logical(d_i, DH.bit_length() - 1) == h_i, 1.0, 0.0).astype(jnp.bfloat16)
                dl_ref[...] = _dot_exact_rhs_t(prod, sel)

    rows = pl.BlockSpec((tm, D), lambda i: (i, 0))
    anyspec = pl.BlockSpec(memory_space=pl.ANY)
    wide = pl.BlockSpec((tm, 3 * D), lambda i: (i, 2))
    return _pc(
        body, name="merge_bwd", grid=(s // tm,),
        in_specs=[rows, wide, rows, rows, rows, rows, anyspec, anyspec, anyspec, anyspec],
        out_specs=[rows, rows, rows, pl.BlockSpec((tm, 3 * D), lambda i: (i, 0)), rows, rows, rows,
                   pl.BlockSpec((tm, LANES), lambda i: (i, 0))],
        out_shape=[_sds((s, D), MXU)] * 3 + [_sds((s, 3 * D), MXU)] + [_sds((s, D), MXU)] * 3 + [_sds((s, LANES), F32)],
        scratch_shapes=_resident(4), compiler_params=_cp(1),
    )(dz, proj, pa, pc, pm, oa, wpa, wpc, wpm, wout)


def _dot_exact_rhs_t(v, b01):
    hi, mid, lo = _split3(v)
    return _dot(hi, b01) + _dot(mid, b01) + _dot(lo, b01)


def _in_proj_bwd(pieces, df, w_main, w_f, x, dx1, g_pre):
    s = x.shape[0]
    tm = min(TM_ROW, s)
    n_main = w_main.shape[1]
    np_ = len(pieces)

    def body(*refs):
        p_refs = refs[:np_]
        df_ref, x_ref, dx1_ref, g_ref, w_h, wf_ref, dx_ref, vec_ref, wbuf, sem = refs[np_:]

        @pl.when(pl.program_id(0) == 0)
        def _():
            vec_ref[...] = jnp.zeros_like(vec_ref)
            cp = pltpu.make_async_copy(w_h, wbuf, sem)
            cp.start()
            cp.wait()

        dh = _dot_nt(df_ref[...], wf_ref[...])
        for p_ref, (_, c0, nc) in zip(p_refs, pieces):
            dh = dh + _dot_nt(p_ref[...], wbuf[:, pl.ds(c0 * D, nc * D)])
        xv = x_ref[...]
        r = _rms(xv)
        vec_ref[0:1, :] += jnp.sum(dh * xv * r, axis=0, keepdims=True)
        dx_ref[...] = dx1_ref[...] + _rms_bwd(xv, r, g_ref[...], dh)

    rows = pl.BlockSpec((tm, D), lambda i: (i, 0))
    p_specs = [pl.BlockSpec((tm, nc * D), lambda i: (i, 0)) for _, _, nc in pieces]
    return _pc(
        body, name="in_proj_bwd", grid=(s // tm,),
        in_specs=p_specs + [pl.BlockSpec((tm, LANES), lambda i: (i, 0)), rows, rows, pl.BlockSpec((1, D), lambda i: (0, 0)),
                            pl.BlockSpec(memory_space=pl.ANY), pl.BlockSpec((D, LANES), lambda i: (0, 0))],
        out_specs=[rows, pl.BlockSpec((8, D), lambda i: (0, 0))],
        out_shape=[_sds((s, D), F32), _sds((8, D), F32)],
        scratch_shapes=[pltpu.VMEM((D, n_main), MXU), pltpu.SemaphoreType.DMA], compiler_params=_cp(1),
    )(*[p for p, _, _ in pieces], df, x, dx1, g_pre, w_main, w_f)


def _wgrad(xa, dy, name):
    s, k = xa.shape
    n = dy.shape[1]
    ts = min(TS_WG, s)
    tk = _tile(k, WG_CAP)
    tn = _tile(n, WG_CAP)

    def body(x_ref, dy_ref, o_ref):
        @pl.when(pl.program_id(2) == 0)
        def _():
            o_ref[...] = jnp.zeros_like(o_ref)

        o_ref[...] += _dot_tn(x_ref[...], dy_ref[...])

    return _pc(
        body, name=name, grid=(k // tk, n // tn, s // ts),
        in_specs=[pl.BlockSpec((ts, tk), lambda a, b, c: (c, a)), pl.BlockSpec((ts, tn), lambda a, b, c: (c, b))],
        out_specs=pl.BlockSpec((tk, tn), lambda a, b, c: (a, b)), out_shape=_sds((k, n), F32), compiler_params=_cp(3),
    )(xa, dy)


def _pair_sum(g, r1, c_idx):
    _, _, hr, cols = g.shape
    tr = _rowtile(hr, cols)

    def body(c_ref, g_ref, r_ref, o_ref):
        o_ref[0] = (g_ref[0, 0].astype(F32) + r_ref[0].astype(F32)).astype(WIRE)

    return _pc(
        body, name="pair_sum_%dx%d" % (hr, cols), out_shape=_sds((N_CHIPS, hr, cols), WIRE),
        grid_spec=pltpu.PrefetchScalarGridSpec(
            num_scalar_prefetch=1, grid=(N_CHIPS, hr // tr),
            in_specs=[pl.BlockSpec((1, 1, tr, cols), lambda d, i, c: (d, c[0], i, 0)),
                      pl.BlockSpec((1, tr, cols), lambda d, i, c: (d, i, 0))],
            out_specs=pl.BlockSpec((1, tr, cols), lambda d, i, c: (d, i, 0))),
        compiler_params=_cp(2),
    )(c_idx, g, r1)


def _chip_sum(r2):
    _, hr, cols = r2.shape
    tr = _rowtile(hr, cols)

    def body(r_ref, o_ref):
        acc = r_ref[0].astype(F32)
        for d in range(1, N_CHIPS):
            acc = acc + r_ref[d].astype(F32)
        o_ref[...] = acc

    return _pc(
        body, name="chip_sum_%dx%d" % (hr, cols), grid=(hr // tr,),
        in_specs=[pl.BlockSpec((N_CHIPS, tr, cols), lambda i: (0, i, 0))],
        out_specs=pl.BlockSpec((tr, cols), lambda i: (i, 0)), out_shape=_sds((hr, cols), F32), compiler_params=_cp(1),
    )(r2)


def _adamw(w, g, m, v):
    rows, cols = w.shape
    tr = _rowtile(rows, cols, 1 << 19)
    c1 = 1.0 / (1.0 - B1 ** STEP)
    c2 = 1.0 / (1.0 - B2 ** STEP)

    def body(w_ref, g_ref, m_ref, v_ref, d_ref, mo_ref, vo_ref):
        gv = g_ref[...]
        mn = B1 * m_ref[...] + (1.0 - B1) * gv
        vn = B2 * v_ref[...] + (1.0 - B2) * (gv * gv)
        mo_ref[...] = mn
        vo_ref[...] = vn
        d_ref[...] = -LR * ((mn * c1) / (jnp.sqrt(vn * c2) + ADAM_EPS) + WD * w_ref[...])

    blk = pl.BlockSpec((tr, cols), lambda i: (i, 0))
    return _pc(
        body, name="adamw_%dx%d" % (rows, cols), grid=(rows // tr,), in_specs=[blk] * 4, out_specs=[blk] * 3,
        out_shape=[_sds((rows, cols), F32)] * 3, compiler_params=_cp(1),
    )(w, g, m, v)


MESH_ID = pl.DeviceIdType.MESH
ANY = pl.BlockSpec(memory_space=pl.ANY)


def _place():
    x, y, c = lax.axis_index("x"), lax.axis_index("y"), lax.axis_index("c")
    others = [(1 - x, y), (x, 1 - y), (1 - x, 1 - y)]
    return x, y, c, others


def _gather_weights(shards):
    nk = len(shards)

    def body(*refs):
        ins, outs = refs[:nk], refs[nk:2 * nk]
        send_sems, recv_sems, local_sems = refs[2 * nk:]
        x, y, c, others = _place()
        me = 2 * x + y
        sib = (x, y, 1 - c)
        local = [pltpu.make_async_copy(ins[k], outs[k].at[me], local_sems.at[k]) for k in range(nk)]
        for cp in local:
            cp.start()
        sends = []
        for k in range(nk):
            hr = shards[k].shape[0] // 2
            for r, (cx, cy) in enumerate(others):
                cp = pltpu.make_async_remote_copy(
                    src_ref=ins[k].at[pl.ds(c * hr, hr)], dst_ref=outs[k].at[me, pl.ds(c * hr, hr)],
                    send_sem=send_sems.at[6 * k + r], recv_sem=recv_sems.at[6 * k + r],
                    device_id=(cx, cy, c), device_id_type=MESH_ID)
                cp.start()
                sends.append(cp)
        for k in range(nk):
            hr = shards[k].shape[0] // 2
            for r, (cx, cy) in enumerate(others):
                blk = outs[k].at[2 * cx + cy, pl.ds(c * hr, hr)]
                pltpu.make_async_remote_copy(
                    src_ref=blk, dst_ref=blk, send_sem=send_sems.at[6 * k + r], recv_sem=recv_sems.at[6 * k + r],
                    device_id=(cx, cy, c), device_id_type=MESH_ID).wait_recv()
                fwd = pltpu.make_async_remote_copy(
                    src_ref=blk, dst_ref=blk, send_sem=send_sems.at[6 * k + 3 + r], recv_sem=recv_sems.at[6 * k + 3 + r],
                    device_id=sib, device_id_type=MESH_ID)
                fwd.start()
                sends.append(fwd)
        for k in range(nk):
            hr = shards[k].shape[0] // 2
            for r, (cx, cy) in enumerate(others):
                blk = outs[k].at[2 * cx + cy, pl.ds((1 - c) * hr, hr)]
                pltpu.make_async_remote_copy(
                    src_ref=blk, dst_ref=blk, send_sem=send_sems.at[6 * k + 3 + r], recv_sem=recv_sems.at[6 * k + 3 + r],
                    device_id=sib, device_id_type=MESH_ID).wait_recv()
        for cp in sends:
            cp.wait_send()
        for cp in local:
            cp.wait()

    return _pc(
        body, name="gather_weights", in_specs=[ANY] * nk, out_specs=[ANY] * nk,
        out_shape=[_sds((N_CHIPS,) + a.shape, a.dtype) for a in shards],
        scratch_shapes=[pltpu.SemaphoreType.DMA((6 * nk,)), pltpu.SemaphoreType.DMA((6 * nk,)), pltpu.SemaphoreType.DMA((nk,))],
    )(*shards)


def _swap_halves(gs):
    nk = len(gs)

    def body(*refs):
        ins, outs = refs[:nk], refs[nk:2 * nk]
        send_sems, recv_sems = refs[2 * nk:]
        x, y, c, _ = _place()
        cps = []
        for k in range(nk):
            hr = gs[k].shape[1] // 2
            cp = pltpu.make_async_remote_copy(
                src_ref=ins[k].at[:, pl.ds((1 - c) * hr, hr)], dst_ref=outs[k],
                send_sem=send_sems.at[k], recv_sem=recv_sems.at[k], device_id=(x, y, 1 - c), device_id_type=MESH_ID)
            cp.start()
            cps.append(cp)
        for cp in cps:
            cp.wait()

    return _pc(
        body, name="swap_halves", in_specs=[ANY] * nk, out_specs=[ANY] * nk,
        out_shape=[_sds((N_CHIPS, a.shape[1] // 2, a.shape[2]), a.dtype) for a in gs],
        scratch_shapes=[pltpu.SemaphoreType.DMA((nk,)), pltpu.SemaphoreType.DMA((nk,))],
    )(*gs)


def _exchange_chips(ps):
    nk = len(ps)

    def body(*refs):
        ins, outs = refs[:nk], refs[nk:2 * nk]
        send_sems, recv_sems, local_sems = refs[2 * nk:]
        x, y, c, others = _place()
        me = 2 * x + y
        local = [pltpu.make_async_copy(ins[k].at[me], outs[k].at[me], local_sems.at[k]) for k in range(nk)]
        for cp in local:
            cp.start()
        sends = []
        for k in range(nk):
            for r, (cx, cy) in enumerate(others):
                cp = pltpu.make_async_remote_copy(
                    src_ref=ins[k].at[2 * cx + cy], dst_ref=outs[k].at[me],
                    send_sem=send_sems.at[3 * k + r], recv_sem=recv_sems.at[3 * k + r],
                    device_id=(cx, cy, c), device_id_type=MESH_ID)
                cp.start()
                sends.append(cp)
        for k in range(nk):
            for r, (cx, cy) in enumerate(others):
                blk = outs[k].at[2 * cx + cy]
                pltpu.make_async_remote_copy(
                    src_ref=blk, dst_ref=blk, send_sem=send_sems.at[3 * k + r], recv_sem=recv_sems.at[3 * k + r],
                    device_id=(cx, cy, c), device_id_type=MESH_ID).wait_recv()
        for cp in sends:
            cp.wait_send()
        for cp in local:
            cp.wait()

    return _pc(
        body, name="exchange_chips", in_specs=[ANY] * nk, out_specs=[ANY] * nk,
        out_shape=[_sds(a.shape, a.dtype) for a in ps],
        scratch_shapes=[pltpu.SemaphoreType.DMA((3 * nk,)), pltpu.SemaphoreType.DMA((3 * nk,)), pltpu.SemaphoreType.DMA((nk,))],
    )(*ps)


def _share_halves(hs):
    nk = len(hs)

    def body(*refs):
        ins, outs = refs[:nk], refs[nk:2 * nk]
        send_sems, recv_sems, local_sems = refs[2 * nk:]
        x, y, c, _ = _place()
        local = [pltpu.make_async_copy(ins[k], outs[k].at[c], local_sems.at[k]) for k in range(nk)]
        for cp in local:
            cp.start()
        sends = []
        for k in range(nk):
            cp = pltpu.make_async_remote_copy(
                src_ref=ins[k], dst_ref=outs[k].at[c], send_sem=send_sems.at[k], recv_sem=recv_sems.at[k],
                device_id=(x, y, 1 - c), device_id_type=MESH_ID)
            cp.start()
            sends.append(cp)
        for k in range(nk):
            blk = outs[k].at[1 - c]
            pltpu.make_async_remote_copy(
                src_ref=blk, dst_ref=blk, send_sem=send_sems.at[k], recv_sem=recv_sems.at[k],
                device_id=(x, y, 1 - c), device_id_type=MESH_ID).wait_recv()
        for cp in sends:
            cp.wait_send()
        for cp in local:
            cp.wait()

    return _pc(
        body, name="share_halves", in_specs=[ANY] * nk, out_specs=[ANY] * nk,
        out_shape=[_sds((2,) + a.shape, a.dtype) for a in hs],
        scratch_shapes=[pltpu.SemaphoreType.DMA((nk,)), pltpu.SemaphoreType.DMA((nk,)), pltpu.SemaphoreType.DMA((nk,))],
    )(*hs)


def _allreduce_small(v):
    rows, cols = v.shape

    def body(v_ref, o_ref, gath, send_sems, recv_sems):
        x, y, c, others = _place()
        sib = (x, y, 1 - c)

        def slot(px, py, pc):
            return gath.at[4 * px + 2 * py + pc]

        def copy(k, block, to, src=None):
            return pltpu.make_async_remote_copy(
                src_ref=slot(*block) if src is None else src, dst_ref=slot(*block),
                send_sem=send_sems.at[k], recv_sem=recv_sems.at[k], device_id=to, device_id_type=MESH_ID)

        me = (x, y, c)
        gath[4 * x + 2 * y + c] = v_ref[...]
        first = [copy(0, me, sib, src=v_ref)]
        first += [copy(1 + r, me, (cx, cy, c), src=v_ref) for r, (cx, cy) in enumerate(others)]
        for cp in first:
            cp.start()
        passed = [copy(4 + r, (cx, cy, c), sib) for r, (cx, cy) in enumerate(others)]
        for r, (cx, cy) in enumerate(others):
            copy(1 + r, (cx, cy, c), me).wait_recv()
            passed[r].start()
        copy(0, (x, y, 1 - c), me).wait_recv()
        for r, (cx, cy) in enumerate(others):
            copy(4 + r, (cx, cy, 1 - c), me).wait_recv()
        for cp in first + passed:
            cp.wait_send()
        acc = gath[0]
        for d in range(1, N_DEV):
            acc = acc + gath[d]
        o_ref[...] = acc

    vm = pl.BlockSpec(memory_space=pltpu.VMEM)
    return _pc(
        body, name="allreduce_small", in_specs=[vm], out_specs=vm, out_shape=_sds((rows, cols), F32),
        scratch_shapes=[pltpu.VMEM((N_DEV, rows, cols), F32), pltpu.SemaphoreType.DMA((7,)), pltpu.SemaphoreType.DMA((7,))],
    )(v)


def _cols_to_chips(a):
    r, c4 = a.shape
    return a.reshape(r, N_CHIPS, c4 // N_CHIPS).transpose(1, 0, 2)


def _chips_to_cols(a):
    n, r, c = a.shape
    return a.transpose(1, 0, 2).reshape(r, n * c)


def _head_rows(a, tq):
    s = a.shape[0]
    return a[:, :HF].reshape(s // tq, tq, NP, 2).transpose(2, 0, 3, 1)


def _head_cols(a):
    s = a.shape[0]
    return a[:, :HF].reshape(s, NP, 2).transpose(1, 0, 2)


def _local_step(x, mem, tgt, sp, w):
    s = x.shape[0]
    tq = min(TQ, s)
    b_f = jnp.pad(sp["b_forget"], ((0, 0), (0, LANES - HF)))
    proj, h, flog = _in_proj(x, sp["norm_mix_pre"], w["w_main"], w["w_f"])
    cf = _logf_cumsum(flog, b_f)
    oa, lse = _fox_fwd(proj, _head_rows(cf, tq))
    y, oc = _conv_fwd(proj, w["conv_w"], sp["conv_b"], sp["conv_ln_g"], sp["conv_ln_b"])
    mem_n, kv = _mem_kv(mem, sp["norm_mem"], w["w_kv"])
    om = _mem_attn_fwd(proj, kv)
    pa, pc, pm, merged, z, x1, h2 = _merge_out(oa, oc, om, proj, x, w["wpa"], w["wpc"], w["wpm"], w["wout"],
                                              sp["norm_mix_post"], sp["norm_ffn_pre"])
    gf, uf, act = _ffn_up(h2, w["w_gu"])
    dffn, dy, vec_f, loss_blk = _ffn_down_loss(act, w["w_d"], x1, tgt, sp["norm_ffn_post"])

    dgf, duf = _ffn_down_bwd(dffn, w["w_d"], gf, uf)
    dx1, dz, vec_n = _ffn_up_bwd(dgf, duf, w["w_gu"], x1, dy, z, sp["norm_ffn_pre"], sp["norm_mix_post"])
    dpa, dpc, dpm, dgl, doa, doc, dom, delta = _merge_bwd(dz, proj, pa, pc, pm, oa, w["wpa"], w["wpc"], w["wpm"], w["wout"])
    lse16 = lse.transpose(1, 0, 2).reshape(s, HF)
    dq, dk, dv, dcs, drs = _fox_bwd(proj, doa, _head_rows(lse16, tq), _head_rows(delta, tq), _head_cols(cf))
    dc16 = dcs.transpose(1, 0, 2).reshape(s, HF) + drs.transpose(1, 3, 0, 2).reshape(s, HF)
    dc = jnp.pad(dc16, ((0, 0), (0, LANES - HF)))
    df, db_blk = _logf_cumsum_bwd(flog, b_f, dc)
    dga, dgg, dcw, vec_c = _conv_bwd(proj, y, doc, w["conv_w"], sp["conv_ln_g"], sp["conv_ln_b"])
    dqm, dkv = _mem_attn_bwd(proj, kv, dom)
    dkv_b = dkv.astype(MXU)
    vec_m = _mem_kv_bwd(mem, sp["norm_mem"], w["w_kv"], dkv_b)
    pieces = [(dq, 0, 1), (dk, 1, 1), (dv, 2, 1), (dga, 3, 1), (dgg, 4, 1), (dqm, 5, 1), (dgl, 6, 3)]
    dx, vec_p = _in_proj_bwd(pieces, df, w["w_main"], w["w_f"], x, dx1, sp["norm_mix_pre"])

    dw_cols = [_wgrad(h, p, "wgrad_in_%d" % c0) for p, c0, _ in pieces]
    dwf = _wgrad(h, df, "wgrad_in_f")
    gw = {
        "w_in": jnp.concatenate(dw_cols[:3] + [dwf[:, :HF]] + dw_cols[3:], axis=1),
        "conv_w": dcw,
        "w_kv": _wgrad(mem_n, dkv_b, "wgrad_kv"),
        "wpa": _wgrad(oa, dpa, "wgrad_pa"), "wpc": _wgrad(oc, dpc, "wgrad_pc"), "wpm": _wgrad(om, dpm, "wgrad_pm"),
        "wout": _wgrad(merged, dz, "wgrad_out"),
        "w_gu": jnp.concatenate([_wgrad(h2, dgf, "wgrad_g"), _wgrad(h2, duf, "wgrad_u")], axis=1),
        "w_d": _wgrad(act, dffn, "wgrad_d"),
    }
    zero_row = jnp.zeros((1, D), F32)
    small = jnp.concatenate([
        vec_p[0:1], vec_n[1:2], vec_m[0:1], vec_c[2:3], vec_c[0:1], vec_c[1:2], vec_n[0:1], vec_f[0:1],
        jnp.pad(db_blk[0:1, :HF], ((0, 0), (0, D - HF))),
        jnp.pad(loss_blk[0:1, 0:1], ((0, 0), (0, D - 1))),
    ] + [zero_row] * (SMALL_ROWS - 10), axis=0)
    return dx, gw, small


SMALL_NAMES = ["norm_mix_pre", "norm_mix_post", "norm_mem", "conv_b", "conv_ln_g", "conv_ln_b", "norm_ffn_pre", "norm_ffn_post"]
PROJ_NAMES = ["w_proj_attn", "w_proj_conv", "w_proj_mem", "w_out"]
WEIGHT_ORDER = ["norm_mix_pre", "norm_mix_post", "norm_mem", "w_in", "b_forget", "conv_w", "conv_b", "conv_ln_g", "conv_ln_b",
                "w_kv_mem", "w_proj_attn", "w_proj_conv", "w_proj_mem", "w_out", "norm_ffn_pre", "norm_ffn_post",
                "w_gate_up", "w_down"]


def _pack_small(p):
    rows = [p[n] for n in SMALL_NAMES] + [jnp.pad(p["b_forget"], ((0, 0), (0, D - HF)))]
    return jnp.concatenate(rows + [jnp.zeros((SMALL_ROWS - len(rows), D), F32)], axis=0)


def _step(params, moms, vels, x, mem, tgt):
    c_idx = lax.axis_index("c").astype(jnp.int32).reshape(1)

    conv_w_p = jnp.pad(params["conv_w"], ((0, CWP - CW), (0, 0)))
    shards = [params["w_in"].astype(WIRE), conv_w_p, params["w_kv_mem"].astype(WIRE),
              jnp.concatenate([params[n] for n in PROJ_NAMES], axis=0).astype(WIRE),
              params["w_gate_up"].astype(WIRE), params["w_down"].astype(WIRE)]
    g_in, g_cw, g_kv, g_pj, g_gu, g_d = _gather_weights(shards)
    w_in_full = _chips_to_cols(g_in)
    pj = g_pj.reshape(N_CHIPS, 4, D // N_CHIPS, D).transpose(1, 0, 2, 3).reshape(4, D, D)
    w = {
        "w_main": jnp.concatenate([w_in_full[:, :3 * D], w_in_full[:, 3 * D + HF:]], axis=1),
        "w_f": jnp.pad(w_in_full[:, 3 * D:3 * D + HF], ((0, 0), (0, LANES - HF))),
        "conv_w": _chips_to_cols(g_cw), "w_kv": _chips_to_cols(g_kv),
        "wpa": pj[0], "wpc": pj[1], "wpm": pj[2], "wout": pj[3],
        "w_gu": _chips_to_cols(g_gu), "w_d": g_d.reshape(FFN, D),
    }

    dx, gw, small = _local_step(x, mem, tgt, params, w)

    pj_g = jnp.stack([gw["wpa"], gw["wpc"], gw["wpm"], gw["wout"]]).reshape(4, N_CHIPS, D // N_CHIPS, D)
    gs = [_cols_to_chips(gw["w_in"]), _cols_to_chips(gw["conv_w"]), _cols_to_chips(gw["w_kv"]),
          pj_g.transpose(1, 0, 2, 3).reshape(N_CHIPS, D, D), _cols_to_chips(gw["w_gu"]),
          gw["w_d"].reshape(N_CHIPS, FFN // N_CHIPS, D)]
    gs = [g.astype(WIRE) for g in gs]
    r1 = _swap_halves(gs)
    ps = [_pair_sum(g.reshape(N_CHIPS, 2, g.shape[1] // 2, g.shape[2]), r, c_idx) for g, r in zip(gs, r1)]
    r2 = _exchange_chips(ps)
    halves = [_chip_sum(r) for r in r2]
    red = [f.reshape(2 * f.shape[1], f.shape[2]) for f in _share_halves(halves)]
    pj_r = red[3].reshape(4, D // N_CHIPS, D)
    grads = {"w_in": red[0], "conv_w": red[1][:CW], "w_kv_mem": red[2], "w_gate_up": red[4], "w_down": red[5]}
    for i, n in enumerate(PROJ_NAMES):
        grads[n] = pj_r[i]

    tot = _allreduce_small(small)
    loss = tot[9, 0]
    for i, n in enumerate(SMALL_NAMES):
        grads[n] = tot[i:i + 1]
    grads["b_forget"] = tot[8:9, :HF]

    delta, new_m, new_v = {}, {}, {}
    ds, ms, vs = _adamw(_pack_small(params), tot.at[9:].set(0.0), _pack_small(moms), _pack_small(vels))
    for i, n in enumerate(SMALL_NAMES):
        delta[n], new_m[n], new_v[n] = ds[i:i + 1], ms[i:i + 1], vs[i:i + 1]
    delta["b_forget"], new_m["b_forget"], new_v["b_forget"] = ds[8:9, :HF], ms[8:9, :HF], vs[8:9, :HF]
    for n in ["w_in", "conv_w", "w_kv_mem", "w_gate_up", "w_down"] + PROJ_NAMES:
        delta[n], new_m[n], new_v[n] = _adamw(params[n], grads[n], moms[n], vels[n])
    return loss, dx, grads, delta, new_m, new_v


def kernel(x, mem, norm_mix_pre, norm_mix_post, norm_mem, w_in, b_forget, conv_w, conv_b, conv_ln_g, conv_ln_b, w_kv_mem, w_proj_attn, w_proj_conv, w_proj_mem, w_out, norm_ffn_pre, norm_ffn_post, w_gate_up, w_down, loss_target, m_norm_mix_pre, m_norm_mix_post, m_norm_mem, m_w_in, m_b_forget, m_conv_w, m_conv_b, m_conv_ln_g, m_conv_ln_b, m_w_kv_mem, m_w_proj_attn, m_w_proj_conv, m_w_proj_mem, m_w_out, m_norm_ffn_pre, m_norm_ffn_post, m_w_gate_up, m_w_down, v_norm_mix_pre, v_norm_mix_post, v_norm_mem, v_w_in, v_b_forget, v_conv_w, v_conv_b, v_conv_ln_g, v_conv_ln_b, v_w_kv_mem, v_w_proj_attn, v_w_proj_conv, v_w_proj_mem, v_w_out, v_norm_ffn_pre, v_norm_ffn_post, v_w_gate_up, v_w_down):
    local = dict(locals())
    lead = {n: local[n].shape[:-2] for n in WEIGHT_ORDER}
    two_d = lambda a: a.reshape(a.shape[-2:])
    params = {n: two_d(local[n]) for n in WEIGHT_ORDER}
    moms = {n: two_d(local["m_" + n]) for n in WEIGHT_ORDER}
    vels = {n: two_d(local["v_" + n]) for n in WEIGHT_ORDER}
    loss, dx, grads, delta, new_m, new_v = _step(params, moms, vels, two_d(x), two_d(mem), two_d(loss_target))
    outs = [loss, dx.reshape(x.shape)]
    for group in (grads, delta, new_m, new_v):
        outs += [group[n].reshape(lead[n] + group[n].shape) for n in WEIGHT_ORDER]
    return tuple(outs)
```

```python
import functools

import jax
import jax.numpy as jnp
from jax import lax
from jax.experimental import pallas as pl
from jax.experimental.pallas import tpu as pltpu

F32 = jnp.float32
MXU = jnp.bfloat16
WIRE = jnp.bfloat16

D = 1024
HF = 16
DH = 64
NP = D // 128
MEM_H = 4
MEM_DH = D // MEM_H
FFN = 2816
CW = 31
CWP = 32
HALO = 32
RMS_EPS = 1e-6
LN_EPS = 1e-5
LR, B1, B2, ADAM_EPS, WD, STEP = 0.001, 0.9, 0.999, 1e-8, 0.01, 10

N_CHIPS = 4
N_DEV = 8
LANES = 128
VMEM_LIMIT = 56 * 1024 * 1024

TM_PROJ = 512
NB_PROJ = 3
TQ = 1024
LOG2E = 1.4426950408889634
LN2 = 0.6931471805599453
QSCALE = DH ** -0.5 * LOG2E
X_BIAS = 0
X_QONE = 6
X_KONE = 8
TM_CONV = 256
TM_ROW = 256
TS_WG = 1024
WG_CAP = 1408
SMALL_ROWS = 16


def _pc(body, **kw):
    return pl.pallas_call(body, **kw)


def _cp(n_axes):
    return pltpu.CompilerParams(dimension_semantics=("arbitrary",) * n_axes, vmem_limit_bytes=VMEM_LIMIT)


def _sds(shape, dtype):
    return jax.ShapeDtypeStruct(shape, dtype)


def _dot(a, b):
    return jnp.dot(a, b, preferred_element_type=F32)


def _dot_nt(a, b):
    return lax.dot_general(a, b, (((1,), (1,)), ((), ())), preferred_element_type=F32)


def _dot_tn(a, b):
    return lax.dot_general(a, b, (((0,), (0,)), ((), ())), preferred_element_type=F32)


def _rms(u):
    return lax.rsqrt(jnp.mean(u * u, axis=-1, keepdims=True) + RMS_EPS)


def _rms_bwd(u, r, g, dn):
    w = dn * g
    return r * w - u * (r * r * r) * jnp.mean(u * w, axis=-1, keepdims=True)


def _sigmoid(z):
    return 1.0 / (1.0 + jnp.exp(-z))


def _tile(n, cap):
    if n <= cap:
        return n
    best = None
    for t in range(LANES, cap + 1, LANES):
        if n % t == 0:
            best = t
    assert best is not None, (n, cap)
    return best


def _rowtile(rows, cols, cap_bytes=1 << 20):
    best = None
    for t in range(8, rows + 1, 8):
        if rows % t == 0 and t * cols * 4 <= cap_bytes:
            best = t
    return best if best is not None else rows


def _split3(v):
    hi = v.astype(jnp.bfloat16)
    r1 = v - hi.astype(F32)
    mid = r1.astype(jnp.bfloat16)
    lo = (r1 - mid.astype(F32)).astype(jnp.bfloat16)
    return hi, mid, lo


def _dot_exact_rhs(a01, v):
    hi, mid, lo = _split3(v)
    return _dot(a01, hi) + _dot(a01, mid) + _dot(a01, lo)


def _in_proj(x, g_pre, w_main, w_f):
    s, d = x.shape
    n = w_main.shape[1]
    tm = min(TM_PROJ, s)
    tn = n // NB_PROJ

    def body(x_ref, g_ref, w_ref, wf_ref, proj_ref, h_ref, flog_ref, hs):
        @pl.when(pl.program_id(1) == 0)
        def _():
            xv = x_ref[...]
            h = (xv * _rms(xv) * g_ref[...]).astype(MXU)
            hs[...] = h
            h_ref[...] = h
            flog_ref[...] = _dot(h, wf_ref[...])

        res = _dot(hs[...], w_ref[...])

        @pl.when(pl.program_id(1) == 0)
        def _():
            proj_ref[:, pl.ds(0, d)] = (res[:, :d] * QSCALE).astype(MXU)
            proj_ref[:, pl.ds(d, tn - d)] = res[:, d:].astype(MXU)

        @pl.when(pl.program_id(1) != 0)
        def _():
            proj_ref[...] = res.astype(MXU)

    assert tn >= d
    return _pc(
        body, name="in_proj", grid=(s // tm, NB_PROJ),
        in_specs=[pl.BlockSpec((tm, d), lambda i, j: (i, 0)), pl.BlockSpec((1, d), lambda i, j: (0, 0)),
                  pl.BlockSpec((d, tn), lambda i, j: (0, j)), pl.BlockSpec((d, LANES), lambda i, j: (0, 0))],
        out_specs=[pl.BlockSpec((tm, tn), lambda i, j: (i, j)), pl.BlockSpec((tm, d), lambda i, j: (i, 0)),
                   pl.BlockSpec((tm, LANES), lambda i, j: (i, 0))],
        out_shape=[_sds((s, n), MXU), _sds((s, d), MXU), _sds((s, LANES), F32)],
        scratch_shapes=[pltpu.VMEM((tm, d), MXU)], compiler_params=_cp(2),
    )(x, g_pre, w_main, w_f)


def _log_sigmoid(z):
    e = jnp.exp(-jnp.abs(z))
    log1p_e = jnp.where(e < 1e-3, e * (1.0 - 0.5 * e), jnp.log(1.0 + e))
    return jnp.minimum(z, 0.0) - log1p_e


def _logf_cumsum(flog, b_f):
    s = flog.shape[0]
    ch = LANES

    def body(f_ref, b_ref, c_ref):
        r = lax.broadcasted_iota(jnp.int32, (ch, ch), 0)
        q = lax.broadcasted_iota(jnp.int32, (ch, ch), 1)
        tri = jnp.where(r >= q, 1.0, 0.0).astype(jnp.bfloat16)

        def step(i, carry):
            rows = pl.ds(pl.multiple_of(i * ch, ch), ch)
            lf = _log_sigmoid(f_ref[rows, :] + b_ref[...])
            c_ref[rows, :] = _dot_exact_rhs(tri, lf) + carry
            return carry + jnp.sum(lf, axis=0, keepdims=True)

        lax.fori_loop(0, s // ch, step, jnp.zeros((1, LANES), F32))

    return _pc(body, name="logf_cumsum", out_shape=_sds((s, LANES), F32),
               compiler_params=pltpu.CompilerParams(vmem_limit_bytes=VMEM_LIMIT))(flog, b_f)


def _logf_cumsum_bwd(flog, b_f, dc):
    s = flog.shape[0]
    ch = LANES

    def body(f_ref, b_ref, dc_ref, df_ref, db_ref):
        r = lax.broadcasted_iota(jnp.int32, (ch, ch), 0)
        q = lax.broadcasted_iota(jnp.int32, (ch, ch), 1)
        tri = jnp.where(r <= q, 1.0, 0.0).astype(jnp.bfloat16)
        nch = s // ch

        def step(t, carry):
            tail, dbsum = carry
            i = nch - 1 - t
            rows = pl.ds(pl.multiple_of(i * ch, ch), ch)
            dcv = dc_ref[rows, :]
            dlf = _dot_exact_rhs(tri, dcv) + tail
            z = f_ref[rows, :] + b_ref[...]
            df = dlf * _sigmoid(-z)
            df_ref[rows, :] = df.astype(MXU)
            return tail + jnp.sum(dcv, axis=0, keepdims=True), dbsum + jnp.sum(df, axis=0, keepdims=True)

        zero = jnp.zeros((1, LANES), F32)
        _, dbsum = lax.fori_loop(0, nch, step, (zero, zero))
        db_ref[...] = jnp.broadcast_to(dbsum, db_ref.shape)

    return _pc(body, name="logf_cumsum_bwd", out_shape=[_sds((s, LANES), MXU), _sds((8, LANES), F32)],
               compiler_params=pltpu.CompilerParams(vmem_limit_bytes=VMEM_LIMIT))(flog, b_f, dc)


def _head_masks(rows):
    lane = lax.broadcasted_iota(jnp.int32, (rows, LANES), 1)
    return lane < DH, lane >= DH


def _ext_masks(rows, key_side):
    lane = lax.broadcasted_iota(jnp.int32, (rows, 2 * LANES), 1)
    ext = lane - LANES
    out = []
    for a in range(2):
        head = (lane >= a * DH) & (lane < (a + 1) * DH)
        bias = (ext >= X_BIAS + 3 * a) & (ext < X_BIAS + 3 * a + 3)
        one = ext == (X_KONE if key_side else X_QONE) + a
        out.append(head | bias | one)
    return out


def _fox_prep(proj, ccol):
    s = proj.shape[0]
    tm = min(TM_PROJ, s)

    def body(q_ref, k_ref, c_ref, qx_ref, kx_ref):
        lane = lax.broadcasted_iota(jnp.int32, (tm, LANES), 1)
        qx_ref[:, pl.ds(0, LANES)] = q_ref[...]
        qx_ref[:, pl.ds(LANES, LANES)] = jnp.where(lane < X_QONE + 2, 1.0, 0.0).astype(MXU)
        kext = jnp.where((lane >= X_KONE) & (lane < X_KONE + 2), 1.0, 0.0).astype(jnp.bfloat16)
        for a in range(2):
            terms = _split3(c_ref[0, :, a:a + 1] * (-LOG2E))
            for t, term in enumerate(terms):
                kext = jnp.where(lane == X_BIAS + 3 * a + t, term, kext)
        kx_ref[:, pl.ds(0, LANES)] = k_ref[...]
        kx_ref[:, pl.ds(LANES, LANES)] = kext.astype(MXU)

    wide = pl.BlockSpec((tm, 2 * LANES), lambda p, i: (i, p))
    return _pc(
        body, name="fox_prep", grid=(NP, s // tm),
        in_specs=[pl.BlockSpec((tm, LANES), lambda p, i: (i, p)), pl.BlockSpec((tm, LANES), lambda p, i: (i, NP + p)),
                  pl.BlockSpec((1, tm, 2), lambda p, i: (p, i, 0))],
        out_specs=[wide, wide], out_shape=[_sds((s, NP * 2 * LANES), MXU)] * 2, compiler_params=_cp(2),
    )(proj, proj, ccol)


def _fox_fwd(proj, qx, kx):
    s = proj.shape[0]
    tq = min(TQ, s)
    nq = s // tq

    def body(q_ref, k_ref, v_ref, o_ref, lse_ref):
        i = pl.program_id(1)
        qv = q_ref[...]
        qmask = _ext_masks(tq, False)
        hmask = _head_masks(tq)
        qas = [jnp.where(qmask[a], qv, jnp.zeros_like(qv)) for a in range(2)]
        row = lax.broadcasted_iota(jnp.int32, (tq, tq), 0)
        col = lax.broadcasted_iota(jnp.int32, (tq, tq), 1)

        def blk(j, carry, diag=False):
            rows = pl.ds(pl.multiple_of(j * tq, tq), tq)
            kj = k_ref[rows, :]
            vj = v_ref[rows, :]
            out = []
            for a in range(2):
                m, acc = carry[a]
                sc = _dot_nt(qas[a], kj)
                if diag:
                    sc = jnp.where(row >= col, sc, -jnp.inf)
                m_new = jnp.maximum(m, jnp.max(sc, axis=-1, keepdims=True))
                p = jnp.exp2(sc - m_new)
                va = jnp.where(hmask[a], vj, jnp.ones_like(vj))
                out.append((m_new, jnp.exp2(m - m_new) * acc + _dot(p.astype(MXU), va)))
            return tuple(out)

        init = (jnp.full((tq, 1), -jnp.inf, F32), jnp.zeros((tq, LANES), F32))
        res = blk(i, lax.fori_loop(0, i, blk, (init, init)), True)
        lane = lax.broadcasted_iota(jnp.int32, (tq, LANES), 1)
        outs, lses = [], []
        for a in range(2):
            m, acc = res[a]
            l = jnp.sum(jnp.where(lane == DH * (1 - a), acc, 0.0), axis=-1, keepdims=True)
            outs.append(acc / l)
            lses.append(m + jnp.log(l) * LOG2E)
        o_ref[...] = jnp.where(hmask[0], outs[0], outs[1]).astype(MXU)
        lane2 = lax.broadcasted_iota(jnp.int32, (tq, 2), 1)
        lse_ref[0] = jnp.where(lane2 == 0, lses[0], lses[1])

    return _pc(
        body, name="fox_fwd", grid=(NP, nq),
        in_specs=[pl.BlockSpec((tq, 2 * LANES), lambda p, i: (i, p)),
                  pl.BlockSpec((s, 2 * LANES), lambda p, i: (0, p)),
                  pl.BlockSpec((s, LANES), lambda p, i: (0, 2 * NP + p))],
        out_specs=[pl.BlockSpec((tq, LANES), lambda p, i: (i, p)), pl.BlockSpec((1, tq, 2), lambda p, i: (p, i, 0))],
        out_shape=[_sds((s, D), MXU), _sds((NP, s, 2), F32)],
        compiler_params=_cp(2),
    )(qx, kx, proj)


def _fox_bwd(proj, qx, kx, do, lse_row, delta_row):
    s = proj.shape[0]
    tq = min(TQ, s)
    nq = s // tq

    def body(k_ref, v_ref, q_ref, do_ref, lse_ref, dl_ref, dq_ref, dqs_ref, dk_ref, dks_ref, dv_ref, dq_acc):
        j = pl.program_id(1)

        @pl.when(j == 0)
        def _():
            dq_acc[...] = jnp.zeros_like(dq_acc)

        kv = k_ref[...]
        v2 = v_ref[...]
        kmask = _ext_masks(tq, True)
        qmask = _ext_masks(tq, False)
        hmask = _head_masks(tq)
        row = lax.broadcasted_iota(jnp.int32, (tq, tq), 0)
        col = lax.broadcasted_iota(jnp.int32, (tq, tq), 1)
        carry = (jnp.zeros((tq, 2 * LANES), F32), jnp.zeros((tq, LANES), F32))
        for a in range(2):
            ka = jnp.where(kmask[a], kv, jnp.zeros_like(kv))
            va = jnp.where(hmask[a], v2, jnp.zeros_like(v2))

            def blk(i, carry, diag, a=a, ka=ka, va=va):
                dk_a, dv_a = carry
                rows = pl.ds(pl.multiple_of(i * tq, tq), tq)
                qi = q_ref[rows, :]
                doi = do_ref[rows, :]
                qa = jnp.where(qmask[a], qi, jnp.zeros_like(qi))
                doa = jnp.where(hmask[a], doi, jnp.zeros_like(doi))
                st = _dot_nt(ka, qi)
                if diag:
                    st = jnp.where(col >= row, st, -jnp.inf)
                pt = jnp.exp2(st - lse_ref[0, i, a:a + 1, :])
                dv_a = dv_a + _dot(pt.astype(MXU), doa)
                dpt = _dot_nt(va, doi)
                dsb = (pt * (dpt - dl_ref[0, i, a:a + 1, :])).astype(MXU)
                dk_a = dk_a + _dot(dsb, qa)
                dq_acc[rows, :] += _dot_tn(dsb, ka)
                return dk_a, dv_a

            carry = blk(j, carry, True)
            carry = lax.fori_loop(j + 1, nq, functools.partial(blk, diag=False), carry)
        dk_acc, dv_acc = carry
        dk_ref[...] = (dk_acc[:, :LANES] * LN2).astype(MXU)
        dks_ref[...] = dk_acc[:, LANES:]
        dv_ref[...] = dv_acc.astype(MXU)

        @pl.when(j == nq - 1)
        def _():
            dq_ref[...] = (dq_acc[:, pl.ds(0, LANES)] * DH ** -0.5).astype(MXU)
            dqs_ref[...] = dq_acc[:, pl.ds(LANES, LANES)]

    stat = pl.BlockSpec((1, nq, 2, tq), lambda p, j: (p, 0, 0, 0))
    whole = pl.BlockSpec((s, LANES), lambda p, j: (0, p))
    tile = pl.BlockSpec((tq, LANES), lambda p, j: (j, p))
    return _pc(
        body, name="fox_bwd", grid=(NP, nq),
        in_specs=[pl.BlockSpec((tq, 2 * LANES), lambda p, j: (j, p)),
                  pl.BlockSpec((tq, LANES), lambda p, j: (j, 2 * NP + p)),
                  pl.BlockSpec((s, 2 * LANES), lambda p, j: (0, p)),
                  whole, stat, stat],
        out_specs=[whole, whole, tile, tile, tile],
        out_shape=[_sds((s, D), MXU), _sds((s, D), F32), _sds((s, D), MXU), _sds((s, D), F32), _sds((s, D), MXU)],
        scratch_shapes=[pltpu.VMEM((s, 2 * LANES), F32)], compiler_params=_cp(2),
    )(kx, proj, qx, do, lse_row, delta_row)


def _glu(a, gate):
    return a.astype(F32) * _sigmoid(gate.astype(F32))


def _conv_fwd(proj, cw, cb, lg, lb):
    s = proj.shape[0]
    tm = min(TM_CONV, s)
    hb = tm // HALO

    def body(a_ref, g_ref, ah_ref, gh_ref, w_ref, cb_ref, lg_ref, lb_ref, y_ref, o_ref, gbuf):
        i = pl.program_id(0)
        gbuf[pl.ds(0, HALO), :] = jnp.where(i > 0, _glu(ah_ref[...], gh_ref[...]), 0.0)
        gbuf[pl.ds(HALO, tm), :] = _glu(a_ref[...], g_ref[...])
        acc = jnp.broadcast_to(cb_ref[...], (tm, D))
        for t in range(CW):
            acc = acc + w_ref[t:t + 1, :] * gbuf[pl.ds(HALO - (CW - 1) + t, tm), :]
        y_ref[...] = acc
        mu = jnp.mean(acc, axis=-1, keepdims=True)
        xc = acc - mu
        r = lax.rsqrt(jnp.mean(xc * xc, axis=-1, keepdims=True) + LN_EPS)
        nrm = xc * r * lg_ref[...] + lb_ref[...]
        o_ref[...] = (nrm * _sigmoid(nrm)).astype(MXU)

    vec = pl.BlockSpec((1, D), lambda i: (0, 0))
    return _pc(
        body, name="conv_fwd", grid=(s // tm,),
        in_specs=[pl.BlockSpec((tm, D), lambda i: (i, 3)), pl.BlockSpec((tm, D), lambda i: (i, 4)),
                  pl.BlockSpec((HALO, D), lambda i: (jnp.maximum(i * hb - 1, 0), 3)),
                  pl.BlockSpec((HALO, D), lambda i: (jnp.maximum(i * hb - 1, 0), 4)),
                  pl.BlockSpec((CWP, D), lambda i: (0, 0)), vec, vec, vec],
        out_specs=[pl.BlockSpec((tm, D), lambda i: (i, 0)), pl.BlockSpec((tm, D), lambda i: (i, 0))],
        out_shape=[_sds((s, D), F32), _sds((s, D), MXU)],
        scratch_shapes=[pltpu.VMEM((tm + HALO, D), F32)], compiler_params=_cp(1),
    )(proj, proj, proj, proj, cw, cb, lg, lb)


def _conv_bwd(proj, y, do, cw, lg, lb):
    s = proj.shape[0]
    tm = min(TM_CONV, s)
    hb = tm // HALO
    nt = s // tm
    last_hblk = s // HALO - 1

    def ln_bwd(yv, dov, lgv, lbv):
        mu = jnp.mean(yv, axis=-1, keepdims=True)
        xc = yv - mu
        r = lax.rsqrt(jnp.mean(xc * xc, axis=-1, keepdims=True) + LN_EPS)
        xh = xc * r
        nrm = xh * lgv + lbv
        sg = _sigmoid(nrm)
        dn = dov.astype(F32) * (sg * (1.0 + nrm * (1.0 - sg)))
        wv = dn * lgv
        dy = r * (wv - jnp.mean(wv, axis=-1, keepdims=True) - xh * jnp.mean(wv * xh, axis=-1, keepdims=True))
        return dy, dn, xh

    def body(a_ref, g_ref, ah_ref, gh_ref, y_ref, yn_ref, do_ref, don_ref, w_ref, lg_ref, lb_ref,
             da_ref, dg_ref, dw_ref, vec_ref, gbuf, dybuf):
        i = pl.program_id(0)

        @pl.when(i == 0)
        def _():
            dw_ref[...] = jnp.zeros_like(dw_ref)
            vec_ref[...] = jnp.zeros_like(vec_ref)

        lgv, lbv = lg_ref[...], lb_ref[...]
        av = a_ref[...].astype(F32)
        sgate = _sigmoid(g_ref[...].astype(F32))
        gbuf[pl.ds(0, HALO), :] = jnp.where(i > 0, _glu(ah_ref[...], gh_ref[...]), 0.0)
        gbuf[pl.ds(HALO, tm), :] = av * sgate
        dy, dn, xh = ln_bwd(y_ref[...], do_ref[...], lgv, lbv)
        dyn, _, _ = ln_bwd(yn_ref[...], don_ref[...], lgv, lbv)
        dybuf[pl.ds(0, tm), :] = dy
        dybuf[pl.ds(tm, HALO), :] = jnp.where(i < nt - 1, dyn, 0.0)
        vec_ref[0:1, :] += jnp.sum(dn * xh, axis=0, keepdims=True)
        vec_ref[1:2, :] += jnp.sum(dn, axis=0, keepdims=True)
        vec_ref[2:3, :] += jnp.sum(dy, axis=0, keepdims=True)
        dgl = jnp.zeros((tm, D), F32)
        for t in range(CW):
            dgl = dgl + w_ref[t:t + 1, :] * dybuf[pl.ds(CW - 1 - t, tm), :]
            dw_ref[t:t + 1, :] += jnp.sum(dy * gbuf[pl.ds(HALO - (CW - 1) + t, tm), :], axis=0, keepdims=True)
        da_ref[...] = (dgl * sgate).astype(MXU)
        dg_ref[...] = (dgl * av * sgate * (1.0 - sgate)).astype(MXU)

    vec = pl.BlockSpec((1, D), lambda i: (0, 0))
    cur = lambda c: pl.BlockSpec((tm, D), lambda i: (i, c))
    prv = lambda c: pl.BlockSpec((HALO, D), lambda i: (jnp.maximum(i * hb - 1, 0), c))
    nxt = pl.BlockSpec((HALO, D), lambda i: (jnp.minimum((i + 1) * hb, last_hblk), 0))
    return _pc(
        body, name="conv_bwd", grid=(nt,),
        in_specs=[cur(3), cur(4), prv(3), prv(4), cur(0), nxt, cur(0), nxt,
                  pl.BlockSpec((CWP, D), lambda i: (0, 0)), vec, vec],
        out_specs=[cur(0), cur(0), pl.BlockSpec((CWP, D), lambda i: (0, 0)), pl.BlockSpec((8, D), lambda i: (0, 0))],
        out_shape=[_sds((s, D), MXU), _sds((s, D), MXU), _sds((CWP, D), F32), _sds((8, D), F32)],
        scratch_shapes=[pltpu.VMEM((tm + HALO, D), F32), pltpu.VMEM((tm + HALO, D), F32)], compiler_params=_cp(1),
    )(proj, proj, proj, proj, y, y, do, do, cw, lg, lb)


def _mem_kv(mem, g_mem, w_kv):
    mm = mem.shape[0]

    def body(m_ref, g_ref, w_ref, mn_ref, kv_ref):
        mv = m_ref[...]
        mn = (mv * _rms(mv) * g_ref[...]).astype(MXU)
        mn_ref[...] = mn
        kv_ref[...] = _dot(mn, w_ref[...]).astype(MXU)

    return _pc(body, name="mem_kv", out_shape=[_sds((mm, D), MXU), _sds((mm, 2 * D), MXU)],
               compiler_params=pltpu.CompilerParams(vmem_limit_bytes=VMEM_LIMIT))(mem, g_mem, w_kv)


def _mem_kv_bwd(mem, g_mem, w_kv, dkv):
    mm = mem.shape[0]

    def body(m_ref, w_ref, dkv_ref, o_ref):
        mv = m_ref[...]
        dmn = _dot_nt(dkv_ref[...], w_ref[...])
        o_ref[...] = jnp.broadcast_to(jnp.sum(dmn * mv * _rms(mv), axis=0, keepdims=True), o_ref.shape)

    return _pc(body, name="mem_kv_bwd", out_shape=_sds((8, D), F32),
               compiler_params=pltpu.CompilerParams(vmem_limit_bytes=VMEM_LIMIT))(mem, w_kv, dkv)


def _mem_attn_fwd(proj, kv):
    s = proj.shape[0]
    mm = kv.shape[0]
    tm = min(TM_ROW, s)
    scale = MEM_DH ** -0.5

    def body(q_ref, kv_ref, o_ref):
        for h in range(MEM_H):
            cols = pl.ds(h * MEM_DH, MEM_DH)
            qh = q_ref[:, cols] * scale
            sc = _dot_nt(qh, kv_ref[:, cols])
            m = jnp.max(sc, axis=-1, keepdims=True)
            e = jnp.exp(sc - m)
            p = e / jnp.sum(e, axis=-1, keepdims=True)
            o_ref[:, cols] = _dot(p.astype(MXU), kv_ref[:, pl.ds(D + h * MEM_DH, MEM_DH)]).astype(MXU)

    return _pc(
        body, name="mem_attn_fwd", grid=(s // tm,),
        in_specs=[pl.BlockSpec((tm, D), lambda i: (i, 5)), pl.BlockSpec((mm, 2 * D), lambda i: (0, 0))],
        out_specs=pl.BlockSpec((tm, D), lambda i: (i, 0)), out_shape=_sds((s, D), MXU), compiler_params=_cp(1),
    )(proj, kv)


def _mem_attn_bwd(proj, kv, do):
    s = proj.shape[0]
    mm = kv.shape[0]
    tm = min(TM_ROW, s)
    scale = MEM_DH ** -0.5

    def body(q_ref, kv_ref, do_ref, dq_ref, dkv_ref):
        @pl.when(pl.program_id(0) == 0)
        def _():
            dkv_ref[...] = jnp.zeros_like(dkv_ref)

        for h in range(MEM_H):
            cols = pl.ds(h * MEM_DH, MEM_DH)
            vcols = pl.ds(D + h * MEM_DH, MEM_DH)
            qh = q_ref[:, cols]
            kh = kv_ref[:, cols] * scale
            doh = do_ref[:, cols]
            st = _dot_nt(kh, qh)
            m = jnp.max(st, axis=0, keepdims=True)
            e = jnp.exp(st - m)
            pt = e / jnp.sum(e, axis=0, keepdims=True)
            dpt = _dot_nt(kv_ref[:, vcols], doh)
            dst = pt * (dpt - jnp.sum(pt * dpt, axis=0, keepdims=True))
            dsb = dst.astype(MXU)
            dkv_ref[:, vcols] += _dot(pt.astype(MXU), doh)
            dkv_ref[:, cols] += _dot(dsb, qh) * scale
            dq_ref[:, cols] = _dot_tn(dsb, kh).astype(MXU)

    return _pc(
        body, name="mem_attn_bwd", grid=(s // tm,),
        in_specs=[pl.BlockSpec((tm, D), lambda i: (i, 5)), pl.BlockSpec((mm, 2 * D), lambda i: (0, 0)),
                  pl.BlockSpec((tm, D), lambda i: (i, 0))],
        out_specs=[pl.BlockSpec((tm, D), lambda i: (i, 0)), pl.BlockSpec((mm, 2 * D), lambda i: (0, 0))],
        out_shape=[_sds((s, D), MXU), _sds((mm, 2 * D), F32)], compiler_params=_cp(1),
    )(proj, kv, do)


def _resident(n):
    return [pltpu.VMEM((n, D, D), MXU), pltpu.SemaphoreType.DMA((n,))]


def _load_resident(hbm_refs, wbuf, sems):
    @pl.when(pl.program_id(0) == 0)
    def _():
        cps = [pltpu.make_async_copy(r, wbuf.at[k], sems.at[k]) for k, r in enumerate(hbm_refs)]
        for cp in cps:
            cp.start()
        for cp in cps:
            cp.wait()


def _merge_out(oa, oc, om, proj, x, wpa, wpc, wpm, wout, g_post, g_fpre):
    s = x.shape[0]
    tm = min(TM_ROW, s)

    def body(oa_ref, oc_ref, om_ref, gl_ref, x_ref, gp_ref, gf_ref, wpa_h, wpc_h, wpm_h, wout_h,
             pa_ref, pc_ref, pm_ref, mg_ref, z_ref, x1_ref, h2_ref, wbuf, sems):
        _load_resident([wpa_h, wpc_h, wpm_h, wout_h], wbuf, sems)
        merged = jnp.zeros((tm, D), F32)
        for b, (o_ref, p_ref) in enumerate(((oa_ref, pa_ref), (oc_ref, pc_ref), (om_ref, pm_ref))):
            pb = _dot(o_ref[...], wbuf[b])
            p_ref[...] = pb.astype(MXU)
            merged = merged + _sigmoid(gl_ref[:, pl.ds(b * D, D)].astype(F32)) * pb
        mg = merged.astype(MXU)
        mg_ref[...] = mg
        z = _dot(mg, wbuf[3])
        z_ref[...] = z
        x1 = x_ref[...] + z * _rms(z) * gp_ref[...]
        x1_ref[...] = x1
        h2_ref[...] = (x1 * _rms(x1) * gf_ref[...]).astype(MXU)

    rows = pl.BlockSpec((tm, D), lambda i: (i, 0))
    vec = pl.BlockSpec((1, D), lambda i: (0, 0))
    anyspec = pl.BlockSpec(memory_space=pl.ANY)
    return _pc(
        body, name="merge_out", grid=(s // tm,),
        in_specs=[rows, rows, rows, pl.BlockSpec((tm, 3 * D), lambda i: (i, 2)), rows, vec, vec,
                  anyspec, anyspec, anyspec, anyspec],
        out_specs=[rows] * 7,
        out_shape=[_sds((s, D), MXU)] * 4 + [_sds((s, D), F32)] * 2 + [_sds((s, D), MXU)],
        scratch_shapes=_resident(4), compiler_params=_cp(1),
    )(oa, oc, om, proj, x, g_post, g_fpre, wpa, wpc, wpm, wout)


def _ffn_up(h2, w_gu):
    s = h2.shape[0]
    tm = min(TM_PROJ, s)
    nb = 2
    bw = FFN // nb

    def body(h_ref, wg_ref, wu_ref, gf_ref, uf_ref, act_ref):
        hv = h_ref[...]
        gf = _dot(hv, wg_ref[...])
        uf = _dot(hv, wu_ref[...])
        gf_ref[...] = gf.astype(MXU)
        uf_ref[...] = uf.astype(MXU)
        act_ref[...] = (gf * _sigmoid(gf) * uf).astype(MXU)

    out = pl.BlockSpec((tm, bw), lambda i, j: (i, j))
    return _pc(
        body, name="ffn_up", grid=(s // tm, nb),
        in_specs=[pl.BlockSpec((tm, D), lambda i, j: (i, 0)), pl.BlockSpec((D, bw), lambda i, j: (0, j)),
                  pl.BlockSpec((D, bw), lambda i, j: (0, nb + j))],
        out_specs=[out, out, out], out_shape=[_sds((s, FFN), MXU)] * 3, compiler_params=_cp(2),
    )(h2, w_gu, w_gu)


def _ffn_down_loss(act, w_d, x1, tgt, g_fpost):
    s = act.shape[0]
    tm = min(TM_ROW, s)

    def body(a_ref, w_ref, x1_ref, t_ref, g_ref, dffn_ref, dy_ref, vec_ref, loss_ref):
        @pl.when(pl.program_id(0) == 0)
        def _():
            vec_ref[...] = jnp.zeros_like(vec_ref)
            loss_ref[...] = jnp.zeros_like(loss_ref)

        ffn = _dot(a_ref[...], w_ref[...])
        r = _rms(ffn)
        gv = g_ref[...]
        e = x1_ref[...] + ffn * r * gv - t_ref[...]
        loss_ref[...] += jnp.sum(e * e) * (0.5 / D)
        dy = e * (1.0 / D)
        dy_ref[...] = dy
        vec_ref[0:1, :] += jnp.sum(dy * ffn * r, axis=0, keepdims=True)
        dffn_ref[...] = _rms_bwd(ffn, r, gv, dy).astype(MXU)

    rows = pl.BlockSpec((tm, D), lambda i: (i, 0))
    return _pc(
        body, name="ffn_down_loss", grid=(s // tm,),
        in_specs=[pl.BlockSpec((tm, FFN), lambda i: (i, 0)), pl.BlockSpec((FFN, D), lambda i: (0, 0)), rows, rows,
                  pl.BlockSpec((1, D), lambda i: (0, 0))],
        out_specs=[rows, rows, pl.BlockSpec((8, D), lambda i: (0, 0)), pl.BlockSpec((8, LANES), lambda i: (0, 0))],
        out_shape=[_sds((s, D), MXU), _sds((s, D), F32), _sds((8, D), F32), _sds((8, LANES), F32)],
        compiler_params=_cp(1),
    )(act, w_d, x1, tgt, g_fpost)


def _ffn_down_bwd(dffn, w_d, gf, uf):
    s = dffn.shape[0]
    tm = min(TM_ROW, s)

    def body(d_ref, w_ref, gf_ref, uf_ref, dgf_ref, duf_ref):
        da = _dot_nt(d_ref[...], w_ref[...])
        gf = gf_ref[...].astype(F32)
        sg = _sigmoid(gf)
        duf_ref[...] = (da * gf * sg).astype(MXU)
        dgf_ref[...] = (da * uf_ref[...].astype(F32) * (sg * (1.0 + gf * (1.0 - sg)))).astype(MXU)

    wide = pl.BlockSpec((tm, FFN), lambda i: (i, 0))
    return _pc(
        body, name="ffn_down_bwd", grid=(s // tm,),
        in_specs=[pl.BlockSpec((tm, D), lambda i: (i, 0)), pl.BlockSpec((FFN, D), lambda i: (0, 0)), wide, wide],
        out_specs=[wide, wide], out_shape=[_sds((s, FFN), MXU)] * 2, compiler_params=_cp(1),
    )(dffn, w_d, gf, uf)


def _ffn_up_bwd(dgf, duf, w_gu, x1, dy, z, g_fpre, g_post):
    s = x1.shape[0]
    tm = min(TM_ROW, s)

    def body(dgf_ref, duf_ref, w_ref, x1_ref, dy_ref, z_ref, gf_ref, gp_ref, dx1_ref, dz_ref, vec_ref):
        @pl.when(pl.program_id(0) == 0)
        def _():
            vec_ref[...] = jnp.zeros_like(vec_ref)

        dh2 = _dot_nt(dgf_ref[...], w_ref[:, pl.ds(0, FFN)]) + _dot_nt(duf_ref[...], w_ref[:, pl.ds(FFN, FFN)])
        x1 = x1_ref[...]
        r2 = _rms(x1)
        vec_ref[0:1, :] += jnp.sum(dh2 * x1 * r2, axis=0, keepdims=True)
        dx1 = dy_ref[...] + _rms_bwd(x1, r2, gf_ref[...], dh2)
        dx1_ref[...] = dx1
        z = z_ref[...]
        rz = _rms(z)
        vec_ref[1:2, :] += jnp.sum(dx1 * z * rz, axis=0, keepdims=True)
        dz_ref[...] = _rms_bwd(z, rz, gp_ref[...], dx1).astype(MXU)

    rows = pl.BlockSpec((tm, D), lambda i: (i, 0))
    wide = pl.BlockSpec((tm, FFN), lambda i: (i, 0))
    vec = pl.BlockSpec((1, D), lambda i: (0, 0))
    return _pc(
        body, name="ffn_up_bwd", grid=(s // tm,),
        in_specs=[wide, wide, pl.BlockSpec((D, 2 * FFN), lambda i: (0, 0)), rows, rows, rows, vec, vec],
        out_specs=[rows, rows, pl.BlockSpec((8, D), lambda i: (0, 0))],
        out_shape=[_sds((s, D), F32), _sds((s, D), MXU), _sds((8, D), F32)], compiler_params=_cp(1),
    )(dgf, duf, w_gu, x1, dy, z, g_fpre, g_post)


def _merge_bwd(dz, proj, pa, pc, pm, oa, wpa, wpc, wpm, wout):
    s = dz.shape[0]
    tm = min(TM_ROW, s)

    def body(dz_ref, gl_ref, pa_ref, pc_ref, pm_ref, oa_ref, wpa_h, wpc_h, wpm_h, wout_h,
             dpa_ref, dpc_ref, dpm_ref, dgl_ref, doa_ref, doc_ref, dom_ref, dl_ref, wbuf, sems):
        _load_resident([wpa_h, wpc_h, wpm_h, wout_h], wbuf, sems)
        dm = _dot_nt(dz_ref[...], wbuf[3])
        quads = ((pa_ref, dpa_ref, doa_ref), (pc_ref, dpc_ref, doc_ref), (pm_ref, dpm_ref, dom_ref))
        for b, (p_ref, dp_ref, do_ref) in enumerate(quads):
            cols = pl.ds(b * D, D)
            gt = _sigmoid(gl_ref[:, cols].astype(F32))
            dp = (dm * gt).astype(MXU)
            dp_ref[...] = dp
            dgl_ref[:, cols] = (dm * p_ref[...].astype(F32) * gt * (1.0 - gt)).astype(MXU)
            dob = _dot_nt(dp, wbuf[b]).astype(MXU)
            do_ref[...] = dob
            if b == 0:
                prod = dob.astype(F32) * oa_ref[...].astype(F32)
                d_i = lax.broadcasted_iota(jnp.int32, (D, LANES), 0)
                h_i = lax.broadcasted_iota(jnp.int32, (D, LANES), 1)
                sel = jnp.where(lax.shift_right_logical(d_i, DH.bit_length() - 1) == h_i, 1.0, 0.0).astype(jnp.bfloat16)
                dl_ref[...] = _dot_exact_rhs_t(prod, sel)

    rows = pl.BlockSpec((tm, D), lambda i: (i, 0))
    anyspec = pl.BlockSpec(memory_space=pl.ANY)
    wide = pl.BlockSpec((tm, 3 * D), lambda i: (i, 2))
    return _pc(
        body, name="merge_bwd", grid=(s // tm,),
        in_specs=[rows, wide, rows, rows, rows, rows, anyspec, anyspec, anyspec, anyspec],
        out_specs=[rows, rows, rows, pl.BlockSpec((tm, 3 * D), lambda i: (i, 0)), rows, rows, rows,
                   pl.BlockSpec((tm, LANES), lambda i: (i, 0))],
        out_shape=[_sds((s, D), MXU)] * 3 + [_sds((s, 3 * D), MXU)] + [_sds((s, D), MXU)] * 3 + [_sds((s, LANES), F32)],
        scratch_shapes=_resident(4), compiler_params=_cp(1),
    )(dz, proj, pa, pc, pm, oa, wpa, wpc, wpm, wout)


def _dot_exact_rhs_t(v, b01):
    hi, mid, lo = _split3(v)
    return _dot(hi, b01) + _dot(mid, b01) + _dot(lo, b01)


def _in_proj_bwd(pieces, df, w_main, w_f, x, dx1, g_pre):
    s = x.shape[0]
    tm = min(TM_ROW, s)
    n_main = w_main.shape[1]
    np_ = len(pieces)

    def body(*refs):
        p_refs = refs[:np_]
        df_ref, x_ref, dx1_ref, g_ref, w_h, wf_ref, dx_ref, vec_ref, wbuf, sem = refs[np_:]

        @pl.when(pl.program_id(0) == 0)
        def _():
            vec_ref[...] = jnp.zeros_like(vec_ref)
            cp = pltpu.make_async_copy(w_h, wbuf, sem)
            cp.start()
            cp.wait()

        dh = _dot_nt(df_ref[...], wf_ref[...])
        for p_ref, (_, c0, nc) in zip(p_refs, pieces):
            dh = dh + _dot_nt(p_ref[...], wbuf[:, pl.ds(c0 * D, nc * D)])
        xv = x_ref[...]
        r = _rms(xv)
        vec_ref[0:1, :] += jnp.sum(dh * xv * r, axis=0, keepdims=True)
        dx_ref[...] = dx1_ref[...] + _rms_bwd(xv, r, g_ref[...], dh)

    rows = pl.BlockSpec((tm, D), lambda i: (i, 0))
    p_specs = [pl.BlockSpec((tm, nc * D), lambda i: (i, 0)) for _, _, nc in pieces]
    return _pc(
        body, name="in_proj_bwd", grid=(s // tm,),
        in_specs=p_specs + [pl.BlockSpec((tm, LANES), lambda i: (i, 0)), rows, rows, pl.BlockSpec((1, D), lambda i: (0, 0)),
                            pl.BlockSpec(memory_space=pl.ANY), pl.BlockSpec((D, LANES), lambda i: (0, 0))],
        out_specs=[rows, pl.BlockSpec((8, D), lambda i: (0, 0))],
        out_shape=[_sds((s, D), F32), _sds((8, D), F32)],
        scratch_shapes=[pltpu.VMEM((D, n_main), MXU), pltpu.SemaphoreType.DMA], compiler_params=_cp(1),
    )(*[p for p, _, _ in pieces], df, x, dx1, g_pre, w_main, w_f)


def _wgrad(xa, dy, name):
    s, k = xa.shape
    n = dy.shape[1]
    ts = min(TS_WG, s)
    tk = _tile(k, WG_CAP)
    tn = _tile(n, WG_CAP)

    def body(x_ref, dy_ref, o_ref):
        @pl.when(pl.program_id(2) == 0)
        def _():
            o_ref[...] = jnp.zeros_like(o_ref)

        o_ref[...] += _dot_tn(x_ref[...], dy_ref[...])

    return _pc(
        body, name=name, grid=(k // tk, n // tn, s // ts),
        in_specs=[pl.BlockSpec((ts, tk), lambda a, b, c: (c, a)), pl.BlockSpec((ts, tn), lambda a, b, c: (c, b))],
        out_specs=pl.BlockSpec((tk, tn), lambda a, b, c: (a, b)), out_shape=_sds((k, n), F32), compiler_params=_cp(3),
    )(xa, dy)


def _pair_sum(g, r1, c_idx):
    _, _, hr, cols = g.shape
    tr = _rowtile(hr, cols)

    def body(c_ref, g_ref, r_ref, o_ref):
        o_ref[0] = (g_ref[0, 0].astype(F32) + r_ref[0].astype(F32)).astype(WIRE)

    return _pc(
        body, name="pair_sum_%dx%d" % (hr, cols), out_shape=_sds((N_CHIPS, hr, cols), WIRE),
        grid_spec=pltpu.PrefetchScalarGridSpec(
            num_scalar_prefetch=1, grid=(N_CHIPS, hr // tr),
            in_specs=[pl.BlockSpec((1, 1, tr, cols), lambda d, i, c: (d, c[0], i, 0)),
                      pl.BlockSpec((1, tr, cols), lambda d, i, c: (d, i, 0))],
            out_specs=pl.BlockSpec((1, tr, cols), lambda d, i, c: (d, i, 0))),
        compiler_params=_cp(2),
    )(c_idx, g, r1)


def _chip_sum(r2, slot, base=None):
    _, hr, cols = r2.shape
    tr = _rowtile(hr, cols)

    def body(s_ref, r_ref, *rest):
        o_ref = rest[-1]
        acc = r_ref[0].astype(F32)
        for d in range(1, N_CHIPS):
            acc = acc + r_ref[d].astype(F32)
        o_ref[0] = acc

    based = base is not None
    return _pc(
        body, name="chip_sum_%dx%d_%d" % (hr, cols, int(based)), out_shape=_sds((2, hr, cols), F32),
        grid_spec=pltpu.PrefetchScalarGridSpec(
            num_scalar_prefetch=1, grid=(hr // tr,),
            in_specs=[pl.BlockSpec((N_CHIPS, tr, cols), lambda i, s: (0, i, 0))] + ([ANY] if based else []),
            out_specs=pl.BlockSpec((1, tr, cols), lambda i, s: (s[0], i, 0))),
        input_output_aliases={2: 0} if based else {}, compiler_params=_cp(1),
    )(*((slot, r2, base) if based else (slot, r2)))


def _adamw(w, g, m, v):
    rows, cols = w.shape
    tr = _rowtile(rows, cols, 1 << 19)
    c1 = 1.0 / (1.0 - B1 ** STEP)
    c2 = 1.0 / (1.0 - B2 ** STEP)

    def body(w_ref, g_ref, m_ref, v_ref, d_ref, mo_ref, vo_ref):
        gv = g_ref[...]
        mn = B1 * m_ref[...] + (1.0 - B1) * gv
        vn = B2 * v_ref[...] + (1.0 - B2) * (gv * gv)
        mo_ref[...] = mn
        vo_ref[...] = vn
        d_ref[...] = -LR * ((mn * c1) / (jnp.sqrt(vn * c2) + ADAM_EPS) + WD * w_ref[...])

    blk = pl.BlockSpec((tr, cols), lambda i: (i, 0))
    return _pc(
        body, name="adamw_%dx%d" % (rows, cols), grid=(rows // tr,), in_specs=[blk] * 4, out_specs=[blk] * 3,
        out_shape=[_sds((rows, cols), F32)] * 3, compiler_params=_cp(1),
    )(w, g, m, v)


MESH_ID = pl.DeviceIdType.MESH
ANY = pl.BlockSpec(memory_space=pl.ANY)


def _place():
    x, y, c = lax.axis_index("x"), lax.axis_index("y"), lax.axis_index("c")
    others = [(1 - x, y), (x, 1 - y), (1 - x, 1 - y)]
    return x, y, c, others


def _gather_weights(shards):
    nk = len(shards)

    def body(*refs):
        ins, outs = refs[:nk], refs[nk:2 * nk]
        send_sems, recv_sems, local_sems = refs[2 * nk:]
        x, y, c, others = _place()
        me = 2 * x + y
        sib = (x, y, 1 - c)
        local = [pltpu.make_async_copy(ins[k], outs[k].at[me], local_sems.at[k]) for k in range(nk)]
        for cp in local:
            cp.start()
        sends = []
        for k in range(nk):
            hr = shards[k].shape[0] // 2
            for r, (cx, cy) in enumerate(others):
                cp = pltpu.make_async_remote_copy(
                    src_ref=ins[k].at[pl.ds(c * hr, hr)], dst_ref=outs[k].at[me, pl.ds(c * hr, hr)],
                    send_sem=send_sems.at[6 * k + r], recv_sem=recv_sems.at[6 * k + r],
                    device_id=(cx, cy, c), device_id_type=MESH_ID)
                cp.start()
                sends.append(cp)
        for k in range(nk):
            hr = shards[k].shape[0] // 2
            for r, (cx, cy) in enumerate(others):
                blk = outs[k].at[2 * cx + cy, pl.ds(c * hr, hr)]
                pltpu.make_async_remote_copy(
                    src_ref=blk, dst_ref=blk, send_sem=send_sems.at[6 * k + r], recv_sem=recv_sems.at[6 * k + r],
                    device_id=(cx, cy, c), device_id_type=MESH_ID).wait_recv()
                fwd = pltpu.make_async_remote_copy(
                    src_ref=blk, dst_ref=blk, send_sem=send_sems.at[6 * k + 3 + r], recv_sem=recv_sems.at[6 * k + 3 + r],
                    device_id=sib, device_id_type=MESH_ID)
                fwd.start()
                sends.append(fwd)
        for k in range(nk):
            hr = shards[k].shape[0] // 2
            for r, (cx, cy) in enumerate(others):
                blk = outs[k].at[2 * cx + cy, pl.ds((1 - c) * hr, hr)]
                pltpu.make_async_remote_copy(
                    src_ref=blk, dst_ref=blk, send_sem=send_sems.at[6 * k + 3 + r], recv_sem=recv_sems.at[6 * k + 3 + r],
                    device_id=sib, device_id_type=MESH_ID).wait_recv()
        for cp in sends:
            cp.wait_send()
        for cp in local:
            cp.wait()

    return _pc(
        body, name="gather_weights", in_specs=[ANY] * nk, out_specs=[ANY] * nk,
        out_shape=[_sds((N_CHIPS,) + a.shape, a.dtype) for a in shards],
        scratch_shapes=[pltpu.SemaphoreType.DMA((6 * nk,)), pltpu.SemaphoreType.DMA((6 * nk,)), pltpu.SemaphoreType.DMA((nk,))],
    )(*shards)


def _swap_sibling(gs, halves):
    nk = len(gs)

    def body(*refs):
        ins, outs = refs[:nk], refs[nk:2 * nk]
        send_sems, recv_sems = refs[2 * nk:]
        x, y, c, _ = _place()
        cps = []
        for k in range(nk):
            hr = gs[k].shape[1] // 2
            cp = pltpu.make_async_remote_copy(
                src_ref=ins[k].at[:, pl.ds((1 - c) * hr, hr)] if halves else ins[k], dst_ref=outs[k],
                send_sem=send_sems.at[k], recv_sem=recv_sems.at[k], device_id=(x, y, 1 - c), device_id_type=MESH_ID)
            cp.start()
            cps.append(cp)
        for cp in cps:
            cp.wait()

    return _pc(
        body, name="swap_halves" if halves else "swap_slabs", in_specs=[ANY] * nk, out_specs=[ANY] * nk,
        out_shape=[_sds((N_CHIPS, a.shape[1] // 2 if halves else a.shape[1], a.shape[2]), a.dtype) for a in gs],
        scratch_shapes=[pltpu.SemaphoreType.DMA((nk,)), pltpu.SemaphoreType.DMA((nk,))],
    )(*gs)


def _exchange_chips(ps):
    nk = len(ps)

    def body(*refs):
        ins, outs = refs[:nk], refs[nk:2 * nk]
        send_sems, recv_sems, local_sems = refs[2 * nk:]
        x, y, c, others = _place()
        me = 2 * x + y
        local = [pltpu.make_async_copy(ins[k].at[me], outs[k].at[me], local_sems.at[k]) for k in range(nk)]
        for cp in local:
            cp.start()
        sends = []
        for k in range(nk):
            for r, (cx, cy) in enumerate(others):
                cp = pltpu.make_async_remote_copy(
                    src_ref=ins[k].at[2 * cx + cy], dst_ref=outs[k].at[me],
                    send_sem=send_sems.at[3 * k + r], recv_sem=recv_sems.at[3 * k + r],
                    device_id=(cx, cy, c), device_id_type=MESH_ID)
                cp.start()
                sends.append(cp)
        for k in range(nk):
            for r, (cx, cy) in enumerate(others):
                blk = outs[k].at[2 * cx + cy]
                pltpu.make_async_remote_copy(
                    src_ref=blk, dst_ref=blk, send_sem=send_sems.at[3 * k + r], recv_sem=recv_sems.at[3 * k + r],
                    device_id=(cx, cy, c), device_id_type=MESH_ID).wait_recv()
        for cp in sends:
            cp.wait_send()
        for cp in local:
            cp.wait()

    return _pc(
        body, name="exchange_chips", in_specs=[ANY] * nk, out_specs=[ANY] * nk,
        out_shape=[_sds(a.shape, a.dtype) for a in ps],
        scratch_shapes=[pltpu.SemaphoreType.DMA((3 * nk,)), pltpu.SemaphoreType.DMA((3 * nk,)), pltpu.SemaphoreType.DMA((nk,))],
    )(*ps)


def _allreduce_small(v):
    rows, cols = v.shape

    def body(v_ref, o_ref, gath, send_sems, recv_sems):
        x, y, c, others = _place()
        sib = (x, y, 1 - c)

        def slot(px, py, pc):
            return gath.at[4 * px + 2 * py + pc]

        def copy(k, block, to, src=None):
            return pltpu.make_async_remote_copy(
                src_ref=slot(*block) if src is None else src, dst_ref=slot(*block),
                send_sem=send_sems.at[k], recv_sem=recv_sems.at[k], device_id=to, device_id_type=MESH_ID)

        me = (x, y, c)
        gath[4 * x + 2 * y + c] = v_ref[...]
        first = [copy(0, me, sib, src=v_ref)]
        first += [copy(1 + r, me, (cx, cy, c), src=v_ref) for r, (cx, cy) in enumerate(others)]
        for cp in first:
            cp.start()
        passed = [copy(4 + r, (cx, cy, c), sib) for r, (cx, cy) in enumerate(others)]
        for r, (cx, cy) in enumerate(others):
            copy(1 + r, (cx, cy, c), me).wait_recv()
            passed[r].start()
        copy(0, (x, y, 1 - c), me).wait_recv()
        for r, (cx, cy) in enumerate(others):
            copy(4 + r, (cx, cy, 1 - c), me).wait_recv()
        for cp in first + passed:
            cp.wait_send()
        acc = gath[0]
        for d in range(1, N_DEV):
            acc = acc + gath[d]
        o_ref[...] = acc

    vm = pl.BlockSpec(memory_space=pltpu.VMEM)
    return _pc(
        body, name="allreduce_small", in_specs=[vm], out_specs=vm, out_shape=_sds((rows, cols), F32),
        scratch_shapes=[pltpu.VMEM((N_DEV, rows, cols), F32), pltpu.SemaphoreType.DMA((7,)), pltpu.SemaphoreType.DMA((7,))],
    )(v)


def _cols_to_chips(a):
    r, c4 = a.shape
    return a.reshape(r, N_CHIPS, c4 // N_CHIPS).transpose(1, 0, 2)


def _chips_to_cols(a):
    n, r, c = a.shape
    return a.transpose(1, 0, 2).reshape(r, n * c)


def _head_rows(a, tq):
    s = a.shape[0]
    return a[:, :HF].reshape(s // tq, tq, NP, 2).transpose(2, 0, 3, 1)


def _head_cols(a):
    s = a.shape[0]
    return a[:, :HF].reshape(s, NP, 2).transpose(1, 0, 2)


def _local_step(x, mem, tgt, sp, w):
    s = x.shape[0]
    tq = min(TQ, s)
    b_f = jnp.pad(sp["b_forget"], ((0, 0), (0, LANES - HF)))
    proj, h, flog = _in_proj(x, sp["norm_mix_pre"], w["w_main"], w["w_f"])
    cf = _logf_cumsum(flog, b_f)
    qx, kx = _fox_prep(proj, _head_cols(cf))
    oa, lse = _fox_fwd(proj, qx, kx)
    y, oc = _conv_fwd(proj, w["conv_w"], sp["conv_b"], sp["conv_ln_g"], sp["conv_ln_b"])
    mem_n, kv = _mem_kv(mem, sp["norm_mem"], w["w_kv"])
    om = _mem_attn_fwd(proj, kv)
    pa, pc, pm, merged, z, x1, h2 = _merge_out(oa, oc, om, proj, x, w["wpa"], w["wpc"], w["wpm"], w["wout"],
                                              sp["norm_mix_post"], sp["norm_ffn_pre"])
    gf, uf, act = _ffn_up(h2, w["w_gu"])
    dffn, dy, vec_f, loss_blk = _ffn_down_loss(act, w["w_d"], x1, tgt, sp["norm_ffn_post"])

    dgf, duf = _ffn_down_bwd(dffn, w["w_d"], gf, uf)
    dx1, dz, vec_n = _ffn_up_bwd(dgf, duf, w["w_gu"], x1, dy, z, sp["norm_ffn_pre"], sp["norm_mix_post"])
    dpa, dpc, dpm, dgl, doa, doc, dom, delta = _merge_bwd(dz, proj, pa, pc, pm, oa, w["wpa"], w["wpc"], w["wpm"], w["wout"])
    lse16 = lse.transpose(1, 0, 2).reshape(s, HF)
    dq, dqs, dk, dks, dv = _fox_bwd(proj, qx, kx, doa, _head_rows(lse16, tq), _head_rows(delta, tq))
    over_keys = dqs.reshape(s, NP, LANES)[:, :, X_KONE:X_KONE + 2].reshape(s, HF)
    over_queries = dks.reshape(s, NP, LANES)[:, :, X_QONE:X_QONE + 2].reshape(s, HF)
    dc = jnp.pad(over_keys - over_queries, ((0, 0), (0, LANES - HF)))
    df, db_blk = _logf_cumsum_bwd(flog, b_f, dc)
    dga, dgg, dcw, vec_c = _conv_bwd(proj, y, doc, w["conv_w"], sp["conv_ln_g"], sp["conv_ln_b"])
    dqm, dkv = _mem_attn_bwd(proj, kv, dom)
    dkv_b = dkv.astype(MXU)
    vec_m = _mem_kv_bwd(mem, sp["norm_mem"], w["w_kv"], dkv_b)
    pieces = [(dq, 0, 1), (dk, 1, 1), (dv, 2, 1), (dga, 3, 1), (dgg, 4, 1), (dqm, 5, 1), (dgl, 6, 3)]
    dx, vec_p = _in_proj_bwd(pieces, df, w["w_main"], w["w_f"], x, dx1, sp["norm_mix_pre"])

    dw_cols = [_wgrad(h, p, "wgrad_in_%d" % c0) for p, c0, _ in pieces]
    dwf = _wgrad(h, df, "wgrad_in_f")
    gw = {
        "w_in": jnp.concatenate(dw_cols[:3] + [dwf[:, :HF]] + dw_cols[3:], axis=1),
        "conv_w": dcw,
        "w_kv": _wgrad(mem_n, dkv_b, "wgrad_kv"),
        "wpa": _wgrad(oa, dpa, "wgrad_pa"), "wpc": _wgrad(oc, dpc, "wgrad_pc"), "wpm": _wgrad(om, dpm, "wgrad_pm"),
        "wout": _wgrad(merged, dz, "wgrad_out"),
        "w_gu": jnp.concatenate([_wgrad(h2, dgf, "wgrad_g"), _wgrad(h2, duf, "wgrad_u")], axis=1),
        "w_d": _wgrad(act, dffn, "wgrad_d"),
    }
    zero_row = jnp.zeros((1, D), F32)
    small = jnp.concatenate([
        vec_p[0:1], vec_n[1:2], vec_m[0:1], vec_c[2:3], vec_c[0:1], vec_c[1:2], vec_n[0:1], vec_f[0:1],
        jnp.pad(db_blk[0:1, :HF], ((0, 0), (0, D - HF))),
        jnp.pad(loss_blk[0:1, 0:1], ((0, 0), (0, D - 1))),
    ] + [zero_row] * (SMALL_ROWS - 10), axis=0)
    return dx, gw, small


SMALL_NAMES = ["norm_mix_pre", "norm_mix_post", "norm_mem", "conv_b", "conv_ln_g", "conv_ln_b", "norm_ffn_pre", "norm_ffn_post"]
PROJ_NAMES = ["w_proj_attn", "w_proj_conv", "w_proj_mem", "w_out"]
WEIGHT_ORDER = ["norm_mix_pre", "norm_mix_post", "norm_mem", "w_in", "b_forget", "conv_w", "conv_b", "conv_ln_g", "conv_ln_b",
                "w_kv_mem", "w_proj_attn", "w_proj_conv", "w_proj_mem", "w_out", "norm_ffn_pre", "norm_ffn_post",
                "w_gate_up", "w_down"]


def _pack_small(p):
    rows = [p[n] for n in SMALL_NAMES] + [jnp.pad(p["b_forget"], ((0, 0), (0, D - HF)))]
    return jnp.concatenate(rows + [jnp.zeros((SMALL_ROWS - len(rows), D), F32)], axis=0)


def _step(params, moms, vels, x, mem, tgt):
    c_idx = lax.axis_index("c").astype(jnp.int32).reshape(1)

    conv_w_p = jnp.pad(params["conv_w"], ((0, CWP - CW), (0, 0)))
    shards = [params["w_in"].astype(WIRE), conv_w_p, params["w_kv_mem"].astype(WIRE),
              jnp.concatenate([params[n] for n in PROJ_NAMES], axis=0).astype(WIRE),
              params["w_gate_up"].astype(WIRE), params["w_down"].astype(WIRE)]
    g_in, g_cw, g_kv, g_pj, g_gu, g_d = _gather_weights(shards)
    w_in_full = _chips_to_cols(g_in)
    pj = g_pj.reshape(N_CHIPS, 4, D // N_CHIPS, D).transpose(1, 0, 2, 3).reshape(4, D, D)
    w = {
        "w_main": jnp.concatenate([w_in_full[:, :3 * D], w_in_full[:, 3 * D + HF:]], axis=1),
        "w_f": jnp.pad(w_in_full[:, 3 * D:3 * D + HF], ((0, 0), (0, LANES - HF))),
        "conv_w": _chips_to_cols(g_cw), "w_kv": _chips_to_cols(g_kv),
        "wpa": pj[0], "wpc": pj[1], "wpm": pj[2], "wout": pj[3],
        "w_gu": _chips_to_cols(g_gu), "w_d": g_d.reshape(FFN, D),
    }

    dx, gw, small = _local_step(x, mem, tgt, params, w)

    pj_g = jnp.stack([gw["wpa"], gw["wpc"], gw["wpm"], gw["wout"]]).reshape(4, N_CHIPS, D // N_CHIPS, D)
    gs = [_cols_to_chips(gw["w_in"]), _cols_to_chips(gw["conv_w"]), _cols_to_chips(gw["w_kv"]),
          pj_g.transpose(1, 0, 2, 3).reshape(N_CHIPS, D, D), _cols_to_chips(gw["w_gu"]),
          gw["w_d"].reshape(N_CHIPS, FFN // N_CHIPS, D)]
    gs = [g.astype(WIRE) for g in gs]
    r1 = _swap_sibling(gs, True)
    ps = [_pair_sum(g.reshape(N_CHIPS, 2, g.shape[1] // 2, g.shape[2]), r, c_idx) for g, r in zip(gs, r1)]
    r2 = _exchange_chips(ps)
    r2_sib = _swap_sibling(r2, False)
    full = [_chip_sum(theirs, 1 - c_idx, _chip_sum(mine, c_idx)) for mine, theirs in zip(r2, r2_sib)]
    red = [f.reshape(2 * f.shape[1], f.shape[2]) for f in full]
    pj_r = red[3].reshape(4, D // N_CHIPS, D)
    grads = {"w_in": red[0], "conv_w": red[1][:CW], "w_kv_mem": red[2], "w_gate_up": red[4], "w_down": red[5]}
    for i, n in enumerate(PROJ_NAMES):
        grads[n] = pj_r[i]

    tot = _allreduce_small(small)
    loss = tot[9, 0]
    for i, n in enumerate(SMALL_NAMES):
        grads[n] = tot[i:i + 1]
    grads["b_forget"] = tot[8:9, :HF]

    delta, new_m, new_v = {}, {}, {}
    ds, ms, vs = _adamw(_pack_small(params), tot.at[9:].set(0.0), _pack_small(moms), _pack_small(vels))
    for i, n in enumerate(SMALL_NAMES):
        delta[n], new_m[n], new_v[n] = ds[i:i + 1], ms[i:i + 1], vs[i:i + 1]
    delta["b_forget"], new_m["b_forget"], new_v["b_forget"] = ds[8:9, :HF], ms[8:9, :HF], vs[8:9, :HF]
    for n in ["w_in", "conv_w", "w_kv_mem", "w_gate_up", "w_down"] + PROJ_NAMES:
        delta[n], new_m[n], new_v[n] = _adamw(params[n], grads[n], moms[n], vels[n])
    return loss, dx, grads, delta, new_m, new_v


def kernel(x, mem, norm_mix_pre, norm_mix_post, norm_mem, w_in, b_forget, conv_w, conv_b, conv_ln_g, conv_ln_b, w_kv_mem, w_proj_attn, w_proj_conv, w_proj_mem, w_out, norm_ffn_pre, norm_ffn_post, w_gate_up, w_down, loss_target, m_norm_mix_pre, m_norm_mix_post, m_norm_mem, m_w_in, m_b_forget, m_conv_w, m_conv_b, m_conv_ln_g, m_conv_ln_b, m_w_kv_mem, m_w_proj_attn, m_w_proj_conv, m_w_proj_mem, m_w_out, m_norm_ffn_pre, m_norm_ffn_post, m_w_gate_up, m_w_down, v_norm_mix_pre, v_norm_mix_post, v_norm_mem, v_w_in, v_b_forget, v_conv_w, v_conv_b, v_conv_ln_g, v_conv_ln_b, v_w_kv_mem, v_w_proj_attn, v_w_proj_conv, v_w_proj_mem, v_w_out, v_norm_ffn_pre, v_norm_ffn_post, v_w_gate_up, v_w_down):
    local = dict(locals())
    lead = {n: local[n].shape[:-2] for n in WEIGHT_ORDER}
    two_d = lambda a: a.reshape(a.shape[-2:])
    params = {n: two_d(local[n]) for n in WEIGHT_ORDER}
    moms = {n: two_d(local["m_" + n]) for n in WEIGHT_ORDER}
    vels = {n: two_d(local["v_" + n]) for n in WEIGHT_ORDER}
    loss, dx, grads, delta, new_m, new_v = _step(params, moms, vels, two_d(x), two_d(mem), two_d(loss_target))
    outs = [loss, dx.reshape(x.shape)]
    for group in (grads, delta, new_m, new_v):
        outs += [group[n].reshape(lead[n] + group[n].shape) for n in WEIGHT_ORDER]
    return tuple(outs)
```

```python
import functools

import jax
import jax.numpy as jnp
from jax import lax
from jax.experimental import pallas as pl
from jax.experimental.pallas import tpu as pltpu

F32 = jnp.float32
MXU = jnp.bfloat16
WIRE = jnp.bfloat16

D = 1024
HF = 16
DH = 64
NP = D // 128
MEM_H = 4
MEM_DH = D // MEM_H
FFN = 2816
CW = 31
CWP = 32
HALO = 32
RMS_EPS = 1e-6
LN_EPS = 1e-5
LR, B1, B2, ADAM_EPS, WD, STEP = 0.001, 0.9, 0.999, 1e-8, 0.01, 10

N_CHIPS = 4
N_DEV = 8
LANES = 128
VMEM_LIMIT = 56 * 1024 * 1024

TM_PROJ = 512
NB_PROJ = 3
TQ = 1024
LOG2E = 1.4426950408889634
LN2 = 0.6931471805599453
QSCALE = DH ** -0.5 * LOG2E
X_BIAS = 0
X_QONE = 6
X_KONE = 8
X_ROWS = 16
TM_CONV = 256
CONV_ROWS = 128
SUB = 8
TM_ROW = 256
TS_WG = 1024
WG_CAP = 1408
SMALL_ROWS = 16


def _pc(body, **kw):
    return pl.pallas_call(body, **kw)


def _cp(n_axes):
    return pltpu.CompilerParams(dimension_semantics=("arbitrary",) * n_axes, vmem_limit_bytes=VMEM_LIMIT)


def _sds(shape, dtype):
    return jax.ShapeDtypeStruct(shape, dtype)


def _dot(a, b):
    return jnp.dot(a, b, preferred_element_type=F32)


def _dot_nt(a, b):
    return lax.dot_general(a, b, (((1,), (1,)), ((), ())), preferred_element_type=F32)


def _dot_tn(a, b):
    return lax.dot_general(a, b, (((0,), (0,)), ((), ())), preferred_element_type=F32)


def _rms(u):
    return lax.rsqrt(jnp.mean(u * u, axis=-1, keepdims=True) + RMS_EPS)


def _rms_bwd(u, r, g, dn):
    w = dn * g
    return r * w - u * (r * r * r) * jnp.mean(u * w, axis=-1, keepdims=True)


def _sigmoid(z):
    return 1.0 / (1.0 + jnp.exp(-z))


def _tile(n, cap):
    if n <= cap:
        return n
    best = None
    for t in range(LANES, cap + 1, LANES):
        if n % t == 0:
            best = t
    assert best is not None, (n, cap)
    return best


def _rowtile(rows, cols, cap_bytes=1 << 20):
    best = None
    for t in range(8, rows + 1, 8):
        if rows % t == 0 and t * cols * 4 <= cap_bytes:
            best = t
    return best if best is not None else rows


def _split3(v):
    hi = v.astype(jnp.bfloat16)
    r1 = v - hi.astype(F32)
    mid = r1.astype(jnp.bfloat16)
    lo = (r1 - mid.astype(F32)).astype(jnp.bfloat16)
    return hi, mid, lo


def _dot_exact_rhs(a01, v):
    hi, mid, lo = _split3(v)
    return _dot(a01, hi) + _dot(a01, mid) + _dot(a01, lo)


def _in_proj(x, g_pre, w_main, w_f):
    s, d = x.shape
    n = w_main.shape[1]
    tm = min(TM_PROJ, s)
    tn = n // NB_PROJ

    def body(x_ref, g_ref, w_ref, wf_ref, proj_ref, h_ref, flog_ref, hs):
        @pl.when(pl.program_id(1) == 0)
        def _():
            xv = x_ref[...]
            h = (xv * _rms(xv) * g_ref[...]).astype(MXU)
            hs[...] = h
            h_ref[...] = h
            flog_ref[...] = _dot(h, wf_ref[...])

        res = _dot(hs[...], w_ref[...])

        @pl.when(pl.program_id(1) == 0)
        def _():
            proj_ref[:, pl.ds(0, d)] = (res[:, :d] * QSCALE).astype(MXU)
            proj_ref[:, pl.ds(d, tn - d)] = res[:, d:].astype(MXU)

        @pl.when(pl.program_id(1) != 0)
        def _():
            proj_ref[...] = res.astype(MXU)

    assert tn >= d
    return _pc(
        body, name="in_proj", grid=(s // tm, NB_PROJ),
        in_specs=[pl.BlockSpec((tm, d), lambda i, j: (i, 0)), pl.BlockSpec((1, d), lambda i, j: (0, 0)),
                  pl.BlockSpec((d, tn), lambda i, j: (0, j)), pl.BlockSpec((d, LANES), lambda i, j: (0, 0))],
        out_specs=[pl.BlockSpec((tm, tn), lambda i, j: (i, j)), pl.BlockSpec((tm, d), lambda i, j: (i, 0)),
                   pl.BlockSpec((tm, LANES), lambda i, j: (i, 0))],
        out_shape=[_sds((s, n), MXU), _sds((s, d), MXU), _sds((s, LANES), F32)],
        scratch_shapes=[pltpu.VMEM((tm, d), MXU)], compiler_params=_cp(2),
    )(x, g_pre, w_main, w_f)


def _log_sigmoid(z):
    e = jnp.exp(-jnp.abs(z))
    log1p_e = jnp.where(e < 1e-3, e * (1.0 - 0.5 * e), jnp.log(1.0 + e))
    return jnp.minimum(z, 0.0) - log1p_e


def _logf_cumsum(flog, b_f):
    s = flog.shape[0]
    ch = LANES

    def body(f_ref, b_ref, c_ref):
        r = lax.broadcasted_iota(jnp.int32, (ch, ch), 0)
        q = lax.broadcasted_iota(jnp.int32, (ch, ch), 1)
        tri = jnp.where(r >= q, 1.0, 0.0).astype(jnp.bfloat16)

        def step(i, carry):
            rows = pl.ds(pl.multiple_of(i * ch, ch), ch)
            lf = _log_sigmoid(f_ref[rows, :] + b_ref[...])
            c_ref[rows, :] = _dot_exact_rhs(tri, lf) + carry
            return carry + jnp.sum(lf, axis=0, keepdims=True)

        lax.fori_loop(0, s // ch, step, jnp.zeros((1, LANES), F32))

    return _pc(body, name="logf_cumsum", out_shape=_sds((s, LANES), F32),
               compiler_params=pltpu.CompilerParams(vmem_limit_bytes=VMEM_LIMIT))(flog, b_f)


def _logf_cumsum_bwd(flog, b_f, dc):
    s = flog.shape[0]
    ch = LANES

    def body(f_ref, b_ref, dc_ref, df_ref, db_ref):
        r = lax.broadcasted_iota(jnp.int32, (ch, ch), 0)
        q = lax.broadcasted_iota(jnp.int32, (ch, ch), 1)
        tri = jnp.where(r <= q, 1.0, 0.0).astype(jnp.bfloat16)
        nch = s // ch

        def step(t, carry):
            tail, dbsum = carry
            i = nch - 1 - t
            rows = pl.ds(pl.multiple_of(i * ch, ch), ch)
            dcv = dc_ref[rows, :]
            dlf = _dot_exact_rhs(tri, dcv) + tail
            z = f_ref[rows, :] + b_ref[...]
            df = dlf * _sigmoid(-z)
            df_ref[rows, :] = df.astype(MXU)
            return tail + jnp.sum(dcv, axis=0, keepdims=True), dbsum + jnp.sum(df, axis=0, keepdims=True)

        zero = jnp.zeros((1, LANES), F32)
        _, dbsum = lax.fori_loop(0, nch, step, (zero, zero))
        db_ref[...] = jnp.broadcast_to(dbsum, db_ref.shape)

    return _pc(body, name="logf_cumsum_bwd", out_shape=[_sds((s, LANES), MXU), _sds((8, LANES), F32)],
               compiler_params=pltpu.CompilerParams(vmem_limit_bytes=VMEM_LIMIT))(flog, b_f, dc)


def _head_masks(rows):
    lane = lax.broadcasted_iota(jnp.int32, (rows, LANES), 1)
    return lane < DH, lane >= DH


def _ext_masks(rows, key_side):
    lane = lax.broadcasted_iota(jnp.int32, (rows, 2 * LANES), 1)
    ext = lane - LANES
    out = []
    for a in range(2):
        head = (lane >= a * DH) & (lane < (a + 1) * DH)
        bias = (ext >= X_BIAS + 3 * a) & (ext < X_BIAS + 3 * a + 3)
        one = ext == (X_KONE if key_side else X_QONE) + a
        out.append(head | bias | one)
    return out


def _fox_prep(proj, ccol):
    s = proj.shape[0]
    tm = min(TM_PROJ, s)

    def body(q_ref, k_ref, c_ref, qx_ref, kx_ref):
        lane = lax.broadcasted_iota(jnp.int32, (tm, LANES), 1)
        qx_ref[:, pl.ds(0, LANES)] = q_ref[...]
        qx_ref[:, pl.ds(LANES, LANES)] = jnp.where(lane < X_QONE + 2, 1.0, 0.0).astype(MXU)
        kext = jnp.where((lane >= X_KONE) & (lane < X_KONE + 2), 1.0, 0.0).astype(jnp.bfloat16)
        for a in range(2):
            terms = _split3(c_ref[0, :, a:a + 1] * (-LOG2E))
            for t, term in enumerate(terms):
                kext = jnp.where(lane == X_BIAS + 3 * a + t, term, kext)
        kx_ref[:, pl.ds(0, LANES)] = k_ref[...]
        kx_ref[:, pl.ds(LANES, LANES)] = kext.astype(MXU)

    wide = pl.BlockSpec((tm, 2 * LANES), lambda p, i: (i, p))
    return _pc(
        body, name="fox_prep", grid=(NP, s // tm),
        in_specs=[pl.BlockSpec((tm, LANES), lambda p, i: (i, p)), pl.BlockSpec((tm, LANES), lambda p, i: (i, NP + p)),
                  pl.BlockSpec((1, tm, 2), lambda p, i: (p, i, 0))],
        out_specs=[wide, wide], out_shape=[_sds((s, NP * 2 * LANES), MXU)] * 2, compiler_params=_cp(2),
    )(proj, proj, ccol)


def _fox_fwd(proj, qx, kx):
    s = proj.shape[0]
    tq = min(TQ, s)
    nq = s // tq

    def body(q_ref, k_ref, v_ref, o_ref, lse_ref):
        i = pl.program_id(1)
        qv = q_ref[...]
        qmask = _ext_masks(tq, False)
        hmask = _head_masks(tq)
        qas = [jnp.where(qmask[a], qv, jnp.zeros_like(qv)) for a in range(2)]
        row = lax.broadcasted_iota(jnp.int32, (tq, tq), 0)
        col = lax.broadcasted_iota(jnp.int32, (tq, tq), 1)

        def blk(j, carry, diag=False):
            rows = pl.ds(pl.multiple_of(j * tq, tq), tq)
            kj = k_ref[rows, :]
            vj = v_ref[rows, :]
            out = []
            for a in range(2):
                m, acc = carry[a]
                sc = _dot_nt(qas[a], kj)
                if diag:
                    sc = jnp.where(row >= col, sc, -jnp.inf)
                m_new = jnp.maximum(m, jnp.max(sc, axis=-1, keepdims=True))
                p = jnp.exp2(sc - m_new)
                va = jnp.where(hmask[a], vj, jnp.ones_like(vj))
                out.append((m_new, jnp.exp2(m - m_new) * acc + _dot(p.astype(MXU), va)))
            return tuple(out)

        init = (jnp.full((tq, 1), -jnp.inf, F32), jnp.zeros((tq, LANES), F32))
        res = blk(i, lax.fori_loop(0, i, blk, (init, init)), True)
        lane = lax.broadcasted_iota(jnp.int32, (tq, LANES), 1)
        outs, lses = [], []
        for a in range(2):
            m, acc = res[a]
            l = jnp.sum(jnp.where(lane == DH * (1 - a), acc, 0.0), axis=-1, keepdims=True)
            outs.append(acc / l)
            lses.append(m + jnp.log(l) * LOG2E)
        o_ref[...] = jnp.where(hmask[0], outs[0], outs[1]).astype(MXU)
        lane2 = lax.broadcasted_iota(jnp.int32, (tq, 2), 1)
        lse_ref[0] = jnp.where(lane2 == 0, lses[0], lses[1])

    return _pc(
        body, name="fox_fwd", grid=(NP, nq),
        in_specs=[pl.BlockSpec((tq, 2 * LANES), lambda p, i: (i, p)),
                  pl.BlockSpec((s, 2 * LANES), lambda p, i: (0, p)),
                  pl.BlockSpec((s, LANES), lambda p, i: (0, 2 * NP + p))],
        out_specs=[pl.BlockSpec((tq, LANES), lambda p, i: (i, p)), pl.BlockSpec((1, tq, 2), lambda p, i: (p, i, 0))],
        out_shape=[_sds((s, D), MXU), _sds((NP, s, 2), F32)],
        compiler_params=_cp(2),
    )(qx, kx, proj)


def _fox_bwd(proj, qx, kx, do, lse_row, delta_row):
    s = proj.shape[0]
    tq = min(TQ, s)
    nq = s // tq

    def body(k_ref, v_ref, q_ref, do_ref, lse_ref, dl_ref, dq_ref, dqs_ref, dk_ref, dks_ref, dv_ref, dq_acc):
        j = pl.program_id(1)

        @pl.when(j == 0)
        def _():
            dq_acc[...] = jnp.zeros_like(dq_acc)

        kv = k_ref[...]
        v2 = v_ref[...]
        kmask = _ext_masks(tq, True)
        qmask = _ext_masks(tq, False)
        hmask = _head_masks(tq)
        row = lax.broadcasted_iota(jnp.int32, (tq, tq), 0)
        col = lax.broadcasted_iota(jnp.int32, (tq, tq), 1)
        carry = (jnp.zeros((tq, 2 * LANES), F32), jnp.zeros((tq, LANES), F32))
        for a in range(2):
            ka = jnp.where(kmask[a], kv, jnp.zeros_like(kv))
            va = jnp.where(hmask[a], v2, jnp.zeros_like(v2))

            def blk(i, carry, diag, a=a, ka=ka, va=va):
                dk_a, dv_a = carry
                rows = pl.ds(pl.multiple_of(i * tq, tq), tq)
                qi = q_ref[rows, :]
                doi = do_ref[rows, :]
                qa = jnp.where(qmask[a], qi, jnp.zeros_like(qi))
                doa = jnp.where(hmask[a], doi, jnp.zeros_like(doi))
                st = _dot_nt(ka, qi)
                if diag:
                    st = jnp.where(col >= row, st, -jnp.inf)
                pt = jnp.exp2(st - lse_ref[0, i, a:a + 1, :])
                dv_a = dv_a + _dot(pt.astype(MXU), doa)
                dpt = _dot_nt(va, doi)
                dsb = (pt * (dpt - dl_ref[0, i, a:a + 1, :])).astype(MXU)
                dk_a = dk_a + _dot(dsb, qa)
                dq_acc[rows, :] += _dot_tn(dsb, ka)
                return dk_a, dv_a

            carry = blk(j, carry, True)
            carry = lax.fori_loop(j + 1, nq, functools.partial(blk, diag=False), carry)
        dk_acc, dv_acc = carry
        dk_ref[...] = (dk_acc[:, :LANES] * LN2).astype(MXU)
        dks_ref[0, 0] = dk_acc[:, LANES:].T[:X_ROWS, :]
        dv_ref[...] = dv_acc.astype(MXU)

        @pl.when(j == nq - 1)
        def _():
            dq_ref[...] = (dq_acc[:, pl.ds(0, LANES)] * DH ** -0.5).astype(MXU)
            for t in range(nq):
                dqs_ref[0, :, pl.ds(t * tq, tq)] = dq_acc[pl.ds(t * tq, tq), pl.ds(LANES, LANES)].T[:X_ROWS, :]

    stat = pl.BlockSpec((1, nq, 2, tq), lambda p, j: (p, 0, 0, 0))
    whole = pl.BlockSpec((s, LANES), lambda p, j: (0, p))
    tile = pl.BlockSpec((tq, LANES), lambda p, j: (j, p))
    return _pc(
        body, name="fox_bwd", grid=(NP, nq),
        in_specs=[pl.BlockSpec((tq, 2 * LANES), lambda p, j: (j, p)),
                  pl.BlockSpec((tq, LANES), lambda p, j: (j, 2 * NP + p)),
                  pl.BlockSpec((s, 2 * LANES), lambda p, j: (0, p)),
                  whole, stat, stat],
        out_specs=[whole, pl.BlockSpec((1, X_ROWS, s), lambda p, j: (p, 0, 0)), tile,
                   pl.BlockSpec((1, 1, X_ROWS, tq), lambda p, j: (p, j, 0, 0)), tile],
        out_shape=[_sds((s, D), MXU), _sds((NP, X_ROWS, s), F32), _sds((s, D), MXU),
                   _sds((NP, nq, X_ROWS, tq), F32), _sds((s, D), MXU)],
        scratch_shapes=[pltpu.VMEM((s, 2 * LANES), F32)], compiler_params=_cp(2),
    )(kx, proj, qx, do, lse_row, delta_row)


def _glu(a, gate):
    return a.astype(F32) * _sigmoid(gate.astype(F32))


def _store_blocked(buf, row0, val):
    for c in range(D // LANES):
        buf[0, c, pl.ds(row0, val.shape[0]), :] = val[:, c * LANES:(c + 1) * LANES]


def _fill_shifted(buf):
    n = buf.shape[2] - SUB
    for r in range(1, SUB):
        buf[r, :, pl.ds(0, n), :] = buf[0, :, pl.ds(r, n), :]


def _shifted(buf, off, rows, c):
    r = off % SUB
    return buf[r, c, pl.ds(off - r, rows), :]


def _conv_fwd(proj, cw, cb, lg, lb):
    s = proj.shape[0]
    tm = min(TM_CONV, s)
    hb = tm // HALO

    rcw = min(CONV_ROWS, tm)

    def body(a_ref, g_ref, ah_ref, gh_ref, w_ref, cb_ref, lg_ref, lb_ref, y_ref, o_ref, gsh):
        i = pl.program_id(0)
        _store_blocked(gsh, 0, jnp.where(i > 0, _glu(ah_ref[...], gh_ref[...]), 0.0))
        _store_blocked(gsh, HALO, _glu(a_ref[...], g_ref[...]))
        _fill_shifted(gsh)
        for c in range(D // LANES):
            cols = pl.ds(c * LANES, LANES)
            for rc in range(tm // rcw):
                acc = jnp.broadcast_to(cb_ref[:, cols], (rcw, LANES))
                for t in range(CW):
                    acc = acc + w_ref[t:t + 1, cols] * _shifted(gsh, HALO - (CW - 1) + t + rc * rcw, rcw, c)
                y_ref[pl.ds(rc * rcw, rcw), cols] = acc
        acc = y_ref[...]
        mu = jnp.mean(acc, axis=-1, keepdims=True)
        xc = acc - mu
        r = lax.rsqrt(jnp.mean(xc * xc, axis=-1, keepdims=True) + LN_EPS)
        nrm = xc * r * lg_ref[...] + lb_ref[...]
        o_ref[...] = (nrm * _sigmoid(nrm)).astype(MXU)

    vec = pl.BlockSpec((1, D), lambda i: (0, 0))
    return _pc(
        body, name="conv_fwd", grid=(s // tm,),
        in_specs=[pl.BlockSpec((tm, D), lambda i: (i, 3)), pl.BlockSpec((tm, D), lambda i: (i, 4)),
                  pl.BlockSpec((HALO, D), lambda i: (jnp.maximum(i * hb - 1, 0), 3)),
                  pl.BlockSpec((HALO, D), lambda i: (jnp.maximum(i * hb - 1, 0), 4)),
                  pl.BlockSpec((CWP, D), lambda i: (0, 0)), vec, vec, vec],
        out_specs=[pl.BlockSpec((tm, D), lambda i: (i, 0)), pl.BlockSpec((tm, D), lambda i: (i, 0))],
        out_shape=[_sds((s, D), F32), _sds((s, D), MXU)],
        scratch_shapes=[pltpu.VMEM((SUB, D // LANES, tm + HALO, LANES), F32)], compiler_params=_cp(1),
    )(proj, proj, proj, proj, cw, cb, lg, lb)


def _conv_bwd(proj, y, do, cw, lg, lb):
    s = proj.shape[0]
    tm = min(TM_CONV, s)
    hb = tm // HALO
    nt = s // tm
    last_hblk = s // HALO - 1

    def ln_bwd(yv, dov, lgv, lbv):
        mu = jnp.mean(yv, axis=-1, keepdims=True)
        xc = yv - mu
        r = lax.rsqrt(jnp.mean(xc * xc, axis=-1, keepdims=True) + LN_EPS)
        xh = xc * r
        nrm = xh * lgv + lbv
        sg = _sigmoid(nrm)
        dn = dov.astype(F32) * (sg * (1.0 + nrm * (1.0 - sg)))
        wv = dn * lgv
        dy = r * (wv - jnp.mean(wv, axis=-1, keepdims=True) - xh * jnp.mean(wv * xh, axis=-1, keepdims=True))
        return dy, dn, xh

    rcw = min(CONV_ROWS, tm)

    def body(a_ref, g_ref, ah_ref, gh_ref, y_ref, yn_ref, do_ref, don_ref, w_ref, lg_ref, lb_ref,
             da_ref, dg_ref, dw_ref, vec_ref, gsh, dysh, dwacc):
        i = pl.program_id(0)

        @pl.when(i == 0)
        def _():
            dwacc[...] = jnp.zeros_like(dwacc)
            vec_ref[...] = jnp.zeros_like(vec_ref)

        lgv, lbv = lg_ref[...], lb_ref[...]
        _store_blocked(gsh, 0, jnp.where(i > 0, _glu(ah_ref[...], gh_ref[...]), 0.0))
        _store_blocked(gsh, HALO, _glu(a_ref[...], g_ref[...]))
        _fill_shifted(gsh)
        dy, dn, xh = ln_bwd(y_ref[...], do_ref[...], lgv, lbv)
        dyn, _, _ = ln_bwd(yn_ref[...], don_ref[...], lgv, lbv)
        _store_blocked(dysh, 0, dy)
        _store_blocked(dysh, tm, jnp.where(i < nt - 1, dyn, 0.0))
        _fill_shifted(dysh)
        vec_ref[0:1, :] += jnp.sum(dn * xh, axis=0, keepdims=True)
        vec_ref[1:2, :] += jnp.sum(dn, axis=0, keepdims=True)
        vec_ref[2:3, :] += jnp.sum(dy, axis=0, keepdims=True)
        for c in range(D // LANES):
            cols = pl.ds(c * LANES, LANES)
            for rc in range(tm // rcw):
                rows = pl.ds(rc * rcw, rcw)
                dyc = dysh[0, c, rows, :]
                dgl = jnp.zeros((rcw, LANES), F32)
                for t in range(CW):
                    dgl = dgl + w_ref[t:t + 1, cols] * _shifted(dysh, CW - 1 - t + rc * rcw, rcw, c)
                    prod = dyc * _shifted(gsh, HALO - (CW - 1) + t + rc * rcw, rcw, c)
                    dwacc[t, :, cols] += jnp.sum(prod.reshape(rcw // SUB, SUB, LANES), axis=0)
                av = a_ref[rows, cols].astype(F32)
                sgate = _sigmoid(g_ref[rows, cols].astype(F32))
                da_ref[rows, cols] = (dgl * sgate).astype(MXU)
                dg_ref[rows, cols] = (dgl * av * sgate * (1.0 - sgate)).astype(MXU)

        @pl.when(i == nt - 1)
        def _():
            dw_ref[...] = jnp.sum(dwacc[...], axis=1)

    vec = pl.BlockSpec((1, D), lambda i: (0, 0))
    cur = lambda c: pl.BlockSpec((tm, D), lambda i: (i, c))
    prv = lambda c: pl.BlockSpec((HALO, D), lambda i: (jnp.maximum(i * hb - 1, 0), c))
    nxt = pl.BlockSpec((HALO, D), lambda i: (jnp.minimum((i + 1) * hb, last_hblk), 0))
    return _pc(
        body, name="conv_bwd", grid=(nt,),
        in_specs=[cur(3), cur(4), prv(3), prv(4), cur(0), nxt, cur(0), nxt,
                  pl.BlockSpec((CWP, D), lambda i: (0, 0)), vec, vec],
        out_specs=[cur(0), cur(0), pl.BlockSpec((CWP, D), lambda i: (0, 0)), pl.BlockSpec((8, D), lambda i: (0, 0))],
        out_shape=[_sds((s, D), MXU), _sds((s, D), MXU), _sds((CWP, D), F32), _sds((8, D), F32)],
        scratch_shapes=[pltpu.VMEM((SUB, D // LANES, tm + HALO, LANES), F32)] * 2 + [pltpu.VMEM((CWP, SUB, D), F32)],
        compiler_params=_cp(1),
    )(proj, proj, proj, proj, y, y, do, do, cw, lg, lb)


def _mem_kv(mem, g_mem, w_kv):
    mm = mem.shape[0]

    def body(m_ref, g_ref, w_ref, mn_ref, kv_ref):
        mv = m_ref[...]
        mn = (mv * _rms(mv) * g_ref[...]).astype(MXU)
        mn_ref[...] = mn
        kv_ref[...] = _dot(mn, w_ref[...]).astype(MXU)

    return _pc(body, name="mem_kv", out_shape=[_sds((mm, D), MXU), _sds((mm, 2 * D), MXU)],
               compiler_params=pltpu.CompilerParams(vmem_limit_bytes=VMEM_LIMIT))(mem, g_mem, w_kv)


def _mem_kv_bwd(mem, g_mem, w_kv, dkv):
    mm = mem.shape[0]

    def body(m_ref, w_ref, dkv_ref, o_ref):
        mv = m_ref[...]
        dmn = _dot_nt(dkv_ref[...], w_ref[...])
        o_ref[...] = jnp.broadcast_to(jnp.sum(dmn * mv * _rms(mv), axis=0, keepdims=True), o_ref.shape)

    return _pc(body, name="mem_kv_bwd", out_shape=_sds((8, D), F32),
               compiler_params=pltpu.CompilerParams(vmem_limit_bytes=VMEM_LIMIT))(mem, w_kv, dkv)


def _mem_attn_fwd(proj, kv):
    s = proj.shape[0]
    mm = kv.shape[0]
    tm = min(TM_ROW, s)
    scale = MEM_DH ** -0.5

    def body(q_ref, kv_ref, o_ref):
        for h in range(MEM_H):
            cols = pl.ds(h * MEM_DH, MEM_DH)
            qh = q_ref[:, cols] * scale
            sc = _dot_nt(qh, kv_ref[:, cols])
            m = jnp.max(sc, axis=-1, keepdims=True)
            e = jnp.exp(sc - m)
            p = e / jnp.sum(e, axis=-1, keepdims=True)
            o_ref[:, cols] = _dot(p.astype(MXU), kv_ref[:, pl.ds(D + h * MEM_DH, MEM_DH)]).astype(MXU)

    return _pc(
        body, name="mem_attn_fwd", grid=(s // tm,),
        in_specs=[pl.BlockSpec((tm, D), lambda i: (i, 5)), pl.BlockSpec((mm, 2 * D), lambda i: (0, 0))],
        out_specs=pl.BlockSpec((tm, D), lambda i: (i, 0)), out_shape=_sds((s, D), MXU), compiler_params=_cp(1),
    )(proj, kv)


def _mem_attn_bwd(proj, kv, do):
    s = proj.shape[0]
    mm = kv.shape[0]
    tm = min(TM_ROW, s)
    scale = MEM_DH ** -0.5

    def body(q_ref, kv_ref, do_ref, dq_ref, dkv_ref):
        @pl.when(pl.program_id(0) == 0)
        def _():
            dkv_ref[...] = jnp.zeros_like(dkv_ref)

        for h in range(MEM_H):
            cols = pl.ds(h * MEM_DH, MEM_DH)
            vcols = pl.ds(D + h * MEM_DH, MEM_DH)
            qh = q_ref[:, cols]
            kh = kv_ref[:, cols] * scale
            doh = do_ref[:, cols]
            st = _dot_nt(kh, qh)
            m = jnp.max(st, axis=0, keepdims=True)
            e = jnp.exp(st - m)
            pt = e / jnp.sum(e, axis=0, keepdims=True)
            dpt = _dot_nt(kv_ref[:, vcols], doh)
            dst = pt * (dpt - jnp.sum(pt * dpt, axis=0, keepdims=True))
            dsb = dst.astype(MXU)
            dkv_ref[:, vcols] += _dot(pt.astype(MXU), doh)
            dkv_ref[:, cols] += _dot(dsb, qh) * scale
            dq_ref[:, cols] = _dot_tn(dsb, kh).astype(MXU)

    return _pc(
        body, name="mem_attn_bwd", grid=(s // tm,),
        in_specs=[pl.BlockSpec((tm, D), lambda i: (i, 5)), pl.BlockSpec((mm, 2 * D), lambda i: (0, 0)),
                  pl.BlockSpec((tm, D), lambda i: (i, 0))],
        out_specs=[pl.BlockSpec((tm, D), lambda i: (i, 0)), pl.BlockSpec((mm, 2 * D), lambda i: (0, 0))],
        out_shape=[_sds((s, D), MXU), _sds((mm, 2 * D), F32)], compiler_params=_cp(1),
    )(proj, kv, do)


def _resident(n):
    return [pltpu.VMEM((n, D, D), MXU), pltpu.SemaphoreType.DMA((n,))]


def _load_resident(hbm_refs, wbuf, sems):
    @pl.when(pl.program_id(0) == 0)
    def _():
        cps = [pltpu.make_async_copy(r, wbuf.at[k], sems.at[k]) for k, r in enumerate(hbm_refs)]
        for cp in cps:
            cp.start()
        for cp in cps:
            cp.wait()


def _merge_out(oa, oc, om, proj, x, wpa, wpc, wpm, wout, g_post, g_fpre):
    s = x.shape[0]
    tm = min(TM_ROW, s)

    def body(oa_ref, oc_ref, om_ref, gl_ref, x_ref, gp_ref, gf_ref, wpa_h, wpc_h, wpm_h, wout_h,
             pa_ref, pc_ref, pm_ref, mg_ref, z_ref, x1_ref, h2_ref, wbuf, sems):
        _load_resident([wpa_h, wpc_h, wpm_h, wout_h], wbuf, sems)
        merged = jnp.zeros((tm, D), F32)
        for b, (o_ref, p_ref) in enumerate(((oa_ref, pa_ref), (oc_ref, pc_ref), (om_ref, pm_ref))):
            pb = _dot(o_ref[...], wbuf[b])
            p_ref[...] = pb.astype(MXU)
            merged = merged + _sigmoid(gl_ref[:, pl.ds(b * D, D)].astype(F32)) * pb
        mg = merged.astype(MXU)
        mg_ref[...] = mg
        z = _dot(mg, wbuf[3])
        z_ref[...] = z
        x1 = x_ref[...] + z * _rms(z) * gp_ref[...]
        x1_ref[...] = x1
        h2_ref[...] = (x1 * _rms(x1) * gf_ref[...]).astype(MXU)

    rows = pl.BlockSpec((tm, D), lambda i: (i, 0))
    vec = pl.BlockSpec((1, D), lambda i: (0, 0))
    anyspec = pl.BlockSpec(memory_space=pl.ANY)
    return _pc(
        body, name="merge_out", grid=(s // tm,),
        in_specs=[rows, rows, rows, pl.BlockSpec((tm, 3 * D), lambda i: (i, 2)), rows, vec, vec,
                  anyspec, anyspec, anyspec, anyspec],
        out_specs=[rows] * 7,
        out_shape=[_sds((s, D), MXU)] * 4 + [_sds((s, D), F32)] * 2 + [_sds((s, D), MXU)],
        scratch_shapes=_resident(4), compiler_params=_cp(1),
    )(oa, oc, om, proj, x, g_post, g_fpre, wpa, wpc, wpm, wout)


def _ffn_up(h2, w_gu):
    s = h2.shape[0]
    tm = min(TM_PROJ, s)
    nb = 2
    bw = FFN // nb

    def body(h_ref, wg_ref, wu_ref, gf_ref, uf_ref, act_ref):
        hv = h_ref[...]
        gf = _dot(hv, wg_ref[...])
        uf = _dot(hv, wu_ref[...])
        gf_ref[...] = gf.astype(MXU)
        uf_ref[...] = uf.astype(MXU)
        act_ref[...] = (gf * _sigmoid(gf) * uf).astype(MXU)

    out = pl.BlockSpec((tm, bw), lambda i, j: (i, j))
    return _pc(
        body, name="ffn_up", grid=(s // tm, nb),
        in_specs=[pl.BlockSpec((tm, D), lambda i, j: (i, 0)), pl.BlockSpec((D, bw), lambda i, j: (0, j)),
                  pl.BlockSpec((D, bw), lambda i, j: (0, nb + j))],
        out_specs=[out, out, out], out_shape=[_sds((s, FFN), MXU)] * 3, compiler_params=_cp(2),
    )(h2, w_gu, w_gu)


def _ffn_down_loss(act, w_d, x1, tgt, g_fpost):
    s = act.shape[0]
    tm = min(TM_ROW, s)

    def body(a_ref, w_ref, x1_ref, t_ref, g_ref, dffn_ref, dy_ref, vec_ref, loss_ref):
        @pl.when(pl.program_id(0) == 0)
        def _():
            vec_ref[...] = jnp.zeros_like(vec_ref)
            loss_ref[...] = jnp.zeros_like(loss_ref)

        ffn = _dot(a_ref[...], w_ref[...])
        r = _rms(ffn)
        gv = g_ref[...]
        e = x1_ref[...] + ffn * r * gv - t_ref[...]
        loss_ref[...] += jnp.sum(e * e) * (0.5 / D)
        dy = e * (1.0 / D)
        dy_ref[...] = dy
        vec_ref[0:1, :] += jnp.sum(dy * ffn * r, axis=0, keepdims=True)
        dffn_ref[...] = _rms_bwd(ffn, r, gv, dy).astype(MXU)

    rows = pl.BlockSpec((tm, D), lambda i: (i, 0))
    return _pc(
        body, name="ffn_down_loss", grid=(s // tm,),
        in_specs=[pl.BlockSpec((tm, FFN), lambda i: (i, 0)), pl.BlockSpec((FFN, D), lambda i: (0, 0)), rows, rows,
                  pl.BlockSpec((1, D), lambda i: (0, 0))],
        out_specs=[rows, rows, pl.BlockSpec((8, D), lambda i: (0, 0)), pl.BlockSpec((8, LANES), lambda i: (0, 0))],
        out_shape=[_sds((s, D), MXU), _sds((s, D), F32), _sds((8, D), F32), _sds((8, LANES), F32)],
        compiler_params=_cp(1),
    )(act, w_d, x1, tgt, g_fpost)


def _ffn_down_bwd(dffn, w_d, gf, uf):
    s = dffn.shape[0]
    tm = min(TM_ROW, s)

    def body(d_ref, w_ref, gf_ref, uf_ref, dgf_ref, duf_ref):
        da = _dot_nt(d_ref[...], w_ref[...])
        gf = gf_ref[...].astype(F32)
        sg = _sigmoid(gf)
        duf_ref[...] = (da * gf * sg).astype(MXU)
        dgf_ref[...] = (da * uf_ref[...].astype(F32) * (sg * (1.0 + gf * (1.0 - sg)))).astype(MXU)

    wide = pl.BlockSpec((tm, FFN), lambda i: (i, 0))
    return _pc(
        body, name="ffn_down_bwd", grid=(s // tm,),
        in_specs=[pl.BlockSpec((tm, D), lambda i: (i, 0)), pl.BlockSpec((FFN, D), lambda i: (0, 0)), wide, wide],
        out_specs=[wide, wide], out_shape=[_sds((s, FFN), MXU)] * 2, compiler_params=_cp(1),
    )(dffn, w_d, gf, uf)


def _ffn_up_bwd(dgf, duf, w_gu, x1, dy, z, g_fpre, g_post):
    s = x1.shape[0]
    tm = min(TM_ROW, s)

    def body(dgf_ref, duf_ref, w_ref, x1_ref, dy_ref, z_ref, gf_ref, gp_ref, dx1_ref, dz_ref, vec_ref):
        @pl.when(pl.program_id(0) == 0)
        def _():
            vec_ref[...] = jnp.zeros_like(vec_ref)

        dh2 = _dot_nt(dgf_ref[...], w_ref[:, pl.ds(0, FFN)]) + _dot_nt(duf_ref[...], w_ref[:, pl.ds(FFN, FFN)])
        x1 = x1_ref[...]
        r2 = _rms(x1)
        vec_ref[0:1, :] += jnp.sum(dh2 * x1 * r2, axis=0, keepdims=True)
        dx1 = dy_ref[...] + _rms_bwd(x1, r2, gf_ref[...], dh2)
        dx1_ref[...] = dx1
        z = z_ref[...]
        rz = _rms(z)
        vec_ref[1:2, :] += jnp.sum(dx1 * z * rz, axis=0, keepdims=True)
        dz_ref[...] = _rms_bwd(z, rz, gp_ref[...], dx1).astype(MXU)

    rows = pl.BlockSpec((tm, D), lambda i: (i, 0))
    wide = pl.BlockSpec((tm, FFN), lambda i: (i, 0))
    vec = pl.BlockSpec((1, D), lambda i: (0, 0))
    return _pc(
        body, name="ffn_up_bwd", grid=(s // tm,),
        in_specs=[wide, wide, pl.BlockSpec((D, 2 * FFN), lambda i: (0, 0)), rows, rows, rows, vec, vec],
        out_specs=[rows, rows, pl.BlockSpec((8, D), lambda i: (0, 0))],
        out_shape=[_sds((s, D), F32), _sds((s, D), MXU), _sds((8, D), F32)], compiler_params=_cp(1),
    )(dgf, duf, w_gu, x1, dy, z, g_fpre, g_post)


def _merge_bwd(dz, proj, pa, pc, pm, oa, wpa, wpc, wpm, wout):
    s = dz.shape[0]
    tm = min(TM_ROW, s)

    def body(dz_ref, gl_ref, pa_ref, pc_ref, pm_ref, oa_ref, wpa_h, wpc_h, wpm_h, wout_h,
             dpa_ref, dpc_ref, dpm_ref, dgl_ref, doa_ref, doc_ref, dom_ref, dl_ref, wbuf, sems):
        _load_resident([wpa_h, wpc_h, wpm_h, wout_h], wbuf, sems)
        dm = _dot_nt(dz_ref[...], wbuf[3])
        quads = ((pa_ref, dpa_ref, doa_ref), (pc_ref, dpc_ref, doc_ref), (pm_ref, dpm_ref, dom_ref))
        for b, (p_ref, dp_ref, do_ref) in enumerate(quads):
            cols = pl.ds(b * D, D)
            gt = _sigmoid(gl_ref[:, cols].astype(F32))
            dp = (dm * gt).astype(MXU)
            dp_ref[...] = dp
            dgl_ref[:, cols] = (dm * p_ref[...].astype(F32) * gt * (1.0 - gt)).astype(MXU)
            dob = _dot_nt(dp, wbuf[b]).astype(MXU)
            do_ref[...] = dob
            if b == 0:
                prod = dob.astype(F32) * oa_ref[...].astype(F32)
                d_i = lax.broadcasted_iota(jnp.int32, (D, LANES), 0)
                h_i = lax.broadcasted_iota(jnp.int32, (D, LANES), 1)
                sel = jnp.where(lax.shift_right_logical(d_i, DH.bit_length() - 1) == h_i, 1.0, 0.0).astype(jnp.bfloat16)
                dl_ref[...] = _dot_exact_rhs_t(prod, sel)

    rows = pl.BlockSpec((tm, D), lambda i: (i, 0))
    anyspec = pl.BlockSpec(memory_space=pl.ANY)
    wide = pl.BlockSpec((tm, 3 * D), lambda i: (i, 2))
    return _pc(
        body, name="merge_bwd", grid=(s // tm,),
        in_specs=[rows, wide, rows, rows, rows, rows, anyspec, anyspec, anyspec, anyspec],
        out_specs=[rows, rows, rows, pl.BlockSpec((tm, 3 * D), lambda i: (i, 0)), rows, rows, rows,
                   pl.BlockSpec((tm, LANES), lambda i: (i, 0))],
        out_shape=[_sds((s, D), MXU)] * 3 + [_sds((s, 3 * D), MXU)] + [_sds((s, D), MXU)] * 3 + [_sds((s, LANES), F32)],
        scratch_shapes=_resident(4), compiler_params=_cp(1),
    )(dz, proj, pa, pc, pm, oa, wpa, wpc, wpm, wout)


def _dot_exact_rhs_t(v, b01):
    hi, mid, lo = _split3(v)
    return _dot(hi, b01) + _dot(mid, b01) + _dot(lo, b01)


def _in_proj_bwd(pieces, df, w_main, w_f, x, dx1, g_pre):
    s = x.shape[0]
    tm = min(TM_ROW, s)
    n_main = w_main.shape[1]
    np_ = len(pieces)

    def body(*refs):
        p_refs = refs[:np_]
        df_ref, x_ref, dx1_ref, g_ref, w_h, wf_ref, dx_ref, vec_ref, wbuf, sem = refs[np_:]

        @pl.when(pl.program_id(0) == 0)
        def _():
            vec_ref[...] = jnp.zeros_like(vec_ref)
            cp = pltpu.make_async_copy(w_h, wbuf, sem)
            cp.start()
            cp.wait()

        dh = _dot_nt(df_ref[...], wf_ref[...])
        for p_ref, (_, c0, nc) in zip(p_refs, pieces):
            dh = dh + _dot_nt(p_ref[...], wbuf[:, pl.ds(c0 * D, nc * D)])
        xv = x_ref[...]
        r = _rms(xv)
        vec_ref[0:1, :] += jnp.sum(dh * xv * r, axis=0, keepdims=True)
        dx_ref[...] = dx1_ref[...] + _rms_bwd(xv, r, g_ref[...], dh)

    rows = pl.BlockSpec((tm, D), lambda i: (i, 0))
    p_specs = [pl.BlockSpec((tm, nc * D), lambda i: (i, 0)) for _, _, nc in pieces]
    return _pc(
        body, name="in_proj_bwd", grid=(s // tm,),
        in_specs=p_specs + [pl.BlockSpec((tm, LANES), lambda i: (i, 0)), rows, rows, pl.BlockSpec((1, D), lambda i: (0, 0)),
                            pl.BlockSpec(memory_space=pl.ANY), pl.BlockSpec((D, LANES), lambda i: (0, 0))],
        out_specs=[rows, pl.BlockSpec((8, D), lambda i: (0, 0))],
        out_shape=[_sds((s, D), F32), _sds((8, D), F32)],
        scratch_shapes=[pltpu.VMEM((D, n_main), MXU), pltpu.SemaphoreType.DMA], compiler_params=_cp(1),
    )(*[p for p, _, _ in pieces], df, x, dx1, g_pre, w_main, w_f)


def _wgrad(xa, dy, name):
    s, k = xa.shape
    n = dy.shape[1]
    ts = min(TS_WG, s)
    tk = _tile(k, WG_CAP)
    tn = _tile(n, WG_CAP)

    def body(x_ref, dy_ref, o_ref):
        @pl.when(pl.program_id(2) == 0)
        def _():
            o_ref[...] = jnp.zeros_like(o_ref)

        o_ref[...] += _dot_tn(x_ref[...], dy_ref[...])

    return _pc(
        body, name=name, grid=(k // tk, n // tn, s // ts),
        in_specs=[pl.BlockSpec((ts, tk), lambda a, b, c: (c, a)), pl.BlockSpec((ts, tn), lambda a, b, c: (c, b))],
        out_specs=pl.BlockSpec((tk, tn), lambda a, b, c: (a, b)), out_shape=_sds((k, n), F32), compiler_params=_cp(3),
    )(xa, dy)


def _pair_sum(g, r1, c_idx):
    _, _, hr, cols = g.shape
    tr = _rowtile(hr, cols)

    def body(c_ref, g_ref, r_ref, o_ref):
        o_ref[0] = (g_ref[0, 0].astype(F32) + r_ref[0].astype(F32)).astype(WIRE)

    return _pc(
        body, name="pair_sum_%dx%d" % (hr, cols), out_shape=_sds((N_CHIPS, hr, cols), WIRE),
        grid_spec=pltpu.PrefetchScalarGridSpec(
            num_scalar_prefetch=1, grid=(N_CHIPS, hr // tr),
            in_specs=[pl.BlockSpec((1, 1, tr, cols), lambda d, i, c: (d, c[0], i, 0)),
                      pl.BlockSpec((1, tr, cols), lambda d, i, c: (d, i, 0))],
            out_specs=pl.BlockSpec((1, tr, cols), lambda d, i, c: (d, i, 0))),
        compiler_params=_cp(2),
    )(c_idx, g, r1)


def _chip_sum(r2, slot, base=None):
    _, hr, cols = r2.shape
    tr = _rowtile(hr, cols)

    def body(s_ref, r_ref, *rest):
        o_ref = rest[-1]
        acc = r_ref[0].astype(F32)
        for d in range(1, N_CHIPS):
            acc = acc + r_ref[d].astype(F32)
        o_ref[0] = acc

    based = base is not None
    return _pc(
        body, name="chip_sum_%dx%d_%d" % (hr, cols, int(based)), out_shape=_sds((2, hr, cols), F32),
        grid_spec=pltpu.PrefetchScalarGridSpec(
            num_scalar_prefetch=1, grid=(hr // tr,),
            in_specs=[pl.BlockSpec((N_CHIPS, tr, cols), lambda i, s: (0, i, 0))] + ([ANY] if based else []),
            out_specs=pl.BlockSpec((1, tr, cols), lambda i, s: (s[0], i, 0))),
        input_output_aliases={2: 0} if based else {}, compiler_params=_cp(1),
    )(*((slot, r2, base) if based else (slot, r2)))


def _adamw(w, g, m, v):
    rows, cols = w.shape
    tr = _rowtile(rows, cols, 1 << 19)
    c1 = 1.0 / (1.0 - B1 ** STEP)
    c2 = 1.0 / (1.0 - B2 ** STEP)

    def body(w_ref, g_ref, m_ref, v_ref, d_ref, mo_ref, vo_ref):
        gv = g_ref[...]
        mn = B1 * m_ref[...] + (1.0 - B1) * gv
        vn = B2 * v_ref[...] + (1.0 - B2) * (gv * gv)
        mo_ref[...] = mn
        vo_ref[...] = vn
        d_ref[...] = -LR * ((mn * c1) / (jnp.sqrt(vn * c2) + ADAM_EPS) + WD * w_ref[...])

    blk = pl.BlockSpec((tr, cols), lambda i: (i, 0))
    return _pc(
        body, name="adamw_%dx%d" % (rows, cols), grid=(rows // tr,), in_specs=[blk] * 4, out_specs=[blk] * 3,
        out_shape=[_sds((rows, cols), F32)] * 3, compiler_params=_cp(1),
    )(w, g, m, v)


MESH_ID = pl.DeviceIdType.MESH
ANY = pl.BlockSpec(memory_space=pl.ANY)


def _place():
    x, y, c = lax.axis_index("x"), lax.axis_index("y"), lax.axis_index("c")
    others = [(1 - x, y), (x, 1 - y), (1 - x, 1 - y)]
    return x, y, c, others


def _gather_weights(shards):
    nk = len(shards)

    def body(*refs):
        ins, outs = refs[:nk], refs[nk:2 * nk]
        send_sems, recv_sems, local_sems = refs[2 * nk:]
        x, y, c, others = _place()
        me = 2 * x + y
        sib = (x, y, 1 - c)
        local = [pltpu.make_async_copy(ins[k], outs[k].at[me], local_sems.at[k]) for k in range(nk)]
        for cp in local:
            cp.start()
        sends = []
        for k in range(nk):
            hr = shards[k].shape[0] // 2
            for r, (cx, cy) in enumerate(others):
                cp = pltpu.make_async_remote_copy(
                    src_ref=ins[k].at[pl.ds(c * hr, hr)], dst_ref=outs[k].at[me, pl.ds(c * hr, hr)],
                    send_sem=send_sems.at[6 * k + r], recv_sem=recv_sems.at[6 * k + r],
                    device_id=(cx, cy, c), device_id_type=MESH_ID)
                cp.start()
                sends.append(cp)
        for k in range(nk):
            hr = shards[k].shape[0] // 2
            for r, (cx, cy) in enumerate(others):
                blk = outs[k].at[2 * cx + cy, pl.ds(c * hr, hr)]
                pltpu.make_async_remote_copy(
                    src_ref=blk, dst_ref=blk, send_sem=send_sems.at[6 * k + r], recv_sem=recv_sems.at[6 * k + r],
                    device_id=(cx, cy, c), device_id_type=MESH_ID).wait_recv()
                fwd = pltpu.make_async_remote_copy(
                    src_ref=blk, dst_ref=blk, send_sem=send_sems.at[6 * k + 3 + r], recv_sem=recv_sems.at[6 * k + 3 + r],
                    device_id=sib, device_id_type=MESH_ID)
                fwd.start()
                sends.append(fwd)
        for k in range(nk):
            hr = shards[k].shape[0] // 2
            for r, (cx, cy) in enumerate(others):
                blk = outs[k].at[2 * cx + cy, pl.ds((1 - c) * hr, hr)]
                pltpu.make_async_remote_copy(
                    src_ref=blk, dst_ref=blk, send_sem=send_sems.at[6 * k + 3 + r], recv_sem=recv_sems.at[6 * k + 3 + r],
                    device_id=sib, device_id_type=MESH_ID).wait_recv()
        for cp in sends:
            cp.wait_send()
        for cp in local:
            cp.wait()

    return _pc(
        body, name="gather_weights", in_specs=[ANY] * nk, out_specs=[ANY] * nk,
        out_shape=[_sds((N_CHIPS,) + a.shape, a.dtype) for a in shards],
        scratch_shapes=[pltpu.SemaphoreType.DMA((6 * nk,)), pltpu.SemaphoreType.DMA((6 * nk,)), pltpu.SemaphoreType.DMA((nk,))],
    )(*shards)


def _swap_sibling(gs, halves):
    nk = len(gs)

    def body(*refs):
        ins, outs = refs[:nk], refs[nk:2 * nk]
        send_sems, recv_sems = refs[2 * nk:]
        x, y, c, _ = _place()
        cps = []
        for k in range(nk):
            hr = gs[k].shape[1] // 2
            cp = pltpu.make_async_remote_copy(
                src_ref=ins[k].at[:, pl.ds((1 - c) * hr, hr)] if halves else ins[k], dst_ref=outs[k],
                send_sem=send_sems.at[k], recv_sem=recv_sems.at[k], device_id=(x, y, 1 - c), device_id_type=MESH_ID)
            cp.start()
            cps.append(cp)
        for cp in cps:
            cp.wait()

    return _pc(
        body, name="swap_halves" if halves else "swap_slabs", in_specs=[ANY] * nk, out_specs=[ANY] * nk,
        out_shape=[_sds((N_CHIPS, a.shape[1] // 2 if halves else a.shape[1], a.shape[2]), a.dtype) for a in gs],
        scratch_shapes=[pltpu.SemaphoreType.DMA((nk,)), pltpu.SemaphoreType.DMA((nk,))],
    )(*gs)


def _exchange_chips(ps):
    nk = len(ps)

    def body(*refs):
        ins, outs = refs[:nk], refs[nk:2 * nk]
        send_sems, recv_sems, local_sems = refs[2 * nk:]
        x, y, c, others = _place()
        me = 2 * x + y
        local = [pltpu.make_async_copy(ins[k].at[me], outs[k].at[me], local_sems.at[k]) for k in range(nk)]
        for cp in local:
            cp.start()
        sends = []
        for k in range(nk):
            for r, (cx, cy) in enumerate(others):
                cp = pltpu.make_async_remote_copy(
                    src_ref=ins[k].at[2 * cx + cy], dst_ref=outs[k].at[me],
                    send_sem=send_sems.at[3 * k + r], recv_sem=recv_sems.at[3 * k + r],
                    device_id=(cx, cy, c), device_id_type=MESH_ID)
                cp.start()
                sends.append(cp)
        for k in range(nk):
            for r, (cx, cy) in enumerate(others):
                blk = outs[k].at[2 * cx + cy]
                pltpu.make_async_remote_copy(
                    src_ref=blk, dst_ref=blk, send_sem=send_sems.at[3 * k + r], recv_sem=recv_sems.at[3 * k + r],
                    device_id=(cx, cy, c), device_id_type=MESH_ID).wait_recv()
        for cp in sends:
            cp.wait_send()
        for cp in local:
            cp.wait()

    return _pc(
        body, name="exchange_chips", in_specs=[ANY] * nk, out_specs=[ANY] * nk,
        out_shape=[_sds(a.shape, a.dtype) for a in ps],
        scratch_shapes=[pltpu.SemaphoreType.DMA((3 * nk,)), pltpu.SemaphoreType.DMA((3 * nk,)), pltpu.SemaphoreType.DMA((nk,))],
    )(*ps)


def _allreduce_small(v):
    rows, cols = v.shape

    def body(v_ref, o_ref, gath, send_sems, recv_sems):
        x, y, c, others = _place()
        sib = (x, y, 1 - c)

        def slot(px, py, pc):
            return gath.at[4 * px + 2 * py + pc]

        def copy(k, block, to, src=None):
            return pltpu.make_async_remote_copy(
                src_ref=slot(*block) if src is None else src, dst_ref=slot(*block),
                send_sem=send_sems.at[k], recv_sem=recv_sems.at[k], device_id=to, device_id_type=MESH_ID)

        me = (x, y, c)
        gath[4 * x + 2 * y + c] = v_ref[...]
        first = [copy(0, me, sib, src=v_ref)]
        first += [copy(1 + r, me, (cx, cy, c), src=v_ref) for r, (cx, cy) in enumerate(others)]
        for cp in first:
            cp.start()
        passed = [copy(4 + r, (cx, cy, c), sib) for r, (cx, cy) in enumerate(others)]
        for r, (cx, cy) in enumerate(others):
            copy(1 + r, (cx, cy, c), me).wait_recv()
            passed[r].start()
        copy(0, (x, y, 1 - c), me).wait_recv()
        for r, (cx, cy) in enumerate(others):
            copy(4 + r, (cx, cy, 1 - c), me).wait_recv()
        for cp in first + passed:
            cp.wait_send()
        acc = gath[0]
        for d in range(1, N_DEV):
            acc = acc + gath[d]
        o_ref[...] = acc

    vm = pl.BlockSpec(memory_space=pltpu.VMEM)
    return _pc(
        body, name="allreduce_small", in_specs=[vm], out_specs=vm, out_shape=_sds((rows, cols), F32),
        scratch_shapes=[pltpu.VMEM((N_DEV, rows, cols), F32), pltpu.SemaphoreType.DMA((7,)), pltpu.SemaphoreType.DMA((7,))],
    )(v)


def _cols_to_chips(a):
    r, c4 = a.shape
    return a.reshape(r, N_CHIPS, c4 // N_CHIPS).transpose(1, 0, 2)


def _chips_to_cols(a):
    n, r, c = a.shape
    return a.transpose(1, 0, 2).reshape(r, n * c)


def _head_rows(a, tq):
    s = a.shape[0]
    return a[:, :HF].reshape(s // tq, tq, NP, 2).transpose(2, 0, 3, 1)


def _head_cols(a):
    s = a.shape[0]
    return a[:, :HF].reshape(s, NP, 2).transpose(1, 0, 2)


def _local_step(x, mem, tgt, sp, w):
    s = x.shape[0]
    tq = min(TQ, s)
    b_f = jnp.pad(sp["b_forget"], ((0, 0), (0, LANES - HF)))
    proj, h, flog = _in_proj(x, sp["norm_mix_pre"], w["w_main"], w["w_f"])
    cf = _logf_cumsum(flog, b_f)
    qx, kx = _fox_prep(proj, _head_cols(cf))
    oa, lse = _fox_fwd(proj, qx, kx)
    y, oc = _conv_fwd(proj, w["conv_w"], sp["conv_b"], sp["conv_ln_g"], sp["conv_ln_b"])
    mem_n, kv = _mem_kv(mem, sp["norm_mem"], w["w_kv"])
    om = _mem_attn_fwd(proj, kv)
    pa, pc, pm, merged, z, x1, h2 = _merge_out(oa, oc, om, proj, x, w["wpa"], w["wpc"], w["wpm"], w["wout"],
                                              sp["norm_mix_post"], sp["norm_ffn_pre"])
    gf, uf, act = _ffn_up(h2, w["w_gu"])
    dffn, dy, vec_f, loss_blk = _ffn_down_loss(act, w["w_d"], x1, tgt, sp["norm_ffn_post"])

    dgf, duf = _ffn_down_bwd(dffn, w["w_d"], gf, uf)
    dx1, dz, vec_n = _ffn_up_bwd(dgf, duf, w["w_gu"], x1, dy, z, sp["norm_ffn_pre"], sp["norm_mix_post"])
    dpa, dpc, dpm, dgl, doa, doc, dom, delta = _merge_bwd(dz, proj, pa, pc, pm, oa, w["wpa"], w["wpc"], w["wpm"], w["wout"])
    lse16 = lse.transpose(1, 0, 2).reshape(s, HF)
    dq, dqs, dk, dks, dv = _fox_bwd(proj, qx, kx, doa, _head_rows(lse16, tq), _head_rows(delta, tq))
    over_keys = dqs[:, X_KONE:X_KONE + 2, :].transpose(2, 0, 1).reshape(s, HF)
    over_queries = dks[:, :, X_QONE:X_QONE + 2, :].transpose(1, 3, 0, 2).reshape(s, HF)
    dc = jnp.pad(over_keys - over_queries, ((0, 0), (0, LANES - HF)))
    df, db_blk = _logf_cumsum_bwd(flog, b_f, dc)
    dga, dgg, dcw, vec_c = _conv_bwd(proj, y, doc, w["conv_w"], sp["conv_ln_g"], sp["conv_ln_b"])
    dqm, dkv = _mem_attn_bwd(proj, kv, dom)
    dkv_b = dkv.astype(MXU)
    vec_m = _mem_kv_bwd(mem, sp["norm_mem"], w["w_kv"], dkv_b)
    pieces = [(dq, 0, 1), (dk, 1, 1), (dv, 2, 1), (dga, 3, 1), (dgg, 4, 1), (dqm, 5, 1), (dgl, 6, 3)]
    dx, vec_p = _in_proj_bwd(pieces, df, w["w_main"], w["w_f"], x, dx1, sp["norm_mix_pre"])

    dw_cols = [_wgrad(h, p, "wgrad_in_%d" % c0) for p, c0, _ in pieces]
    dwf = _wgrad(h, df, "wgrad_in_f")
    gw = {
        "w_in": jnp.concatenate(dw_cols[:3] + [dwf[:, :HF]] + dw_cols[3:], axis=1),
        "conv_w": dcw,
        "w_kv": _wgrad(mem_n, dkv_b, "wgrad_kv"),
        "wpa": _wgrad(oa, dpa, "wgrad_pa"), "wpc": _wgrad(oc, dpc, "wgrad_pc"), "wpm": _wgrad(om, dpm, "wgrad_pm"),
        "wout": _wgrad(merged, dz, "wgrad_out"),
        "w_gu": jnp.concatenate([_wgrad(h2, dgf, "wgrad_g"), _wgrad(h2, duf, "wgrad_u")], axis=1),
        "w_d": _wgrad(act, dffn, "wgrad_d"),
    }
    zero_row = jnp.zeros((1, D), F32)
    small = jnp.concatenate([
        vec_p[0:1], vec_n[1:2], vec_m[0:1], vec_c[2:3], vec_c[0:1], vec_c[1:2], vec_n[0:1], vec_f[0:1],
        jnp.pad(db_blk[0:1, :HF], ((0, 0), (0, D - HF))),
        jnp.pad(loss_blk[0:1, 0:1], ((0, 0), (0, D - 1))),
    ] + [zero_row] * (SMALL_ROWS - 10), axis=0)
    return dx, gw, small


SMALL_NAMES = ["norm_mix_pre", "norm_mix_post", "norm_mem", "conv_b", "conv_ln_g", "conv_ln_b", "norm_ffn_pre", "norm_ffn_post"]
PROJ_NAMES = ["w_proj_attn", "w_proj_conv", "w_proj_mem", "w_out"]
WEIGHT_ORDER = ["norm_mix_pre", "norm_mix_post", "norm_mem", "w_in", "b_forget", "conv_w", "conv_b", "conv_ln_g", "conv_ln_b",
                "w_kv_mem", "w_proj_attn", "w_proj_conv", "w_proj_mem", "w_out", "norm_ffn_pre", "norm_ffn_post",
                "w_gate_up", "w_down"]


def _pack_small(p):
    rows = [p[n] for n in SMALL_NAMES] + [jnp.pad(p["b_forget"], ((0, 0), (0, D - HF)))]
    return jnp.concatenate(rows + [jnp.zeros((SMALL_ROWS - len(rows), D), F32)], axis=0)


def _step(params, moms, vels, x, mem, tgt):
    c_idx = lax.axis_index("c").astype(jnp.int32).reshape(1)

    conv_w_p = jnp.pad(params["conv_w"], ((0, CWP - CW), (0, 0)))
    shards = [params["w_in"].astype(WIRE), conv_w_p, params["w_kv_mem"].astype(WIRE),
              jnp.concatenate([params[n] for n in PROJ_NAMES], axis=0).astype(WIRE),
              params["w_gate_up"].astype(WIRE), params["w_down"].astype(WIRE)]
    g_in, g_cw, g_kv, g_pj, g_gu, g_d = _gather_weights(shards)
    w_in_full = _chips_to_cols(g_in)
    pj = g_pj.reshape(N_CHIPS, 4, D // N_CHIPS, D).transpose(1, 0, 2, 3).reshape(4, D, D)
    w = {
        "w_main": jnp.concatenate([w_in_full[:, :3 * D], w_in_full[:, 3 * D + HF:]], axis=1),
        "w_f": jnp.pad(w_in_full[:, 3 * D:3 * D + HF], ((0, 0), (0, LANES - HF))),
        "conv_w": _chips_to_cols(g_cw), "w_kv": _chips_to_cols(g_kv),
        "wpa": pj[0], "wpc": pj[1], "wpm": pj[2], "wout": pj[3],
        "w_gu": _chips_to_cols(g_gu), "w_d": g_d.reshape(FFN, D),
    }

    dx, gw, small = _local_step(x, mem, tgt, params, w)

    pj_g = jnp.stack([gw["wpa"], gw["wpc"], gw["wpm"], gw["wout"]]).reshape(4, N_CHIPS, D // N_CHIPS, D)
    gs = [_cols_to_chips(gw["w_in"]), _cols_to_chips(gw["conv_w"]), _cols_to_chips(gw["w_kv"]),
          pj_g.transpose(1, 0, 2, 3).reshape(N_CHIPS, D, D), _cols_to_chips(gw["w_gu"]),
          gw["w_d"].reshape(N_CHIPS, FFN // N_CHIPS, D)]
    gs = [g.astype(WIRE) for g in gs]
    r1 = _swap_sibling(gs, True)
    ps = [_pair_sum(g.reshape(N_CHIPS, 2, g.shape[1] // 2, g.shape[2]), r, c_idx) for g, r in zip(gs, r1)]
    r2 = _exchange_chips(ps)
    r2_sib = _swap_sibling(r2, False)
    full = [_chip_sum(theirs, 1 - c_idx, _chip_sum(mine, c_idx)) for mine, theirs in zip(r2, r2_sib)]
    red = [f.reshape(2 * f.shape[1], f.shape[2]) for f in full]
    pj_r = red[3].reshape(4, D // N_CHIPS, D)
    grads = {"w_in": red[0], "conv_w": red[1][:CW], "w_kv_mem": red[2], "w_gate_up": red[4], "w_down": red[5]}
    for i, n in enumerate(PROJ_NAMES):
        grads[n] = pj_r[i]

    tot = _allreduce_small(small)
    loss = tot[9, 0]
    for i, n in enumerate(SMALL_NAMES):
        grads[n] = tot[i:i + 1]
    grads["b_forget"] = tot[8:9, :HF]

    delta, new_m, new_v = {}, {}, {}
    ds, ms, vs = _adamw(_pack_small(params), tot.at[9:].set(0.0), _pack_small(moms), _pack_small(vels))
    for i, n in enumerate(SMALL_NAMES):
        delta[n], new_m[n], new_v[n] = ds[i:i + 1], ms[i:i + 1], vs[i:i + 1]
    delta["b_forget"], new_m["b_forget"], new_v["b_forget"] = ds[8:9, :HF], ms[8:9, :HF], vs[8:9, :HF]
    for n in ["w_in", "conv_w", "w_kv_mem", "w_gate_up", "w_down"] + PROJ_NAMES:
        delta[n], new_m[n], new_v[n] = _adamw(params[n], grads[n], moms[n], vels[n])
    return loss, dx, grads, delta, new_m, new_v


def kernel(x, mem, norm_mix_pre, norm_mix_post, norm_mem, w_in, b_forget, conv_w, conv_b, conv_ln_g, conv_ln_b, w_kv_mem, w_proj_attn, w_proj_conv, w_proj_mem, w_out, norm_ffn_pre, norm_ffn_post, w_gate_up, w_down, loss_target, m_norm_mix_pre, m_norm_mix_post, m_norm_mem, m_w_in, m_b_forget, m_conv_w, m_conv_b, m_conv_ln_g, m_conv_ln_b, m_w_kv_mem, m_w_proj_attn, m_w_proj_conv, m_w_proj_mem, m_w_out, m_norm_ffn_pre, m_norm_ffn_post, m_w_gate_up, m_w_down, v_norm_mix_pre, v_norm_mix_post, v_norm_mem, v_w_in, v_b_forget, v_conv_w, v_conv_b, v_conv_ln_g, v_conv_ln_b, v_w_kv_mem, v_w_proj_attn, v_w_proj_conv, v_w_proj_mem, v_w_out, v_norm_ffn_pre, v_norm_ffn_post, v_w_gate_up, v_w_down):
    local = dict(locals())
    lead = {n: local[n].shape[:-2] for n in WEIGHT_ORDER}
    two_d = lambda a: a.reshape(a.shape[-2:])
    params = {n: two_d(local[n]) for n in WEIGHT_ORDER}
    moms = {n: two_d(local["m_" + n]) for n in WEIGHT_ORDER}
    vels = {n: two_d(local["v_" + n]) for n in WEIGHT_ORDER}
    loss, dx, grads, delta, new_m, new_v = _step(params, moms, vels, two_d(x), two_d(mem), two_d(loss_target))
    outs = [loss, dx.reshape(x.shape)]
    for group in (grads, delta, new_m, new_v):
        outs += [group[n].reshape(lead[n] + group[n].shape) for n in WEIGHT_ORDER]
    return tuple(outs)
```

```python
import functools

import jax
import jax.numpy as jnp
from jax import lax
from jax.experimental import pallas as pl
from jax.experimental.pallas import tpu as pltpu

F32 = jnp.float32
MXU = jnp.bfloat16
WIRE = jnp.bfloat16

D = 1024
HF = 16
DH = 64
NP = D // 128
MEM_H = 4
MEM_DH = D // MEM_H
FFN = 2816
CW = 31
CWP = 32
HALO = 32
RMS_EPS = 1e-6
LN_EPS = 1e-5
LR, B1, B2, ADAM_EPS, WD, STEP = 0.001, 0.9, 0.999, 1e-8, 0.01, 10

N_CHIPS = 4
N_DEV = 8
LANES = 128
VMEM_LIMIT = 56 * 1024 * 1024

TM_PROJ = 512
NB_PROJ = 3
TQ = 1024
LOG2E = 1.4426950408889634
LN2 = 0.6931471805599453
QSCALE = DH ** -0.5 * LOG2E
X_BIAS = 0
X_QONE = 6
X_KONE = 8
X_ROWS = 16
TM_CONV = 256
CONV_ROWS = 128
SUB = 8
TM_ROW = 256
TS_WG = 1024
WG_CAP = 1408
SMALL_ROWS = 16


def _pc(body, **kw):
    return pl.pallas_call(body, **kw)


def _cp(n_axes):
    return pltpu.CompilerParams(dimension_semantics=("arbitrary",) * n_axes, vmem_limit_bytes=VMEM_LIMIT)


def _sds(shape, dtype):
    return jax.ShapeDtypeStruct(shape, dtype)


def _dot(a, b):
    return jnp.dot(a, b, preferred_element_type=F32)


def _dot_nt(a, b):
    return lax.dot_general(a, b, (((1,), (1,)), ((), ())), preferred_element_type=F32)


def _dot_tn(a, b):
    return lax.dot_general(a, b, (((0,), (0,)), ((), ())), preferred_element_type=F32)


def _rms(u):
    return lax.rsqrt(jnp.mean(u * u, axis=-1, keepdims=True) + RMS_EPS)


def _rms_bwd(u, r, g, dn):
    w = dn * g
    return r * w - u * (r * r * r) * jnp.mean(u * w, axis=-1, keepdims=True)


def _sigmoid(z):
    return 1.0 / (1.0 + jnp.exp(-z))


def _tile(n, cap):
    if n <= cap:
        return n
    best = None
    for t in range(LANES, cap + 1, LANES):
        if n % t == 0:
            best = t
    assert best is not None, (n, cap)
    return best


def _rowtile(rows, cols, cap_bytes=1 << 20):
    best = None
    for t in range(8, rows + 1, 8):
        if rows % t == 0 and t * cols * 4 <= cap_bytes:
            best = t
    return best if best is not None else rows


def _split3(v):
    hi = v.astype(jnp.bfloat16)
    r1 = v - hi.astype(F32)
    mid = r1.astype(jnp.bfloat16)
    lo = (r1 - mid.astype(F32)).astype(jnp.bfloat16)
    return hi, mid, lo


def _dot_exact_rhs(a01, v):
    hi, mid, lo = _split3(v)
    return _dot(a01, hi) + _dot(a01, mid) + _dot(a01, lo)


def _in_proj(x, g_pre, w_main, w_f):
    s, d = x.shape
    n = w_main.shape[1]
    tm = min(TM_PROJ, s)
    tn = n // NB_PROJ

    def body(x_ref, g_ref, w_ref, wf_ref, proj_ref, h_ref, flog_ref, hs):
        @pl.when(pl.program_id(1) == 0)
        def _():
            xv = x_ref[...]
            h = (xv * _rms(xv) * g_ref[...]).astype(MXU)
            hs[...] = h
            h_ref[...] = h
            flog_ref[...] = _dot(h, wf_ref[...])

        res = _dot(hs[...], w_ref[...])

        @pl.when(pl.program_id(1) == 0)
        def _():
            proj_ref[:, pl.ds(0, d)] = (res[:, :d] * QSCALE).astype(MXU)
            proj_ref[:, pl.ds(d, tn - d)] = res[:, d:].astype(MXU)

        @pl.when(pl.program_id(1) != 0)
        def _():
            proj_ref[...] = res.astype(MXU)

    assert tn >= d
    return _pc(
        body, name="in_proj", grid=(s // tm, NB_PROJ),
        in_specs=[pl.BlockSpec((tm, d), lambda i, j: (i, 0)), pl.BlockSpec((1, d), lambda i, j: (0, 0)),
                  pl.BlockSpec((d, tn), lambda i, j: (0, j)), pl.BlockSpec((d, LANES), lambda i, j: (0, 0))],
        out_specs=[pl.BlockSpec((tm, tn), lambda i, j: (i, j)), pl.BlockSpec((tm, d), lambda i, j: (i, 0)),
                   pl.BlockSpec((tm, LANES), lambda i, j: (i, 0))],
        out_shape=[_sds((s, n), MXU), _sds((s, d), MXU), _sds((s, LANES), F32)],
        scratch_shapes=[pltpu.VMEM((tm, d), MXU)], compiler_params=_cp(2),
    )(x, g_pre, w_main, w_f)


def _log_sigmoid(z):
    e = jnp.exp(-jnp.abs(z))
    log1p_e = jnp.where(e < 1e-3, e * (1.0 - 0.5 * e), jnp.log(1.0 + e))
    return jnp.minimum(z, 0.0) - log1p_e


def _logf_cumsum(flog, b_f):
    s = flog.shape[0]
    ch = LANES

    def body(f_ref, b_ref, c_ref):
        r = lax.broadcasted_iota(jnp.int32, (ch, ch), 0)
        q = lax.broadcasted_iota(jnp.int32, (ch, ch), 1)
        tri = jnp.where(r >= q, 1.0, 0.0).astype(jnp.bfloat16)

        def step(i, carry):
            rows = pl.ds(pl.multiple_of(i * ch, ch), ch)
            lf = _log_sigmoid(f_ref[rows, :] + b_ref[...])
            c_ref[rows, :] = _dot_exact_rhs(tri, lf) + carry
            return carry + jnp.sum(lf, axis=0, keepdims=True)

        lax.fori_loop(0, s // ch, step, jnp.zeros((1, LANES), F32))

    return _pc(body, name="logf_cumsum", out_shape=_sds((s, LANES), F32),
               compiler_params=pltpu.CompilerParams(vmem_limit_bytes=VMEM_LIMIT))(flog, b_f)


def _logf_cumsum_bwd(flog, b_f, dc):
    s = flog.shape[0]
    ch = LANES

    def body(f_ref, b_ref, dc_ref, df_ref, db_ref):
        r = lax.broadcasted_iota(jnp.int32, (ch, ch), 0)
        q = lax.broadcasted_iota(jnp.int32, (ch, ch), 1)
        tri = jnp.where(r <= q, 1.0, 0.0).astype(jnp.bfloat16)
        nch = s // ch

        def step(t, carry):
            tail, dbsum = carry
            i = nch - 1 - t
            rows = pl.ds(pl.multiple_of(i * ch, ch), ch)
            dcv = dc_ref[rows, :]
            dlf = _dot_exact_rhs(tri, dcv) + tail
            z = f_ref[rows, :] + b_ref[...]
            df = dlf * _sigmoid(-z)
            df_ref[rows, :] = df.astype(MXU)
            return tail + jnp.sum(dcv, axis=0, keepdims=True), dbsum + jnp.sum(df, axis=0, keepdims=True)

        zero = jnp.zeros((1, LANES), F32)
        _, dbsum = lax.fori_loop(0, nch, step, (zero, zero))
        db_ref[...] = jnp.broadcast_to(dbsum, db_ref.shape)

    return _pc(body, name="logf_cumsum_bwd", out_shape=[_sds((s, LANES), MXU), _sds((8, LANES), F32)],
               compiler_params=pltpu.CompilerParams(vmem_limit_bytes=VMEM_LIMIT))(flog, b_f, dc)


def _head_masks(rows):
    lane = lax.broadcasted_iota(jnp.int32, (rows, LANES), 1)
    return lane < DH, lane >= DH


def _ext_masks(rows, key_side):
    lane = lax.broadcasted_iota(jnp.int32, (rows, 2 * LANES), 1)
    ext = lane - LANES
    out = []
    for a in range(2):
        head = (lane >= a * DH) & (lane < (a + 1) * DH)
        bias = (ext >= X_BIAS + 3 * a) & (ext < X_BIAS + 3 * a + 3)
        one = ext == (X_KONE if key_side else X_QONE) + a
        out.append(head | bias | one)
    return out


def _fox_prep(proj, ccol):
    s = proj.shape[0]
    tm = min(TM_PROJ, s)

    def body(q_ref, k_ref, c_ref, qx_ref, kx_ref):
        lane = lax.broadcasted_iota(jnp.int32, (tm, LANES), 1)
        qx_ref[:, pl.ds(0, LANES)] = q_ref[...]
        qx_ref[:, pl.ds(LANES, LANES)] = jnp.where(lane < X_QONE + 2, 1.0, 0.0).astype(MXU)
        kext = jnp.where((lane >= X_KONE) & (lane < X_KONE + 2), 1.0, 0.0).astype(jnp.bfloat16)
        for a in range(2):
            terms = _split3(c_ref[0, :, a:a + 1] * (-LOG2E))
            for t, term in enumerate(terms):
                kext = jnp.where(lane == X_BIAS + 3 * a + t, term, kext)
        kx_ref[:, pl.ds(0, LANES)] = k_ref[...]
        kx_ref[:, pl.ds(LANES, LANES)] = kext.astype(MXU)

    wide = pl.BlockSpec((tm, 2 * LANES), lambda p, i: (i, p))
    return _pc(
        body, name="fox_prep", grid=(NP, s // tm),
        in_specs=[pl.BlockSpec((tm, LANES), lambda p, i: (i, p)), pl.BlockSpec((tm, LANES), lambda p, i: (i, NP + p)),
                  pl.BlockSpec((1, tm, 2), lambda p, i: (p, i, 0))],
        out_specs=[wide, wide], out_shape=[_sds((s, NP * 2 * LANES), MXU)] * 2, compiler_params=_cp(2),
    )(proj, proj, ccol)


def _fox_fwd(proj, qx, kx, plan):
    s = proj.shape[0]
    tq = min(TQ, s)
    nq = s // tq

    nx = len(plan["arrays"])

    def body(*refs):
        q_ref, k_ref, v_ref = refs[:3]
        o_ref, lse_ref = refs[3 + nx:5 + nx]
        p_id, i = pl.program_id(0), pl.program_id(1)
        _host_plan(plan, refs[3:3 + nx], refs[5 + nx:5 + 2 * nx], refs[5 + 2 * nx:], (p_id == 0) & (i == 0),
                   (p_id == NP // 2) & (i == 0), (p_id == NP - 1) & (i == nq - 1))
        qv = q_ref[...]
        qmask = _ext_masks(tq, False)
        hmask = _head_masks(tq)
        qas = [jnp.where(qmask[a], qv, jnp.zeros_like(qv)) for a in range(2)]
        row = lax.broadcasted_iota(jnp.int32, (tq, tq), 0)
        col = lax.broadcasted_iota(jnp.int32, (tq, tq), 1)

        def blk(j, carry, diag=False):
            rows = pl.ds(pl.multiple_of(j * tq, tq), tq)
            kj = k_ref[rows, :]
            vj = v_ref[rows, :]
            out = []
            for a in range(2):
                m, acc = carry[a]
                sc = _dot_nt(qas[a], kj)
                if diag:
                    sc = jnp.where(row >= col, sc, -jnp.inf)
                m_new = jnp.maximum(m, jnp.max(sc, axis=-1, keepdims=True))
                p = jnp.exp2(sc - m_new)
                va = jnp.where(hmask[a], vj, jnp.ones_like(vj))
                out.append((m_new, jnp.exp2(m - m_new) * acc + _dot(p.astype(MXU), va)))
            return tuple(out)

        init = (jnp.full((tq, 1), -jnp.inf, F32), jnp.zeros((tq, LANES), F32))
        res = blk(i, lax.fori_loop(0, i, blk, (init, init)), True)
        lane = lax.broadcasted_iota(jnp.int32, (tq, LANES), 1)
        outs, lses = [], []
        for a in range(2):
            m, acc = res[a]
            l = jnp.sum(jnp.where(lane == DH * (1 - a), acc, 0.0), axis=-1, keepdims=True)
            outs.append(acc / l)
            lses.append(m + jnp.log(l) * LOG2E)
        o_ref[...] = jnp.where(hmask[0], outs[0], outs[1]).astype(MXU)
        lane2 = lax.broadcasted_iota(jnp.int32, (tq, 2), 1)
        lse_ref[0] = jnp.where(lane2 == 0, lses[0], lses[1])

    return _pc(
        body, name="fox_fwd", grid=(NP, nq),
        in_specs=[pl.BlockSpec((tq, 2 * LANES), lambda p, i: (i, p)),
                  pl.BlockSpec((s, 2 * LANES), lambda p, i: (0, p)),
                  pl.BlockSpec((s, LANES), lambda p, i: (0, 2 * NP + p))] + [ANY] * nx,
        out_specs=[pl.BlockSpec((tq, LANES), lambda p, i: (i, p)),
                   pl.BlockSpec((1, tq, 2), lambda p, i: (p, i, 0))] + [ANY] * nx,
        out_shape=[_sds((s, D), MXU), _sds((NP, s, 2), F32)] + plan["out_shape"],
        scratch_shapes=plan["scratch"], compiler_params=_cp(2),
    )(qx, kx, proj, *plan["arrays"])


def _fox_bwd(proj, qx, kx, do, lse_row, delta_row, plan):
    s = proj.shape[0]
    tq = min(TQ, s)
    nq = s // tq
    nx = len(plan["arrays"])

    def body(*refs):
        k_ref, v_ref, q_ref, do_ref, lse_ref, dl_ref = refs[:6]
        dq_ref, dqs_ref, dk_ref, dks_ref, dv_ref = refs[6 + nx:11 + nx]
        dq_acc = refs[11 + 2 * nx]
        p_id, j = pl.program_id(0), pl.program_id(1)
        _host_plan(plan, refs[6:6 + nx], refs[11 + nx:11 + 2 * nx], refs[12 + 2 * nx:], (p_id == 0) & (j == 0),
                   (p_id == NP // 2) & (j == 0), (p_id == NP - 1) & (j == nq - 1))

        @pl.when(j == 0)
        def _():
            dq_acc[...] = jnp.zeros_like(dq_acc)

        kv = k_ref[...]
        v2 = v_ref[...]
        kmask = _ext_masks(tq, True)
        qmask = _ext_masks(tq, False)
        hmask = _head_masks(tq)
        row = lax.broadcasted_iota(jnp.int32, (tq, tq), 0)
        col = lax.broadcasted_iota(jnp.int32, (tq, tq), 1)
        carry = (jnp.zeros((tq, 2 * LANES), F32), jnp.zeros((tq, LANES), F32))
        for a in range(2):
            ka = jnp.where(kmask[a], kv, jnp.zeros_like(kv))
            va = jnp.where(hmask[a], v2, jnp.zeros_like(v2))

            def blk(i, carry, diag, a=a, ka=ka, va=va):
                dk_a, dv_a = carry
                rows = pl.ds(pl.multiple_of(i * tq, tq), tq)
                qi = q_ref[rows, :]
                doi = do_ref[rows, :]
                qa = jnp.where(qmask[a], qi, jnp.zeros_like(qi))
                doa = jnp.where(hmask[a], doi, jnp.zeros_like(doi))
                st = _dot_nt(ka, qi)
                if diag:
                    st = jnp.where(col >= row, st, -jnp.inf)
                pt = jnp.exp2(st - lse_ref[0, i, a:a + 1, :])
                dv_a = dv_a + _dot(pt.astype(MXU), doa)
                dpt = _dot_nt(va, doi)
                dsb = (pt * (dpt - dl_ref[0, i, a:a + 1, :])).astype(MXU)
                dk_a = dk_a + _dot(dsb, qa)
                dq_acc[rows, :] += _dot_tn(dsb, ka)
                return dk_a, dv_a

            carry = blk(j, carry, True)
            carry = lax.fori_loop(j + 1, nq, functools.partial(blk, diag=False), carry)
        dk_acc, dv_acc = carry
        dk_ref[...] = (dk_acc[:, :LANES] * LN2).astype(MXU)
        dks_ref[0, 0] = dk_acc[:, LANES:].T[:X_ROWS, :]
        dv_ref[...] = dv_acc.astype(MXU)

        @pl.when(j == nq - 1)
        def _():
            dq_ref[...] = (dq_acc[:, pl.ds(0, LANES)] * DH ** -0.5).astype(MXU)
            for t in range(nq):
                dqs_ref[0, :, pl.ds(t * tq, tq)] = dq_acc[pl.ds(t * tq, tq), pl.ds(LANES, LANES)].T[:X_ROWS, :]

    stat = pl.BlockSpec((1, nq, 2, tq), lambda p, j: (p, 0, 0, 0))
    whole = pl.BlockSpec((s, LANES), lambda p, j: (0, p))
    tile = pl.BlockSpec((tq, LANES), lambda p, j: (j, p))
    return _pc(
        body, name="fox_bwd", grid=(NP, nq),
        in_specs=[pl.BlockSpec((tq, 2 * LANES), lambda p, j: (j, p)),
                  pl.BlockSpec((tq, LANES), lambda p, j: (j, 2 * NP + p)),
                  pl.BlockSpec((s, 2 * LANES), lambda p, j: (0, p)),
                  whole, stat, stat] + [ANY] * nx,
        out_specs=[whole, pl.BlockSpec((1, X_ROWS, s), lambda p, j: (p, 0, 0)), tile,
                   pl.BlockSpec((1, 1, X_ROWS, tq), lambda p, j: (p, j, 0, 0)), tile] + [ANY] * nx,
        out_shape=[_sds((s, D), MXU), _sds((NP, X_ROWS, s), F32), _sds((s, D), MXU),
                   _sds((NP, nq, X_ROWS, tq), F32), _sds((s, D), MXU)] + plan["out_shape"],
        scratch_shapes=[pltpu.VMEM((s, 2 * LANES), F32)] + plan["scratch"], compiler_params=_cp(2),
    )(kx, proj, qx, do, lse_row, delta_row, *plan["arrays"])


def _glu(a, gate):
    return a.astype(F32) * _sigmoid(gate.astype(F32))


def _store_blocked(buf, row0, val):
    for c in range(D // LANES):
        buf[0, c, pl.ds(row0, val.shape[0]), :] = val[:, c * LANES:(c + 1) * LANES]


def _fill_shifted(buf):
    n = buf.shape[2] - SUB
    for r in range(1, SUB):
        buf[r, :, pl.ds(0, n), :] = buf[0, :, pl.ds(r, n), :]


def _shifted(buf, off, rows, c):
    r = off % SUB
    return buf[r, c, pl.ds(off - r, rows), :]


def _conv_fwd(proj, cw, cb, lg, lb):
    s = proj.shape[0]
    tm = min(TM_CONV, s)
    hb = tm // HALO

    rcw = min(CONV_ROWS, tm)

    def body(a_ref, g_ref, ah_ref, gh_ref, w_ref, cb_ref, lg_ref, lb_ref, y_ref, o_ref, gsh):
        i = pl.program_id(0)
        _store_blocked(gsh, 0, jnp.where(i > 0, _glu(ah_ref[...], gh_ref[...]), 0.0))
        _store_blocked(gsh, HALO, _glu(a_ref[...], g_ref[...]))
        _fill_shifted(gsh)
        for c in range(D // LANES):
            cols = pl.ds(c * LANES, LANES)
            for rc in range(tm // rcw):
                acc = jnp.broadcast_to(cb_ref[:, cols], (rcw, LANES))
                for t in range(CW):
                    acc = acc + w_ref[t:t + 1, cols] * _shifted(gsh, HALO - (CW - 1) + t + rc * rcw, rcw, c)
                y_ref[pl.ds(rc * rcw, rcw), cols] = acc
        acc = y_ref[...]
        mu = jnp.mean(acc, axis=-1, keepdims=True)
        xc = acc - mu
        r = lax.rsqrt(jnp.mean(xc * xc, axis=-1, keepdims=True) + LN_EPS)
        nrm = xc * r * lg_ref[...] + lb_ref[...]
        o_ref[...] = (nrm * _sigmoid(nrm)).astype(MXU)

    vec = pl.BlockSpec((1, D), lambda i: (0, 0))
    return _pc(
        body, name="conv_fwd", grid=(s // tm,),
        in_specs=[pl.BlockSpec((tm, D), lambda i: (i, 3)), pl.BlockSpec((tm, D), lambda i: (i, 4)),
                  pl.BlockSpec((HALO, D), lambda i: (jnp.maximum(i * hb - 1, 0), 3)),
                  pl.BlockSpec((HALO, D), lambda i: (jnp.maximum(i * hb - 1, 0), 4)),
                  pl.BlockSpec((CWP, D), lambda i: (0, 0)), vec, vec, vec],
        out_specs=[pl.BlockSpec((tm, D), lambda i: (i, 0)), pl.BlockSpec((tm, D), lambda i: (i, 0))],
        out_shape=[_sds((s, D), F32), _sds((s, D), MXU)],
        scratch_shapes=[pltpu.VMEM((SUB, D // LANES, tm + HALO, LANES), F32)], compiler_params=_cp(1),
    )(proj, proj, proj, proj, cw, cb, lg, lb)


def _conv_bwd(proj, y, do, cw, lg, lb):
    s = proj.shape[0]
    tm = min(TM_CONV, s)
    hb = tm // HALO
    nt = s // tm
    last_hblk = s // HALO - 1

    def ln_bwd(yv, dov, lgv, lbv):
        mu = jnp.mean(yv, axis=-1, keepdims=True)
        xc = yv - mu
        r = lax.rsqrt(jnp.mean(xc * xc, axis=-1, keepdims=True) + LN_EPS)
        xh = xc * r
        nrm = xh * lgv + lbv
        sg = _sigmoid(nrm)
        dn = dov.astype(F32) * (sg * (1.0 + nrm * (1.0 - sg)))
        wv = dn * lgv
        dy = r * (wv - jnp.mean(wv, axis=-1, keepdims=True) - xh * jnp.mean(wv * xh, axis=-1, keepdims=True))
        return dy, dn, xh

    rcw = min(CONV_ROWS, tm)

    def body(a_ref, g_ref, ah_ref, gh_ref, y_ref, yn_ref, do_ref, don_ref, w_ref, lg_ref, lb_ref,
             da_ref, dg_ref, dw_ref, vec_ref, gsh, dysh, dwacc):
        i = pl.program_id(0)

        @pl.when(i == 0)
        def _():
            dwacc[...] = jnp.zeros_like(dwacc)
            vec_ref[...] = jnp.zeros_like(vec_ref)

        lgv, lbv = lg_ref[...], lb_ref[...]
        _store_blocked(gsh, 0, jnp.where(i > 0, _glu(ah_ref[...], gh_ref[...]), 0.0))
        _store_blocked(gsh, HALO, _glu(a_ref[...], g_ref[...]))
        _fill_shifted(gsh)
        dy, dn, xh = ln_bwd(y_ref[...], do_ref[...], lgv, lbv)
        dyn, _, _ = ln_bwd(yn_ref[...], don_ref[...], lgv, lbv)
        _store_blocked(dysh, 0, dy)
        _store_blocked(dysh, tm, jnp.where(i < nt - 1, dyn, 0.0))
        _fill_shifted(dysh)
        vec_ref[0:1, :] += jnp.sum(dn * xh, axis=0, keepdims=True)
        vec_ref[1:2, :] += jnp.sum(dn, axis=0, keepdims=True)
        vec_ref[2:3, :] += jnp.sum(dy, axis=0, keepdims=True)
        for c in range(D // LANES):
            cols = pl.ds(c * LANES, LANES)
            for rc in range(tm // rcw):
                rows = pl.ds(rc * rcw, rcw)
                dyc = dysh[0, c, rows, :]
                dgl = jnp.zeros((rcw, LANES), F32)
                for t in range(CW):
                    dgl = dgl + w_ref[t:t + 1, cols] * _shifted(dysh, CW - 1 - t + rc * rcw, rcw, c)
                    prod = dyc * _shifted(gsh, HALO - (CW - 1) + t + rc * rcw, rcw, c)
                    dwacc[t, :, cols] += jnp.sum(prod.reshape(rcw // SUB, SUB, LANES), axis=0)
                av = a_ref[rows, cols].astype(F32)
                sgate = _sigmoid(g_ref[rows, cols].astype(F32))
                da_ref[rows, cols] = (dgl * sgate).astype(MXU)
                dg_ref[rows, cols] = (dgl * av * sgate * (1.0 - sgate)).astype(MXU)

        @pl.when(i == nt - 1)
        def _():
            dw_ref[...] = jnp.sum(dwacc[...], axis=1)

    vec = pl.BlockSpec((1, D), lambda i: (0, 0))
    cur = lambda c: pl.BlockSpec((tm, D), lambda i: (i, c))
    prv = lambda c: pl.BlockSpec((HALO, D), lambda i: (jnp.maximum(i * hb - 1, 0), c))
    nxt = pl.BlockSpec((HALO, D), lambda i: (jnp.minimum((i + 1) * hb, last_hblk), 0))
    return _pc(
        body, name="conv_bwd", grid=(nt,),
        in_specs=[cur(3), cur(4), prv(3), prv(4), cur(0), nxt, cur(0), nxt,
                  pl.BlockSpec((CWP, D), lambda i: (0, 0)), vec, vec],
        out_specs=[cur(0), cur(0), pl.BlockSpec((CWP, D), lambda i: (0, 0)), pl.BlockSpec((8, D), lambda i: (0, 0))],
        out_shape=[_sds((s, D), MXU), _sds((s, D), MXU), _sds((CWP, D), F32), _sds((8, D), F32)],
        scratch_shapes=[pltpu.VMEM((SUB, D // LANES, tm + HALO, LANES), F32)] * 2 + [pltpu.VMEM((CWP, SUB, D), F32)],
        compiler_params=_cp(1),
    )(proj, proj, proj, proj, y, y, do, do, cw, lg, lb)


def _mem_kv(mem, g_mem, w_kv):
    mm = mem.shape[0]

    def body(m_ref, g_ref, w_ref, mn_ref, kv_ref):
        mv = m_ref[...]
        mn = (mv * _rms(mv) * g_ref[...]).astype(MXU)
        mn_ref[...] = mn
        kv_ref[...] = _dot(mn, w_ref[...]).astype(MXU)

    return _pc(body, name="mem_kv", out_shape=[_sds((mm, D), MXU), _sds((mm, 2 * D), MXU)],
               compiler_params=pltpu.CompilerParams(vmem_limit_bytes=VMEM_LIMIT))(mem, g_mem, w_kv)


def _mem_kv_bwd(mem, g_mem, w_kv, dkv):
    mm = mem.shape[0]

    def body(m_ref, w_ref, dkv_ref, o_ref):
        mv = m_ref[...]
        dmn = _dot_nt(dkv_ref[...], w_ref[...])
        o_ref[...] = jnp.broadcast_to(jnp.sum(dmn * mv * _rms(mv), axis=0, keepdims=True), o_ref.shape)

    return _pc(body, name="mem_kv_bwd", out_shape=_sds((8, D), F32),
               compiler_params=pltpu.CompilerParams(vmem_limit_bytes=VMEM_LIMIT))(mem, w_kv, dkv)


def _mem_attn_fwd(proj, kv):
    s = proj.shape[0]
    mm = kv.shape[0]
    tm = min(TM_ROW, s)
    scale = MEM_DH ** -0.5

    def body(q_ref, kv_ref, o_ref):
        for h in range(MEM_H):
            cols = pl.ds(h * MEM_DH, MEM_DH)
            qh = q_ref[:, cols] * scale
            sc = _dot_nt(qh, kv_ref[:, cols])
            m = jnp.max(sc, axis=-1, keepdims=True)
            e = jnp.exp(sc - m)
            p = e / jnp.sum(e, axis=-1, keepdims=True)
            o_ref[:, cols] = _dot(p.astype(MXU), kv_ref[:, pl.ds(D + h * MEM_DH, MEM_DH)]).astype(MXU)

    return _pc(
        body, name="mem_attn_fwd", grid=(s // tm,),
        in_specs=[pl.BlockSpec((tm, D), lambda i: (i, 5)), pl.BlockSpec((mm, 2 * D), lambda i: (0, 0))],
        out_specs=pl.BlockSpec((tm, D), lambda i: (i, 0)), out_shape=_sds((s, D), MXU), compiler_params=_cp(1),
    )(proj, kv)


def _mem_attn_bwd(proj, kv, do):
    s = proj.shape[0]
    mm = kv.shape[0]
    tm = min(TM_ROW, s)
    scale = MEM_DH ** -0.5

    def body(q_ref, kv_ref, do_ref, dq_ref, dkv_ref):
        @pl.when(pl.program_id(0) == 0)
        def _():
            dkv_ref[...] = jnp.zeros_like(dkv_ref)

        for h in range(MEM_H):
            cols = pl.ds(h * MEM_DH, MEM_DH)
            vcols = pl.ds(D + h * MEM_DH, MEM_DH)
            qh = q_ref[:, cols]
            kh = kv_ref[:, cols] * scale
            doh = do_ref[:, cols]
            st = _dot_nt(kh, qh)
            m = jnp.max(st, axis=0, keepdims=True)
            e = jnp.exp(st - m)
            pt = e / jnp.sum(e, axis=0, keepdims=True)
            dpt = _dot_nt(kv_ref[:, vcols], doh)
            dst = pt * (dpt - jnp.sum(pt * dpt, axis=0, keepdims=True))
            dsb = dst.astype(MXU)
            dkv_ref[:, vcols] += _dot(pt.astype(MXU), doh)
            dkv_ref[:, cols] += _dot(dsb, qh) * scale
            dq_ref[:, cols] = _dot_tn(dsb, kh).astype(MXU)

    return _pc(
        body, name="mem_attn_bwd", grid=(s // tm,),
        in_specs=[pl.BlockSpec((tm, D), lambda i: (i, 5)), pl.BlockSpec((mm, 2 * D), lambda i: (0, 0)),
                  pl.BlockSpec((tm, D), lambda i: (i, 0))],
        out_specs=[pl.BlockSpec((tm, D), lambda i: (i, 0)), pl.BlockSpec((mm, 2 * D), lambda i: (0, 0))],
        out_shape=[_sds((s, D), MXU), _sds((mm, 2 * D), F32)], compiler_params=_cp(1),
    )(proj, kv, do)


def _resident(n):
    return [pltpu.VMEM((n, D, D), MXU), pltpu.SemaphoreType.DMA((n,))]


def _load_resident(hbm_refs, wbuf, sems):
    @pl.when(pl.program_id(0) == 0)
    def _():
        cps = [pltpu.make_async_copy(r, wbuf.at[k], sems.at[k]) for k, r in enumerate(hbm_refs)]
        for cp in cps:
            cp.start()
        for cp in cps:
            cp.wait()


def _merge_out(oa, oc, om, proj, x, wpa, wpc, wpm, wout, g_post, g_fpre):
    s = x.shape[0]
    tm = min(TM_ROW, s)

    def body(oa_ref, oc_ref, om_ref, gl_ref, x_ref, gp_ref, gf_ref, wpa_h, wpc_h, wpm_h, wout_h,
             pa_ref, pc_ref, pm_ref, mg_ref, z_ref, x1_ref, h2_ref, wbuf, sems):
        _load_resident([wpa_h, wpc_h, wpm_h, wout_h], wbuf, sems)
        merged = jnp.zeros((tm, D), F32)
        for b, (o_ref, p_ref) in enumerate(((oa_ref, pa_ref), (oc_ref, pc_ref), (om_ref, pm_ref))):
            pb = _dot(o_ref[...], wbuf[b])
            p_ref[...] = pb.astype(MXU)
            merged = merged + _sigmoid(gl_ref[:, pl.ds(b * D, D)].astype(F32)) * pb
        mg = merged.astype(MXU)
        mg_ref[...] = mg
        z = _dot(mg, wbuf[3])
        z_ref[...] = z
        x1 = x_ref[...] + z * _rms(z) * gp_ref[...]
        x1_ref[...] = x1
        h2_ref[...] = (x1 * _rms(x1) * gf_ref[...]).astype(MXU)

    rows = pl.BlockSpec((tm, D), lambda i: (i, 0))
    vec = pl.BlockSpec((1, D), lambda i: (0, 0))
    anyspec = pl.BlockSpec(memory_space=pl.ANY)
    return _pc(
        body, name="merge_out", grid=(s // tm,),
        in_specs=[rows, rows, rows, pl.BlockSpec((tm, 3 * D), lambda i: (i, 2)), rows, vec, vec,
                  anyspec, anyspec, anyspec, anyspec],
        out_specs=[rows] * 7,
        out_shape=[_sds((s, D), MXU)] * 4 + [_sds((s, D), F32)] * 2 + [_sds((s, D), MXU)],
        scratch_shapes=_resident(4), compiler_params=_cp(1),
    )(oa, oc, om, proj, x, g_post, g_fpre, wpa, wpc, wpm, wout)


def _ffn_up(h2, w_gu):
    s = h2.shape[0]
    tm = min(TM_PROJ, s)
    nb = 2
    bw = FFN // nb

    def body(h_ref, wg_ref, wu_ref, gf_ref, uf_ref, act_ref):
        hv = h_ref[...]
        gf = _dot(hv, wg_ref[...])
        uf = _dot(hv, wu_ref[...])
        gf_ref[...] = gf.astype(MXU)
        uf_ref[...] = uf.astype(MXU)
        act_ref[...] = (gf * _sigmoid(gf) * uf).astype(MXU)

    out = pl.BlockSpec((tm, bw), lambda i, j: (i, j))
    return _pc(
        body, name="ffn_up", grid=(s // tm, nb),
        in_specs=[pl.BlockSpec((tm, D), lambda i, j: (i, 0)), pl.BlockSpec((D, bw), lambda i, j: (0, j)),
                  pl.BlockSpec((D, bw), lambda i, j: (0, nb + j))],
        out_specs=[out, out, out], out_shape=[_sds((s, FFN), MXU)] * 3, compiler_params=_cp(2),
    )(h2, w_gu, w_gu)


def _ffn_down_loss(act, w_d, x1, tgt, g_fpost):
    s = act.shape[0]
    tm = min(TM_ROW, s)

    def body(a_ref, w_ref, x1_ref, t_ref, g_ref, dffn_ref, dy_ref, vec_ref, loss_ref):
        @pl.when(pl.program_id(0) == 0)
        def _():
            vec_ref[...] = jnp.zeros_like(vec_ref)
            loss_ref[...] = jnp.zeros_like(loss_ref)

        ffn = _dot(a_ref[...], w_ref[...])
        r = _rms(ffn)
        gv = g_ref[...]
        e = x1_ref[...] + ffn * r * gv - t_ref[...]
        loss_ref[...] += jnp.sum(e * e) * (0.5 / D)
        dy = e * (1.0 / D)
        dy_ref[...] = dy
        vec_ref[0:1, :] += jnp.sum(dy * ffn * r, axis=0, keepdims=True)
        dffn_ref[...] = _rms_bwd(ffn, r, gv, dy).astype(MXU)

    rows = pl.BlockSpec((tm, D), lambda i: (i, 0))
    return _pc(
        body, name="ffn_down_loss", grid=(s // tm,),
        in_specs=[pl.BlockSpec((tm, FFN), lambda i: (i, 0)), pl.BlockSpec((FFN, D), lambda i: (0, 0)), rows, rows,
                  pl.BlockSpec((1, D), lambda i: (0, 0))],
        out_specs=[rows, rows, pl.BlockSpec((8, D), lambda i: (0, 0)), pl.BlockSpec((8, LANES), lambda i: (0, 0))],
        out_shape=[_sds((s, D), MXU), _sds((s, D), F32), _sds((8, D), F32), _sds((8, LANES), F32)],
        compiler_params=_cp(1),
    )(act, w_d, x1, tgt, g_fpost)


def _ffn_down_bwd(dffn, w_d, gf, uf):
    s = dffn.shape[0]
    tm = min(TM_ROW, s)

    def body(d_ref, w_ref, gf_ref, uf_ref, dgf_ref, duf_ref):
        da = _dot_nt(d_ref[...], w_ref[...])
        gf = gf_ref[...].astype(F32)
        sg = _sigmoid(gf)
        duf_ref[...] = (da * gf * sg).astype(MXU)
        dgf_ref[...] = (da * uf_ref[...].astype(F32) * (sg * (1.0 + gf * (1.0 - sg)))).astype(MXU)

    wide = pl.BlockSpec((tm, FFN), lambda i: (i, 0))
    return _pc(
        body, name="ffn_down_bwd", grid=(s // tm,),
        in_specs=[pl.BlockSpec((tm, D), lambda i: (i, 0)), pl.BlockSpec((FFN, D), lambda i: (0, 0)), wide, wide],
        out_specs=[wide, wide], out_shape=[_sds((s, FFN), MXU)] * 2, compiler_params=_cp(1),
    )(dffn, w_d, gf, uf)


def _ffn_up_bwd(dgf, duf, w_gu, x1, dy, z, g_fpre, g_post):
    s = x1.shape[0]
    tm = min(TM_ROW, s)

    def body(dgf_ref, duf_ref, w_ref, x1_ref, dy_ref, z_ref, gf_ref, gp_ref, dx1_ref, dz_ref, vec_ref):
        @pl.when(pl.program_id(0) == 0)
        def _():
            vec_ref[...] = jnp.zeros_like(vec_ref)

        dh2 = _dot_nt(dgf_ref[...], w_ref[:, pl.ds(0, FFN)]) + _dot_nt(duf_ref[...], w_ref[:, pl.ds(FFN, FFN)])
        x1 = x1_ref[...]
        r2 = _rms(x1)
        vec_ref[0:1, :] += jnp.sum(dh2 * x1 * r2, axis=0, keepdims=True)
        dx1 = dy_ref[...] + _rms_bwd(x1, r2, gf_ref[...], dh2)
        dx1_ref[...] = dx1
        z = z_ref[...]
        rz = _rms(z)
        vec_ref[1:2, :] += jnp.sum(dx1 * z * rz, axis=0, keepdims=True)
        dz_ref[...] = _rms_bwd(z, rz, gp_ref[...], dx1).astype(MXU)

    rows = pl.BlockSpec((tm, D), lambda i: (i, 0))
    wide = pl.BlockSpec((tm, FFN), lambda i: (i, 0))
    vec = pl.BlockSpec((1, D), lambda i: (0, 0))
    return _pc(
        body, name="ffn_up_bwd", grid=(s // tm,),
        in_specs=[wide, wide, pl.BlockSpec((D, 2 * FFN), lambda i: (0, 0)), rows, rows, rows, vec, vec],
        out_specs=[rows, rows, pl.BlockSpec((8, D), lambda i: (0, 0))],
        out_shape=[_sds((s, D), F32), _sds((s, D), MXU), _sds((8, D), F32)], compiler_params=_cp(1),
    )(dgf, duf, w_gu, x1, dy, z, g_fpre, g_post)


def _merge_bwd(dz, proj, pa, pc, pm, oa, wpa, wpc, wpm, wout):
    s = dz.shape[0]
    tm = min(TM_ROW, s)

    def body(dz_ref, gl_ref, pa_ref, pc_ref, pm_ref, oa_ref, wpa_h, wpc_h, wpm_h, wout_h,
             dpa_ref, dpc_ref, dpm_ref, dgl_ref, doa_ref, doc_ref, dom_ref, dl_ref, wbuf, sems):
        _load_resident([wpa_h, wpc_h, wpm_h, wout_h], wbuf, sems)
        dm = _dot_nt(dz_ref[...], wbuf[3])
        quads = ((pa_ref, dpa_ref, doa_ref), (pc_ref, dpc_ref, doc_ref), (pm_ref, dpm_ref, dom_ref))
        for b, (p_ref, dp_ref, do_ref) in enumerate(quads):
            cols = pl.ds(b * D, D)
            gt = _sigmoid(gl_ref[:, cols].astype(F32))
            dp = (dm * gt).astype(MXU)
            dp_ref[...] = dp
            dgl_ref[:, cols] = (dm * p_ref[...].astype(F32) * gt * (1.0 - gt)).astype(MXU)
            dob = _dot_nt(dp, wbuf[b]).astype(MXU)
            do_ref[...] = dob
            if b == 0:
                prod = dob.astype(F32) * oa_ref[...].astype(F32)
                d_i = lax.broadcasted_iota(jnp.int32, (D, LANES), 0)
                h_i = lax.broadcasted_iota(jnp.int32, (D, LANES), 1)
                sel = jnp.where(lax.shift_right_logical(d_i, DH.bit_length() - 1) == h_i, 1.0, 0.0).astype(jnp.bfloat16)
                dl_ref[...] = _dot_exact_rhs_t(prod, sel)

    rows = pl.BlockSpec((tm, D), lambda i: (i, 0))
    anyspec = pl.BlockSpec(memory_space=pl.ANY)
    wide = pl.BlockSpec((tm, 3 * D), lambda i: (i, 2))
    return _pc(
        body, name="merge_bwd", grid=(s // tm,),
        in_specs=[rows, wide, rows, rows, rows, rows, anyspec, anyspec, anyspec, anyspec],
        out_specs=[rows, rows, rows, pl.BlockSpec((tm, 3 * D), lambda i: (i, 0)), rows, rows, rows,
                   pl.BlockSpec((tm, LANES), lambda i: (i, 0))],
        out_shape=[_sds((s, D), MXU)] * 3 + [_sds((s, 3 * D), MXU)] + [_sds((s, D), MXU)] * 3 + [_sds((s, LANES), F32)],
        scratch_shapes=_resident(4), compiler_params=_cp(1),
    )(dz, proj, pa, pc, pm, oa, wpa, wpc, wpm, wout)


def _dot_exact_rhs_t(v, b01):
    hi, mid, lo = _split3(v)
    return _dot(hi, b01) + _dot(mid, b01) + _dot(lo, b01)


def _in_proj_bwd(pieces, df, w_main, w_f, x, dx1, g_pre):
    s = x.shape[0]
    tm = min(TM_ROW, s)
    n_main = w_main.shape[1]
    np_ = len(pieces)

    def body(*refs):
        p_refs = refs[:np_]
        df_ref, x_ref, dx1_ref, g_ref, w_h, wf_ref, dx_ref, vec_ref, wbuf, sem = refs[np_:]

        @pl.when(pl.program_id(0) == 0)
        def _():
            vec_ref[...] = jnp.zeros_like(vec_ref)
            cp = pltpu.make_async_copy(w_h, wbuf, sem)
            cp.start()
            cp.wait()

        dh = _dot_nt(df_ref[...], wf_ref[...])
        for p_ref, (_, c0, nc) in zip(p_refs, pieces):
            dh = dh + _dot_nt(p_ref[...], wbuf[:, pl.ds(c0 * D, nc * D)])
        xv = x_ref[...]
        r = _rms(xv)
        vec_ref[0:1, :] += jnp.sum(dh * xv * r, axis=0, keepdims=True)
        dx_ref[...] = dx1_ref[...] + _rms_bwd(xv, r, g_ref[...], dh)

    rows = pl.BlockSpec((tm, D), lambda i: (i, 0))
    p_specs = [pl.BlockSpec((tm, nc * D), lambda i: (i, 0)) for _, _, nc in pieces]
    return _pc(
        body, name="in_proj_bwd", grid=(s // tm,),
        in_specs=p_specs + [pl.BlockSpec((tm, LANES), lambda i: (i, 0)), rows, rows, pl.BlockSpec((1, D), lambda i: (0, 0)),
                            pl.BlockSpec(memory_space=pl.ANY), pl.BlockSpec((D, LANES), lambda i: (0, 0))],
        out_specs=[rows, pl.BlockSpec((8, D), lambda i: (0, 0))],
        out_shape=[_sds((s, D), F32), _sds((8, D), F32)],
        scratch_shapes=[pltpu.VMEM((D, n_main), MXU), pltpu.SemaphoreType.DMA], compiler_params=_cp(1),
    )(*[p for p, _, _ in pieces], df, x, dx1, g_pre, w_main, w_f)


def _wgrad(xa, dy, name):
    s, k = xa.shape
    n = dy.shape[1]
    ts = min(TS_WG, s)
    tk = _tile(k, WG_CAP)
    tn = _tile(n, WG_CAP)

    def body(x_ref, dy_ref, o_ref):
        @pl.when(pl.program_id(2) == 0)
        def _():
            o_ref[...] = jnp.zeros_like(o_ref)

        o_ref[...] += _dot_tn(x_ref[...], dy_ref[...])

    return _pc(
        body, name=name, grid=(k // tk, n // tn, s // ts),
        in_specs=[pl.BlockSpec((ts, tk), lambda a, b, c: (c, a)), pl.BlockSpec((ts, tn), lambda a, b, c: (c, b))],
        out_specs=pl.BlockSpec((tk, tn), lambda a, b, c: (a, b)), out_shape=_sds((k, n), F32), compiler_params=_cp(3),
    )(xa, dy)


def _pair_sum(g, r1, c_idx):
    _, _, hr, cols = g.shape
    tr = _rowtile(hr, cols)

    def body(c_ref, g_ref, r_ref, o_ref):
        o_ref[0] = (g_ref[0, 0].astype(F32) + r_ref[0].astype(F32)).astype(WIRE)

    return _pc(
        body, name="pair_sum_%dx%d" % (hr, cols), out_shape=_sds((N_CHIPS, hr, cols), WIRE),
        grid_spec=pltpu.PrefetchScalarGridSpec(
            num_scalar_prefetch=1, grid=(N_CHIPS, hr // tr),
            in_specs=[pl.BlockSpec((1, 1, tr, cols), lambda d, i, c: (d, c[0], i, 0)),
                      pl.BlockSpec((1, tr, cols), lambda d, i, c: (d, i, 0))],
            out_specs=pl.BlockSpec((1, tr, cols), lambda d, i, c: (d, i, 0))),
        compiler_params=_cp(2),
    )(c_idx, g, r1)


def _chip_sum(r2, slot, base=None):
    _, hr, cols = r2.shape
    tr = _rowtile(hr, cols)

    def body(s_ref, r_ref, *rest):
        o_ref = rest[-1]
        acc = r_ref[0].astype(F32)
        for d in range(1, N_CHIPS):
            acc = acc + r_ref[d].astype(F32)
        o_ref[0] = acc

    based = base is not None
    return _pc(
        body, name="chip_sum_%dx%d_%d" % (hr, cols, int(based)), out_shape=_sds((2, hr, cols), F32),
        grid_spec=pltpu.PrefetchScalarGridSpec(
            num_scalar_prefetch=1, grid=(hr // tr,),
            in_specs=[pl.BlockSpec((N_CHIPS, tr, cols), lambda i, s: (0, i, 0))] + ([ANY] if based else []),
            out_specs=pl.BlockSpec((1, tr, cols), lambda i, s: (s[0], i, 0))),
        input_output_aliases={2: 0} if based else {}, compiler_params=_cp(1),
    )(*((slot, r2, base) if based else (slot, r2)))


def _adamw(w, g, m, v):
    rows, cols = w.shape
    tr = _rowtile(rows, cols, 1 << 19)
    c1 = 1.0 / (1.0 - B1 ** STEP)
    c2 = 1.0 / (1.0 - B2 ** STEP)

    def body(w_ref, g_ref, m_ref, v_ref, d_ref, mo_ref, vo_ref):
        gv = g_ref[...]
        mn = B1 * m_ref[...] + (1.0 - B1) * gv
        vn = B2 * v_ref[...] + (1.0 - B2) * (gv * gv)
        mo_ref[...] = mn
        vo_ref[...] = vn
        d_ref[...] = -LR * ((mn * c1) / (jnp.sqrt(vn * c2) + ADAM_EPS) + WD * w_ref[...])

    blk = pl.BlockSpec((tr, cols), lambda i: (i, 0))
    return _pc(
        body, name="adamw_%dx%d" % (rows, cols), grid=(rows // tr,), in_specs=[blk] * 4, out_specs=[blk] * 3,
        out_shape=[_sds((rows, cols), F32)] * 3, compiler_params=_cp(1),
    )(w, g, m, v)


MESH_ID = pl.DeviceIdType.MESH
ANY = pl.BlockSpec(memory_space=pl.ANY)


def _place():
    x, y, c = lax.axis_index("x"), lax.axis_index("y"), lax.axis_index("c")
    others = [(1 - x, y), (x, 1 - y), (1 - x, 1 - y)]
    return x, y, c, others


def _remote(src, dst, sems, idx, to):
    return pltpu.make_async_remote_copy(src_ref=src, dst_ref=dst, send_sem=sems[0].at[idx], recv_sem=sems[1].at[idx],
                                        device_id=to, device_id_type=MESH_ID)


def _gather_plan(shards):
    nk = len(shards)

    def copies(ins, outs, sems):
        x, y, c, others = _place()
        me = 2 * x + y
        sib = (x, y, 1 - c)
        local = [pltpu.make_async_copy(ins[k], outs[k].at[me], sems[2].at[k]) for k in range(nk)]
        ici, landed, fwd, fwd_landed = [], [], [], []
        for k in range(nk):
            hr = shards[k].shape[0] // 2
            for r, (cx, cy) in enumerate(others):
                mine = pl.ds(c * hr, hr)
                ici.append(_remote(ins[k].at[mine], outs[k].at[me, mine], sems, 6 * k + r, (cx, cy, c)))
                got = outs[k].at[2 * cx + cy, mine]
                landed.append(_remote(got, got, sems, 6 * k + r, (cx, cy, c)))
                fwd.append(_remote(got, got, sems, 6 * k + 3 + r, sib))
                theirs = outs[k].at[2 * cx + cy, pl.ds((1 - c) * hr, hr)]
                fwd_landed.append(_remote(theirs, theirs, sems, 6 * k + 3 + r, sib))
        return local, ici, landed, fwd, fwd_landed

    def start(ins, outs, sems):
        local, ici, _, _, _ = copies(ins, outs, sems)
        for cp in local + ici:
            cp.start()

    def forward(ins, outs, sems):
        _, _, landed, fwd, _ = copies(ins, outs, sems)
        for got, cp in zip(landed, fwd):
            got.wait_recv()
            cp.start()

    def finish(ins, outs, sems):
        local, ici, _, fwd, fwd_landed = copies(ins, outs, sems)
        for got in fwd_landed:
            got.wait_recv()
        for cp in ici + fwd:
            cp.wait_send()
        for cp in local:
            cp.wait()

    return dict(
        arrays=list(shards), out_shape=[_sds((N_CHIPS,) + a.shape, a.dtype) for a in shards],
        scratch=[pltpu.SemaphoreType.DMA((6 * nk,)), pltpu.SemaphoreType.DMA((6 * nk,)), pltpu.SemaphoreType.DMA((nk,))],
        phases=[start, forward, finish])


def _scatter_plan(ps):
    nk = len(ps)

    def copies(ins, outs, sems):
        x, y, c, others = _place()
        me = 2 * x + y
        local = [pltpu.make_async_copy(ins[k].at[me], outs[k].at[me], sems[2].at[k]) for k in range(nk)]
        ici, landed = [], []
        for k in range(nk):
            for r, (cx, cy) in enumerate(others):
                ici.append(_remote(ins[k].at[2 * cx + cy], outs[k].at[me], sems, 3 * k + r, (cx, cy, c)))
                got = outs[k].at[2 * cx + cy]
                landed.append(_remote(got, got, sems, 3 * k + r, (cx, cy, c)))
        return local, ici, landed

    def start(ins, outs, sems):
        local, ici, _ = copies(ins, outs, sems)
        for cp in local + ici:
            cp.start()

    def finish(ins, outs, sems):
        local, ici, landed = copies(ins, outs, sems)
        for got in landed:
            got.wait_recv()
        for cp in ici:
            cp.wait_send()
        for cp in local:
            cp.wait()

    return dict(
        arrays=list(ps), out_shape=[_sds(a.shape, a.dtype) for a in ps],
        scratch=[pltpu.SemaphoreType.DMA((3 * nk,)), pltpu.SemaphoreType.DMA((3 * nk,)), pltpu.SemaphoreType.DMA((nk,))],
        phases=[start, finish])


def _run_plan(plan, name):
    nk = len(plan["arrays"])

    def body(*refs):
        ins, outs, sems = refs[:nk], refs[nk:2 * nk], refs[2 * nk:]
        for phase in plan["phases"]:
            phase(ins, outs, sems)

    return _pc(body, name=name, in_specs=[ANY] * nk, out_specs=[ANY] * nk, out_shape=plan["out_shape"],
               scratch_shapes=plan["scratch"])(*plan["arrays"])


def _host_plan(plan, ins, outs, sems, first, middle, last):
    points = [first, last] if len(plan["phases"]) == 2 else [first, middle, last]
    for phase, at in zip(plan["phases"], points):
        pl.when(at)(functools.partial(phase, ins, outs, sems))


def _swap_sibling(gs, halves, tag):
    nk = len(gs)

    def body(*refs):
        ins, outs = refs[:nk], refs[nk:2 * nk]
        send_sems, recv_sems = refs[2 * nk:]
        x, y, c, _ = _place()
        cps = []
        for k in range(nk):
            hr = gs[k].shape[1] // 2
            cp = pltpu.make_async_remote_copy(
                src_ref=ins[k].at[:, pl.ds((1 - c) * hr, hr)] if halves else ins[k], dst_ref=outs[k],
                send_sem=send_sems.at[k], recv_sem=recv_sems.at[k], device_id=(x, y, 1 - c), device_id_type=MESH_ID)
            cp.start()
            cps.append(cp)
        for cp in cps:
            cp.wait()

    return _pc(
        body, name=("swap_halves_" if halves else "swap_slabs_") + tag, in_specs=[ANY] * nk, out_specs=[ANY] * nk,
        out_shape=[_sds((N_CHIPS, a.shape[1] // 2 if halves else a.shape[1], a.shape[2]), a.dtype) for a in gs],
        scratch_shapes=[pltpu.SemaphoreType.DMA((nk,)), pltpu.SemaphoreType.DMA((nk,))],
    )(*gs)


def _allreduce_small(v):
    rows, cols = v.shape

    def body(v_ref, o_ref, gath, send_sems, recv_sems):
        x, y, c, others = _place()
        sib = (x, y, 1 - c)

        def slot(px, py, pc):
            return gath.at[4 * px + 2 * py + pc]

        def copy(k, block, to, src=None):
            return pltpu.make_async_remote_copy(
                src_ref=slot(*block) if src is None else src, dst_ref=slot(*block),
                send_sem=send_sems.at[k], recv_sem=recv_sems.at[k], device_id=to, device_id_type=MESH_ID)

        me = (x, y, c)
        gath[4 * x + 2 * y + c] = v_ref[...]
        first = [copy(0, me, sib, src=v_ref)]
        first += [copy(1 + r, me, (cx, cy, c), src=v_ref) for r, (cx, cy) in enumerate(others)]
        for cp in first:
            cp.start()
        passed = [copy(4 + r, (cx, cy, c), sib) for r, (cx, cy) in enumerate(others)]
        for r, (cx, cy) in enumerate(others):
            copy(1 + r, (cx, cy, c), me).wait_recv()
            passed[r].start()
        copy(0, (x, y, 1 - c), me).wait_recv()
        for r, (cx, cy) in enumerate(others):
            copy(4 + r, (cx, cy, 1 - c), me).wait_recv()
        for cp in first + passed:
            cp.wait_send()
        acc = gath[0]
        for d in range(1, N_DEV):
            acc = acc + gath[d]
        o_ref[...] = acc

    vm = pl.BlockSpec(memory_space=pltpu.VMEM)
    return _pc(
        body, name="allreduce_small", in_specs=[vm], out_specs=vm, out_shape=_sds((rows, cols), F32),
        scratch_shapes=[pltpu.VMEM((N_DEV, rows, cols), F32), pltpu.SemaphoreType.DMA((7,)), pltpu.SemaphoreType.DMA((7,))],
    )(v)


def _cols_to_chips(a):
    r, c4 = a.shape
    return a.reshape(r, N_CHIPS, c4 // N_CHIPS).transpose(1, 0, 2)


def _chips_to_cols(a):
    n, r, c = a.shape
    return a.transpose(1, 0, 2).reshape(r, n * c)


def _head_rows(a, tq):
    s = a.shape[0]
    return a[:, :HF].reshape(s // tq, tq, NP, 2).transpose(2, 0, 3, 1)


def _head_cols(a):
    s = a.shape[0]
    return a[:, :HF].reshape(s, NP, 2).transpose(1, 0, 2)


def _to_wire(gs, c_idx, tag):
    gs = [g.astype(WIRE) for g in gs]
    r1 = _swap_sibling(gs, True, tag)
    return [_pair_sum(g.reshape(N_CHIPS, 2, g.shape[1] // 2, g.shape[2]), r, c_idx) for g, r in zip(gs, r1)]


def _local_step(x, mem, tgt, sp, w_main, w_f, rest_shards, c_idx):
    s = x.shape[0]
    tq = min(TQ, s)
    b_f = jnp.pad(sp["b_forget"], ((0, 0), (0, LANES - HF)))
    proj, h, flog = _in_proj(x, sp["norm_mix_pre"], w_main, w_f)
    cf = _logf_cumsum(flog, b_f)
    qx, kx = _fox_prep(proj, _head_cols(cf))
    oa, lse, g_cw, g_kv, g_pj, g_gu, g_d = _fox_fwd(proj, qx, kx, _gather_plan(rest_shards))
    pj = g_pj.reshape(N_CHIPS, 4, D // N_CHIPS, D).transpose(1, 0, 2, 3).reshape(4, D, D)
    w = {"conv_w": _chips_to_cols(g_cw), "w_kv": _chips_to_cols(g_kv), "wpa": pj[0], "wpc": pj[1], "wpm": pj[2],
         "wout": pj[3], "w_gu": _chips_to_cols(g_gu), "w_d": g_d.reshape(FFN, D)}
    y, oc = _conv_fwd(proj, w["conv_w"], sp["conv_b"], sp["conv_ln_g"], sp["conv_ln_b"])
    mem_n, kv = _mem_kv(mem, sp["norm_mem"], w["w_kv"])
    om = _mem_attn_fwd(proj, kv)
    pa, pc, pm, merged, z, x1, h2 = _merge_out(oa, oc, om, proj, x, w["wpa"], w["wpc"], w["wpm"], w["wout"],
                                              sp["norm_mix_post"], sp["norm_ffn_pre"])
    gf, uf, act = _ffn_up(h2, w["w_gu"])
    dffn, dy, vec_f, loss_blk = _ffn_down_loss(act, w["w_d"], x1, tgt, sp["norm_ffn_post"])

    dgf, duf = _ffn_down_bwd(dffn, w["w_d"], gf, uf)
    dx1, dz, vec_n = _ffn_up_bwd(dgf, duf, w["w_gu"], x1, dy, z, sp["norm_ffn_pre"], sp["norm_mix_post"])
    dpa, dpc, dpm, dgl, doa, doc, dom, delta = _merge_bwd(dz, proj, pa, pc, pm, oa, w["wpa"], w["wpc"], w["wpm"], w["wout"])
    pj_g = jnp.stack([_wgrad(oa, dpa, "wgrad_pa"), _wgrad(oc, dpc, "wgrad_pc"), _wgrad(om, dpm, "wgrad_pm"),
                      _wgrad(merged, dz, "wgrad_out")]).reshape(4, N_CHIPS, D // N_CHIPS, D)
    early = [pj_g.transpose(1, 0, 2, 3).reshape(N_CHIPS, D, D),
             _cols_to_chips(jnp.concatenate([_wgrad(h2, dgf, "wgrad_g"), _wgrad(h2, duf, "wgrad_u")], axis=1)),
             _wgrad(act, dffn, "wgrad_d").reshape(N_CHIPS, FFN // N_CHIPS, D)]
    lse16 = lse.transpose(1, 0, 2).reshape(s, HF)
    dq, dqs, dk, dks, dv, *r2_early = _fox_bwd(proj, qx, kx, doa, _head_rows(lse16, tq), _head_rows(delta, tq),
                                              _scatter_plan(_to_wire(early, c_idx, "early")))
    over_keys = dqs[:, X_KONE:X_KONE + 2, :].transpose(2, 0, 1).reshape(s, HF)
    over_queries = dks[:, :, X_QONE:X_QONE + 2, :].transpose(1, 3, 0, 2).reshape(s, HF)
    dc = jnp.pad(over_keys - over_queries, ((0, 0), (0, LANES - HF)))
    df, db_blk = _logf_cumsum_bwd(flog, b_f, dc)
    dga, dgg, dcw, vec_c = _conv_bwd(proj, y, doc, w["conv_w"], sp["conv_ln_g"], sp["conv_ln_b"])
    dqm, dkv = _mem_attn_bwd(proj, kv, dom)
    dkv_b = dkv.astype(MXU)
    vec_m = _mem_kv_bwd(mem, sp["norm_mem"], w["w_kv"], dkv_b)
    pieces = [(dq, 0, 1), (dk, 1, 1), (dv, 2, 1), (dga, 3, 1), (dgg, 4, 1), (dqm, 5, 1), (dgl, 6, 3)]
    dx, vec_p = _in_proj_bwd(pieces, df, w_main, w_f, x, dx1, sp["norm_mix_pre"])

    dw_cols = [_wgrad(h, p, "wgrad_in_%d" % c0) for p, c0, _ in pieces]
    dwf = _wgrad(h, df, "wgrad_in_f")
    late = [_cols_to_chips(jnp.concatenate(dw_cols[:3] + [dwf[:, :HF]] + dw_cols[3:], axis=1)), _cols_to_chips(dcw),
            _cols_to_chips(_wgrad(mem_n, dkv_b, "wgrad_kv"))]
    r2_late = _run_plan(_scatter_plan(_to_wire(late, c_idx, "late")), "exchange_late")
    zero_row = jnp.zeros((1, D), F32)
    small = jnp.concatenate([
        vec_p[0:1], vec_n[1:2], vec_m[0:1], vec_c[2:3], vec_c[0:1], vec_c[1:2], vec_n[0:1], vec_f[0:1],
        jnp.pad(db_blk[0:1, :HF], ((0, 0), (0, D - HF))),
        jnp.pad(loss_blk[0:1, 0:1], ((0, 0), (0, D - 1))),
    ] + [zero_row] * (SMALL_ROWS - 10), axis=0)
    return dx, list(r2_late) + list(r2_early), small


SMALL_NAMES = ["norm_mix_pre", "norm_mix_post", "norm_mem", "conv_b", "conv_ln_g", "conv_ln_b", "norm_ffn_pre", "norm_ffn_post"]
PROJ_NAMES = ["w_proj_attn", "w_proj_conv", "w_proj_mem", "w_out"]
WEIGHT_ORDER = ["norm_mix_pre", "norm_mix_post", "norm_mem", "w_in", "b_forget", "conv_w", "conv_b", "conv_ln_g", "conv_ln_b",
                "w_kv_mem", "w_proj_attn", "w_proj_conv", "w_proj_mem", "w_out", "norm_ffn_pre", "norm_ffn_post",
                "w_gate_up", "w_down"]


def _pack_small(p):
    rows = [p[n] for n in SMALL_NAMES] + [jnp.pad(p["b_forget"], ((0, 0), (0, D - HF)))]
    return jnp.concatenate(rows + [jnp.zeros((SMALL_ROWS - len(rows), D), F32)], axis=0)


def _step(params, moms, vels, x, mem, tgt):
    c_idx = lax.axis_index("c").astype(jnp.int32).reshape(1)

    (g_in,) = _run_plan(_gather_plan([params["w_in"].astype(WIRE)]), "gather_w_in")
    w_in_full = _chips_to_cols(g_in)
    w_main = jnp.concatenate([w_in_full[:, :3 * D], w_in_full[:, 3 * D + HF:]], axis=1)
    w_f = jnp.pad(w_in_full[:, 3 * D:3 * D + HF], ((0, 0), (0, LANES - HF)))
    rest = [jnp.pad(params["conv_w"], ((0, CWP - CW), (0, 0))), params["w_kv_mem"].astype(WIRE),
            jnp.concatenate([params[n] for n in PROJ_NAMES], axis=0).astype(WIRE),
            params["w_gate_up"].astype(WIRE), params["w_down"].astype(WIRE)]

    dx, r2, small = _local_step(x, mem, tgt, params, w_main, w_f, rest, c_idx)

    r2_sib = _swap_sibling(r2, False, "all")
    full = [_chip_sum(theirs, 1 - c_idx, _chip_sum(mine, c_idx)) for mine, theirs in zip(r2, r2_sib)]
    red = [f.reshape(2 * f.shape[1], f.shape[2]) for f in full]
    pj_r = red[3].reshape(4, D // N_CHIPS, D)
    grads = {"w_in": red[0], "conv_w": red[1][:CW], "w_kv_mem": red[2], "w_gate_up": red[4], "w_down": red[5]}
    for i, n in enumerate(PROJ_NAMES):
        grads[n] = pj_r[i]

    tot = _allreduce_small(small)
    loss = tot[9, 0]
    for i, n in enumerate(SMALL_NAMES):
        grads[n] = tot[i:i + 1]
    grads["b_forget"] = tot[8:9, :HF]

    delta, new_m, new_v = {}, {}, {}
    ds, ms, vs = _adamw(_pack_small(params), tot.at[9:].set(0.0), _pack_small(moms), _pack_small(vels))
    for i, n in enumerate(SMALL_NAMES):
        delta[n], new_m[n], new_v[n] = ds[i:i + 1], ms[i:i + 1], vs[i:i + 1]
    delta["b_forget"], new_m["b_forget"], new_v["b_forget"] = ds[8:9, :HF], ms[8:9, :HF], vs[8:9, :HF]
    for n in ["w_in", "conv_w", "w_kv_mem", "w_gate_up", "w_down"] + PROJ_NAMES:
        delta[n], new_m[n], new_v[n] = _adamw(params[n], grads[n], moms[n], vels[n])
    return loss, dx, grads, delta, new_m, new_v


def kernel(x, mem, norm_mix_pre, norm_mix_post, norm_mem, w_in, b_forget, conv_w, conv_b, conv_ln_g, conv_ln_b, w_kv_mem, w_proj_attn, w_proj_conv, w_proj_mem, w_out, norm_ffn_pre, norm_ffn_post, w_gate_up, w_down, loss_target, m_norm_mix_pre, m_norm_mix_post, m_norm_mem, m_w_in, m_b_forget, m_conv_w, m_conv_b, m_conv_ln_g, m_conv_ln_b, m_w_kv_mem, m_w_proj_attn, m_w_proj_conv, m_w_proj_mem, m_w_out, m_norm_ffn_pre, m_norm_ffn_post, m_w_gate_up, m_w_down, v_norm_mix_pre, v_norm_mix_post, v_norm_mem, v_w_in, v_b_forget, v_conv_w, v_conv_b, v_conv_ln_g, v_conv_ln_b, v_w_kv_mem, v_w_proj_attn, v_w_proj_conv, v_w_proj_mem, v_w_out, v_norm_ffn_pre, v_norm_ffn_post, v_w_gate_up, v_w_down):
    local = dict(locals())
    lead = {n: local[n].shape[:-2] for n in WEIGHT_ORDER}
    two_d = lambda a: a.reshape(a.shape[-2:])
    params = {n: two_d(local[n]) for n in WEIGHT_ORDER}
    moms = {n: two_d(local["m_" + n]) for n in WEIGHT_ORDER}
    vels = {n: two_d(local["v_" + n]) for n in WEIGHT_ORDER}
    loss, dx, grads, delta, new_m, new_v = _step(params, moms, vels, two_d(x), two_d(mem), two_d(loss_target))
    outs = [loss, dx.reshape(x.shape)]
    for group in (grads, delta, new_m, new_v):
        outs += [group[n].reshape(lead[n] + group[n].shape) for n in WEIGHT_ORDER]
    return tuple(outs)
```

```python
import functools

import jax
import jax.numpy as jnp
from jax import lax
from jax.experimental import pallas as pl
from jax.experimental.pallas import tpu as pltpu

F32 = jnp.float32
MXU = jnp.bfloat16
WIRE = jnp.bfloat16

D = 1024
HF = 16
DH = 64
NP = D // 128
MEM_H = 4
MEM_DH = D // MEM_H
FFN = 2816
CW = 31
CWP = 32
HALO = 32
RMS_EPS = 1e-6
LN_EPS = 1e-5
LR, B1, B2, ADAM_EPS, WD, STEP = 0.001, 0.9, 0.999, 1e-8, 0.01, 10

N_CHIPS = 4
N_DEV = 8
LANES = 128
VMEM_LIMIT = 56 * 1024 * 1024

TM_PROJ = 512
NB_PROJ = 3
TQ = 1024
LOG2E = 1.4426950408889634
LN2 = 0.6931471805599453
QSCALE = DH ** -0.5 * LOG2E
X_BIAS = 0
X_QONE = 6
X_KONE = 8
X_ROWS = 16
TM_CONV = 256
CONV_ROWS = 128
SUB = 8
TM_ROW = 256
TM_WIDE = 2048
TS_WG = 1024
WG_CAP = 1408
SMALL_ROWS = 16


def _pc(body, **kw):
    return pl.pallas_call(body, **kw)


def _cp(n_axes):
    return pltpu.CompilerParams(dimension_semantics=("arbitrary",) * n_axes, vmem_limit_bytes=VMEM_LIMIT)


def _sds(shape, dtype):
    return jax.ShapeDtypeStruct(shape, dtype)


def _dot(a, b):
    return jnp.dot(a, b, preferred_element_type=F32)


def _dot_nt(a, b):
    return lax.dot_general(a, b, (((1,), (1,)), ((), ())), preferred_element_type=F32)


def _dot_tn(a, b):
    return lax.dot_general(a, b, (((0,), (0,)), ((), ())), preferred_element_type=F32)


def _rms(u):
    return lax.rsqrt(jnp.mean(u * u, axis=-1, keepdims=True) + RMS_EPS)


def _rms_bwd(u, r, g, dn):
    w = dn * g
    return r * w - u * (r * r * r) * jnp.mean(u * w, axis=-1, keepdims=True)


def _sigmoid(z):
    return 1.0 / (1.0 + jnp.exp(-z))


def _tile(n, cap):
    if n <= cap:
        return n
    best = None
    for t in range(LANES, cap + 1, LANES):
        if n % t == 0:
            best = t
    assert best is not None, (n, cap)
    return best


def _rowtile(rows, cols, cap_bytes=1 << 20):
    best = None
    for t in range(8, rows + 1, 8):
        if rows % t == 0 and t * cols * 4 <= cap_bytes:
            best = t
    return best if best is not None else rows


def _split3(v):
    hi = v.astype(jnp.bfloat16)
    r1 = v - hi.astype(F32)
    mid = r1.astype(jnp.bfloat16)
    lo = (r1 - mid.astype(F32)).astype(jnp.bfloat16)
    return hi, mid, lo


def _dot_exact_rhs(a01, v):
    hi, mid, lo = _split3(v)
    return _dot(a01, hi) + _dot(a01, mid) + _dot(a01, lo)


def _in_proj(x, g_pre, w_main, w_f):
    s, d = x.shape
    n = w_main.shape[1]
    tm = min(TM_PROJ, s)
    tn = n // NB_PROJ

    def body(x_ref, g_ref, w_ref, wf_ref, proj_ref, h_ref, flog_ref, hs):
        @pl.when(pl.program_id(1) == 0)
        def _():
            xv = x_ref[...]
            h = (xv * _rms(xv) * g_ref[...]).astype(MXU)
            hs[...] = h
            h_ref[...] = h
            flog_ref[...] = _dot(h, wf_ref[...])

        res = _dot(hs[...], w_ref[...])

        @pl.when(pl.program_id(1) == 0)
        def _():
            proj_ref[:, pl.ds(0, d)] = (res[:, :d] * QSCALE).astype(MXU)
            proj_ref[:, pl.ds(d, tn - d)] = res[:, d:].astype(MXU)

        @pl.when(pl.program_id(1) != 0)
        def _():
            proj_ref[...] = res.astype(MXU)

    assert tn >= d
    return _pc(
        body, name="in_proj", grid=(s // tm, NB_PROJ),
        in_specs=[pl.BlockSpec((tm, d), lambda i, j: (i, 0)), pl.BlockSpec((1, d), lambda i, j: (0, 0)),
                  pl.BlockSpec((d, tn), lambda i, j: (0, j)), pl.BlockSpec((d, LANES), lambda i, j: (0, 0))],
        out_specs=[pl.BlockSpec((tm, tn), lambda i, j: (i, j)), pl.BlockSpec((tm, d), lambda i, j: (i, 0)),
                   pl.BlockSpec((tm, LANES), lambda i, j: (i, 0))],
        out_shape=[_sds((s, n), MXU), _sds((s, d), MXU), _sds((s, LANES), F32)],
        scratch_shapes=[pltpu.VMEM((tm, d), MXU)], compiler_params=_cp(2),
    )(x, g_pre, w_main, w_f)


def _log_sigmoid(z):
    e = jnp.exp(-jnp.abs(z))
    log1p_e = jnp.where(e < 1e-3, e * (1.0 - 0.5 * e), jnp.log(1.0 + e))
    return jnp.minimum(z, 0.0) - log1p_e


def _logf_cumsum(flog, b_f):
    s = flog.shape[0]
    ch = LANES

    def body(f_ref, b_ref, c_ref):
        r = lax.broadcasted_iota(jnp.int32, (ch, ch), 0)
        q = lax.broadcasted_iota(jnp.int32, (ch, ch), 1)
        tri = jnp.where(r >= q, 1.0, 0.0).astype(jnp.bfloat16)

        def step(i, carry):
            rows = pl.ds(pl.multiple_of(i * ch, ch), ch)
            lf = _log_sigmoid(f_ref[rows, :] + b_ref[...])
            c_ref[rows, :] = _dot_exact_rhs(tri, lf) + carry
            return carry + jnp.sum(lf, axis=0, keepdims=True)

        lax.fori_loop(0, s // ch, step, jnp.zeros((1, LANES), F32))

    return _pc(body, name="logf_cumsum", out_shape=_sds((s, LANES), F32),
               compiler_params=pltpu.CompilerParams(vmem_limit_bytes=VMEM_LIMIT))(flog, b_f)


def _logf_cumsum_bwd(flog, b_f, dc):
    s = flog.shape[0]
    ch = LANES

    def body(f_ref, b_ref, dc_ref, df_ref, db_ref):
        r = lax.broadcasted_iota(jnp.int32, (ch, ch), 0)
        q = lax.broadcasted_iota(jnp.int32, (ch, ch), 1)
        tri = jnp.where(r <= q, 1.0, 0.0).astype(jnp.bfloat16)
        nch = s // ch

        def step(t, carry):
            tail, dbsum = carry
            i = nch - 1 - t
            rows = pl.ds(pl.multiple_of(i * ch, ch), ch)
            dcv = dc_ref[rows, :]
            dlf = _dot_exact_rhs(tri, dcv) + tail
            z = f_ref[rows, :] + b_ref[...]
            df = dlf * _sigmoid(-z)
            df_ref[rows, :] = df.astype(MXU)
            return tail + jnp.sum(dcv, axis=0, keepdims=True), dbsum + jnp.sum(df, axis=0, keepdims=True)

        zero = jnp.zeros((1, LANES), F32)
        _, dbsum = lax.fori_loop(0, nch, step, (zero, zero))
        db_ref[...] = jnp.broadcast_to(dbsum, db_ref.shape)

    return _pc(body, name="logf_cumsum_bwd", out_shape=[_sds((s, LANES), MXU), _sds((8, LANES), F32)],
               compiler_params=pltpu.CompilerParams(vmem_limit_bytes=VMEM_LIMIT))(flog, b_f, dc)


def _head_masks(rows):
    lane = lax.broadcasted_iota(jnp.int32, (rows, LANES), 1)
    return lane < DH, lane >= DH


def _ext_masks(rows, key_side):
    lane = lax.broadcasted_iota(jnp.int32, (rows, 2 * LANES), 1)
    ext = lane - LANES
    out = []
    for a in range(2):
        head = (lane >= a * DH) & (lane < (a + 1) * DH)
        bias = (ext >= X_BIAS + 3 * a) & (ext < X_BIAS + 3 * a + 3)
        one = ext == (X_KONE if key_side else X_QONE) + a
        out.append(head | bias | one)
    return out


def _fox_prep(proj, ccol):
    s = proj.shape[0]
    tm = min(TM_WIDE, s)

    def body(q_ref, k_ref, c_ref, qx_ref, kx_ref):
        lane = lax.broadcasted_iota(jnp.int32, (tm, LANES), 1)
        qx_ref[:, pl.ds(0, LANES)] = q_ref[...]
        qx_ref[:, pl.ds(LANES, LANES)] = jnp.where(lane < X_QONE + 2, 1.0, 0.0).astype(MXU)
        kext = jnp.where((lane >= X_KONE) & (lane < X_KONE + 2), 1.0, 0.0).astype(jnp.bfloat16)
        for a in range(2):
            terms = _split3(c_ref[0, :, a:a + 1] * (-LOG2E))
            for t, term in enumerate(terms):
                kext = jnp.where(lane == X_BIAS + 3 * a + t, term, kext)
        kx_ref[:, pl.ds(0, LANES)] = k_ref[...]
        kx_ref[:, pl.ds(LANES, LANES)] = kext.astype(MXU)

    wide = pl.BlockSpec((tm, 2 * LANES), lambda p, i: (i, p))
    return _pc(
        body, name="fox_prep", grid=(NP, s // tm),
        in_specs=[pl.BlockSpec((tm, LANES), lambda p, i: (i, p)), pl.BlockSpec((tm, LANES), lambda p, i: (i, NP + p)),
                  pl.BlockSpec((1, tm, 2), lambda p, i: (p, i, 0))],
        out_specs=[wide, wide], out_shape=[_sds((s, NP * 2 * LANES), MXU)] * 2, compiler_params=_cp(2),
    )(proj, proj, ccol)


def _fox_fwd(proj, qx, kx, plan):
    s = proj.shape[0]
    tq = min(TQ, s)
    nq = s // tq
    nx = len(plan["arrays"])

    def body(*refs):
        q_ref, k_ref, v_ref = refs[:3]
        o_ref, lse_ref = refs[3 + nx:5 + nx]
        p_id, i = pl.program_id(0), pl.program_id(1)
        _host_plan(plan, refs[3:3 + nx], refs[5 + nx:5 + 2 * nx], refs[5 + 2 * nx:], (p_id == 0) & (i == 0),
                   (p_id == NP // 2) & (i == 0), (p_id == NP - 1) & (i == nq - 1))
        qv = q_ref[...]
        qmask = _ext_masks(tq, False)
        hmask = _head_masks(tq)
        qas = [jnp.where(qmask[a], qv, jnp.zeros_like(qv)) for a in range(2)]
        row = lax.broadcasted_iota(jnp.int32, (tq, tq), 0)
        col = lax.broadcasted_iota(jnp.int32, (tq, tq), 1)

        def blk(j, carry, diag=False):
            rows = pl.ds(pl.multiple_of(j * tq, tq), tq)
            kj = k_ref[rows, :]
            vj = v_ref[rows, :]
            out = []
            for a in range(2):
                m, acc = carry[a]
                sc = _dot_nt(qas[a], kj)
                if diag:
                    sc = jnp.where(row >= col, sc, -jnp.inf)
                m_new = jnp.maximum(m, jnp.max(sc, axis=-1, keepdims=True))
                p = jnp.exp2(sc - m_new)
                va = jnp.where(hmask[a], vj, jnp.ones_like(vj))
                out.append((m_new, jnp.exp2(m - m_new) * acc + _dot(p.astype(MXU), va)))
            return tuple(out)

        init = (jnp.full((tq, 1), -jnp.inf, F32), jnp.zeros((tq, LANES), F32))
        res = blk(i, lax.fori_loop(0, i, blk, (init, init)), True)
        lane = lax.broadcasted_iota(jnp.int32, (tq, LANES), 1)
        outs, lses = [], []
        for a in range(2):
            m, acc = res[a]
            l = jnp.sum(jnp.where(lane == DH * (1 - a), acc, 0.0), axis=-1, keepdims=True)
            outs.append(acc / l)
            lses.append(m + jnp.log(l) * LOG2E)
        o_ref[...] = jnp.where(hmask[0], outs[0], outs[1]).astype(MXU)
        lane2 = lax.broadcasted_iota(jnp.int32, (tq, 2), 1)
        lse_ref[0] = jnp.where(lane2 == 0, lses[0], lses[1])

    return _pc(
        body, name="fox_fwd", grid=(NP, nq),
        in_specs=[pl.BlockSpec((tq, 2 * LANES), lambda p, i: (i, p)),
                  pl.BlockSpec((s, 2 * LANES), lambda p, i: (0, p)),
                  pl.BlockSpec((s, LANES), lambda p, i: (0, 2 * NP + p))] + [ANY] * nx,
        out_specs=[pl.BlockSpec((tq, LANES), lambda p, i: (i, p)),
                   pl.BlockSpec((1, tq, 2), lambda p, i: (p, i, 0))] + [ANY] * nx,
        out_shape=[_sds((s, D), MXU), _sds((NP, s, 2), F32)] + plan["out_shape"],
        scratch_shapes=plan["scratch"], compiler_params=_cp(2),
    )(qx, kx, proj, *plan["arrays"])


def _fox_bwd(proj, qx, kx, do, lse_row, delta_row, plan):
    s = proj.shape[0]
    tq = min(TQ, s)
    nq = s // tq
    nx = len(plan["arrays"])

    def body(*refs):
        k_ref, v_ref, q_ref, do_ref, lse_ref, dl_ref = refs[:6]
        dq_ref, dqs_ref, dk_ref, dks_ref, dv_ref = refs[6 + nx:11 + nx]
        dq_acc = refs[11 + 2 * nx]
        p_id, j = pl.program_id(0), pl.program_id(1)
        _host_plan(plan, refs[6:6 + nx], refs[11 + nx:11 + 2 * nx], refs[12 + 2 * nx:], (p_id == 0) & (j == 0),
                   (p_id == NP // 2) & (j == 0), (p_id == NP - 1) & (j == nq - 1))

        @pl.when(j == 0)
        def _():
            dq_acc[...] = jnp.zeros_like(dq_acc)

        kv = k_ref[...]
        v2 = v_ref[...]
        kmask = _ext_masks(tq, True)
        qmask = _ext_masks(tq, False)
        hmask = _head_masks(tq)
        row = lax.broadcasted_iota(jnp.int32, (tq, tq), 0)
        col = lax.broadcasted_iota(jnp.int32, (tq, tq), 1)
        carry = (jnp.zeros((tq, 2 * LANES), F32), jnp.zeros((tq, LANES), F32))
        for a in range(2):
            ka = jnp.where(kmask[a], kv, jnp.zeros_like(kv))
            va = jnp.where(hmask[a], v2, jnp.zeros_like(v2))

            def blk(i, carry, diag, a=a, ka=ka, va=va):
                dk_a, dv_a = carry
                rows = pl.ds(pl.multiple_of(i * tq, tq), tq)
                qi = q_ref[rows, :]
                doi = do_ref[rows, :]
                qa = jnp.where(qmask[a], qi, jnp.zeros_like(qi))
                doa = jnp.where(hmask[a], doi, jnp.zeros_like(doi))
                st = _dot_nt(ka, qi)
                if diag:
                    st = jnp.where(col >= row, st, -jnp.inf)
                pt = jnp.exp2(st - lse_ref[0, i, a:a + 1, :])
                dv_a = dv_a + _dot(pt.astype(MXU), doa)
                dpt = _dot_nt(va, doi)
                dsb = (pt * (dpt - dl_ref[0, i, a:a + 1, :])).astype(MXU)
                dk_a = dk_a + _dot(dsb, qa)
                dq_acc[rows, :] += _dot_tn(dsb, ka)
                return dk_a, dv_a

            carry = blk(j, carry, True)
            carry = lax.fori_loop(j + 1, nq, functools.partial(blk, diag=False), carry)
        dk_acc, dv_acc = carry
        dk_ref[...] = (dk_acc[:, :LANES] * LN2).astype(MXU)
        dks_ref[0, 0] = dk_acc[:, LANES:].T[:X_ROWS, :]
        dv_ref[...] = dv_acc.astype(MXU)

        @pl.when(j == nq - 1)
        def _():
            dq_ref[...] = (dq_acc[:, pl.ds(0, LANES)] * DH ** -0.5).astype(MXU)
            for t in range(nq):
                dqs_ref[0, :, pl.ds(t * tq, tq)] = dq_acc[pl.ds(t * tq, tq), pl.ds(LANES, LANES)].T[:X_ROWS, :]

    stat = pl.BlockSpec((1, nq, 2, tq), lambda p, j: (p, 0, 0, 0))
    whole = pl.BlockSpec((s, LANES), lambda p, j: (0, p))
    tile = pl.BlockSpec((tq, LANES), lambda p, j: (j, p))
    return _pc(
        body, name="fox_bwd", grid=(NP, nq),
        in_specs=[pl.BlockSpec((tq, 2 * LANES), lambda p, j: (j, p)),
                  pl.BlockSpec((tq, LANES), lambda p, j: (j, 2 * NP + p)),
                  pl.BlockSpec((s, 2 * LANES), lambda p, j: (0, p)),
                  whole, stat, stat] + [ANY] * nx,
        out_specs=[whole, pl.BlockSpec((1, X_ROWS, s), lambda p, j: (p, 0, 0)), tile,
                   pl.BlockSpec((1, 1, X_ROWS, tq), lambda p, j: (p, j, 0, 0)), tile] + [ANY] * nx,
        out_shape=[_sds((s, D), MXU), _sds((NP, X_ROWS, s), F32), _sds((s, D), MXU),
                   _sds((NP, nq, X_ROWS, tq), F32), _sds((s, D), MXU)] + plan["out_shape"],
        scratch_shapes=[pltpu.VMEM((s, 2 * LANES), F32)] + plan["scratch"], compiler_params=_cp(2),
    )(kx, proj, qx, do, lse_row, delta_row, *plan["arrays"])


def _glu(a, gate):
    return a.astype(F32) * _sigmoid(gate.astype(F32))


def _store_blocked(buf, row0, val):
    for c in range(D // LANES):
        buf[0, c, pl.ds(row0, val.shape[0]), :] = val[:, c * LANES:(c + 1) * LANES]


def _fill_shifted(buf):
    n = buf.shape[2] - SUB
    for r in range(1, SUB):
        buf[r, :, pl.ds(0, n), :] = buf[0, :, pl.ds(r, n), :]


def _shifted(buf, off, rows, c):
    r = off % SUB
    return buf[r, c, pl.ds(off - r, rows), :]


def _conv_fwd(proj, cw, cb, lg, lb):
    s = proj.shape[0]
    tm = min(TM_CONV, s)
    hb = tm // HALO

    rcw = min(CONV_ROWS, tm)

    def body(a_ref, g_ref, ah_ref, gh_ref, w_ref, cb_ref, lg_ref, lb_ref, y_ref, o_ref, gsh):
        i = pl.program_id(0)
        _store_blocked(gsh, 0, jnp.where(i > 0, _glu(ah_ref[...], gh_ref[...]), 0.0))
        _store_blocked(gsh, HALO, _glu(a_ref[...], g_ref[...]))
        _fill_shifted(gsh)
        for c in range(D // LANES):
            cols = pl.ds(c * LANES, LANES)
            for rc in range(tm // rcw):
                acc = jnp.broadcast_to(cb_ref[:, cols], (rcw, LANES))
                for t in range(CW):
                    acc = acc + w_ref[t:t + 1, cols] * _shifted(gsh, HALO - (CW - 1) + t + rc * rcw, rcw, c)
                y_ref[pl.ds(rc * rcw, rcw), cols] = acc
        acc = y_ref[...]
        mu = jnp.mean(acc, axis=-1, keepdims=True)
        xc = acc - mu
        r = lax.rsqrt(jnp.mean(xc * xc, axis=-1, keepdims=True) + LN_EPS)
        nrm = xc * r * lg_ref[...] + lb_ref[...]
        o_ref[...] = (nrm * _sigmoid(nrm)).astype(MXU)

    vec = pl.BlockSpec((1, D), lambda i: (0, 0))
    return _pc(
        body, name="conv_fwd", grid=(s // tm,),
        in_specs=[pl.BlockSpec((tm, D), lambda i: (i, 3)), pl.BlockSpec((tm, D), lambda i: (i, 4)),
                  pl.BlockSpec((HALO, D), lambda i: (jnp.maximum(i * hb - 1, 0), 3)),
                  pl.BlockSpec((HALO, D), lambda i: (jnp.maximum(i * hb - 1, 0), 4)),
                  pl.BlockSpec((CWP, D), lambda i: (0, 0)), vec, vec, vec],
        out_specs=[pl.BlockSpec((tm, D), lambda i: (i, 0)), pl.BlockSpec((tm, D), lambda i: (i, 0))],
        out_shape=[_sds((s, D), F32), _sds((s, D), MXU)],
        scratch_shapes=[pltpu.VMEM((SUB, D // LANES, tm + HALO, LANES), F32)], compiler_params=_cp(1),
    )(proj, proj, proj, proj, cw, cb, lg, lb)


def _conv_bwd(proj, y, do, cw, lg, lb):
    s = proj.shape[0]
    tm = min(TM_CONV, s)
    hb = tm // HALO
    nt = s // tm
    last_hblk = s // HALO - 1

    def ln_bwd(yv, dov, lgv, lbv):
        mu = jnp.mean(yv, axis=-1, keepdims=True)
        xc = yv - mu
        r = lax.rsqrt(jnp.mean(xc * xc, axis=-1, keepdims=True) + LN_EPS)
        xh = xc * r
        nrm = xh * lgv + lbv
        sg = _sigmoid(nrm)
        dn = dov.astype(F32) * (sg * (1.0 + nrm * (1.0 - sg)))
        wv = dn * lgv
        dy = r * (wv - jnp.mean(wv, axis=-1, keepdims=True) - xh * jnp.mean(wv * xh, axis=-1, keepdims=True))
        return dy, dn, xh

    rcw = min(CONV_ROWS, tm)

    def body(a_ref, g_ref, ah_ref, gh_ref, y_ref, yn_ref, do_ref, don_ref, w_ref, lg_ref, lb_ref,
             da_ref, dg_ref, dw_ref, vec_ref, gsh, dysh, dwacc):
        i = pl.program_id(0)

        @pl.when(i == 0)
        def _():
            dwacc[...] = jnp.zeros_like(dwacc)
            vec_ref[...] = jnp.zeros_like(vec_ref)

        lgv, lbv = lg_ref[...], lb_ref[...]
        _store_blocked(gsh, 0, jnp.where(i > 0, _glu(ah_ref[...], gh_ref[...]), 0.0))
        _store_blocked(gsh, HALO, _glu(a_ref[...], g_ref[...]))
        _fill_shifted(gsh)
        dy, dn, xh = ln_bwd(y_ref[...], do_ref[...], lgv, lbv)
        dyn, _, _ = ln_bwd(yn_ref[...], don_ref[...], lgv, lbv)
        _store_blocked(dysh, 0, dy)
        _store_blocked(dysh, tm, jnp.where(i < nt - 1, dyn, 0.0))
        _fill_shifted(dysh)
        vec_ref[0:1, :] += jnp.sum(dn * xh, axis=0, keepdims=True)
        vec_ref[1:2, :] += jnp.sum(dn, axis=0, keepdims=True)
        vec_ref[2:3, :] += jnp.sum(dy, axis=0, keepdims=True)
        for c in range(D // LANES):
            cols = pl.ds(c * LANES, LANES)
            for rc in range(tm // rcw):
                rows = pl.ds(rc * rcw, rcw)
                dyc = dysh[0, c, rows, :]
                dgl = jnp.zeros((rcw, LANES), F32)
                for t in range(CW):
                    dgl = dgl + w_ref[t:t + 1, cols] * _shifted(dysh, CW - 1 - t + rc * rcw, rcw, c)
                    prod = dyc * _shifted(gsh, HALO - (CW - 1) + t + rc * rcw, rcw, c)
                    dwacc[t, :, cols] += jnp.sum(prod.reshape(rcw // SUB, SUB, LANES), axis=0)
                av = a_ref[rows, cols].astype(F32)
                sgate = _sigmoid(g_ref[rows, cols].astype(F32))
                da_ref[rows, cols] = (dgl * sgate).astype(MXU)
                dg_ref[rows, cols] = (dgl * av * sgate * (1.0 - sgate)).astype(MXU)

        @pl.when(i == nt - 1)
        def _():
            dw_ref[...] = jnp.sum(dwacc[...], axis=1)

    vec = pl.BlockSpec((1, D), lambda i: (0, 0))
    cur = lambda c: pl.BlockSpec((tm, D), lambda i: (i, c))
    prv = lambda c: pl.BlockSpec((HALO, D), lambda i: (jnp.maximum(i * hb - 1, 0), c))
    nxt = pl.BlockSpec((HALO, D), lambda i: (jnp.minimum((i + 1) * hb, last_hblk), 0))
    return _pc(
        body, name="conv_bwd", grid=(nt,),
        in_specs=[cur(3), cur(4), prv(3), prv(4), cur(0), nxt, cur(0), nxt,
                  pl.BlockSpec((CWP, D), lambda i: (0, 0)), vec, vec],
        out_specs=[cur(0), cur(0), pl.BlockSpec((CWP, D), lambda i: (0, 0)), pl.BlockSpec((8, D), lambda i: (0, 0))],
        out_shape=[_sds((s, D), MXU), _sds((s, D), MXU), _sds((CWP, D), F32), _sds((8, D), F32)],
        scratch_shapes=[pltpu.VMEM((SUB, D // LANES, tm + HALO, LANES), F32)] * 2 + [pltpu.VMEM((CWP, SUB, D), F32)],
        compiler_params=_cp(1),
    )(proj, proj, proj, proj, y, y, do, do, cw, lg, lb)


def _mem_kv(mem, g_mem, w_kv):
    mm = mem.shape[0]

    def body(m_ref, g_ref, w_ref, mn_ref, kv_ref):
        mv = m_ref[...]
        mn = (mv * _rms(mv) * g_ref[...]).astype(MXU)
        mn_ref[...] = mn
        kv_ref[...] = _dot(mn, w_ref[...]).astype(MXU)

    return _pc(body, name="mem_kv", out_shape=[_sds((mm, D), MXU), _sds((mm, 2 * D), MXU)],
               compiler_params=pltpu.CompilerParams(vmem_limit_bytes=VMEM_LIMIT))(mem, g_mem, w_kv)


def _mem_kv_bwd(mem, g_mem, w_kv, dkv):
    mm = mem.shape[0]

    def body(m_ref, w_ref, dkv_ref, o_ref):
        mv = m_ref[...]
        dmn = _dot_nt(dkv_ref[...], w_ref[...])
        o_ref[...] = jnp.broadcast_to(jnp.sum(dmn * mv * _rms(mv), axis=0, keepdims=True), o_ref.shape)

    return _pc(body, name="mem_kv_bwd", out_shape=_sds((8, D), F32),
               compiler_params=pltpu.CompilerParams(vmem_limit_bytes=VMEM_LIMIT))(mem, w_kv, dkv)


def _mem_attn_fwd(proj, kv):
    s = proj.shape[0]
    mm = kv.shape[0]
    tm = min(TM_PROJ, s)
    scale = MEM_DH ** -0.5

    def body(q_ref, kv_ref, o_ref):
        for h in range(MEM_H):
            cols = pl.ds(h * MEM_DH, MEM_DH)
            qh = q_ref[:, cols] * scale
            sc = _dot_nt(qh, kv_ref[:, cols])
            m = jnp.max(sc, axis=-1, keepdims=True)
            e = jnp.exp(sc - m)
            p = e / jnp.sum(e, axis=-1, keepdims=True)
            o_ref[:, cols] = _dot(p.astype(MXU), kv_ref[:, pl.ds(D + h * MEM_DH, MEM_DH)]).astype(MXU)

    return _pc(
        body, name="mem_attn_fwd", grid=(s // tm,),
        in_specs=[pl.BlockSpec((tm, D), lambda i: (i, 5)), pl.BlockSpec((mm, 2 * D), lambda i: (0, 0))],
        out_specs=pl.BlockSpec((tm, D), lambda i: (i, 0)), out_shape=_sds((s, D), MXU), compiler_params=_cp(1),
    )(proj, kv)


def _mem_attn_bwd(proj, kv, do):
    s = proj.shape[0]
    mm = kv.shape[0]
    tm = min(TM_PROJ, s)
    scale = MEM_DH ** -0.5

    def body(q_ref, kv_ref, do_ref, dq_ref, dkv_ref):
        @pl.when(pl.program_id(0) == 0)
        def _():
            dkv_ref[...] = jnp.zeros_like(dkv_ref)

        for h in range(MEM_H):
            cols = pl.ds(h * MEM_DH, MEM_DH)
            vcols = pl.ds(D + h * MEM_DH, MEM_DH)
            qh = q_ref[:, cols]
            kh = kv_ref[:, cols] * scale
            doh = do_ref[:, cols]
            st = _dot_nt(kh, qh)
            m = jnp.max(st, axis=0, keepdims=True)
            e = jnp.exp(st - m)
            pt = e / jnp.sum(e, axis=0, keepdims=True)
            dpt = _dot_nt(kv_ref[:, vcols], doh)
            dst = pt * (dpt - jnp.sum(pt * dpt, axis=0, keepdims=True))
            dsb = dst.astype(MXU)
            dkv_ref[:, vcols] += _dot(pt.astype(MXU), doh)
            dkv_ref[:, cols] += _dot(dsb, qh) * scale
            dq_ref[:, cols] = _dot_tn(dsb, kh).astype(MXU)

    return _pc(
        body, name="mem_attn_bwd", grid=(s // tm,),
        in_specs=[pl.BlockSpec((tm, D), lambda i: (i, 5)), pl.BlockSpec((mm, 2 * D), lambda i: (0, 0)),
                  pl.BlockSpec((tm, D), lambda i: (i, 0))],
        out_specs=[pl.BlockSpec((tm, D), lambda i: (i, 0)), pl.BlockSpec((mm, 2 * D), lambda i: (0, 0))],
        out_shape=[_sds((s, D), MXU), _sds((mm, 2 * D), F32)], compiler_params=_cp(1),
    )(proj, kv, do)


def _resident(n):
    return [pltpu.VMEM((n, D, D), MXU), pltpu.SemaphoreType.DMA((n,))]


def _load_resident(hbm_refs, wbuf, sems):
    @pl.when(pl.program_id(0) == 0)
    def _():
        cps = [pltpu.make_async_copy(r, wbuf.at[k], sems.at[k]) for k, r in enumerate(hbm_refs)]
        for cp in cps:
            cp.start()
        for cp in cps:
            cp.wait()


def _merge_out(oa, oc, om, proj, x, wpa, wpc, wpm, wout, g_post, g_fpre):
    s = x.shape[0]
    tm = min(TM_ROW, s)

    def body(oa_ref, oc_ref, om_ref, gl_ref, x_ref, gp_ref, gf_ref, wpa_h, wpc_h, wpm_h, wout_h,
             pa_ref, pc_ref, pm_ref, mg_ref, z_ref, x1_ref, h2_ref, wbuf, sems):
        _load_resident([wpa_h, wpc_h, wpm_h, wout_h], wbuf, sems)
        merged = jnp.zeros((tm, D), F32)
        for b, (o_ref, p_ref) in enumerate(((oa_ref, pa_ref), (oc_ref, pc_ref), (om_ref, pm_ref))):
            pb = _dot(o_ref[...], wbuf[b])
            p_ref[...] = pb.astype(MXU)
            merged = merged + _sigmoid(gl_ref[:, pl.ds(b * D, D)].astype(F32)) * pb
        mg = merged.astype(MXU)
        mg_ref[...] = mg
        z = _dot(mg, wbuf[3])
        z_ref[...] = z
        x1 = x_ref[...] + z * _rms(z) * gp_ref[...]
        x1_ref[...] = x1
        h2_ref[...] = (x1 * _rms(x1) * gf_ref[...]).astype(MXU)

    rows = pl.BlockSpec((tm, D), lambda i: (i, 0))
    vec = pl.BlockSpec((1, D), lambda i: (0, 0))
    anyspec = pl.BlockSpec(memory_space=pl.ANY)
    return _pc(
        body, name="merge_out", grid=(s // tm,),
        in_specs=[rows, rows, rows, pl.BlockSpec((tm, 3 * D), lambda i: (i, 2)), rows, vec, vec,
                  anyspec, anyspec, anyspec, anyspec],
        out_specs=[rows] * 7,
        out_shape=[_sds((s, D), MXU)] * 4 + [_sds((s, D), F32)] * 2 + [_sds((s, D), MXU)],
        scratch_shapes=_resident(4), compiler_params=_cp(1),
    )(oa, oc, om, proj, x, g_post, g_fpre, wpa, wpc, wpm, wout)


def _ffn_up(h2, w_gu):
    s = h2.shape[0]
    tm = min(TM_PROJ, s)
    nb = 2
    bw = FFN // nb

    def body(h_ref, wg_ref, wu_ref, gf_ref, uf_ref, act_ref):
        hv = h_ref[...]
        gf = _dot(hv, wg_ref[...])
        uf = _dot(hv, wu_ref[...])
        gf_ref[...] = gf.astype(MXU)
        uf_ref[...] = uf.astype(MXU)
        act_ref[...] = (gf * _sigmoid(gf) * uf).astype(MXU)

    out = pl.BlockSpec((tm, bw), lambda i, j: (i, j))
    return _pc(
        body, name="ffn_up", grid=(s // tm, nb),
        in_specs=[pl.BlockSpec((tm, D), lambda i, j: (i, 0)), pl.BlockSpec((D, bw), lambda i, j: (0, j)),
                  pl.BlockSpec((D, bw), lambda i, j: (0, nb + j))],
        out_specs=[out, out, out], out_shape=[_sds((s, FFN), MXU)] * 3, compiler_params=_cp(2),
    )(h2, w_gu, w_gu)


def _ffn_down_loss(act, w_d, x1, tgt, g_fpost):
    s = act.shape[0]
    tm = min(TM_PROJ, s)

    def body(a_ref, w_ref, x1_ref, t_ref, g_ref, dffn_ref, dy_ref, vec_ref, loss_ref):
        @pl.when(pl.program_id(0) == 0)
        def _():
            vec_ref[...] = jnp.zeros_like(vec_ref)
            loss_ref[...] = jnp.zeros_like(loss_ref)

        ffn = _dot(a_ref[...], w_ref[...])
        r = _rms(ffn)
        gv = g_ref[...]
        e = x1_ref[...] + ffn * r * gv - t_ref[...]
        loss_ref[...] += jnp.sum(e * e) * (0.5 / D)
        dy = e * (1.0 / D)
        dy_ref[...] = dy
        vec_ref[0:1, :] += jnp.sum(dy * ffn * r, axis=0, keepdims=True)
        dffn_ref[...] = _rms_bwd(ffn, r, gv, dy).astype(MXU)

    rows = pl.BlockSpec((tm, D), lambda i: (i, 0))
    return _pc(
        body, name="ffn_down_loss", grid=(s // tm,),
        in_specs=[pl.BlockSpec((tm, FFN), lambda i: (i, 0)), pl.BlockSpec((FFN, D), lambda i: (0, 0)), rows, rows,
                  pl.BlockSpec((1, D), lambda i: (0, 0))],
        out_specs=[rows, rows, pl.BlockSpec((8, D), lambda i: (0, 0)), pl.BlockSpec((8, LANES), lambda i: (0, 0))],
        out_shape=[_sds((s, D), MXU), _sds((s, D), F32), _sds((8, D), F32), _sds((8, LANES), F32)],
        compiler_params=_cp(1),
    )(act, w_d, x1, tgt, g_fpost)


def _ffn_down_bwd(dffn, w_d, gf, uf):
    s = dffn.shape[0]
    tm = min(TM_ROW, s)

    def body(d_ref, w_ref, gf_ref, uf_ref, dgf_ref, duf_ref):
        da = _dot_nt(d_ref[...], w_ref[...])
        gf = gf_ref[...].astype(F32)
        sg = _sigmoid(gf)
        duf_ref[...] = (da * gf * sg).astype(MXU)
        dgf_ref[...] = (da * uf_ref[...].astype(F32) * (sg * (1.0 + gf * (1.0 - sg)))).astype(MXU)

    wide = pl.BlockSpec((tm, FFN), lambda i: (i, 0))
    return _pc(
        body, name="ffn_down_bwd", grid=(s // tm,),
        in_specs=[pl.BlockSpec((tm, D), lambda i: (i, 0)), pl.BlockSpec((FFN, D), lambda i: (0, 0)), wide, wide],
        out_specs=[wide, wide], out_shape=[_sds((s, FFN), MXU)] * 2, compiler_params=_cp(1),
    )(dffn, w_d, gf, uf)


def _ffn_up_bwd(dgf, duf, w_gu, x1, dy, z, g_fpre, g_post):
    s = x1.shape[0]
    tm = min(TM_ROW, s)

    def body(dgf_ref, duf_ref, w_ref, x1_ref, dy_ref, z_ref, gf_ref, gp_ref, dx1_ref, dz_ref, vec_ref):
        @pl.when(pl.program_id(0) == 0)
        def _():
            vec_ref[...] = jnp.zeros_like(vec_ref)

        dh2 = _dot_nt(dgf_ref[...], w_ref[:, pl.ds(0, FFN)]) + _dot_nt(duf_ref[...], w_ref[:, pl.ds(FFN, FFN)])
        x1 = x1_ref[...]
        r2 = _rms(x1)
        vec_ref[0:1, :] += jnp.sum(dh2 * x1 * r2, axis=0, keepdims=True)
        dx1 = dy_ref[...] + _rms_bwd(x1, r2, gf_ref[...], dh2)
        dx1_ref[...] = dx1
        z = z_ref[...]
        rz = _rms(z)
        vec_ref[1:2, :] += jnp.sum(dx1 * z * rz, axis=0, keepdims=True)
        dz_ref[...] = _rms_bwd(z, rz, gp_ref[...], dx1).astype(MXU)

    rows = pl.BlockSpec((tm, D), lambda i: (i, 0))
    wide = pl.BlockSpec((tm, FFN), lambda i: (i, 0))
    vec = pl.BlockSpec((1, D), lambda i: (0, 0))
    return _pc(
        body, name="ffn_up_bwd", grid=(s // tm,),
        in_specs=[wide, wide, pl.BlockSpec((D, 2 * FFN), lambda i: (0, 0)), rows, rows, rows, vec, vec],
        out_specs=[rows, rows, pl.BlockSpec((8, D), lambda i: (0, 0))],
        out_shape=[_sds((s, D), F32), _sds((s, D), MXU), _sds((8, D), F32)], compiler_params=_cp(1),
    )(dgf, duf, w_gu, x1, dy, z, g_fpre, g_post)


def _merge_bwd(dz, proj, pa, pc, pm, oa, wpa, wpc, wpm, wout):
    s = dz.shape[0]
    tm = min(TM_ROW, s)

    def body(dz_ref, gl_ref, pa_ref, pc_ref, pm_ref, oa_ref, wpa_h, wpc_h, wpm_h, wout_h,
             dpa_ref, dpc_ref, dpm_ref, dgl_ref, doa_ref, doc_ref, dom_ref, dl_ref, wbuf, sems):
        _load_resident([wpa_h, wpc_h, wpm_h, wout_h], wbuf, sems)
        dm = _dot_nt(dz_ref[...], wbuf[3])
        quads = ((pa_ref, dpa_ref, doa_ref), (pc_ref, dpc_ref, doc_ref), (pm_ref, dpm_ref, dom_ref))
        for b, (p_ref, dp_ref, do_ref) in enumerate(quads):
            cols = pl.ds(b * D, D)
            gt = _sigmoid(gl_ref[:, cols].astype(F32))
            dp = (dm * gt).astype(MXU)
            dp_ref[...] = dp
            dgl_ref[:, cols] = (dm * p_ref[...].astype(F32) * gt * (1.0 - gt)).astype(MXU)
            dob = _dot_nt(dp, wbuf[b]).astype(MXU)
            do_ref[...] = dob
            if b == 0:
                prod = dob.astype(F32) * oa_ref[...].astype(F32)
                d_i = lax.broadcasted_iota(jnp.int32, (D, LANES), 0)
                h_i = lax.broadcasted_iota(jnp.int32, (D, LANES), 1)
                sel = jnp.where(lax.shift_right_logical(d_i, DH.bit_length() - 1) == h_i, 1.0, 0.0).astype(jnp.bfloat16)
                dl_ref[...] = _dot_exact_rhs_t(prod, sel)

    rows = pl.BlockSpec((tm, D), lambda i: (i, 0))
    anyspec = pl.BlockSpec(memory_space=pl.ANY)
    wide = pl.BlockSpec((tm, 3 * D), lambda i: (i, 2))
    return _pc(
        body, name="merge_bwd", grid=(s // tm,),
        in_specs=[rows, wide, rows, rows, rows, rows, anyspec, anyspec, anyspec, anyspec],
        out_specs=[rows, rows, rows, pl.BlockSpec((tm, 3 * D), lambda i: (i, 0)), rows, rows, rows,
                   pl.BlockSpec((tm, LANES), lambda i: (i, 0))],
        out_shape=[_sds((s, D), MXU)] * 3 + [_sds((s, 3 * D), MXU)] + [_sds((s, D), MXU)] * 3 + [_sds((s, LANES), F32)],
        scratch_shapes=_resident(4), compiler_params=_cp(1),
    )(dz, proj, pa, pc, pm, oa, wpa, wpc, wpm, wout)


def _dot_exact_rhs_t(v, b01):
    hi, mid, lo = _split3(v)
    return _dot(hi, b01) + _dot(mid, b01) + _dot(lo, b01)


def _in_proj_bwd(pieces, df, w_main, w_f, x, dx1, g_pre):
    s = x.shape[0]
    tm = min(TM_ROW, s)
    n_main = w_main.shape[1]
    np_ = len(pieces)

    def body(*refs):
        p_refs = refs[:np_]
        df_ref, x_ref, dx1_ref, g_ref, w_h, wf_ref, dx_ref, vec_ref, wbuf, sem = refs[np_:]

        @pl.when(pl.program_id(0) == 0)
        def _():
            vec_ref[...] = jnp.zeros_like(vec_ref)
            cp = pltpu.make_async_copy(w_h, wbuf, sem)
            cp.start()
            cp.wait()

        dh = _dot_nt(df_ref[...], wf_ref[...])
        for p_ref, (_, c0, nc) in zip(p_refs, pieces):
            dh = dh + _dot_nt(p_ref[...], wbuf[:, pl.ds(c0 * D, nc * D)])
        xv = x_ref[...]
        r = _rms(xv)
        vec_ref[0:1, :] += jnp.sum(dh * xv * r, axis=0, keepdims=True)
        dx_ref[...] = dx1_ref[...] + _rms_bwd(xv, r, g_ref[...], dh)

    rows = pl.BlockSpec((tm, D), lambda i: (i, 0))
    p_specs = [pl.BlockSpec((tm, nc * D), lambda i: (i, 0)) for _, _, nc in pieces]
    return _pc(
        body, name="in_proj_bwd", grid=(s // tm,),
        in_specs=p_specs + [pl.BlockSpec((tm, LANES), lambda i: (i, 0)), rows, rows, pl.BlockSpec((1, D), lambda i: (0, 0)),
                            pl.BlockSpec(memory_space=pl.ANY), pl.BlockSpec((D, LANES), lambda i: (0, 0))],
        out_specs=[rows, pl.BlockSpec((8, D), lambda i: (0, 0))],
        out_shape=[_sds((s, D), F32), _sds((8, D), F32)],
        scratch_shapes=[pltpu.VMEM((D, n_main), MXU), pltpu.SemaphoreType.DMA], compiler_params=_cp(1),
    )(*[p for p, _, _ in pieces], df, x, dx1, g_pre, w_main, w_f)


def _wgrad(xa, dy, name):
    s, k = xa.shape
    n = dy.shape[1]
    ts = min(TS_WG, s)
    tk = _tile(k, WG_CAP)
    tn = _tile(n, WG_CAP)

    def body(x_ref, dy_ref, o_ref):
        @pl.when(pl.program_id(2) == 0)
        def _():
            o_ref[...] = jnp.zeros_like(o_ref)

        o_ref[...] += _dot_tn(x_ref[...], dy_ref[...])

    return _pc(
        body, name=name, grid=(k // tk, n // tn, s // ts),
        in_specs=[pl.BlockSpec((ts, tk), lambda a, b, c: (c, a)), pl.BlockSpec((ts, tn), lambda a, b, c: (c, b))],
        out_specs=pl.BlockSpec((tk, tn), lambda a, b, c: (a, b)), out_shape=_sds((k, n), F32), compiler_params=_cp(3),
    )(xa, dy)


def _pair_sum(g, r1, c_idx):
    _, _, hr, cols = g.shape
    tr = _rowtile(hr, cols)

    def body(c_ref, g_ref, r_ref, o_ref):
        o_ref[0] = (g_ref[0, 0].astype(F32) + r_ref[0].astype(F32)).astype(WIRE)

    return _pc(
        body, name="pair_sum_%dx%d" % (hr, cols), out_shape=_sds((N_CHIPS, hr, cols), WIRE),
        grid_spec=pltpu.PrefetchScalarGridSpec(
            num_scalar_prefetch=1, grid=(N_CHIPS, hr // tr),
            in_specs=[pl.BlockSpec((1, 1, tr, cols), lambda d, i, c: (d, c[0], i, 0)),
                      pl.BlockSpec((1, tr, cols), lambda d, i, c: (d, i, 0))],
            out_specs=pl.BlockSpec((1, tr, cols), lambda d, i, c: (d, i, 0))),
        compiler_params=_cp(2),
    )(c_idx, g, r1)


def _chip_sum(r2, slot, base=None):
    _, hr, cols = r2.shape
    tr = _rowtile(hr, cols)

    def body(s_ref, r_ref, *rest):
        o_ref = rest[-1]
        acc = r_ref[0].astype(F32)
        for d in range(1, N_CHIPS):
            acc = acc + r_ref[d].astype(F32)
        o_ref[0] = acc

    based = base is not None
    return _pc(
        body, name="chip_sum_%dx%d_%d" % (hr, cols, int(based)), out_shape=_sds((2, hr, cols), F32),
        grid_spec=pltpu.PrefetchScalarGridSpec(
            num_scalar_prefetch=1, grid=(hr // tr,),
            in_specs=[pl.BlockSpec((N_CHIPS, tr, cols), lambda i, s: (0, i, 0))] + ([ANY] if based else []),
            out_specs=pl.BlockSpec((1, tr, cols), lambda i, s: (s[0], i, 0))),
        input_output_aliases={2: 0} if based else {}, compiler_params=_cp(1),
    )(*((slot, r2, base) if based else (slot, r2)))


def _adamw(w, g, m, v):
    rows, cols = w.shape
    tr = _rowtile(rows, cols)
    c1 = 1.0 / (1.0 - B1 ** STEP)
    c2 = 1.0 / (1.0 - B2 ** STEP)

    def body(w_ref, g_ref, m_ref, v_ref, d_ref, mo_ref, vo_ref):
        gv = g_ref[...]
        mn = B1 * m_ref[...] + (1.0 - B1) * gv
        vn = B2 * v_ref[...] + (1.0 - B2) * (gv * gv)
        mo_ref[...] = mn
        vo_ref[...] = vn
        d_ref[...] = -LR * ((mn * c1) / (jnp.sqrt(vn * c2) + ADAM_EPS) + WD * w_ref[...])

    blk = pl.BlockSpec((tr, cols), lambda i: (i, 0))
    return _pc(
        body, name="adamw_%dx%d" % (rows, cols), grid=(rows // tr,), in_specs=[blk] * 4, out_specs=[blk] * 3,
        out_shape=[_sds((rows, cols), F32)] * 3, compiler_params=_cp(1),
    )(w, g, m, v)


MESH_ID = pl.DeviceIdType.MESH
ANY = pl.BlockSpec(memory_space=pl.ANY)


def _place():
    x, y, c = lax.axis_index("x"), lax.axis_index("y"), lax.axis_index("c")
    others = [(1 - x, y), (x, 1 - y), (1 - x, 1 - y)]
    return x, y, c, others


def _remote(src, dst, sems, idx, to):
    return pltpu.make_async_remote_copy(src_ref=src, dst_ref=dst, send_sem=sems[0].at[idx], recv_sem=sems[1].at[idx],
                                        device_id=to, device_id_type=MESH_ID)


def _gather_plan(shards):
    nk = len(shards)

    def copies(ins, outs, sems):
        x, y, c, others = _place()
        me = 2 * x + y
        sib = (x, y, 1 - c)
        local = [pltpu.make_async_copy(ins[k], outs[k].at[me], sems[2].at[k]) for k in range(nk)]
        ici, landed, fwd, fwd_landed = [], [], [], []
        for k in range(nk):
            hr = shards[k].shape[0] // 2
            for r, (cx, cy) in enumerate(others):
                mine = pl.ds(c * hr, hr)
                ici.append(_remote(ins[k].at[mine], outs[k].at[me, mine], sems, 6 * k + r, (cx, cy, c)))
                got = outs[k].at[2 * cx + cy, mine]
                landed.append(_remote(got, got, sems, 6 * k + r, (cx, cy, c)))
                fwd.append(_remote(got, got, sems, 6 * k + 3 + r, sib))
                theirs = outs[k].at[2 * cx + cy, pl.ds((1 - c) * hr, hr)]
                fwd_landed.append(_remote(theirs, theirs, sems, 6 * k + 3 + r, sib))
        return local, ici, landed, fwd, fwd_landed

    def start(ins, outs, sems):
        local, ici, _, _, _ = copies(ins, outs, sems)
        for cp in local + ici:
            cp.start()

    def forward(ins, outs, sems):
        _, _, landed, fwd, _ = copies(ins, outs, sems)
        for got, cp in zip(landed, fwd):
            got.wait_recv()
            cp.start()

    def finish(ins, outs, sems):
        local, ici, _, fwd, fwd_landed = copies(ins, outs, sems)
        for got in fwd_landed:
            got.wait_recv()
        for cp in ici + fwd:
            cp.wait_send()
        for cp in local:
            cp.wait()

    return dict(
        arrays=list(shards), out_shape=[_sds((N_CHIPS,) + a.shape, a.dtype) for a in shards],
        scratch=[pltpu.SemaphoreType.DMA((6 * nk,)), pltpu.SemaphoreType.DMA((6 * nk,)), pltpu.SemaphoreType.DMA((nk,))],
        phases=[start, forward, finish])


def _scatter_plan(ps):
    nk = len(ps)

    def copies(ins, outs, sems):
        x, y, c, others = _place()
        me = 2 * x + y
        local = [pltpu.make_async_copy(ins[k].at[me], outs[k].at[me], sems[2].at[k]) for k in range(nk)]
        ici, landed = [], []
        for k in range(nk):
            for r, (cx, cy) in enumerate(others):
                ici.append(_remote(ins[k].at[2 * cx + cy], outs[k].at[me], sems, 3 * k + r, (cx, cy, c)))
                got = outs[k].at[2 * cx + cy]
                landed.append(_remote(got, got, sems, 3 * k + r, (cx, cy, c)))
        return local, ici, landed

    def start(ins, outs, sems):
        local, ici, _ = copies(ins, outs, sems)
        for cp in local + ici:
            cp.start()

    def finish(ins, outs, sems):
        local, ici, landed = copies(ins, outs, sems)
        for got in landed:
            got.wait_recv()
        for cp in ici:
            cp.wait_send()
        for cp in local:
            cp.wait()

    return dict(
        arrays=list(ps), out_shape=[_sds(a.shape, a.dtype) for a in ps],
        scratch=[pltpu.SemaphoreType.DMA((3 * nk,)), pltpu.SemaphoreType.DMA((3 * nk,)), pltpu.SemaphoreType.DMA((nk,))],
        phases=[start, finish])


def _run_plan(plan, name):
    nk = len(plan["arrays"])

    def body(*refs):
        ins, outs, sems = refs[:nk], refs[nk:2 * nk], refs[2 * nk:]
        for phase in plan["phases"]:
            phase(ins, outs, sems)

    return _pc(body, name=name, in_specs=[ANY] * nk, out_specs=[ANY] * nk, out_shape=plan["out_shape"],
               scratch_shapes=plan["scratch"])(*plan["arrays"])


def _host_plan(plan, ins, outs, sems, first, middle, last):
    points = [first, last] if len(plan["phases"]) == 2 else [first, middle, last]
    for phase, at in zip(plan["phases"], points):
        pl.when(at)(functools.partial(phase, ins, outs, sems))


def _swap_sibling(gs, halves, tag):
    nk = len(gs)

    def body(*refs):
        ins, outs = refs[:nk], refs[nk:2 * nk]
        send_sems, recv_sems = refs[2 * nk:]
        x, y, c, _ = _place()
        cps = []
        for k in range(nk):
            hr = gs[k].shape[1] // 2
            cp = pltpu.make_async_remote_copy(
                src_ref=ins[k].at[:, pl.ds((1 - c) * hr, hr)] if halves else ins[k], dst_ref=outs[k],
                send_sem=send_sems.at[k], recv_sem=recv_sems.at[k], device_id=(x, y, 1 - c), device_id_type=MESH_ID)
            cp.start()
            cps.append(cp)
        for cp in cps:
            cp.wait()

    return _pc(
        body, name=("swap_halves_" if halves else "swap_slabs_") + tag, in_specs=[ANY] * nk, out_specs=[ANY] * nk,
        out_shape=[_sds((N_CHIPS, a.shape[1] // 2 if halves else a.shape[1], a.shape[2]), a.dtype) for a in gs],
        scratch_shapes=[pltpu.SemaphoreType.DMA((nk,)), pltpu.SemaphoreType.DMA((nk,))],
    )(*gs)


def _allreduce_small(v):
    rows, cols = v.shape

    def body(v_ref, o_ref, gath, send_sems, recv_sems):
        x, y, c, others = _place()
        sib = (x, y, 1 - c)

        def slot(px, py, pc):
            return gath.at[4 * px + 2 * py + pc]

        def copy(k, block, to, src=None):
            return pltpu.make_async_remote_copy(
                src_ref=slot(*block) if src is None else src, dst_ref=slot(*block),
                send_sem=send_sems.at[k], recv_sem=recv_sems.at[k], device_id=to, device_id_type=MESH_ID)

        me = (x, y, c)
        gath[4 * x + 2 * y + c] = v_ref[...]
        first = [copy(0, me, sib, src=v_ref)]
        first += [copy(1 + r, me, (cx, cy, c), src=v_ref) for r, (cx, cy) in enumerate(others)]
        for cp in first:
            cp.start()
        passed = [copy(4 + r, (cx, cy, c), sib) for r, (cx, cy) in enumerate(others)]
        for r, (cx, cy) in enumerate(others):
            copy(1 + r, (cx, cy, c), me).wait_recv()
            passed[r].start()
        copy(0, (x, y, 1 - c), me).wait_recv()
        for r, (cx, cy) in enumerate(others):
            copy(4 + r, (cx, cy, 1 - c), me).wait_recv()
        for cp in first + passed:
            cp.wait_send()
        acc = gath[0]
        for d in range(1, N_DEV):
            acc = acc + gath[d]
        o_ref[...] = acc

    vm = pl.BlockSpec(memory_space=pltpu.VMEM)
    return _pc(
        body, name="allreduce_small", in_specs=[vm], out_specs=vm, out_shape=_sds((rows, cols), F32),
        scratch_shapes=[pltpu.VMEM((N_DEV, rows, cols), F32), pltpu.SemaphoreType.DMA((7,)), pltpu.SemaphoreType.DMA((7,))],
    )(v)


def _cols_to_chips(a):
    r, c4 = a.shape
    return a.reshape(r, N_CHIPS, c4 // N_CHIPS).transpose(1, 0, 2)


def _chips_to_cols(a):
    n, r, c = a.shape
    return a.transpose(1, 0, 2).reshape(r, n * c)


def _head_rows(a, tq):
    s = a.shape[0]
    return a[:, :HF].reshape(s // tq, tq, NP, 2).transpose(2, 0, 3, 1)


def _head_cols(a):
    s = a.shape[0]
    return a[:, :HF].reshape(s, NP, 2).transpose(1, 0, 2)


def _to_wire(gs, c_idx, tag):
    gs = [g.astype(WIRE) for g in gs]
    r1 = _swap_sibling(gs, True, tag)
    return [_pair_sum(g.reshape(N_CHIPS, 2, g.shape[1] // 2, g.shape[2]), r, c_idx) for g, r in zip(gs, r1)]


def _local_step(x, mem, tgt, sp, w_main, w_f, rest_shards, c_idx):
    s = x.shape[0]
    tq = min(TQ, s)
    b_f = jnp.pad(sp["b_forget"], ((0, 0), (0, LANES - HF)))
    proj, h, flog = _in_proj(x, sp["norm_mix_pre"], w_main, w_f)
    cf = _logf_cumsum(flog, b_f)
    qx, kx = _fox_prep(proj, _head_cols(cf))
    oa, lse, g_cw, g_kv, g_pj, g_gu, g_d = _fox_fwd(proj, qx, kx, _gather_plan(rest_shards))
    pj = g_pj.reshape(N_CHIPS, 4, D // N_CHIPS, D).transpose(1, 0, 2, 3).reshape(4, D, D)
    w = {"conv_w": _chips_to_cols(g_cw), "w_kv": _chips_to_cols(g_kv), "wpa": pj[0], "wpc": pj[1], "wpm": pj[2],
         "wout": pj[3], "w_gu": _chips_to_cols(g_gu), "w_d": g_d.reshape(FFN, D)}
    y, oc = _conv_fwd(proj, w["conv_w"], sp["conv_b"], sp["conv_ln_g"], sp["conv_ln_b"])
    mem_n, kv = _mem_kv(mem, sp["norm_mem"], w["w_kv"])
    om = _mem_attn_fwd(proj, kv)
    pa, pc, pm, merged, z, x1, h2 = _merge_out(oa, oc, om, proj, x, w["wpa"], w["wpc"], w["wpm"], w["wout"],
                                              sp["norm_mix_post"], sp["norm_ffn_pre"])
    gf, uf, act = _ffn_up(h2, w["w_gu"])
    dffn, dy, vec_f, loss_blk = _ffn_down_loss(act, w["w_d"], x1, tgt, sp["norm_ffn_post"])

    dgf, duf = _ffn_down_bwd(dffn, w["w_d"], gf, uf)
    dx1, dz, vec_n = _ffn_up_bwd(dgf, duf, w["w_gu"], x1, dy, z, sp["norm_ffn_pre"], sp["norm_mix_post"])
    dpa, dpc, dpm, dgl, doa, doc, dom, delta = _merge_bwd(dz, proj, pa, pc, pm, oa, w["wpa"], w["wpc"], w["wpm"], w["wout"])
    pj_g = jnp.stack([_wgrad(oa, dpa, "wgrad_pa"), _wgrad(oc, dpc, "wgrad_pc"), _wgrad(om, dpm, "wgrad_pm"),
                      _wgrad(merged, dz, "wgrad_out")]).reshape(4, N_CHIPS, D // N_CHIPS, D)
    early = [pj_g.transpose(1, 0, 2, 3).reshape(N_CHIPS, D, D),
             _cols_to_chips(jnp.concatenate([_wgrad(h2, dgf, "wgrad_g"), _wgrad(h2, duf, "wgrad_u")], axis=1)),
             _wgrad(act, dffn, "wgrad_d").reshape(N_CHIPS, FFN // N_CHIPS, D)]
    lse16 = lse.transpose(1, 0, 2).reshape(s, HF)
    dq, dqs, dk, dks, dv, *r2_early = _fox_bwd(proj, qx, kx, doa, _head_rows(lse16, tq), _head_rows(delta, tq),
                                              _scatter_plan(_to_wire(early, c_idx, "early")))
    over_keys = dqs[:, X_KONE:X_KONE + 2, :].transpose(2, 0, 1).reshape(s, HF)
    over_queries = dks[:, :, X_QONE:X_QONE + 2, :].transpose(1, 3, 0, 2).reshape(s, HF)
    dc = jnp.pad(over_keys - over_queries, ((0, 0), (0, LANES - HF)))
    df, db_blk = _logf_cumsum_bwd(flog, b_f, dc)
    dga, dgg, dcw, vec_c = _conv_bwd(proj, y, doc, w["conv_w"], sp["conv_ln_g"], sp["conv_ln_b"])
    dqm, dkv = _mem_attn_bwd(proj, kv, dom)
    dkv_b = dkv.astype(MXU)
    vec_m = _mem_kv_bwd(mem, sp["norm_mem"], w["w_kv"], dkv_b)
    pieces = [(dq, 0, 1), (dk, 1, 1), (dv, 2, 1), (dga, 3, 1), (dgg, 4, 1), (dqm, 5, 1), (dgl, 6, 3)]
    dx, vec_p = _in_proj_bwd(pieces, df, w_main, w_f, x, dx1, sp["norm_mix_pre"])

    dw_cols = [_wgrad(h, p, "wgrad_in_%d" % c0) for p, c0, _ in pieces]
    dwf = _wgrad(h, df, "wgrad_in_f")
    late = [_cols_to_chips(jnp.concatenate(dw_cols[:3] + [dwf[:, :HF]] + dw_cols[3:], axis=1)), _cols_to_chips(dcw),
            _cols_to_chips(_wgrad(mem_n, dkv_b, "wgrad_kv"))]
    r2_late = _run_plan(_scatter_plan(_to_wire(late, c_idx, "late")), "exchange_late")
    zero_row = jnp.zeros((1, D), F32)
    small = jnp.concatenate([
        vec_p[0:1], vec_n[1:2], vec_m[0:1], vec_c[2:3], vec_c[0:1], vec_c[1:2], vec_n[0:1], vec_f[0:1],
        jnp.pad(db_blk[0:1, :HF], ((0, 0), (0, D - HF))),
        jnp.pad(loss_blk[0:1, 0:1], ((0, 0), (0, D - 1))),
    ] + [zero_row] * (SMALL_ROWS - 10), axis=0)
    return dx, list(r2_late) + list(r2_early), small


SMALL_NAMES = ["norm_mix_pre", "norm_mix_post", "norm_mem", "conv_b", "conv_ln_g", "conv_ln_b", "norm_ffn_pre", "norm_ffn_post"]
PROJ_NAMES = ["w_proj_attn", "w_proj_conv", "w_proj_mem", "w_out"]
WEIGHT_ORDER = ["norm_mix_pre", "norm_mix_post", "norm_mem", "w_in", "b_forget", "conv_w", "conv_b", "conv_ln_g", "conv_ln_b",
                "w_kv_mem", "w_proj_attn", "w_proj_conv", "w_proj_mem", "w_out", "norm_ffn_pre", "norm_ffn_post",
                "w_gate_up", "w_down"]


def _pack_small(p):
    rows = [p[n] for n in SMALL_NAMES] + [jnp.pad(p["b_forget"], ((0, 0), (0, D - HF)))]
    return jnp.concatenate(rows + [jnp.zeros((SMALL_ROWS - len(rows), D), F32)], axis=0)


def _step(params, moms, vels, x, mem, tgt):
    c_idx = lax.axis_index("c").astype(jnp.int32).reshape(1)

    (g_in,) = _run_plan(_gather_plan([params["w_in"].astype(WIRE)]), "gather_w_in")
    w_in_full = _chips_to_cols(g_in)
    w_main = jnp.concatenate([w_in_full[:, :3 * D], w_in_full[:, 3 * D + HF:]], axis=1)
    w_f = jnp.pad(w_in_full[:, 3 * D:3 * D + HF], ((0, 0), (0, LANES - HF)))
    rest = [jnp.pad(params["conv_w"], ((0, CWP - CW), (0, 0))), params["w_kv_mem"].astype(WIRE),
            jnp.concatenate([params[n] for n in PROJ_NAMES], axis=0).astype(WIRE),
            params["w_gate_up"].astype(WIRE), params["w_down"].astype(WIRE)]

    dx, r2, small = _local_step(x, mem, tgt, params, w_main, w_f, rest, c_idx)

    r2_sib = _swap_sibling(r2, False, "all")
    full = [_chip_sum(theirs, 1 - c_idx, _chip_sum(mine, c_idx)) for mine, theirs in zip(r2, r2_sib)]
    red = [f.reshape(2 * f.shape[1], f.shape[2]) for f in full]
    pj_r = red[3].reshape(4, D // N_CHIPS, D)
    grads = {"w_in": red[0], "conv_w": red[1][:CW], "w_kv_mem": red[2], "w_gate_up": red[4], "w_down": red[5]}
    for i, n in enumerate(PROJ_NAMES):
        grads[n] = pj_r[i]

    tot = _allreduce_small(small)
    loss = tot[9, 0]
    for i, n in enumerate(SMALL_NAMES):
        grads[n] = tot[i:i + 1]
    grads["b_forget"] = tot[8:9, :HF]

    delta, new_m, new_v = {}, {}, {}
    ds, ms, vs = _adamw(_pack_small(params), tot.at[9:].set(0.0), _pack_small(moms), _pack_small(vels))
    for i, n in enumerate(SMALL_NAMES):
        delta[n], new_m[n], new_v[n] = ds[i:i + 1], ms[i:i + 1], vs[i:i + 1]
    delta["b_forget"], new_m["b_forget"], new_v["b_forget"] = ds[8:9, :HF], ms[8:9, :HF], vs[8:9, :HF]
    for n in ["w_in", "conv_w", "w_kv_mem", "w_gate_up", "w_down"] + PROJ_NAMES:
        delta[n], new_m[n], new_v[n] = _adamw(params[n], grads[n], moms[n], vels[n])
    return loss, dx, grads, delta, new_m, new_v


def kernel(x, mem, norm_mix_pre, norm_mix_post, norm_mem, w_in, b_forget, conv_w, conv_b, conv_ln_g, conv_ln_b, w_kv_mem, w_proj_attn, w_proj_conv, w_proj_mem, w_out, norm_ffn_pre, norm_ffn_post, w_gate_up, w_down, loss_target, m_norm_mix_pre, m_norm_mix_post, m_norm_mem, m_w_in, m_b_forget, m_conv_w, m_conv_b, m_conv_ln_g, m_conv_ln_b, m_w_kv_mem, m_w_proj_attn, m_w_proj_conv, m_w_proj_mem, m_w_out, m_norm_ffn_pre, m_norm_ffn_post, m_w_gate_up, m_w_down, v_norm_mix_pre, v_norm_mix_post, v_norm_mem, v_w_in, v_b_forget, v_conv_w, v_conv_b, v_conv_ln_g, v_conv_ln_b, v_w_kv_mem, v_w_proj_attn, v_w_proj_conv, v_w_proj_mem, v_w_out, v_norm_ffn_pre, v_norm_ffn_post, v_w_gate_up, v_w_down):
    local = dict(locals())
    lead = {n: local[n].shape[:-2] for n in WEIGHT_ORDER}
    two_d = lambda a: a.reshape(a.shape[-2:])
    params = {n: two_d(local[n]) for n in WEIGHT_ORDER}
    moms = {n: two_d(local["m_" + n]) for n in WEIGHT_ORDER}
    vels = {n: two_d(local["v_" + n]) for n in WEIGHT_ORDER}
    loss, dx, grads, delta, new_m, new_v = _step(params, moms, vels, two_d(x), two_d(mem), two_d(loss_target))
    outs = [loss, dx.reshape(x.shape)]
    for group in (grads, delta, new_m, new_v):
        outs += [group[n].reshape(lead[n] + group[n].shape) for n in WEIGHT_ORDER]
    return tuple(outs)
```

```python
import functools

import jax
import jax.numpy as jnp
from jax import lax
from jax.experimental import pallas as pl
from jax.experimental.pallas import tpu as pltpu

F32 = jnp.float32
MXU = jnp.bfloat16
WIRE = jnp.bfloat16

D = 1024
HF = 16
DH = 64
NP = D // 128
MEM_H = 4
MEM_DH = D // MEM_H
FFN = 2816
CW = 31
CWP = 32
HALO = 32
RMS_EPS = 1e-6
LN_EPS = 1e-5
LR, B1, B2, ADAM_EPS, WD, STEP = 0.001, 0.9, 0.999, 1e-8, 0.01, 10

N_CHIPS = 4
N_DEV = 8
LANES = 128
VMEM_LIMIT = 56 * 1024 * 1024

TM_PROJ = 512
NB_PROJ = 3
TQ = 1024
LOG2E = 1.4426950408889634
LN2 = 0.6931471805599453
QSCALE = DH ** -0.5 * LOG2E
X_BIAS = 0
X_QONE = 6
X_KONE = 8
X_ROWS = 16
TM_CONV = 256
CONV_ROWS = 128
SUB = 8
TM_ROW = 256
TM_WIDE = 2048
TS_WG = 1024
WG_CAP = 1408
SMALL_ROWS = 16


def _pc(body, **kw):
    return pl.pallas_call(body, **kw)


def _cp(n_axes):
    return pltpu.CompilerParams(dimension_semantics=("arbitrary",) * n_axes, vmem_limit_bytes=VMEM_LIMIT)


def _sds(shape, dtype):
    return jax.ShapeDtypeStruct(shape, dtype)


def _dot(a, b):
    return jnp.dot(a, b, preferred_element_type=F32)


def _dot_nt(a, b):
    return lax.dot_general(a, b, (((1,), (1,)), ((), ())), preferred_element_type=F32)


def _dot_tn(a, b):
    return lax.dot_general(a, b, (((0,), (0,)), ((), ())), preferred_element_type=F32)


def _rms(u):
    return lax.rsqrt(jnp.mean(u * u, axis=-1, keepdims=True) + RMS_EPS)


def _rms_bwd(u, r, g, dn):
    w = dn * g
    return r * w - u * (r * r * r) * jnp.mean(u * w, axis=-1, keepdims=True)


def _sigmoid(z):
    return 1.0 / (1.0 + jnp.exp(-z))


def _tile(n, cap):
    if n <= cap:
        return n
    best = None
    for t in range(LANES, cap + 1, LANES):
        if n % t == 0:
            best = t
    assert best is not None, (n, cap)
    return best


def _rowtile(rows, cols, cap_bytes=1 << 20):
    best = None
    for t in range(8, rows + 1, 8):
        if rows % t == 0 and t * cols * 4 <= cap_bytes:
            best = t
    return best if best is not None else rows


def _split3(v):
    hi = v.astype(jnp.bfloat16)
    r1 = v - hi.astype(F32)
    mid = r1.astype(jnp.bfloat16)
    lo = (r1 - mid.astype(F32)).astype(jnp.bfloat16)
    return hi, mid, lo


def _dot_exact_rhs(a01, v):
    hi, mid, lo = _split3(v)
    return _dot(a01, hi) + _dot(a01, mid) + _dot(a01, lo)


def _in_proj(x, g_pre, w_main, w_f):
    s, d = x.shape
    n = w_main.shape[1]
    tm = min(TM_PROJ, s)
    tn = n // NB_PROJ

    def body(x_ref, g_ref, w_ref, wf_ref, proj_ref, h_ref, flog_ref, hs):
        @pl.when(pl.program_id(1) == 0)
        def _():
            xv = x_ref[...]
            h = (xv * _rms(xv) * g_ref[...]).astype(MXU)
            hs[...] = h
            h_ref[...] = h
            flog_ref[...] = _dot(h, wf_ref[...])

        res = _dot(hs[...], w_ref[...])

        @pl.when(pl.program_id(1) == 0)
        def _():
            proj_ref[:, pl.ds(0, d)] = (res[:, :d] * QSCALE).astype(MXU)
            proj_ref[:, pl.ds(d, tn - d)] = res[:, d:].astype(MXU)

        @pl.when(pl.program_id(1) != 0)
        def _():
            proj_ref[...] = res.astype(MXU)

    assert tn >= d
    return _pc(
        body, name="in_proj", grid=(s // tm, NB_PROJ),
        in_specs=[pl.BlockSpec((tm, d), lambda i, j: (i, 0)), pl.BlockSpec((1, d), lambda i, j: (0, 0)),
                  pl.BlockSpec((d, tn), lambda i, j: (0, j)), pl.BlockSpec((d, LANES), lambda i, j: (0, 0))],
        out_specs=[pl.BlockSpec((tm, tn), lambda i, j: (i, j)), pl.BlockSpec((tm, d), lambda i, j: (i, 0)),
                   pl.BlockSpec((tm, LANES), lambda i, j: (i, 0))],
        out_shape=[_sds((s, n), MXU), _sds((s, d), MXU), _sds((s, LANES), F32)],
        scratch_shapes=[pltpu.VMEM((tm, d), MXU)], compiler_params=_cp(2),
    )(x, g_pre, w_main, w_f)


def _log_sigmoid(z):
    e = jnp.exp(-jnp.abs(z))
    log1p_e = jnp.where(e < 1e-3, e * (1.0 - 0.5 * e), jnp.log(1.0 + e))
    return jnp.minimum(z, 0.0) - log1p_e


def _logf_cumsum(flog, b_f):
    s = flog.shape[0]
    ch = LANES

    def body(f_ref, b_ref, c_ref):
        r = lax.broadcasted_iota(jnp.int32, (ch, ch), 0)
        q = lax.broadcasted_iota(jnp.int32, (ch, ch), 1)
        tri = jnp.where(r >= q, 1.0, 0.0).astype(jnp.bfloat16)

        def step(i, carry):
            rows = pl.ds(pl.multiple_of(i * ch, ch), ch)
            lf = _log_sigmoid(f_ref[rows, :] + b_ref[...])
            c_ref[rows, :] = _dot_exact_rhs(tri, lf) + carry
            return carry + jnp.sum(lf, axis=0, keepdims=True)

        lax.fori_loop(0, s // ch, step, jnp.zeros((1, LANES), F32))

    return _pc(body, name="logf_cumsum", out_shape=_sds((s, LANES), F32),
               compiler_params=pltpu.CompilerParams(vmem_limit_bytes=VMEM_LIMIT))(flog, b_f)


def _logf_cumsum_bwd(flog, b_f, dc):
    s = flog.shape[0]
    ch = LANES

    def body(f_ref, b_ref, dc_ref, df_ref, db_ref):
        r = lax.broadcasted_iota(jnp.int32, (ch, ch), 0)
        q = lax.broadcasted_iota(jnp.int32, (ch, ch), 1)
        tri = jnp.where(r <= q, 1.0, 0.0).astype(jnp.bfloat16)
        nch = s // ch

        def step(t, carry):
            tail, dbsum = carry
            i = nch - 1 - t
            rows = pl.ds(pl.multiple_of(i * ch, ch), ch)
            dcv = dc_ref[rows, :]
            dlf = _dot_exact_rhs(tri, dcv) + tail
            z = f_ref[rows, :] + b_ref[...]
            df = dlf * _sigmoid(-z)
            df_ref[rows, :] = df.astype(MXU)
            return tail + jnp.sum(dcv, axis=0, keepdims=True), dbsum + jnp.sum(df, axis=0, keepdims=True)

        zero = jnp.zeros((1, LANES), F32)
        _, dbsum = lax.fori_loop(0, nch, step, (zero, zero))
        db_ref[...] = jnp.broadcast_to(dbsum, db_ref.shape)

    return _pc(body, name="logf_cumsum_bwd", out_shape=[_sds((s, LANES), MXU), _sds((8, LANES), F32)],
               compiler_params=pltpu.CompilerParams(vmem_limit_bytes=VMEM_LIMIT))(flog, b_f, dc)


def _head_masks(rows):
    lane = lax.broadcasted_iota(jnp.int32, (rows, LANES), 1)
    return lane < DH, lane >= DH


def _ext_masks(rows, key_side):
    lane = lax.broadcasted_iota(jnp.int32, (rows, 2 * LANES), 1)
    ext = lane - LANES
    out = []
    for a in range(2):
        head = (lane >= a * DH) & (lane < (a + 1) * DH)
        bias = (ext >= X_BIAS + 3 * a) & (ext < X_BIAS + 3 * a + 3)
        one = ext == (X_KONE if key_side else X_QONE) + a
        out.append(head | bias | one)
    return out


def _fox_prep(proj, ccol):
    s = proj.shape[0]
    tm = min(TM_WIDE, s)

    def body(q_ref, k_ref, c_ref, qx_ref, kx_ref):
        lane = lax.broadcasted_iota(jnp.int32, (tm, LANES), 1)
        qx_ref[:, pl.ds(0, LANES)] = q_ref[...]
        qx_ref[:, pl.ds(LANES, LANES)] = jnp.where(lane < X_QONE + 2, 1.0, 0.0).astype(MXU)
        kext = jnp.where((lane >= X_KONE) & (lane < X_KONE + 2), 1.0, 0.0).astype(jnp.bfloat16)
        for a in range(2):
            terms = _split3(c_ref[0, :, a:a + 1] * (-LOG2E))
            for t, term in enumerate(terms):
                kext = jnp.where(lane == X_BIAS + 3 * a + t, term, kext)
        kx_ref[:, pl.ds(0, LANES)] = k_ref[...]
        kx_ref[:, pl.ds(LANES, LANES)] = kext.astype(MXU)

    wide = pl.BlockSpec((tm, 2 * LANES), lambda p, i: (i, p))
    return _pc(
        body, name="fox_prep", grid=(NP, s // tm),
        in_specs=[pl.BlockSpec((tm, LANES), lambda p, i: (i, p)), pl.BlockSpec((tm, LANES), lambda p, i: (i, NP + p)),
                  pl.BlockSpec((1, tm, 2), lambda p, i: (p, i, 0))],
        out_specs=[wide, wide], out_shape=[_sds((s, NP * 2 * LANES), MXU)] * 2, compiler_params=_cp(2),
    )(proj, proj, ccol)


def _fox_fwd(proj, qx, kx, plan):
    s = proj.shape[0]
    tq = min(TQ, s)
    nq = s // tq
    nx = len(plan["arrays"])

    def body(*refs):
        q_ref, k_ref, v_ref = refs[:3]
        o_ref, lse_ref = refs[3 + nx:5 + nx]
        p_id, i = pl.program_id(0), pl.program_id(1)
        _host_plan(plan, refs[3:3 + nx], refs[5 + nx:5 + 2 * nx], refs[5 + 2 * nx:], (p_id == 0) & (i == 0),
                   (p_id == NP // 2) & (i == 0), (p_id == NP - 1) & (i == nq - 1))
        qv = q_ref[...]
        qmask = _ext_masks(tq, False)
        hmask = _head_masks(tq)
        qas = [jnp.where(qmask[a], qv, jnp.zeros_like(qv)) for a in range(2)]
        row = lax.broadcasted_iota(jnp.int32, (tq, tq), 0)
        col = lax.broadcasted_iota(jnp.int32, (tq, tq), 1)

        def blk(j, carry, diag=False):
            rows = pl.ds(pl.multiple_of(j * tq, tq), tq)
            kj = k_ref[rows, :]
            vj = v_ref[rows, :]
            out = []
            for a in range(2):
                m, acc = carry[a]
                sc = _dot_nt(qas[a], kj)
                if diag:
                    sc = jnp.where(row >= col, sc, -jnp.inf)
                m_new = jnp.maximum(m, jnp.max(sc, axis=-1, keepdims=True))
                p = jnp.exp2(sc - m_new)
                va = jnp.where(hmask[a], vj, jnp.ones_like(vj))
                out.append((m_new, jnp.exp2(m - m_new) * acc + _dot(p.astype(MXU), va)))
            return tuple(out)

        init = (jnp.full((tq, 1), -jnp.inf, F32), jnp.zeros((tq, LANES), F32))
        res = blk(i, lax.fori_loop(0, i, blk, (init, init)), True)
        lane = lax.broadcasted_iota(jnp.int32, (tq, LANES), 1)
        outs, lses = [], []
        for a in range(2):
            m, acc = res[a]
            l = jnp.sum(jnp.where(lane == DH * (1 - a), acc, 0.0), axis=-1, keepdims=True)
            outs.append(acc / l)
            lses.append(m + jnp.log(l) * LOG2E)
        o_ref[...] = jnp.where(hmask[0], outs[0], outs[1]).astype(MXU)
        lane2 = lax.broadcasted_iota(jnp.int32, (tq, 2), 1)
        lse_ref[0] = jnp.where(lane2 == 0, lses[0], lses[1])

    return _pc(
        body, name="fox_fwd", grid=(NP, nq),
        in_specs=[pl.BlockSpec((tq, 2 * LANES), lambda p, i: (i, p)),
                  pl.BlockSpec((s, 2 * LANES), lambda p, i: (0, p)),
                  pl.BlockSpec((s, LANES), lambda p, i: (0, 2 * NP + p))] + [ANY] * nx,
        out_specs=[pl.BlockSpec((tq, LANES), lambda p, i: (i, p)),
                   pl.BlockSpec((1, tq, 2), lambda p, i: (p, i, 0))] + [ANY] * nx,
        out_shape=[_sds((s, D), MXU), _sds((NP, s, 2), F32)] + plan["out_shape"],
        scratch_shapes=plan["scratch"], compiler_params=_cp(2),
    )(qx, kx, proj, *plan["arrays"])


def _fox_bwd(proj, qx, kx, do, lse_row, delta_row, plan):
    s = proj.shape[0]
    tq = min(TQ, s)
    nq = s // tq
    nx = len(plan["arrays"])

    def body(*refs):
        k_ref, v_ref, q_ref, do_ref, lse_ref, dl_ref = refs[:6]
        dq_ref, dqs_ref, dk_ref, dks_ref, dv_ref = refs[6 + nx:11 + nx]
        dq_acc = refs[11 + 2 * nx]
        p_id, j = pl.program_id(0), pl.program_id(1)
        _host_plan(plan, refs[6:6 + nx], refs[11 + nx:11 + 2 * nx], refs[12 + 2 * nx:], (p_id == 0) & (j == 0),
                   (p_id == NP // 2) & (j == 0), (p_id == NP - 1) & (j == nq - 1))

        @pl.when(j == 0)
        def _():
            dq_acc[...] = jnp.zeros_like(dq_acc)

        kv = k_ref[...]
        v2 = v_ref[...]
        kmask = _ext_masks(tq, True)
        qmask = _ext_masks(tq, False)
        hmask = _head_masks(tq)
        row = lax.broadcasted_iota(jnp.int32, (tq, tq), 0)
        col = lax.broadcasted_iota(jnp.int32, (tq, tq), 1)
        carry = (jnp.zeros((tq, 2 * LANES), F32), jnp.zeros((tq, LANES), F32))
        for a in range(2):
            ka = jnp.where(kmask[a], kv, jnp.zeros_like(kv))
            va = jnp.where(hmask[a], v2, jnp.zeros_like(v2))

            def blk(i, carry, diag, a=a, ka=ka, va=va):
                dk_a, dv_a = carry
                rows = pl.ds(pl.multiple_of(i * tq, tq), tq)
                qi = q_ref[rows, :]
                doi = do_ref[rows, :]
                qa = jnp.where(qmask[a], qi, jnp.zeros_like(qi))
                doa = jnp.where(hmask[a], doi, jnp.zeros_like(doi))
                st = _dot_nt(ka, qi)
                if diag:
                    st = jnp.where(col >= row, st, -jnp.inf)
                pt = jnp.exp2(st - lse_ref[0, i, a:a + 1, :])
                dv_a = dv_a + _dot(pt.astype(MXU), doa)
                dpt = _dot_nt(va, doi)
                dsb = (pt * (dpt - dl_ref[0, i, a:a + 1, :])).astype(MXU)
                dk_a = dk_a + _dot(dsb, qa)
                dq_acc[rows, :] += _dot_tn(dsb, ka)
                return dk_a, dv_a

            carry = blk(j, carry, True)
            carry = lax.fori_loop(j + 1, nq, functools.partial(blk, diag=False), carry)
        dk_acc, dv_acc = carry
        dk_ref[...] = (dk_acc[:, :LANES] * LN2).astype(MXU)
        dks_ref[0, 0] = dk_acc[:, LANES:].T[:X_ROWS, :]
        dv_ref[...] = dv_acc.astype(MXU)

        @pl.when(j == nq - 1)
        def _():
            dq_ref[...] = (dq_acc[:, pl.ds(0, LANES)] * DH ** -0.5).astype(MXU)
            for t in range(nq):
                dqs_ref[0, :, pl.ds(t * tq, tq)] = dq_acc[pl.ds(t * tq, tq), pl.ds(LANES, LANES)].T[:X_ROWS, :]

    stat = pl.BlockSpec((1, nq, 2, tq), lambda p, j: (p, 0, 0, 0))
    whole = pl.BlockSpec((s, LANES), lambda p, j: (0, p))
    tile = pl.BlockSpec((tq, LANES), lambda p, j: (j, p))
    return _pc(
        body, name="fox_bwd", grid=(NP, nq),
        in_specs=[pl.BlockSpec((tq, 2 * LANES), lambda p, j: (j, p)),
                  pl.BlockSpec((tq, LANES), lambda p, j: (j, 2 * NP + p)),
                  pl.BlockSpec((s, 2 * LANES), lambda p, j: (0, p)),
                  whole, stat, stat] + [ANY] * nx,
        out_specs=[whole, pl.BlockSpec((1, X_ROWS, s), lambda p, j: (p, 0, 0)), tile,
                   pl.BlockSpec((1, 1, X_ROWS, tq), lambda p, j: (p, j, 0, 0)), tile] + [ANY] * nx,
        out_shape=[_sds((s, D), MXU), _sds((NP, X_ROWS, s), F32), _sds((s, D), MXU),
                   _sds((NP, nq, X_ROWS, tq), F32), _sds((s, D), MXU)] + plan["out_shape"],
        scratch_shapes=[pltpu.VMEM((s, 2 * LANES), F32)] + plan["scratch"], compiler_params=_cp(2),
    )(kx, proj, qx, do, lse_row, delta_row, *plan["arrays"])


def _glu(a, gate):
    return a.astype(F32) * _sigmoid(gate.astype(F32))


def _store_blocked(buf, row0, val):
    for c in range(D // LANES):
        buf[0, c, pl.ds(row0, val.shape[0]), :] = val[:, c * LANES:(c + 1) * LANES]


def _fill_shifted(buf):
    n = buf.shape[2] - SUB
    for r in range(1, SUB):
        buf[r, :, pl.ds(0, n), :] = buf[0, :, pl.ds(r, n), :]


def _shifted(buf, off, rows, c):
    r = off % SUB
    return buf[r, c, pl.ds(off - r, rows), :]


def _conv_fwd(proj, cw, cb, lg, lb):
    s = proj.shape[0]
    tm = min(TM_CONV, s)
    hb = tm // HALO

    rcw = min(CONV_ROWS, tm)

    def body(a_ref, g_ref, ah_ref, gh_ref, w_ref, cb_ref, lg_ref, lb_ref, y_ref, o_ref, gsh):
        i = pl.program_id(0)
        _store_blocked(gsh, 0, jnp.where(i > 0, _glu(ah_ref[...], gh_ref[...]), 0.0))
        _store_blocked(gsh, HALO, _glu(a_ref[...], g_ref[...]))
        _fill_shifted(gsh)
        for c in range(D // LANES):
            cols = pl.ds(c * LANES, LANES)
            for rc in range(tm // rcw):
                acc = jnp.broadcast_to(cb_ref[:, cols], (rcw, LANES))
                for t in range(CW):
                    acc = acc + w_ref[t:t + 1, cols] * _shifted(gsh, HALO - (CW - 1) + t + rc * rcw, rcw, c)
                y_ref[pl.ds(rc * rcw, rcw), cols] = acc
        acc = y_ref[...]
        mu = jnp.mean(acc, axis=-1, keepdims=True)
        xc = acc - mu
        r = lax.rsqrt(jnp.mean(xc * xc, axis=-1, keepdims=True) + LN_EPS)
        nrm = xc * r * lg_ref[...] + lb_ref[...]
        o_ref[...] = (nrm * _sigmoid(nrm)).astype(MXU)

    vec = pl.BlockSpec((1, D), lambda i: (0, 0))
    return _pc(
        body, name="conv_fwd", grid=(s // tm,),
        in_specs=[pl.BlockSpec((tm, D), lambda i: (i, 3)), pl.BlockSpec((tm, D), lambda i: (i, 4)),
                  pl.BlockSpec((HALO, D), lambda i: (jnp.maximum(i * hb - 1, 0), 3)),
                  pl.BlockSpec((HALO, D), lambda i: (jnp.maximum(i * hb - 1, 0), 4)),
                  pl.BlockSpec((CWP, D), lambda i: (0, 0)), vec, vec, vec],
        out_specs=[pl.BlockSpec((tm, D), lambda i: (i, 0)), pl.BlockSpec((tm, D), lambda i: (i, 0))],
        out_shape=[_sds((s, D), F32), _sds((s, D), MXU)],
        scratch_shapes=[pltpu.VMEM((SUB, D // LANES, tm + HALO, LANES), F32)], compiler_params=_cp(1),
    )(proj, proj, proj, proj, cw, cb, lg, lb)


def _conv_bwd(proj, y, do, cw, lg, lb):
    s = proj.shape[0]
    tm = min(TM_CONV, s)
    hb = tm // HALO
    nt = s // tm
    last_hblk = s // HALO - 1

    def ln_bwd(yv, dov, lgv, lbv):
        mu = jnp.mean(yv, axis=-1, keepdims=True)
        xc = yv - mu
        r = lax.rsqrt(jnp.mean(xc * xc, axis=-1, keepdims=True) + LN_EPS)
        xh = xc * r
        nrm = xh * lgv + lbv
        sg = _sigmoid(nrm)
        dn = dov.astype(F32) * (sg * (1.0 + nrm * (1.0 - sg)))
        wv = dn * lgv
        dy = r * (wv - jnp.mean(wv, axis=-1, keepdims=True) - xh * jnp.mean(wv * xh, axis=-1, keepdims=True))
        return dy, dn, xh

    rcw = min(CONV_ROWS, tm)

    def body(a_ref, g_ref, ah_ref, gh_ref, y_ref, yn_ref, do_ref, don_ref, w_ref, lg_ref, lb_ref,
             da_ref, dg_ref, dw_ref, vec_ref, gsh, dysh, dwacc):
        i = pl.program_id(0)

        @pl.when(i == 0)
        def _():
            dwacc[...] = jnp.zeros_like(dwacc)
            vec_ref[...] = jnp.zeros_like(vec_ref)

        lgv, lbv = lg_ref[...], lb_ref[...]
        _store_blocked(gsh, 0, jnp.where(i > 0, _glu(ah_ref[...], gh_ref[...]), 0.0))
        _store_blocked(gsh, HALO, _glu(a_ref[...], g_ref[...]))
        _fill_shifted(gsh)
        dy, dn, xh = ln_bwd(y_ref[...], do_ref[...], lgv, lbv)
        dyn, _, _ = ln_bwd(yn_ref[...], don_ref[...], lgv, lbv)
        _store_blocked(dysh, 0, dy)
        _store_blocked(dysh, tm, jnp.where(i < nt - 1, dyn, 0.0))
        _fill_shifted(dysh)
        vec_ref[0:1, :] += jnp.sum(dn * xh, axis=0, keepdims=True)
        vec_ref[1:2, :] += jnp.sum(dn, axis=0, keepdims=True)
        vec_ref[2:3, :] += jnp.sum(dy, axis=0, keepdims=True)
        for c in range(D // LANES):
            cols = pl.ds(c * LANES, LANES)
            for rc in range(tm // rcw):
                rows = pl.ds(rc * rcw, rcw)
                dyc = dysh[0, c, rows, :]
                dgl = jnp.zeros((rcw, LANES), F32)
                for t in range(CW):
                    dgl = dgl + w_ref[t:t + 1, cols] * _shifted(dysh, CW - 1 - t + rc * rcw, rcw, c)
                    prod = dyc * _shifted(gsh, HALO - (CW - 1) + t + rc * rcw, rcw, c)
                    dwacc[t, :, cols] += jnp.sum(prod.reshape(rcw // SUB, SUB, LANES), axis=0)
                av = a_ref[rows, cols].astype(F32)
                sgate = _sigmoid(g_ref[rows, cols].astype(F32))
                da_ref[rows, cols] = (dgl * sgate).astype(MXU)
                dg_ref[rows, cols] = (dgl * av * sgate * (1.0 - sgate)).astype(MXU)

        @pl.when(i == nt - 1)
        def _():
            dw_ref[...] = jnp.sum(dwacc[...], axis=1)

    vec = pl.BlockSpec((1, D), lambda i: (0, 0))
    cur = lambda c: pl.BlockSpec((tm, D), lambda i: (i, c))
    prv = lambda c: pl.BlockSpec((HALO, D), lambda i: (jnp.maximum(i * hb - 1, 0), c))
    nxt = pl.BlockSpec((HALO, D), lambda i: (jnp.minimum((i + 1) * hb, last_hblk), 0))
    return _pc(
        body, name="conv_bwd", grid=(nt,),
        in_specs=[cur(3), cur(4), prv(3), prv(4), cur(0), nxt, cur(0), nxt,
                  pl.BlockSpec((CWP, D), lambda i: (0, 0)), vec, vec],
        out_specs=[cur(0), cur(0), pl.BlockSpec((CWP, D), lambda i: (0, 0)), pl.BlockSpec((8, D), lambda i: (0, 0))],
        out_shape=[_sds((s, D), MXU), _sds((s, D), MXU), _sds((CWP, D), F32), _sds((8, D), F32)],
        scratch_shapes=[pltpu.VMEM((SUB, D // LANES, tm + HALO, LANES), F32)] * 2 + [pltpu.VMEM((CWP, SUB, D), F32)],
        compiler_params=_cp(1),
    )(proj, proj, proj, proj, y, y, do, do, cw, lg, lb)


def _mem_kv(mem, g_mem, w_kv):
    mm = mem.shape[0]

    def body(m_ref, g_ref, w_ref, mn_ref, kv_ref):
        mv = m_ref[...]
        mn = (mv * _rms(mv) * g_ref[...]).astype(MXU)
        mn_ref[...] = mn
        kv_ref[...] = _dot(mn, w_ref[...]).astype(MXU)

    return _pc(body, name="mem_kv", out_shape=[_sds((mm, D), MXU), _sds((mm, 2 * D), MXU)],
               compiler_params=pltpu.CompilerParams(vmem_limit_bytes=VMEM_LIMIT))(mem, g_mem, w_kv)


def _mem_kv_bwd(mem, g_mem, w_kv, dkv):
    mm = mem.shape[0]

    def body(m_ref, w_ref, dkv_ref, o_ref):
        mv = m_ref[...]
        dmn = _dot_nt(dkv_ref[...], w_ref[...])
        o_ref[...] = jnp.broadcast_to(jnp.sum(dmn * mv * _rms(mv), axis=0, keepdims=True), o_ref.shape)

    return _pc(body, name="mem_kv_bwd", out_shape=_sds((8, D), F32),
               compiler_params=pltpu.CompilerParams(vmem_limit_bytes=VMEM_LIMIT))(mem, w_kv, dkv)


def _mem_attn_fwd(proj, kv):
    s = proj.shape[0]
    mm = kv.shape[0]
    tm = min(TM_PROJ, s)
    scale = MEM_DH ** -0.5

    def body(q_ref, kv_ref, o_ref):
        for h in range(MEM_H):
            cols = pl.ds(h * MEM_DH, MEM_DH)
            qh = q_ref[:, cols] * scale
            sc = _dot_nt(qh, kv_ref[:, cols])
            m = jnp.max(sc, axis=-1, keepdims=True)
            e = jnp.exp(sc - m)
            p = e / jnp.sum(e, axis=-1, keepdims=True)
            o_ref[:, cols] = _dot(p.astype(MXU), kv_ref[:, pl.ds(D + h * MEM_DH, MEM_DH)]).astype(MXU)

    return _pc(
        body, name="mem_attn_fwd", grid=(s // tm,),
        in_specs=[pl.BlockSpec((tm, D), lambda i: (i, 5)), pl.BlockSpec((mm, 2 * D), lambda i: (0, 0))],
        out_specs=pl.BlockSpec((tm, D), lambda i: (i, 0)), out_shape=_sds((s, D), MXU), compiler_params=_cp(1),
    )(proj, kv)


def _mem_attn_bwd(proj, kv, do):
    s = proj.shape[0]
    mm = kv.shape[0]
    tm = min(TM_PROJ, s)
    scale = MEM_DH ** -0.5

    def body(q_ref, kv_ref, do_ref, dq_ref, dkv_ref):
        @pl.when(pl.program_id(0) == 0)
        def _():
            dkv_ref[...] = jnp.zeros_like(dkv_ref)

        for h in range(MEM_H):
            cols = pl.ds(h * MEM_DH, MEM_DH)
            vcols = pl.ds(D + h * MEM_DH, MEM_DH)
            qh = q_ref[:, cols]
            kh = kv_ref[:, cols] * scale
            doh = do_ref[:, cols]
            st = _dot_nt(kh, qh)
            m = jnp.max(st, axis=0, keepdims=True)
            e = jnp.exp(st - m)
            pt = e / jnp.sum(e, axis=0, keepdims=True)
            dpt = _dot_nt(kv_ref[:, vcols], doh)
            dst = pt * (dpt - jnp.sum(pt * dpt, axis=0, keepdims=True))
            dsb = dst.astype(MXU)
            dkv_ref[:, vcols] += _dot(pt.astype(MXU), doh)
            dkv_ref[:, cols] += _dot(dsb, qh) * scale
            dq_ref[:, cols] = _dot_tn(dsb, kh).astype(MXU)

    return _pc(
        body, name="mem_attn_bwd", grid=(s // tm,),
        in_specs=[pl.BlockSpec((tm, D), lambda i: (i, 5)), pl.BlockSpec((mm, 2 * D), lambda i: (0, 0)),
                  pl.BlockSpec((tm, D), lambda i: (i, 0))],
        out_specs=[pl.BlockSpec((tm, D), lambda i: (i, 0)), pl.BlockSpec((mm, 2 * D), lambda i: (0, 0))],
        out_shape=[_sds((s, D), MXU), _sds((mm, 2 * D), F32)], compiler_params=_cp(1),
    )(proj, kv, do)


def _resident(n):
    return [pltpu.VMEM((n, D, D), MXU), pltpu.SemaphoreType.DMA((n,))]


def _load_resident(hbm_refs, wbuf, sems):
    @pl.when(pl.program_id(0) == 0)
    def _():
        cps = [pltpu.make_async_copy(r, wbuf.at[k], sems.at[k]) for k, r in enumerate(hbm_refs)]
        for cp in cps:
            cp.start()
        for cp in cps:
            cp.wait()


def _merge_out(oa, oc, om, proj, x, wpa, wpc, wpm, wout, g_post, g_fpre):
    s = x.shape[0]
    tm = min(TM_ROW, s)

    def body(oa_ref, oc_ref, om_ref, gl_ref, x_ref, gp_ref, gf_ref, wpa_h, wpc_h, wpm_h, wout_h,
             pa_ref, pc_ref, pm_ref, mg_ref, z_ref, x1_ref, h2_ref, wbuf, sems):
        _load_resident([wpa_h, wpc_h, wpm_h, wout_h], wbuf, sems)
        merged = jnp.zeros((tm, D), F32)
        for b, (o_ref, p_ref) in enumerate(((oa_ref, pa_ref), (oc_ref, pc_ref), (om_ref, pm_ref))):
            pb = _dot(o_ref[...], wbuf[b])
            p_ref[...] = pb.astype(MXU)
            merged = merged + _sigmoid(gl_ref[:, pl.ds(b * D, D)].astype(F32)) * pb
        mg = merged.astype(MXU)
        mg_ref[...] = mg
        z = _dot(mg, wbuf[3])
        z_ref[...] = z
        x1 = x_ref[...] + z * _rms(z) * gp_ref[...]
        x1_ref[...] = x1
        h2_ref[...] = (x1 * _rms(x1) * gf_ref[...]).astype(MXU)

    rows = pl.BlockSpec((tm, D), lambda i: (i, 0))
    vec = pl.BlockSpec((1, D), lambda i: (0, 0))
    anyspec = pl.BlockSpec(memory_space=pl.ANY)
    return _pc(
        body, name="merge_out", grid=(s // tm,),
        in_specs=[rows, rows, rows, pl.BlockSpec((tm, 3 * D), lambda i: (i, 2)), rows, vec, vec,
                  anyspec, anyspec, anyspec, anyspec],
        out_specs=[rows] * 7,
        out_shape=[_sds((s, D), MXU)] * 4 + [_sds((s, D), F32)] * 2 + [_sds((s, D), MXU)],
        scratch_shapes=_resident(4), compiler_params=_cp(1),
    )(oa, oc, om, proj, x, g_post, g_fpre, wpa, wpc, wpm, wout)


def _ffn_up(h2, w_gu):
    s = h2.shape[0]
    tm = min(TM_PROJ, s)
    nb = 2
    bw = FFN // nb

    def body(h_ref, wg_ref, wu_ref, gf_ref, uf_ref, act_ref):
        hv = h_ref[...]
        gf = _dot(hv, wg_ref[...])
        uf = _dot(hv, wu_ref[...])
        gf_ref[...] = gf.astype(MXU)
        uf_ref[...] = uf.astype(MXU)
        act_ref[...] = (gf * _sigmoid(gf) * uf).astype(MXU)

    out = pl.BlockSpec((tm, bw), lambda i, j: (i, j))
    return _pc(
        body, name="ffn_up", grid=(s // tm, nb),
        in_specs=[pl.BlockSpec((tm, D), lambda i, j: (i, 0)), pl.BlockSpec((D, bw), lambda i, j: (0, j)),
                  pl.BlockSpec((D, bw), lambda i, j: (0, nb + j))],
        out_specs=[out, out, out], out_shape=[_sds((s, FFN), MXU)] * 3, compiler_params=_cp(2),
    )(h2, w_gu, w_gu)


def _ffn_down_loss(act, w_d, x1, tgt, g_fpost):
    s = act.shape[0]
    tm = min(TM_PROJ, s)

    def body(a_ref, w_ref, x1_ref, t_ref, g_ref, dffn_ref, dy_ref, vec_ref, loss_ref):
        @pl.when(pl.program_id(0) == 0)
        def _():
            vec_ref[...] = jnp.zeros_like(vec_ref)
            loss_ref[...] = jnp.zeros_like(loss_ref)

        ffn = _dot(a_ref[...], w_ref[...])
        r = _rms(ffn)
        gv = g_ref[...]
        e = x1_ref[...] + ffn * r * gv - t_ref[...]
        loss_ref[...] += jnp.sum(e * e) * (0.5 / D)
        dy = e * (1.0 / D)
        dy_ref[...] = dy
        vec_ref[0:1, :] += jnp.sum(dy * ffn * r, axis=0, keepdims=True)
        dffn_ref[...] = _rms_bwd(ffn, r, gv, dy).astype(MXU)

    rows = pl.BlockSpec((tm, D), lambda i: (i, 0))
    return _pc(
        body, name="ffn_down_loss", grid=(s // tm,),
        in_specs=[pl.BlockSpec((tm, FFN), lambda i: (i, 0)), pl.BlockSpec((FFN, D), lambda i: (0, 0)), rows, rows,
                  pl.BlockSpec((1, D), lambda i: (0, 0))],
        out_specs=[rows, rows, pl.BlockSpec((8, D), lambda i: (0, 0)), pl.BlockSpec((8, LANES), lambda i: (0, 0))],
        out_shape=[_sds((s, D), MXU), _sds((s, D), F32), _sds((8, D), F32), _sds((8, LANES), F32)],
        compiler_params=_cp(1),
    )(act, w_d, x1, tgt, g_fpost)


def _ffn_down_bwd(dffn, w_d, gf, uf):
    s = dffn.shape[0]
    tm = min(TM_ROW, s)

    def body(d_ref, w_ref, gf_ref, uf_ref, dgf_ref, duf_ref):
        da = _dot_nt(d_ref[...], w_ref[...])
        gf = gf_ref[...].astype(F32)
        sg = _sigmoid(gf)
        duf_ref[...] = (da * gf * sg).astype(MXU)
        dgf_ref[...] = (da * uf_ref[...].astype(F32) * (sg * (1.0 + gf * (1.0 - sg)))).astype(MXU)

    wide = pl.BlockSpec((tm, FFN), lambda i: (i, 0))
    return _pc(
        body, name="ffn_down_bwd", grid=(s // tm,),
        in_specs=[pl.BlockSpec((tm, D), lambda i: (i, 0)), pl.BlockSpec((FFN, D), lambda i: (0, 0)), wide, wide],
        out_specs=[wide, wide], out_shape=[_sds((s, FFN), MXU)] * 2, compiler_params=_cp(1),
    )(dffn, w_d, gf, uf)


def _ffn_up_bwd(dgf, duf, w_gu, x1, dy, z, g_fpre, g_post):
    s = x1.shape[0]
    tm = min(TM_ROW, s)

    def body(dgf_ref, duf_ref, w_ref, x1_ref, dy_ref, z_ref, gf_ref, gp_ref, dx1_ref, dz_ref, vec_ref):
        @pl.when(pl.program_id(0) == 0)
        def _():
            vec_ref[...] = jnp.zeros_like(vec_ref)

        dh2 = _dot_nt(dgf_ref[...], w_ref[:, pl.ds(0, FFN)]) + _dot_nt(duf_ref[...], w_ref[:, pl.ds(FFN, FFN)])
        x1 = x1_ref[...]
        r2 = _rms(x1)
        vec_ref[0:1, :] += jnp.sum(dh2 * x1 * r2, axis=0, keepdims=True)
        dx1 = dy_ref[...] + _rms_bwd(x1, r2, gf_ref[...], dh2)
        dx1_ref[...] = dx1
        z = z_ref[...]
        rz = _rms(z)
        vec_ref[1:2, :] += jnp.sum(dx1 * z * rz, axis=0, keepdims=True)
        dz_ref[...] = _rms_bwd(z, rz, gp_ref[...], dx1).astype(MXU)

    rows = pl.BlockSpec((tm, D), lambda i: (i, 0))
    wide = pl.BlockSpec((tm, FFN), lambda i: (i, 0))
    vec = pl.BlockSpec((1, D), lambda i: (0, 0))
    return _pc(
        body, name="ffn_up_bwd", grid=(s // tm,),
        in_specs=[wide, wide, pl.BlockSpec((D, 2 * FFN), lambda i: (0, 0)), rows, rows, rows, vec, vec],
        out_specs=[rows, rows, pl.BlockSpec((8, D), lambda i: (0, 0))],
        out_shape=[_sds((s, D), F32), _sds((s, D), MXU), _sds((8, D), F32)], compiler_params=_cp(1),
    )(dgf, duf, w_gu, x1, dy, z, g_fpre, g_post)


def _merge_bwd(dz, proj, pa, pc, pm, oa, wpa, wpc, wpm, wout):
    s = dz.shape[0]
    tm = min(TM_ROW, s)

    def body(dz_ref, gl_ref, pa_ref, pc_ref, pm_ref, oa_ref, wpa_h, wpc_h, wpm_h, wout_h,
             dpa_ref, dpc_ref, dpm_ref, dgl_ref, doa_ref, doc_ref, dom_ref, dl_ref, wbuf, sems):
        _load_resident([wpa_h, wpc_h, wpm_h, wout_h], wbuf, sems)
        dm = _dot_nt(dz_ref[...], wbuf[3])
        quads = ((pa_ref, dpa_ref, doa_ref), (pc_ref, dpc_ref, doc_ref), (pm_ref, dpm_ref, dom_ref))
        for b, (p_ref, dp_ref, do_ref) in enumerate(quads):
            cols = pl.ds(b * D, D)
            gt = _sigmoid(gl_ref[:, cols].astype(F32))
            dp = (dm * gt).astype(MXU)
            dp_ref[...] = dp
            dgl_ref[:, cols] = (dm * p_ref[...].astype(F32) * gt * (1.0 - gt)).astype(MXU)
            dob = _dot_nt(dp, wbuf[b]).astype(MXU)
            do_ref[...] = dob
            if b == 0:
                prod = dob.astype(F32) * oa_ref[...].astype(F32)
                d_i = lax.broadcasted_iota(jnp.int32, (D, LANES), 0)
                h_i = lax.broadcasted_iota(jnp.int32, (D, LANES), 1)
                sel = jnp.where(lax.shift_right_logical(d_i, DH.bit_length() - 1) == h_i, 1.0, 0.0).astype(jnp.bfloat16)
                dl_ref[...] = _dot_exact_rhs_t(prod, sel)

    rows = pl.BlockSpec((tm, D), lambda i: (i, 0))
    anyspec = pl.BlockSpec(memory_space=pl.ANY)
    wide = pl.BlockSpec((tm, 3 * D), lambda i: (i, 2))
    return _pc(
        body, name="merge_bwd", grid=(s // tm,),
        in_specs=[rows, wide, rows, rows, rows, rows, anyspec, anyspec, anyspec, anyspec],
        out_specs=[rows, rows, rows, pl.BlockSpec((tm, 3 * D), lambda i: (i, 0)), rows, rows, rows,
                   pl.BlockSpec((tm, LANES), lambda i: (i, 0))],
        out_shape=[_sds((s, D), MXU)] * 3 + [_sds((s, 3 * D), MXU)] + [_sds((s, D), MXU)] * 3 + [_sds((s, LANES), F32)],
        scratch_shapes=_resident(4), compiler_params=_cp(1),
    )(dz, proj, pa, pc, pm, oa, wpa, wpc, wpm, wout)


def _dot_exact_rhs_t(v, b01):
    hi, mid, lo = _split3(v)
    return _dot(hi, b01) + _dot(mid, b01) + _dot(lo, b01)


def _in_proj_bwd(pieces, df, w_main, w_f, x, dx1, g_pre):
    s = x.shape[0]
    tm = min(TM_ROW, s)
    n_main = w_main.shape[1]
    np_ = len(pieces)

    def body(*refs):
        p_refs = refs[:np_]
        df_ref, x_ref, dx1_ref, g_ref, w_h, wf_ref, dx_ref, vec_ref, wbuf, sem = refs[np_:]

        @pl.when(pl.program_id(0) == 0)
        def _():
            vec_ref[...] = jnp.zeros_like(vec_ref)
            cp = pltpu.make_async_copy(w_h, wbuf, sem)
            cp.start()
            cp.wait()

        dh = _dot_nt(df_ref[...], wf_ref[...])
        for p_ref, (_, c0, nc) in zip(p_refs, pieces):
            dh = dh + _dot_nt(p_ref[...], wbuf[:, pl.ds(c0 * D, nc * D)])
        xv = x_ref[...]
        r = _rms(xv)
        vec_ref[0:1, :] += jnp.sum(dh * xv * r, axis=0, keepdims=True)
        dx_ref[...] = dx1_ref[...] + _rms_bwd(xv, r, g_ref[...], dh)

    rows = pl.BlockSpec((tm, D), lambda i: (i, 0))
    p_specs = [pl.BlockSpec((tm, nc * D), lambda i: (i, 0)) for _, _, nc in pieces]
    return _pc(
        body, name="in_proj_bwd", grid=(s // tm,),
        in_specs=p_specs + [pl.BlockSpec((tm, LANES), lambda i: (i, 0)), rows, rows, pl.BlockSpec((1, D), lambda i: (0, 0)),
                            pl.BlockSpec(memory_space=pl.ANY), pl.BlockSpec((D, LANES), lambda i: (0, 0))],
        out_specs=[rows, pl.BlockSpec((8, D), lambda i: (0, 0))],
        out_shape=[_sds((s, D), F32), _sds((8, D), F32)],
        scratch_shapes=[pltpu.VMEM((D, n_main), MXU), pltpu.SemaphoreType.DMA], compiler_params=_cp(1),
    )(*[p for p, _, _ in pieces], df, x, dx1, g_pre, w_main, w_f)


def _wgrad(xa, dy, name):
    s, k = xa.shape
    n = dy.shape[1]
    ts = min(TS_WG, s)
    tk = _tile(k, WG_CAP)
    tn = _tile(n, WG_CAP)

    def body(x_ref, dy_ref, o_ref, acc):
        @pl.when(pl.program_id(2) == 0)
        def _():
            acc[...] = jnp.zeros_like(acc)

        acc[...] += _dot_tn(x_ref[...], dy_ref[...])

        @pl.when(pl.program_id(2) == s // ts - 1)
        def _():
            o_ref[...] = acc[...].astype(WIRE)

    return _pc(
        body, name=name, grid=(k // tk, n // tn, s // ts),
        in_specs=[pl.BlockSpec((ts, tk), lambda a, b, c: (c, a)), pl.BlockSpec((ts, tn), lambda a, b, c: (c, b))],
        out_specs=pl.BlockSpec((tk, tn), lambda a, b, c: (a, b)), out_shape=_sds((k, n), WIRE),
        scratch_shapes=[pltpu.VMEM((tk, tn), F32)], compiler_params=_cp(3),
    )(xa, dy)


def _pair_sum(g, r1, c_idx):
    _, _, hr, cols = g.shape
    tr = _rowtile(hr, cols)

    def body(c_ref, g_ref, r_ref, o_ref):
        o_ref[0] = (g_ref[0, 0].astype(F32) + r_ref[0].astype(F32)).astype(WIRE)

    return _pc(
        body, name="pair_sum_%dx%d" % (hr, cols), out_shape=_sds((N_CHIPS, hr, cols), WIRE),
        grid_spec=pltpu.PrefetchScalarGridSpec(
            num_scalar_prefetch=1, grid=(N_CHIPS, hr // tr),
            in_specs=[pl.BlockSpec((1, 1, tr, cols), lambda d, i, c: (d, c[0], i, 0)),
                      pl.BlockSpec((1, tr, cols), lambda d, i, c: (d, i, 0))],
            out_specs=pl.BlockSpec((1, tr, cols), lambda d, i, c: (d, i, 0))),
        compiler_params=_cp(2),
    )(c_idx, g, r1)


def _chip_sum(r2, slot, base=None):
    _, hr, cols = r2.shape
    tr = _rowtile(hr, cols)

    def body(s_ref, r_ref, *rest):
        o_ref = rest[-1]
        acc = r_ref[0].astype(F32)
        for d in range(1, N_CHIPS):
            acc = acc + r_ref[d].astype(F32)
        o_ref[0] = acc

    based = base is not None
    return _pc(
        body, name="chip_sum_%dx%d_%d" % (hr, cols, int(based)), out_shape=_sds((2, hr, cols), F32),
        grid_spec=pltpu.PrefetchScalarGridSpec(
            num_scalar_prefetch=1, grid=(hr // tr,),
            in_specs=[pl.BlockSpec((N_CHIPS, tr, cols), lambda i, s: (0, i, 0))] + ([ANY] if based else []),
            out_specs=pl.BlockSpec((1, tr, cols), lambda i, s: (s[0], i, 0))),
        input_output_aliases={2: 0} if based else {}, compiler_params=_cp(1),
    )(*((slot, r2, base) if based else (slot, r2)))


def _adamw(w, g, m, v):
    rows, cols = w.shape
    tr = _rowtile(rows, cols)
    c1 = 1.0 / (1.0 - B1 ** STEP)
    c2 = 1.0 / (1.0 - B2 ** STEP)

    def body(w_ref, g_ref, m_ref, v_ref, d_ref, mo_ref, vo_ref):
        gv = g_ref[...]
        mn = B1 * m_ref[...] + (1.0 - B1) * gv
        vn = B2 * v_ref[...] + (1.0 - B2) * (gv * gv)
        mo_ref[...] = mn
        vo_ref[...] = vn
        d_ref[...] = -LR * ((mn * c1) / (jnp.sqrt(vn * c2) + ADAM_EPS) + WD * w_ref[...])

    blk = pl.BlockSpec((tr, cols), lambda i: (i, 0))
    return _pc(
        body, name="adamw_%dx%d" % (rows, cols), grid=(rows // tr,), in_specs=[blk] * 4, out_specs=[blk] * 3,
        out_shape=[_sds((rows, cols), F32)] * 3, compiler_params=_cp(1),
    )(w, g, m, v)


MESH_ID = pl.DeviceIdType.MESH
ANY = pl.BlockSpec(memory_space=pl.ANY)


def _place():
    x, y, c = lax.axis_index("x"), lax.axis_index("y"), lax.axis_index("c")
    others = [(1 - x, y), (x, 1 - y), (1 - x, 1 - y)]
    return x, y, c, others


def _remote(src, dst, sems, idx, to):
    return pltpu.make_async_remote_copy(src_ref=src, dst_ref=dst, send_sem=sems[0].at[idx], recv_sem=sems[1].at[idx],
                                        device_id=to, device_id_type=MESH_ID)


def _gather_plan(shards):
    nk = len(shards)

    def copies(ins, outs, sems):
        x, y, c, others = _place()
        me = 2 * x + y
        sib = (x, y, 1 - c)
        local = [pltpu.make_async_copy(ins[k], outs[k].at[me], sems[2].at[k]) for k in range(nk)]
        ici, landed, fwd, fwd_landed = [], [], [], []
        for k in range(nk):
            hr = shards[k].shape[0] // 2
            for r, (cx, cy) in enumerate(others):
                mine = pl.ds(c * hr, hr)
                ici.append(_remote(ins[k].at[mine], outs[k].at[me, mine], sems, 6 * k + r, (cx, cy, c)))
                got = outs[k].at[2 * cx + cy, mine]
                landed.append(_remote(got, got, sems, 6 * k + r, (cx, cy, c)))
                fwd.append(_remote(got, got, sems, 6 * k + 3 + r, sib))
                theirs = outs[k].at[2 * cx + cy, pl.ds((1 - c) * hr, hr)]
                fwd_landed.append(_remote(theirs, theirs, sems, 6 * k + 3 + r, sib))
        return local, ici, landed, fwd, fwd_landed

    def start(ins, outs, sems):
        local, ici, _, _, _ = copies(ins, outs, sems)
        for cp in local + ici:
            cp.start()

    def forward(ins, outs, sems):
        _, _, landed, fwd, _ = copies(ins, outs, sems)
        for got, cp in zip(landed, fwd):
            got.wait_recv()
            cp.start()

    def finish(ins, outs, sems):
        local, ici, _, fwd, fwd_landed = copies(ins, outs, sems)
        for got in fwd_landed:
            got.wait_recv()
        for cp in ici + fwd:
            cp.wait_send()
        for cp in local:
            cp.wait()

    return dict(
        arrays=list(shards), out_shape=[_sds((N_CHIPS,) + a.shape, a.dtype) for a in shards],
        scratch=[pltpu.SemaphoreType.DMA((6 * nk,)), pltpu.SemaphoreType.DMA((6 * nk,)), pltpu.SemaphoreType.DMA((nk,))],
        phases=[start, forward, finish])


def _scatter_plan(ps):
    nk = len(ps)

    def copies(ins, outs, sems):
        x, y, c, others = _place()
        me = 2 * x + y
        local = [pltpu.make_async_copy(ins[k].at[me], outs[k].at[me], sems[2].at[k]) for k in range(nk)]
        ici, landed = [], []
        for k in range(nk):
            for r, (cx, cy) in enumerate(others):
                ici.append(_remote(ins[k].at[2 * cx + cy], outs[k].at[me], sems, 3 * k + r, (cx, cy, c)))
                got = outs[k].at[2 * cx + cy]
                landed.append(_remote(got, got, sems, 3 * k + r, (cx, cy, c)))
        return local, ici, landed

    def start(ins, outs, sems):
        local, ici, _ = copies(ins, outs, sems)
        for cp in local + ici:
            cp.start()

    def finish(ins, outs, sems):
        local, ici, landed = copies(ins, outs, sems)
        for got in landed:
            got.wait_recv()
        for cp in ici:
            cp.wait_send()
        for cp in local:
            cp.wait()

    return dict(
        arrays=list(ps), out_shape=[_sds(a.shape, a.dtype) for a in ps],
        scratch=[pltpu.SemaphoreType.DMA((3 * nk,)), pltpu.SemaphoreType.DMA((3 * nk,)), pltpu.SemaphoreType.DMA((nk,))],
        phases=[start, finish])


def _run_plan(plan, name):
    nk = len(plan["arrays"])

    def body(*refs):
        ins, outs, sems = refs[:nk], refs[nk:2 * nk], refs[2 * nk:]
        for phase in plan["phases"]:
            phase(ins, outs, sems)

    return _pc(body, name=name, in_specs=[ANY] * nk, out_specs=[ANY] * nk, out_shape=plan["out_shape"],
               scratch_shapes=plan["scratch"])(*plan["arrays"])


def _host_plan(plan, ins, outs, sems, first, middle, last):
    points = [first, last] if len(plan["phases"]) == 2 else [first, middle, last]
    for phase, at in zip(plan["phases"], points):
        pl.when(at)(functools.partial(phase, ins, outs, sems))


def _swap_sibling(gs, halves, tag):
    nk = len(gs)

    def body(*refs):
        ins, outs = refs[:nk], refs[nk:2 * nk]
        send_sems, recv_sems = refs[2 * nk:]
        x, y, c, _ = _place()
        cps = []
        for k in range(nk):
            hr = gs[k].shape[1] // 2
            cp = pltpu.make_async_remote_copy(
                src_ref=ins[k].at[:, pl.ds((1 - c) * hr, hr)] if halves else ins[k], dst_ref=outs[k],
                send_sem=send_sems.at[k], recv_sem=recv_sems.at[k], device_id=(x, y, 1 - c), device_id_type=MESH_ID)
            cp.start()
            cps.append(cp)
        for cp in cps:
            cp.wait()

    return _pc(
        body, name=("swap_halves_" if halves else "swap_slabs_") + tag, in_specs=[ANY] * nk, out_specs=[ANY] * nk,
        out_shape=[_sds((N_CHIPS, a.shape[1] // 2 if halves else a.shape[1], a.shape[2]), a.dtype) for a in gs],
        scratch_shapes=[pltpu.SemaphoreType.DMA((nk,)), pltpu.SemaphoreType.DMA((nk,))],
    )(*gs)


def _allreduce_small(v):
    rows, cols = v.shape

    def body(v_ref, o_ref, gath, send_sems, recv_sems):
        x, y, c, others = _place()
        sib = (x, y, 1 - c)

        def slot(px, py, pc):
            return gath.at[4 * px + 2 * py + pc]

        def copy(k, block, to, src=None):
            return pltpu.make_async_remote_copy(
                src_ref=slot(*block) if src is None else src, dst_ref=slot(*block),
                send_sem=send_sems.at[k], recv_sem=recv_sems.at[k], device_id=to, device_id_type=MESH_ID)

        me = (x, y, c)
        gath[4 * x + 2 * y + c] = v_ref[...]
        first = [copy(0, me, sib, src=v_ref)]
        first += [copy(1 + r, me, (cx, cy, c), src=v_ref) for r, (cx, cy) in enumerate(others)]
        for cp in first:
            cp.start()
        passed = [copy(4 + r, (cx, cy, c), sib) for r, (cx, cy) in enumerate(others)]
        for r, (cx, cy) in enumerate(others):
            copy(1 + r, (cx, cy, c), me).wait_recv()
            passed[r].start()
        copy(0, (x, y, 1 - c), me).wait_recv()
        for r, (cx, cy) in enumerate(others):
            copy(4 + r, (cx, cy, 1 - c), me).wait_recv()
        for cp in first + passed:
            cp.wait_send()
        acc = gath[0]
        for d in range(1, N_DEV):
            acc = acc + gath[d]
        o_ref[...] = acc

    vm = pl.BlockSpec(memory_space=pltpu.VMEM)
    return _pc(
        body, name="allreduce_small", in_specs=[vm], out_specs=vm, out_shape=_sds((rows, cols), F32),
        scratch_shapes=[pltpu.VMEM((N_DEV, rows, cols), F32), pltpu.SemaphoreType.DMA((7,)), pltpu.SemaphoreType.DMA((7,))],
    )(v)


def _cols_to_chips(a):
    r, c4 = a.shape
    return a.reshape(r, N_CHIPS, c4 // N_CHIPS).transpose(1, 0, 2)


def _chips_to_cols(a):
    n, r, c = a.shape
    return a.transpose(1, 0, 2).reshape(r, n * c)


def _head_rows(a, tq):
    s = a.shape[0]
    return a[:, :HF].reshape(s // tq, tq, NP, 2).transpose(2, 0, 3, 1)


def _head_cols(a):
    s = a.shape[0]
    return a[:, :HF].reshape(s, NP, 2).transpose(1, 0, 2)


def _to_wire(gs, c_idx, tag):
    gs = [g.astype(WIRE) for g in gs]
    r1 = _swap_sibling(gs, True, tag)
    return [_pair_sum(g.reshape(N_CHIPS, 2, g.shape[1] // 2, g.shape[2]), r, c_idx) for g, r in zip(gs, r1)]


def _local_step(x, mem, tgt, sp, w_main, w_f, rest_shards, c_idx):
    s = x.shape[0]
    tq = min(TQ, s)
    b_f = jnp.pad(sp["b_forget"], ((0, 0), (0, LANES - HF)))
    proj, h, flog = _in_proj(x, sp["norm_mix_pre"], w_main, w_f)
    cf = _logf_cumsum(flog, b_f)
    qx, kx = _fox_prep(proj, _head_cols(cf))
    oa, lse, g_cw, g_kv, g_pj, g_gu, g_d = _fox_fwd(proj, qx, kx, _gather_plan(rest_shards))
    pj = g_pj.reshape(N_CHIPS, 4, D // N_CHIPS, D).transpose(1, 0, 2, 3).reshape(4, D, D)
    w = {"conv_w": _chips_to_cols(g_cw), "w_kv": _chips_to_cols(g_kv), "wpa": pj[0], "wpc": pj[1], "wpm": pj[2],
         "wout": pj[3], "w_gu": _chips_to_cols(g_gu), "w_d": g_d.reshape(FFN, D)}
    y, oc = _conv_fwd(proj, w["conv_w"], sp["conv_b"], sp["conv_ln_g"], sp["conv_ln_b"])
    mem_n, kv = _mem_kv(mem, sp["norm_mem"], w["w_kv"])
    om = _mem_attn_fwd(proj, kv)
    pa, pc, pm, merged, z, x1, h2 = _merge_out(oa, oc, om, proj, x, w["wpa"], w["wpc"], w["wpm"], w["wout"],
                                              sp["norm_mix_post"], sp["norm_ffn_pre"])
    gf, uf, act = _ffn_up(h2, w["w_gu"])
    dffn, dy, vec_f, loss_blk = _ffn_down_loss(act, w["w_d"], x1, tgt, sp["norm_ffn_post"])

    dgf, duf = _ffn_down_bwd(dffn, w["w_d"], gf, uf)
    dx1, dz, vec_n = _ffn_up_bwd(dgf, duf, w["w_gu"], x1, dy, z, sp["norm_ffn_pre"], sp["norm_mix_post"])
    dpa, dpc, dpm, dgl, doa, doc, dom, delta = _merge_bwd(dz, proj, pa, pc, pm, oa, w["wpa"], w["wpc"], w["wpm"], w["wout"])
    pj_g = jnp.stack([_wgrad(oa, dpa, "wgrad_pa"), _wgrad(oc, dpc, "wgrad_pc"), _wgrad(om, dpm, "wgrad_pm"),
                      _wgrad(merged, dz, "wgrad_out")]).reshape(4, N_CHIPS, D // N_CHIPS, D)
    early = [pj_g.transpose(1, 0, 2, 3).reshape(N_CHIPS, D, D),
             _cols_to_chips(jnp.concatenate([_wgrad(h2, dgf, "wgrad_g"), _wgrad(h2, duf, "wgrad_u")], axis=1)),
             _wgrad(act, dffn, "wgrad_d").reshape(N_CHIPS, FFN // N_CHIPS, D)]
    lse16 = lse.transpose(1, 0, 2).reshape(s, HF)
    dq, dqs, dk, dks, dv, *r2_early = _fox_bwd(proj, qx, kx, doa, _head_rows(lse16, tq), _head_rows(delta, tq),
                                              _scatter_plan(_to_wire(early, c_idx, "early")))
    over_keys = dqs[:, X_KONE:X_KONE + 2, :].transpose(2, 0, 1).reshape(s, HF)
    over_queries = dks[:, :, X_QONE:X_QONE + 2, :].transpose(1, 3, 0, 2).reshape(s, HF)
    dc = jnp.pad(over_keys - over_queries, ((0, 0), (0, LANES - HF)))
    df, db_blk = _logf_cumsum_bwd(flog, b_f, dc)
    dga, dgg, dcw, vec_c = _conv_bwd(proj, y, doc, w["conv_w"], sp["conv_ln_g"], sp["conv_ln_b"])
    dqm, dkv = _mem_attn_bwd(proj, kv, dom)
    dkv_b = dkv.astype(MXU)
    vec_m = _mem_kv_bwd(mem, sp["norm_mem"], w["w_kv"], dkv_b)
    pieces = [(dq, 0, 1), (dk, 1, 1), (dv, 2, 1), (dga, 3, 1), (dgg, 4, 1), (dqm, 5, 1), (dgl, 6, 3)]
    dx, vec_p = _in_proj_bwd(pieces, df, w_main, w_f, x, dx1, sp["norm_mix_pre"])

    dw_cols = [_wgrad(h, p, "wgrad_in_%d" % c0) for p, c0, _ in pieces]
    dwf = _wgrad(h, df, "wgrad_in_f")
    late = [_cols_to_chips(jnp.concatenate(dw_cols[:3] + [dwf[:, :HF]] + dw_cols[3:], axis=1)), _cols_to_chips(dcw),
            _cols_to_chips(_wgrad(mem_n, dkv_b, "wgrad_kv"))]
    r2_late = _run_plan(_scatter_plan(_to_wire(late, c_idx, "late")), "exchange_late")
    zero_row = jnp.zeros((1, D), F32)
    small = jnp.concatenate([
        vec_p[0:1], vec_n[1:2], vec_m[0:1], vec_c[2:3], vec_c[0:1], vec_c[1:2], vec_n[0:1], vec_f[0:1],
        jnp.pad(db_blk[0:1, :HF], ((0, 0), (0, D - HF))),
        jnp.pad(loss_blk[0:1, 0:1], ((0, 0), (0, D - 1))),
    ] + [zero_row] * (SMALL_ROWS - 10), axis=0)
    return dx, list(r2_late) + list(r2_early), small


SMALL_NAMES = ["norm_mix_pre", "norm_mix_post", "norm_mem", "conv_b", "conv_ln_g", "conv_ln_b", "norm_ffn_pre", "norm_ffn_post"]
PROJ_NAMES = ["w_proj_attn", "w_proj_conv", "w_proj_mem", "w_out"]
WEIGHT_ORDER = ["norm_mix_pre", "norm_mix_post", "norm_mem", "w_in", "b_forget", "conv_w", "conv_b", "conv_ln_g", "conv_ln_b",
                "w_kv_mem", "w_proj_attn", "w_proj_conv", "w_proj_mem", "w_out", "norm_ffn_pre", "norm_ffn_post",
                "w_gate_up", "w_down"]


def _pack_small(p):
    rows = [p[n] for n in SMALL_NAMES] + [jnp.pad(p["b_forget"], ((0, 0), (0, D - HF)))]
    return jnp.concatenate(rows + [jnp.zeros((SMALL_ROWS - len(rows), D), F32)], axis=0)


def _step(params, moms, vels, x, mem, tgt):
    c_idx = lax.axis_index("c").astype(jnp.int32).reshape(1)

    (g_in,) = _run_plan(_gather_plan([params["w_in"].astype(WIRE)]), "gather_w_in")
    w_in_full = _chips_to_cols(g_in)
    w_main = jnp.concatenate([w_in_full[:, :3 * D], w_in_full[:, 3 * D + HF:]], axis=1)
    w_f = jnp.pad(w_in_full[:, 3 * D:3 * D + HF], ((0, 0), (0, LANES - HF)))
    rest = [jnp.pad(params["conv_w"], ((0, CWP - CW), (0, 0))), params["w_kv_mem"].astype(WIRE),
            jnp.concatenate([params[n] for n in PROJ_NAMES], axis=0).astype(WIRE),
            params["w_gate_up"].astype(WIRE), params["w_down"].astype(WIRE)]

    dx, r2, small = _local_step(x, mem, tgt, params, w_main, w_f, rest, c_idx)

    r2_sib = _swap_sibling(r2, False, "all")
    full = [_chip_sum(theirs, 1 - c_idx, _chip_sum(mine, c_idx)) for mine, theirs in zip(r2, r2_sib)]
    red = [f.reshape(2 * f.shape[1], f.shape[2]) for f in full]
    pj_r = red[3].reshape(4, D // N_CHIPS, D)
    grads = {"w_in": red[0], "conv_w": red[1][:CW], "w_kv_mem": red[2], "w_gate_up": red[4], "w_down": red[5]}
    for i, n in enumerate(PROJ_NAMES):
        grads[n] = pj_r[i]

    tot = _allreduce_small(small)
    loss = tot[9, 0]
    for i, n in enumerate(SMALL_NAMES):
        grads[n] = tot[i:i + 1]
    grads["b_forget"] = tot[8:9, :HF]

    delta, new_m, new_v = {}, {}, {}
    ds, ms, vs = _adamw(_pack_small(params), tot.at[9:].set(0.0), _pack_small(moms), _pack_small(vels))
    for i, n in enumerate(SMALL_NAMES):
        delta[n], new_m[n], new_v[n] = ds[i:i + 1], ms[i:i + 1], vs[i:i + 1]
    delta["b_forget"], new_m["b_forget"], new_v["b_forget"] = ds[8:9, :HF], ms[8:9, :HF], vs[8:9, :HF]
    for n in ["w_in", "conv_w", "w_kv_mem", "w_gate_up", "w_down"] + PROJ_NAMES:
        delta[n], new_m[n], new_v[n] = _adamw(params[n], grads[n], moms[n], vels[n])
    return loss, dx, grads, delta, new_m, new_v


def kernel(x, mem, norm_mix_pre, norm_mix_post, norm_mem, w_in, b_forget, conv_w, conv_b, conv_ln_g, conv_ln_b, w_kv_mem, w_proj_attn, w_proj_conv, w_proj_mem, w_out, norm_ffn_pre, norm_ffn_post, w_gate_up, w_down, loss_target, m_norm_mix_pre, m_norm_mix_post, m_norm_mem, m_w_in, m_b_forget, m_conv_w, m_conv_b, m_conv_ln_g, m_conv_ln_b, m_w_kv_mem, m_w_proj_attn, m_w_proj_conv, m_w_proj_mem, m_w_out, m_norm_ffn_pre, m_norm_ffn_post, m_w_gate_up, m_w_down, v_norm_mix_pre, v_norm_mix_post, v_norm_mem, v_w_in, v_b_forget, v_conv_w, v_conv_b, v_conv_ln_g, v_conv_ln_b, v_w_kv_mem, v_w_proj_attn, v_w_proj_conv, v_w_proj_mem, v_w_out, v_norm_ffn_pre, v_norm_ffn_post, v_w_gate_up, v_w_down):
    local = dict(locals())
    lead = {n: local[n].shape[:-2] for n in WEIGHT_ORDER}
    two_d = lambda a: a.reshape(a.shape[-2:])
    params = {n: two_d(local[n]) for n in WEIGHT_ORDER}
    moms = {n: two_d(local["m_" + n]) for n in WEIGHT_ORDER}
    vels = {n: two_d(local["v_" + n]) for n in WEIGHT_ORDER}
    loss, dx, grads, delta, new_m, new_v = _step(params, moms, vels, two_d(x), two_d(mem), two_d(loss_target))
    outs = [loss, dx.reshape(x.shape)]
    for group in (grads, delta, new_m, new_v):
        outs += [group[n].reshape(lead[n] + group[n].shape) for n in WEIGHT_ORDER]
    return tuple(outs)
```

```python
import functools

import jax
import jax.numpy as jnp
from jax import lax
from jax.experimental import pallas as pl
from jax.experimental.pallas import tpu as pltpu

F32 = jnp.float32
MXU = jnp.bfloat16
WIRE = jnp.bfloat16

D = 1024
HF = 16
DH = 64
NP = D // 128
MEM_H = 4
MEM_DH = D // MEM_H
FFN = 2816
CW = 31
CWP = 32
HALO = 32
RMS_EPS = 1e-6
LN_EPS = 1e-5
LR, B1, B2, ADAM_EPS, WD, STEP = 0.001, 0.9, 0.999, 1e-8, 0.01, 10

N_CHIPS = 4
N_DEV = 8
LANES = 128
VMEM_LIMIT = 56 * 1024 * 1024

TM_PROJ = 512
NB_PROJ = 3
TQ = 1024
LOG2E = 1.4426950408889634
LN2 = 0.6931471805599453
QSCALE = DH ** -0.5 * LOG2E
X_BIAS = 0
X_QONE = 6
X_KONE = 8
X_ROWS = 16
TM_CONV = 256
CONV_ROWS = 128
SUB = 8
TM_ROW = 256
TM_WIDE = 4096
TS_WG = 2048
WG_CAP = 1408
SMALL_ROWS = 16


def _pc(body, **kw):
    return pl.pallas_call(body, **kw)


def _cp(n_axes):
    return pltpu.CompilerParams(dimension_semantics=("arbitrary",) * n_axes, vmem_limit_bytes=VMEM_LIMIT)


def _sds(shape, dtype):
    return jax.ShapeDtypeStruct(shape, dtype)


def _dot(a, b):
    return jnp.dot(a, b, preferred_element_type=F32)


def _dot_nt(a, b):
    return lax.dot_general(a, b, (((1,), (1,)), ((), ())), preferred_element_type=F32)


def _dot_tn(a, b):
    return lax.dot_general(a, b, (((0,), (0,)), ((), ())), preferred_element_type=F32)


def _rms(u):
    return lax.rsqrt(jnp.mean(u * u, axis=-1, keepdims=True) + RMS_EPS)


def _rms_bwd(u, r, g, dn):
    w = dn * g
    return r * w - u * (r * r * r) * jnp.mean(u * w, axis=-1, keepdims=True)


def _sigmoid(z):
    return 1.0 / (1.0 + jnp.exp(-z))


def _tile(n, cap):
    if n <= cap:
        return n
    best = None
    for t in range(LANES, cap + 1, LANES):
        if n % t == 0:
            best = t
    assert best is not None, (n, cap)
    return best


def _rowtile(rows, cols, cap_bytes=1 << 20):
    best = None
    for t in range(8, rows + 1, 8):
        if rows % t == 0 and t * cols * 4 <= cap_bytes:
            best = t
    return best if best is not None else rows


def _split3(v):
    hi = v.astype(jnp.bfloat16)
    r1 = v - hi.astype(F32)
    mid = r1.astype(jnp.bfloat16)
    lo = (r1 - mid.astype(F32)).astype(jnp.bfloat16)
    return hi, mid, lo


def _dot_exact_rhs(a01, v):
    hi, mid, lo = _split3(v)
    return _dot(a01, hi) + _dot(a01, mid) + _dot(a01, lo)


def _in_proj(x, g_pre, w_main, w_f):
    s, d = x.shape
    n = w_main.shape[1]
    tm = min(TM_PROJ, s)
    tn = n // NB_PROJ

    def body(x_ref, g_ref, w_ref, wf_ref, proj_ref, h_ref, flog_ref, hs):
        @pl.when(pl.program_id(1) == 0)
        def _():
            xv = x_ref[...]
            h = (xv * _rms(xv) * g_ref[...]).astype(MXU)
            hs[...] = h
            h_ref[...] = h
            flog_ref[...] = _dot(h, wf_ref[...])

        res = _dot(hs[...], w_ref[...])

        @pl.when(pl.program_id(1) == 0)
        def _():
            proj_ref[:, pl.ds(0, d)] = (res[:, :d] * QSCALE).astype(MXU)
            proj_ref[:, pl.ds(d, tn - d)] = res[:, d:].astype(MXU)

        @pl.when(pl.program_id(1) != 0)
        def _():
            proj_ref[...] = res.astype(MXU)

    assert tn >= d
    return _pc(
        body, name="in_proj", grid=(s // tm, NB_PROJ),
        in_specs=[pl.BlockSpec((tm, d), lambda i, j: (i, 0)), pl.BlockSpec((1, d), lambda i, j: (0, 0)),
                  pl.BlockSpec((d, tn), lambda i, j: (0, j)), pl.BlockSpec((d, LANES), lambda i, j: (0, 0))],
        out_specs=[pl.BlockSpec((tm, tn), lambda i, j: (i, j)), pl.BlockSpec((tm, d), lambda i, j: (i, 0)),
                   pl.BlockSpec((tm, LANES), lambda i, j: (i, 0))],
        out_shape=[_sds((s, n), MXU), _sds((s, d), MXU), _sds((s, LANES), F32)],
        scratch_shapes=[pltpu.VMEM((tm, d), MXU)], compiler_params=_cp(2),
    )(x, g_pre, w_main, w_f)


def _log_sigmoid(z):
    e = jnp.exp(-jnp.abs(z))
    log1p_e = jnp.where(e < 1e-3, e * (1.0 - 0.5 * e), jnp.log(1.0 + e))
    return jnp.minimum(z, 0.0) - log1p_e


def _logf_cumsum(flog, b_f):
    s = flog.shape[0]
    ch = LANES

    def body(f_ref, b_ref, c_ref):
        r = lax.broadcasted_iota(jnp.int32, (ch, ch), 0)
        q = lax.broadcasted_iota(jnp.int32, (ch, ch), 1)
        tri = jnp.where(r >= q, 1.0, 0.0).astype(jnp.bfloat16)

        def step(i, carry):
            rows = pl.ds(pl.multiple_of(i * ch, ch), ch)
            lf = _log_sigmoid(f_ref[rows, :] + b_ref[...])
            c_ref[rows, :] = _dot_exact_rhs(tri, lf) + carry
            return carry + jnp.sum(lf, axis=0, keepdims=True)

        lax.fori_loop(0, s // ch, step, jnp.zeros((1, LANES), F32))

    return _pc(body, name="logf_cumsum", out_shape=_sds((s, LANES), F32),
               compiler_params=pltpu.CompilerParams(vmem_limit_bytes=VMEM_LIMIT))(flog, b_f)


def _logf_cumsum_bwd(flog, b_f, dc):
    s = flog.shape[0]
    ch = LANES

    def body(f_ref, b_ref, dc_ref, df_ref, db_ref):
        r = lax.broadcasted_iota(jnp.int32, (ch, ch), 0)
        q = lax.broadcasted_iota(jnp.int32, (ch, ch), 1)
        tri = jnp.where(r <= q, 1.0, 0.0).astype(jnp.bfloat16)
        nch = s // ch

        def step(t, carry):
            tail, dbsum = carry
            i = nch - 1 - t
            rows = pl.ds(pl.multiple_of(i * ch, ch), ch)
            dcv = dc_ref[rows, :]
            dlf = _dot_exact_rhs(tri, dcv) + tail
            z = f_ref[rows, :] + b_ref[...]
            df = dlf * _sigmoid(-z)
            df_ref[rows, :] = df.astype(MXU)
            return tail + jnp.sum(dcv, axis=0, keepdims=True), dbsum + jnp.sum(df, axis=0, keepdims=True)

        zero = jnp.zeros((1, LANES), F32)
        _, dbsum = lax.fori_loop(0, nch, step, (zero, zero))
        db_ref[...] = jnp.broadcast_to(dbsum, db_ref.shape)

    return _pc(body, name="logf_cumsum_bwd", out_shape=[_sds((s, LANES), MXU), _sds((8, LANES), F32)],
               compiler_params=pltpu.CompilerParams(vmem_limit_bytes=VMEM_LIMIT))(flog, b_f, dc)


def _head_masks(rows):
    lane = lax.broadcasted_iota(jnp.int32, (rows, LANES), 1)
    return lane < DH, lane >= DH


def _ext_masks(rows, key_side):
    lane = lax.broadcasted_iota(jnp.int32, (rows, 2 * LANES), 1)
    ext = lane - LANES
    out = []
    for a in range(2):
        head = (lane >= a * DH) & (lane < (a + 1) * DH)
        bias = (ext >= X_BIAS + 3 * a) & (ext < X_BIAS + 3 * a + 3)
        one = ext == (X_KONE if key_side else X_QONE) + a
        out.append(head | bias | one)
    return out


def _fox_prep(proj, ccol):
    s = proj.shape[0]
    tm = min(TM_WIDE, s)

    def body(q_ref, k_ref, c_ref, qx_ref, kx_ref):
        lane = lax.broadcasted_iota(jnp.int32, (tm, LANES), 1)
        qx_ref[:, pl.ds(0, LANES)] = q_ref[...]
        qx_ref[:, pl.ds(LANES, LANES)] = jnp.where(lane < X_QONE + 2, 1.0, 0.0).astype(MXU)
        kext = jnp.where((lane >= X_KONE) & (lane < X_KONE + 2), 1.0, 0.0).astype(jnp.bfloat16)
        for a in range(2):
            terms = _split3(c_ref[0, :, a:a + 1] * (-LOG2E))
            for t, term in enumerate(terms):
                kext = jnp.where(lane == X_BIAS + 3 * a + t, term, kext)
        kx_ref[:, pl.ds(0, LANES)] = k_ref[...]
        kx_ref[:, pl.ds(LANES, LANES)] = kext.astype(MXU)

    wide = pl.BlockSpec((tm, 2 * LANES), lambda p, i: (i, p))
    return _pc(
        body, name="fox_prep", grid=(NP, s // tm),
        in_specs=[pl.BlockSpec((tm, LANES), lambda p, i: (i, p)), pl.BlockSpec((tm, LANES), lambda p, i: (i, NP + p)),
                  pl.BlockSpec((1, tm, 2), lambda p, i: (p, i, 0))],
        out_specs=[wide, wide], out_shape=[_sds((s, NP * 2 * LANES), MXU)] * 2, compiler_params=_cp(2),
    )(proj, proj, ccol)


def _fox_fwd(proj, qx, kx, plan):
    s = proj.shape[0]
    tq = min(TQ, s)
    nq = s // tq
    nx = len(plan["arrays"])

    def body(*refs):
        q_ref, k_ref, v_ref = refs[:3]
        o_ref, lse_ref = refs[3 + nx:5 + nx]
        p_id, i = pl.program_id(0), pl.program_id(1)
        _host_plan(plan, refs[3:3 + nx], refs[5 + nx:5 + 2 * nx], refs[5 + 2 * nx:], (p_id == 0) & (i == 0),
                   (p_id == NP // 2) & (i == 0), (p_id == NP - 1) & (i == nq - 1))
        qv = q_ref[...]
        qmask = _ext_masks(tq, False)
        hmask = _head_masks(tq)
        qas = [jnp.where(qmask[a], qv, jnp.zeros_like(qv)) for a in range(2)]
        row = lax.broadcasted_iota(jnp.int32, (tq, tq), 0)
        col = lax.broadcasted_iota(jnp.int32, (tq, tq), 1)

        def blk(j, carry, diag=False):
            rows = pl.ds(pl.multiple_of(j * tq, tq), tq)
            kj = k_ref[rows, :]
            vj = v_ref[rows, :]
            out = []
            for a in range(2):
                m, acc = carry[a]
                sc = _dot_nt(qas[a], kj)
                if diag:
                    sc = jnp.where(row >= col, sc, -jnp.inf)
                m_new = jnp.maximum(m, jnp.max(sc, axis=-1, keepdims=True))
                p = jnp.exp2(sc - m_new)
                va = jnp.where(hmask[a], vj, jnp.ones_like(vj))
                out.append((m_new, jnp.exp2(m - m_new) * acc + _dot(p.astype(MXU), va)))
            return tuple(out)

        init = (jnp.full((tq, 1), -jnp.inf, F32), jnp.zeros((tq, LANES), F32))
        res = blk(i, lax.fori_loop(0, i, blk, (init, init)), True)
        lane = lax.broadcasted_iota(jnp.int32, (tq, LANES), 1)
        outs, lses = [], []
        for a in range(2):
            m, acc = res[a]
            l = jnp.sum(jnp.where(lane == DH * (1 - a), acc, 0.0), axis=-1, keepdims=True)
            outs.append(acc / l)
            lses.append(m + jnp.log(l) * LOG2E)
        o_ref[...] = jnp.where(hmask[0], outs[0], outs[1]).astype(MXU)
        lane2 = lax.broadcasted_iota(jnp.int32, (tq, 2), 1)
        lse_ref[0] = jnp.where(lane2 == 0, lses[0], lses[1])

    return _pc(
        body, name="fox_fwd", grid=(NP, nq),
        in_specs=[pl.BlockSpec((tq, 2 * LANES), lambda p, i: (i, p)),
                  pl.BlockSpec((s, 2 * LANES), lambda p, i: (0, p)),
                  pl.BlockSpec((s, LANES), lambda p, i: (0, 2 * NP + p))] + [ANY] * nx,
        out_specs=[pl.BlockSpec((tq, LANES), lambda p, i: (i, p)),
                   pl.BlockSpec((1, tq, 2), lambda p, i: (p, i, 0))] + [ANY] * nx,
        out_shape=[_sds((s, D), MXU), _sds((NP, s, 2), F32)] + plan["out_shape"],
        scratch_shapes=plan["scratch"], compiler_params=_cp(2),
    )(qx, kx, proj, *plan["arrays"])


def _fox_bwd(proj, qx, kx, do, lse_row, delta_row, plan):
    s = proj.shape[0]
    tq = min(TQ, s)
    nq = s // tq
    nx = len(plan["arrays"])

    def body(*refs):
        k_ref, v_ref, q_ref, do_ref, lse_ref, dl_ref = refs[:6]
        dq_ref, dqs_ref, dk_ref, dks_ref, dv_ref = refs[6 + nx:11 + nx]
        dq_acc = refs[11 + 2 * nx]
        p_id, j = pl.program_id(0), pl.program_id(1)
        _host_plan(plan, refs[6:6 + nx], refs[11 + nx:11 + 2 * nx], refs[12 + 2 * nx:], (p_id == 0) & (j == 0),
                   (p_id == NP // 2) & (j == 0), (p_id == NP - 1) & (j == nq - 1))

        @pl.when(j == 0)
        def _():
            dq_acc[...] = jnp.zeros_like(dq_acc)

        kv = k_ref[...]
        v2 = v_ref[...]
        kmask = _ext_masks(tq, True)
        qmask = _ext_masks(tq, False)
        hmask = _head_masks(tq)
        row = lax.broadcasted_iota(jnp.int32, (tq, tq), 0)
        col = lax.broadcasted_iota(jnp.int32, (tq, tq), 1)
        carry = (jnp.zeros((tq, 2 * LANES), F32), jnp.zeros((tq, LANES), F32))
        for a in range(2):
            ka = jnp.where(kmask[a], kv, jnp.zeros_like(kv))
            va = jnp.where(hmask[a], v2, jnp.zeros_like(v2))

            def blk(i, carry, diag, a=a, ka=ka, va=va):
                dk_a, dv_a = carry
                rows = pl.ds(pl.multiple_of(i * tq, tq), tq)
                qi = q_ref[rows, :]
                doi = do_ref[rows, :]
                qa = jnp.where(qmask[a], qi, jnp.zeros_like(qi))
                doa = jnp.where(hmask[a], doi, jnp.zeros_like(doi))
                st = _dot_nt(ka, qi)
                if diag:
                    st = jnp.where(col >= row, st, -jnp.inf)
                pt = jnp.exp2(st - lse_ref[0, i, a:a + 1, :])
                dv_a = dv_a + _dot(pt.astype(MXU), doa)
                dpt = _dot_nt(va, doi)
                dsb = (pt * (dpt - dl_ref[0, i, a:a + 1, :])).astype(MXU)
                dk_a = dk_a + _dot(dsb, qa)
                dq_acc[rows, :] += _dot_tn(dsb, ka)
                return dk_a, dv_a

            carry = blk(j, carry, True)
            carry = lax.fori_loop(j + 1, nq, functools.partial(blk, diag=False), carry)
        dk_acc, dv_acc = carry
        dk_ref[...] = (dk_acc[:, :LANES] * LN2).astype(MXU)
        dks_ref[0, 0] = dk_acc[:, LANES:].T[:X_ROWS, :]
        dv_ref[...] = dv_acc.astype(MXU)

        @pl.when(j == nq - 1)
        def _():
            dq_ref[...] = (dq_acc[:, pl.ds(0, LANES)] * DH ** -0.5).astype(MXU)
            for t in range(nq):
                dqs_ref[0, :, pl.ds(t * tq, tq)] = dq_acc[pl.ds(t * tq, tq), pl.ds(LANES, LANES)].T[:X_ROWS, :]

    stat = pl.BlockSpec((1, nq, 2, tq), lambda p, j: (p, 0, 0, 0))
    whole = pl.BlockSpec((s, LANES), lambda p, j: (0, p))
    tile = pl.BlockSpec((tq, LANES), lambda p, j: (j, p))
    return _pc(
        body, name="fox_bwd", grid=(NP, nq),
        in_specs=[pl.BlockSpec((tq, 2 * LANES), lambda p, j: (j, p)),
                  pl.BlockSpec((tq, LANES), lambda p, j: (j, 2 * NP + p)),
                  pl.BlockSpec((s, 2 * LANES), lambda p, j: (0, p)),
                  whole, stat, stat] + [ANY] * nx,
        out_specs=[whole, pl.BlockSpec((1, X_ROWS, s), lambda p, j: (p, 0, 0)), tile,
                   pl.BlockSpec((1, 1, X_ROWS, tq), lambda p, j: (p, j, 0, 0)), tile] + [ANY] * nx,
        out_shape=[_sds((s, D), MXU), _sds((NP, X_ROWS, s), F32), _sds((s, D), MXU),
                   _sds((NP, nq, X_ROWS, tq), F32), _sds((s, D), MXU)] + plan["out_shape"],
        scratch_shapes=[pltpu.VMEM((s, 2 * LANES), F32)] + plan["scratch"], compiler_params=_cp(2),
    )(kx, proj, qx, do, lse_row, delta_row, *plan["arrays"])


def _glu(a, gate):
    return a.astype(F32) * _sigmoid(gate.astype(F32))


def _store_blocked(buf, row0, val):
    for c in range(D // LANES):
        buf[0, c, pl.ds(row0, val.shape[0]), :] = val[:, c * LANES:(c + 1) * LANES]


def _fill_shifted(buf):
    n = buf.shape[2] - SUB
    for r in range(1, SUB):
        buf[r, :, pl.ds(0, n), :] = buf[0, :, pl.ds(r, n), :]


def _shifted(buf, off, rows, c):
    r = off % SUB
    return buf[r, c, pl.ds(off - r, rows), :]


def _conv_fwd(proj, cw, cb, lg, lb):
    s = proj.shape[0]
    tm = min(TM_CONV, s)
    hb = tm // HALO

    rcw = min(CONV_ROWS, tm)

    def body(a_ref, g_ref, ah_ref, gh_ref, w_ref, cb_ref, lg_ref, lb_ref, y_ref, o_ref, gsh):
        i = pl.program_id(0)
        _store_blocked(gsh, 0, jnp.where(i > 0, _glu(ah_ref[...], gh_ref[...]), 0.0))
        _store_blocked(gsh, HALO, _glu(a_ref[...], g_ref[...]))
        _fill_shifted(gsh)
        for c in range(D // LANES):
            cols = pl.ds(c * LANES, LANES)
            for rc in range(tm // rcw):
                acc = jnp.broadcast_to(cb_ref[:, cols], (rcw, LANES))
                for t in range(CW):
                    acc = acc + w_ref[t:t + 1, cols] * _shifted(gsh, HALO - (CW - 1) + t + rc * rcw, rcw, c)
                y_ref[pl.ds(rc * rcw, rcw), cols] = acc
        acc = y_ref[...]
        mu = jnp.mean(acc, axis=-1, keepdims=True)
        xc = acc - mu
        r = lax.rsqrt(jnp.mean(xc * xc, axis=-1, keepdims=True) + LN_EPS)
        nrm = xc * r * lg_ref[...] + lb_ref[...]
        o_ref[...] = (nrm * _sigmoid(nrm)).astype(MXU)

    vec = pl.BlockSpec((1, D), lambda i: (0, 0))
    return _pc(
        body, name="conv_fwd", grid=(s // tm,),
        in_specs=[pl.BlockSpec((tm, D), lambda i: (i, 3)), pl.BlockSpec((tm, D), lambda i: (i, 4)),
                  pl.BlockSpec((HALO, D), lambda i: (jnp.maximum(i * hb - 1, 0), 3)),
                  pl.BlockSpec((HALO, D), lambda i: (jnp.maximum(i * hb - 1, 0), 4)),
                  pl.BlockSpec((CWP, D), lambda i: (0, 0)), vec, vec, vec],
        out_specs=[pl.BlockSpec((tm, D), lambda i: (i, 0)), pl.BlockSpec((tm, D), lambda i: (i, 0))],
        out_shape=[_sds((s, D), F32), _sds((s, D), MXU)],
        scratch_shapes=[pltpu.VMEM((SUB, D // LANES, tm + HALO, LANES), F32)], compiler_params=_cp(1),
    )(proj, proj, proj, proj, cw, cb, lg, lb)


def _conv_bwd(proj, y, do, cw, lg, lb):
    s = proj.shape[0]
    tm = min(TM_CONV, s)
    hb = tm // HALO
    nt = s // tm
    last_hblk = s // HALO - 1

    def ln_bwd(yv, dov, lgv, lbv):
        mu = jnp.mean(yv, axis=-1, keepdims=True)
        xc = yv - mu
        r = lax.rsqrt(jnp.mean(xc * xc, axis=-1, keepdims=True) + LN_EPS)
        xh = xc * r
        nrm = xh * lgv + lbv
        sg = _sigmoid(nrm)
        dn = dov.astype(F32) * (sg * (1.0 + nrm * (1.0 - sg)))
        wv = dn * lgv
        dy = r * (wv - jnp.mean(wv, axis=-1, keepdims=True) - xh * jnp.mean(wv * xh, axis=-1, keepdims=True))
        return dy, dn, xh

    rcw = min(CONV_ROWS, tm)

    def body(a_ref, g_ref, ah_ref, gh_ref, y_ref, yn_ref, do_ref, don_ref, w_ref, lg_ref, lb_ref,
             da_ref, dg_ref, dw_ref, vec_ref, gsh, dysh, dwacc):
        i = pl.program_id(0)

        @pl.when(i == 0)
        def _():
            dwacc[...] = jnp.zeros_like(dwacc)
            vec_ref[...] = jnp.zeros_like(vec_ref)

        lgv, lbv = lg_ref[...], lb_ref[...]
        _store_blocked(gsh, 0, jnp.where(i > 0, _glu(ah_ref[...], gh_ref[...]), 0.0))
        _store_blocked(gsh, HALO, _glu(a_ref[...], g_ref[...]))
        _fill_shifted(gsh)
        dy, dn, xh = ln_bwd(y_ref[...], do_ref[...], lgv, lbv)
        dyn, _, _ = ln_bwd(yn_ref[...], don_ref[...], lgv, lbv)
        _store_blocked(dysh, 0, dy)
        _store_blocked(dysh, tm, jnp.where(i < nt - 1, dyn, 0.0))
        _fill_shifted(dysh)
        vec_ref[0:1, :] += jnp.sum(dn * xh, axis=0, keepdims=True)
        vec_ref[1:2, :] += jnp.sum(dn, axis=0, keepdims=True)
        vec_ref[2:3, :] += jnp.sum(dy, axis=0, keepdims=True)
        for c in range(D // LANES):
            cols = pl.ds(c * LANES, LANES)
            for rc in range(tm // rcw):
                rows = pl.ds(rc * rcw, rcw)
                dyc = dysh[0, c, rows, :]
                dgl = jnp.zeros((rcw, LANES), F32)
                for t in range(CW):
                    dgl = dgl + w_ref[t:t + 1, cols] * _shifted(dysh, CW - 1 - t + rc * rcw, rcw, c)
                    prod = dyc * _shifted(gsh, HALO - (CW - 1) + t + rc * rcw, rcw, c)
                    dwacc[t, :, cols] += jnp.sum(prod.reshape(rcw // SUB, SUB, LANES), axis=0)
                av = a_ref[rows, cols].astype(F32)
                sgate = _sigmoid(g_ref[rows, cols].astype(F32))
                da_ref[rows, cols] = (dgl * sgate).astype(MXU)
                dg_ref[rows, cols] = (dgl * av * sgate * (1.0 - sgate)).astype(MXU)

        @pl.when(i == nt - 1)
        def _():
            dw_ref[...] = jnp.sum(dwacc[...], axis=1)

    vec = pl.BlockSpec((1, D), lambda i: (0, 0))
    cur = lambda c: pl.BlockSpec((tm, D), lambda i: (i, c))
    prv = lambda c: pl.BlockSpec((HALO, D), lambda i: (jnp.maximum(i * hb - 1, 0), c))
    nxt = pl.BlockSpec((HALO, D), lambda i: (jnp.minimum((i + 1) * hb, last_hblk), 0))
    return _pc(
        body, name="conv_bwd", grid=(nt,),
        in_specs=[cur(3), cur(4), prv(3), prv(4), cur(0), nxt, cur(0), nxt,
                  pl.BlockSpec((CWP, D), lambda i: (0, 0)), vec, vec],
        out_specs=[cur(0), cur(0), pl.BlockSpec((CWP, D), lambda i: (0, 0)), pl.BlockSpec((8, D), lambda i: (0, 0))],
        out_shape=[_sds((s, D), MXU), _sds((s, D), MXU), _sds((CWP, D), F32), _sds((8, D), F32)],
        scratch_shapes=[pltpu.VMEM((SUB, D // LANES, tm + HALO, LANES), F32)] * 2 + [pltpu.VMEM((CWP, SUB, D), F32)],
        compiler_params=_cp(1),
    )(proj, proj, proj, proj, y, y, do, do, cw, lg, lb)


def _mem_kv(mem, g_mem, w_kv):
    mm = mem.shape[0]

    def body(m_ref, g_ref, w_ref, mn_ref, kv_ref):
        mv = m_ref[...]
        mn = (mv * _rms(mv) * g_ref[...]).astype(MXU)
        mn_ref[...] = mn
        kv_ref[...] = _dot(mn, w_ref[...]).astype(MXU)

    return _pc(body, name="mem_kv", out_shape=[_sds((mm, D), MXU), _sds((mm, 2 * D), MXU)],
               compiler_params=pltpu.CompilerParams(vmem_limit_bytes=VMEM_LIMIT))(mem, g_mem, w_kv)


def _mem_kv_bwd(mem, g_mem, w_kv, dkv):
    mm = mem.shape[0]

    def body(m_ref, w_ref, dkv_ref, o_ref):
        mv = m_ref[...]
        dmn = _dot_nt(dkv_ref[...], w_ref[...])
        o_ref[...] = jnp.broadcast_to(jnp.sum(dmn * mv * _rms(mv), axis=0, keepdims=True), o_ref.shape)

    return _pc(body, name="mem_kv_bwd", out_shape=_sds((8, D), F32),
               compiler_params=pltpu.CompilerParams(vmem_limit_bytes=VMEM_LIMIT))(mem, w_kv, dkv)


def _mem_attn_fwd(proj, kv):
    s = proj.shape[0]
    mm = kv.shape[0]
    tm = min(TM_PROJ, s)
    scale = MEM_DH ** -0.5

    def body(q_ref, kv_ref, o_ref):
        for h in range(MEM_H):
            cols = pl.ds(h * MEM_DH, MEM_DH)
            qh = q_ref[:, cols] * scale
            sc = _dot_nt(qh, kv_ref[:, cols])
            m = jnp.max(sc, axis=-1, keepdims=True)
            e = jnp.exp(sc - m)
            p = e / jnp.sum(e, axis=-1, keepdims=True)
            o_ref[:, cols] = _dot(p.astype(MXU), kv_ref[:, pl.ds(D + h * MEM_DH, MEM_DH)]).astype(MXU)

    return _pc(
        body, name="mem_attn_fwd", grid=(s // tm,),
        in_specs=[pl.BlockSpec((tm, D), lambda i: (i, 5)), pl.BlockSpec((mm, 2 * D), lambda i: (0, 0))],
        out_specs=pl.BlockSpec((tm, D), lambda i: (i, 0)), out_shape=_sds((s, D), MXU), compiler_params=_cp(1),
    )(proj, kv)


def _mem_attn_bwd(proj, kv, do):
    s = proj.shape[0]
    mm = kv.shape[0]
    tm = min(TM_PROJ, s)
    scale = MEM_DH ** -0.5

    def body(q_ref, kv_ref, do_ref, dq_ref, dkv_ref):
        @pl.when(pl.program_id(0) == 0)
        def _():
            dkv_ref[...] = jnp.zeros_like(dkv_ref)

        for h in range(MEM_H):
            cols = pl.ds(h * MEM_DH, MEM_DH)
            vcols = pl.ds(D + h * MEM_DH, MEM_DH)
            qh = q_ref[:, cols]
            kh = kv_ref[:, cols] * scale
            doh = do_ref[:, cols]
            st = _dot_nt(kh, qh)
            m = jnp.max(st, axis=0, keepdims=True)
            e = jnp.exp(st - m)
            pt = e / jnp.sum(e, axis=0, keepdims=True)
            dpt = _dot_nt(kv_ref[:, vcols], doh)
            dst = pt * (dpt - jnp.sum(pt * dpt, axis=0, keepdims=True))
            dsb = dst.astype(MXU)
            dkv_ref[:, vcols] += _dot(pt.astype(MXU), doh)
            dkv_ref[:, cols] += _dot(dsb, qh) * scale
            dq_ref[:, cols] = _dot_tn(dsb, kh).astype(MXU)

    return _pc(
        body, name="mem_attn_bwd", grid=(s // tm,),
        in_specs=[pl.BlockSpec((tm, D), lambda i: (i, 5)), pl.BlockSpec((mm, 2 * D), lambda i: (0, 0)),
                  pl.BlockSpec((tm, D), lambda i: (i, 0))],
        out_specs=[pl.BlockSpec((tm, D), lambda i: (i, 0)), pl.BlockSpec((mm, 2 * D), lambda i: (0, 0))],
        out_shape=[_sds((s, D), MXU), _sds((mm, 2 * D), F32)], compiler_params=_cp(1),
    )(proj, kv, do)


def _resident(n):
    return [pltpu.VMEM((n, D, D), MXU), pltpu.SemaphoreType.DMA((n,))]


def _load_resident(hbm_refs, wbuf, sems):
    @pl.when(pl.program_id(0) == 0)
    def _():
        cps = [pltpu.make_async_copy(r, wbuf.at[k], sems.at[k]) for k, r in enumerate(hbm_refs)]
        for cp in cps:
            cp.start()
        for cp in cps:
            cp.wait()


def _merge_out(oa, oc, om, proj, x, wpa, wpc, wpm, wout, g_post, g_fpre):
    s = x.shape[0]
    tm = min(TM_ROW, s)

    def body(oa_ref, oc_ref, om_ref, gl_ref, x_ref, gp_ref, gf_ref, wpa_h, wpc_h, wpm_h, wout_h,
             pa_ref, pc_ref, pm_ref, mg_ref, z_ref, x1_ref, h2_ref, wbuf, sems):
        _load_resident([wpa_h, wpc_h, wpm_h, wout_h], wbuf, sems)
        merged = jnp.zeros((tm, D), F32)
        for b, (o_ref, p_ref) in enumerate(((oa_ref, pa_ref), (oc_ref, pc_ref), (om_ref, pm_ref))):
            pb = _dot(o_ref[...], wbuf[b])
            p_ref[...] = pb.astype(MXU)
            merged = merged + _sigmoid(gl_ref[:, pl.ds(b * D, D)].astype(F32)) * pb
        mg = merged.astype(MXU)
        mg_ref[...] = mg
        z = _dot(mg, wbuf[3])
        z_ref[...] = z
        x1 = x_ref[...] + z * _rms(z) * gp_ref[...]
        x1_ref[...] = x1
        h2_ref[...] = (x1 * _rms(x1) * gf_ref[...]).astype(MXU)

    rows = pl.BlockSpec((tm, D), lambda i: (i, 0))
    vec = pl.BlockSpec((1, D), lambda i: (0, 0))
    anyspec = pl.BlockSpec(memory_space=pl.ANY)
    return _pc(
        body, name="merge_out", grid=(s // tm,),
        in_specs=[rows, rows, rows, pl.BlockSpec((tm, 3 * D), lambda i: (i, 2)), rows, vec, vec,
                  anyspec, anyspec, anyspec, anyspec],
        out_specs=[rows] * 7,
        out_shape=[_sds((s, D), MXU)] * 4 + [_sds((s, D), F32)] * 2 + [_sds((s, D), MXU)],
        scratch_shapes=_resident(4), compiler_params=_cp(1),
    )(oa, oc, om, proj, x, g_post, g_fpre, wpa, wpc, wpm, wout)


def _ffn_up(h2, w_gu):
    s = h2.shape[0]
    tm = min(TM_PROJ, s)
    nb = 2
    bw = FFN // nb

    def body(h_ref, wg_ref, wu_ref, gf_ref, uf_ref, act_ref):
        hv = h_ref[...]
        gf = _dot(hv, wg_ref[...])
        uf = _dot(hv, wu_ref[...])
        gf_ref[...] = gf.astype(MXU)
        uf_ref[...] = uf.astype(MXU)
        act_ref[...] = (gf * _sigmoid(gf) * uf).astype(MXU)

    out = pl.BlockSpec((tm, bw), lambda i, j: (i, j))
    return _pc(
        body, name="ffn_up", grid=(s // tm, nb),
        in_specs=[pl.BlockSpec((tm, D), lambda i, j: (i, 0)), pl.BlockSpec((D, bw), lambda i, j: (0, j)),
                  pl.BlockSpec((D, bw), lambda i, j: (0, nb + j))],
        out_specs=[out, out, out], out_shape=[_sds((s, FFN), MXU)] * 3, compiler_params=_cp(2),
    )(h2, w_gu, w_gu)


def _ffn_down_loss(act, w_d, x1, tgt, g_fpost):
    s = act.shape[0]
    tm = min(TM_PROJ, s)

    def body(a_ref, w_ref, x1_ref, t_ref, g_ref, dffn_ref, dy_ref, vec_ref, loss_ref):
        @pl.when(pl.program_id(0) == 0)
        def _():
            vec_ref[...] = jnp.zeros_like(vec_ref)
            loss_ref[...] = jnp.zeros_like(loss_ref)

        ffn = _dot(a_ref[...], w_ref[...])
        r = _rms(ffn)
        gv = g_ref[...]
        e = x1_ref[...] + ffn * r * gv - t_ref[...]
        loss_ref[...] += jnp.sum(e * e) * (0.5 / D)
        dy = e * (1.0 / D)
        dy_ref[...] = dy
        vec_ref[0:1, :] += jnp.sum(dy * ffn * r, axis=0, keepdims=True)
        dffn_ref[...] = _rms_bwd(ffn, r, gv, dy).astype(MXU)

    rows = pl.BlockSpec((tm, D), lambda i: (i, 0))
    return _pc(
        body, name="ffn_down_loss", grid=(s // tm,),
        in_specs=[pl.BlockSpec((tm, FFN), lambda i: (i, 0)), pl.BlockSpec((FFN, D), lambda i: (0, 0)), rows, rows,
                  pl.BlockSpec((1, D), lambda i: (0, 0))],
        out_specs=[rows, rows, pl.BlockSpec((8, D), lambda i: (0, 0)), pl.BlockSpec((8, LANES), lambda i: (0, 0))],
        out_shape=[_sds((s, D), MXU), _sds((s, D), F32), _sds((8, D), F32), _sds((8, LANES), F32)],
        compiler_params=_cp(1),
    )(act, w_d, x1, tgt, g_fpost)


def _ffn_down_bwd(dffn, w_d, gf, uf):
    s = dffn.shape[0]
    tm = min(TM_ROW, s)

    def body(d_ref, w_ref, gf_ref, uf_ref, dgf_ref, duf_ref):
        da = _dot_nt(d_ref[...], w_ref[...])
        gf = gf_ref[...].astype(F32)
        sg = _sigmoid(gf)
        duf_ref[...] = (da * gf * sg).astype(MXU)
        dgf_ref[...] = (da * uf_ref[...].astype(F32) * (sg * (1.0 + gf * (1.0 - sg)))).astype(MXU)

    wide = pl.BlockSpec((tm, FFN), lambda i: (i, 0))
    return _pc(
        body, name="ffn_down_bwd", grid=(s // tm,),
        in_specs=[pl.BlockSpec((tm, D), lambda i: (i, 0)), pl.BlockSpec((FFN, D), lambda i: (0, 0)), wide, wide],
        out_specs=[wide, wide], out_shape=[_sds((s, FFN), MXU)] * 2, compiler_params=_cp(1),
    )(dffn, w_d, gf, uf)


def _ffn_up_bwd(dgf, duf, w_gu, x1, dy, z, g_fpre, g_post):
    s = x1.shape[0]
    tm = min(TM_ROW, s)

    def body(dgf_ref, duf_ref, w_ref, x1_ref, dy_ref, z_ref, gf_ref, gp_ref, dx1_ref, dz_ref, vec_ref):
        @pl.when(pl.program_id(0) == 0)
        def _():
            vec_ref[...] = jnp.zeros_like(vec_ref)

        dh2 = _dot_nt(dgf_ref[...], w_ref[:, pl.ds(0, FFN)]) + _dot_nt(duf_ref[...], w_ref[:, pl.ds(FFN, FFN)])
        x1 = x1_ref[...]
        r2 = _rms(x1)
        vec_ref[0:1, :] += jnp.sum(dh2 * x1 * r2, axis=0, keepdims=True)
        dx1 = dy_ref[...] + _rms_bwd(x1, r2, gf_ref[...], dh2)
        dx1_ref[...] = dx1
        z = z_ref[...]
        rz = _rms(z)
        vec_ref[1:2, :] += jnp.sum(dx1 * z * rz, axis=0, keepdims=True)
        dz_ref[...] = _rms_bwd(z, rz, gp_ref[...], dx1).astype(MXU)

    rows = pl.BlockSpec((tm, D), lambda i: (i, 0))
    wide = pl.BlockSpec((tm, FFN), lambda i: (i, 0))
    vec = pl.BlockSpec((1, D), lambda i: (0, 0))
    return _pc(
        body, name="ffn_up_bwd", grid=(s // tm,),
        in_specs=[wide, wide, pl.BlockSpec((D, 2 * FFN), lambda i: (0, 0)), rows, rows, rows, vec, vec],
        out_specs=[rows, rows, pl.BlockSpec((8, D), lambda i: (0, 0))],
        out_shape=[_sds((s, D), F32), _sds((s, D), MXU), _sds((8, D), F32)], compiler_params=_cp(1),
    )(dgf, duf, w_gu, x1, dy, z, g_fpre, g_post)


def _merge_bwd(dz, proj, pa, pc, pm, oa, wpa, wpc, wpm, wout):
    s = dz.shape[0]
    tm = min(TM_ROW, s)

    def body(dz_ref, gl_ref, pa_ref, pc_ref, pm_ref, oa_ref, wpa_h, wpc_h, wpm_h, wout_h,
             dpa_ref, dpc_ref, dpm_ref, dgl_ref, doa_ref, doc_ref, dom_ref, dl_ref, wbuf, sems):
        _load_resident([wpa_h, wpc_h, wpm_h, wout_h], wbuf, sems)
        dm = _dot_nt(dz_ref[...], wbuf[3])
        quads = ((pa_ref, dpa_ref, doa_ref), (pc_ref, dpc_ref, doc_ref), (pm_ref, dpm_ref, dom_ref))
        for b, (p_ref, dp_ref, do_ref) in enumerate(quads):
            cols = pl.ds(b * D, D)
            gt = _sigmoid(gl_ref[:, cols].astype(F32))
            dp = (dm * gt).astype(MXU)
            dp_ref[...] = dp
            dgl_ref[:, cols] = (dm * p_ref[...].astype(F32) * gt * (1.0 - gt)).astype(MXU)
            dob = _dot_nt(dp, wbuf[b]).astype(MXU)
            do_ref[...] = dob
            if b == 0:
                prod = dob.astype(F32) * oa_ref[...].astype(F32)
                d_i = lax.broadcasted_iota(jnp.int32, (D, LANES), 0)
                h_i = lax.broadcasted_iota(jnp.int32, (D, LANES), 1)
                sel = jnp.where(lax.shift_right_logical(d_i, DH.bit_length() - 1) == h_i, 1.0, 0.0).astype(jnp.bfloat16)
                dl_ref[...] = _dot_exact_rhs_t(prod, sel)

    rows = pl.BlockSpec((tm, D), lambda i: (i, 0))
    anyspec = pl.BlockSpec(memory_space=pl.ANY)
    wide = pl.BlockSpec((tm, 3 * D), lambda i: (i, 2))
    return _pc(
        body, name="merge_bwd", grid=(s // tm,),
        in_specs=[rows, wide, rows, rows, rows, rows, anyspec, anyspec, anyspec, anyspec],
        out_specs=[rows, rows, rows, pl.BlockSpec((tm, 3 * D), lambda i: (i, 0)), rows, rows, rows,
                   pl.BlockSpec((tm, LANES), lambda i: (i, 0))],
        out_shape=[_sds((s, D), MXU)] * 3 + [_sds((s, 3 * D), MXU)] + [_sds((s, D), MXU)] * 3 + [_sds((s, LANES), F32)],
        scratch_shapes=_resident(4), compiler_params=_cp(1),
    )(dz, proj, pa, pc, pm, oa, wpa, wpc, wpm, wout)


def _dot_exact_rhs_t(v, b01):
    hi, mid, lo = _split3(v)
    return _dot(hi, b01) + _dot(mid, b01) + _dot(lo, b01)


def _in_proj_bwd(pieces, df, w_main, w_f, x, dx1, g_pre):
    s = x.shape[0]
    tm = min(TM_ROW, s)
    n_main = w_main.shape[1]
    np_ = len(pieces)

    def body(*refs):
        p_refs = refs[:np_]
        df_ref, x_ref, dx1_ref, g_ref, w_h, wf_ref, dx_ref, vec_ref, wbuf, sem = refs[np_:]

        @pl.when(pl.program_id(0) == 0)
        def _():
            vec_ref[...] = jnp.zeros_like(vec_ref)
            cp = pltpu.make_async_copy(w_h, wbuf, sem)
            cp.start()
            cp.wait()

        dh = _dot_nt(df_ref[...], wf_ref[...])
        for p_ref, (_, c0, nc) in zip(p_refs, pieces):
            dh = dh + _dot_nt(p_ref[...], wbuf[:, pl.ds(c0 * D, nc * D)])
        xv = x_ref[...]
        r = _rms(xv)
        vec_ref[0:1, :] += jnp.sum(dh * xv * r, axis=0, keepdims=True)
        dx_ref[...] = dx1_ref[...] + _rms_bwd(xv, r, g_ref[...], dh)

    rows = pl.BlockSpec((tm, D), lambda i: (i, 0))
    p_specs = [pl.BlockSpec((tm, nc * D), lambda i: (i, 0)) for _, _, nc in pieces]
    return _pc(
        body, name="in_proj_bwd", grid=(s // tm,),
        in_specs=p_specs + [pl.BlockSpec((tm, LANES), lambda i: (i, 0)), rows, rows, pl.BlockSpec((1, D), lambda i: (0, 0)),
                            pl.BlockSpec(memory_space=pl.ANY), pl.BlockSpec((D, LANES), lambda i: (0, 0))],
        out_specs=[rows, pl.BlockSpec((8, D), lambda i: (0, 0))],
        out_shape=[_sds((s, D), F32), _sds((8, D), F32)],
        scratch_shapes=[pltpu.VMEM((D, n_main), MXU), pltpu.SemaphoreType.DMA], compiler_params=_cp(1),
    )(*[p for p, _, _ in pieces], df, x, dx1, g_pre, w_main, w_f)


def _wgrad(xa, dy, name):
    s, k = xa.shape
    n = dy.shape[1]
    ts = min(TS_WG, s)
    tk = _tile(k, WG_CAP)
    tn = _tile(n, WG_CAP)

    def body(x_ref, dy_ref, o_ref, acc):
        @pl.when(pl.program_id(2) == 0)
        def _():
            acc[...] = jnp.zeros_like(acc)

        acc[...] += _dot_tn(x_ref[...], dy_ref[...])

        @pl.when(pl.program_id(2) == s // ts - 1)
        def _():
            o_ref[...] = acc[...].astype(WIRE)

    return _pc(
        body, name=name, grid=(k // tk, n // tn, s // ts),
        in_specs=[pl.BlockSpec((ts, tk), lambda a, b, c: (c, a)), pl.BlockSpec((ts, tn), lambda a, b, c: (c, b))],
        out_specs=pl.BlockSpec((tk, tn), lambda a, b, c: (a, b)), out_shape=_sds((k, n), WIRE),
        scratch_shapes=[pltpu.VMEM((tk, tn), F32)], compiler_params=_cp(3),
    )(xa, dy)


def _pair_sum(g, r1, c_idx):
    _, _, hr, cols = g.shape
    tr = _rowtile(hr, cols)

    def body(c_ref, g_ref, r_ref, o_ref):
        o_ref[0] = (g_ref[0, 0].astype(F32) + r_ref[0].astype(F32)).astype(WIRE)

    return _pc(
        body, name="pair_sum_%dx%d" % (hr, cols), out_shape=_sds((N_CHIPS, hr, cols), WIRE),
        grid_spec=pltpu.PrefetchScalarGridSpec(
            num_scalar_prefetch=1, grid=(N_CHIPS, hr // tr),
            in_specs=[pl.BlockSpec((1, 1, tr, cols), lambda d, i, c: (d, c[0], i, 0)),
                      pl.BlockSpec((1, tr, cols), lambda d, i, c: (d, i, 0))],
            out_specs=pl.BlockSpec((1, tr, cols), lambda d, i, c: (d, i, 0))),
        compiler_params=_cp(2),
    )(c_idx, g, r1)


def _chip_sum(r2, slot, base=None):
    _, hr, cols = r2.shape
    tr = _rowtile(hr, cols)

    def body(s_ref, r_ref, *rest):
        o_ref = rest[-1]
        acc = r_ref[0].astype(F32)
        for d in range(1, N_CHIPS):
            acc = acc + r_ref[d].astype(F32)
        o_ref[0] = acc

    based = base is not None
    return _pc(
        body, name="chip_sum_%dx%d_%d" % (hr, cols, int(based)), out_shape=_sds((2, hr, cols), F32),
        grid_spec=pltpu.PrefetchScalarGridSpec(
            num_scalar_prefetch=1, grid=(hr // tr,),
            in_specs=[pl.BlockSpec((N_CHIPS, tr, cols), lambda i, s: (0, i, 0))] + ([ANY] if based else []),
            out_specs=pl.BlockSpec((1, tr, cols), lambda i, s: (s[0], i, 0))),
        input_output_aliases={2: 0} if based else {}, compiler_params=_cp(1),
    )(*((slot, r2, base) if based else (slot, r2)))


def _adamw(w, g, m, v):
    rows, cols = w.shape
    tr = _rowtile(rows, cols)
    c1 = 1.0 / (1.0 - B1 ** STEP)
    c2 = 1.0 / (1.0 - B2 ** STEP)

    def body(w_ref, g_ref, m_ref, v_ref, d_ref, mo_ref, vo_ref):
        gv = g_ref[...]
        mn = B1 * m_ref[...] + (1.0 - B1) * gv
        vn = B2 * v_ref[...] + (1.0 - B2) * (gv * gv)
        mo_ref[...] = mn
        vo_ref[...] = vn
        d_ref[...] = -LR * ((mn * c1) / (jnp.sqrt(vn * c2) + ADAM_EPS) + WD * w_ref[...])

    blk = pl.BlockSpec((tr, cols), lambda i: (i, 0))
    return _pc(
        body, name="adamw_%dx%d" % (rows, cols), grid=(rows // tr,), in_specs=[blk] * 4, out_specs=[blk] * 3,
        out_shape=[_sds((rows, cols), F32)] * 3, compiler_params=_cp(1),
    )(w, g, m, v)


MESH_ID = pl.DeviceIdType.MESH
ANY = pl.BlockSpec(memory_space=pl.ANY)


def _place():
    x, y, c = lax.axis_index("x"), lax.axis_index("y"), lax.axis_index("c")
    others = [(1 - x, y), (x, 1 - y), (1 - x, 1 - y)]
    return x, y, c, others


def _remote(src, dst, sems, idx, to):
    return pltpu.make_async_remote_copy(src_ref=src, dst_ref=dst, send_sem=sems[0].at[idx], recv_sem=sems[1].at[idx],
                                        device_id=to, device_id_type=MESH_ID)


def _gather_plan(shards):
    nk = len(shards)

    def copies(ins, outs, sems):
        x, y, c, others = _place()
        me = 2 * x + y
        sib = (x, y, 1 - c)
        local = [pltpu.make_async_copy(ins[k], outs[k].at[me], sems[2].at[k]) for k in range(nk)]
        ici, landed, fwd, fwd_landed = [], [], [], []
        for k in range(nk):
            hr = shards[k].shape[0] // 2
            for r, (cx, cy) in enumerate(others):
                mine = pl.ds(c * hr, hr)
                ici.append(_remote(ins[k].at[mine], outs[k].at[me, mine], sems, 6 * k + r, (cx, cy, c)))
                got = outs[k].at[2 * cx + cy, mine]
                landed.append(_remote(got, got, sems, 6 * k + r, (cx, cy, c)))
                fwd.append(_remote(got, got, sems, 6 * k + 3 + r, sib))
                theirs = outs[k].at[2 * cx + cy, pl.ds((1 - c) * hr, hr)]
                fwd_landed.append(_remote(theirs, theirs, sems, 6 * k + 3 + r, sib))
        return local, ici, landed, fwd, fwd_landed

    def start(ins, outs, sems):
        local, ici, _, _, _ = copies(ins, outs, sems)
        for cp in local + ici:
            cp.start()

    def forward(ins, outs, sems):
        _, _, landed, fwd, _ = copies(ins, outs, sems)
        for got, cp in zip(landed, fwd):
            got.wait_recv()
            cp.start()

    def finish(ins, outs, sems):
        local, ici, _, fwd, fwd_landed = copies(ins, outs, sems)
        for got in fwd_landed:
            got.wait_recv()
        for cp in ici + fwd:
            cp.wait_send()
        for cp in local:
            cp.wait()

    return dict(
        arrays=list(shards), out_shape=[_sds((N_CHIPS,) + a.shape, a.dtype) for a in shards],
        scratch=[pltpu.SemaphoreType.DMA((6 * nk,)), pltpu.SemaphoreType.DMA((6 * nk,)), pltpu.SemaphoreType.DMA((nk,))],
        phases=[start, forward, finish])


def _scatter_plan(ps):
    nk = len(ps)

    def copies(ins, outs, sems):
        x, y, c, others = _place()
        me = 2 * x + y
        local = [pltpu.make_async_copy(ins[k].at[me], outs[k].at[me], sems[2].at[k]) for k in range(nk)]
        ici, landed = [], []
        for k in range(nk):
            for r, (cx, cy) in enumerate(others):
                ici.append(_remote(ins[k].at[2 * cx + cy], outs[k].at[me], sems, 3 * k + r, (cx, cy, c)))
                got = outs[k].at[2 * cx + cy]
                landed.append(_remote(got, got, sems, 3 * k + r, (cx, cy, c)))
        return local, ici, landed

    def start(ins, outs, sems):
        local, ici, _ = copies(ins, outs, sems)
        for cp in local + ici:
            cp.start()

    def finish(ins, outs, sems):
        local, ici, landed = copies(ins, outs, sems)
        for got in landed:
            got.wait_recv()
        for cp in ici:
            cp.wait_send()
        for cp in local:
            cp.wait()

    return dict(
        arrays=list(ps), out_shape=[_sds(a.shape, a.dtype) for a in ps],
        scratch=[pltpu.SemaphoreType.DMA((3 * nk,)), pltpu.SemaphoreType.DMA((3 * nk,)), pltpu.SemaphoreType.DMA((nk,))],
        phases=[start, finish])


def _run_plan(plan, name):
    nk = len(plan["arrays"])

    def body(*refs):
        ins, outs, sems = refs[:nk], refs[nk:2 * nk], refs[2 * nk:]
        for phase in plan["phases"]:
            phase(ins, outs, sems)

    return _pc(body, name=name, in_specs=[ANY] * nk, out_specs=[ANY] * nk, out_shape=plan["out_shape"],
               scratch_shapes=plan["scratch"])(*plan["arrays"])


def _host_plan(plan, ins, outs, sems, first, middle, last):
    points = [first, last] if len(plan["phases"]) == 2 else [first, middle, last]
    for phase, at in zip(plan["phases"], points):
        pl.when(at)(functools.partial(phase, ins, outs, sems))


def _swap_sibling(gs, halves, tag):
    nk = len(gs)

    def body(*refs):
        ins, outs = refs[:nk], refs[nk:2 * nk]
        send_sems, recv_sems = refs[2 * nk:]
        x, y, c, _ = _place()
        cps = []
        for k in range(nk):
            hr = gs[k].shape[1] // 2
            cp = pltpu.make_async_remote_copy(
                src_ref=ins[k].at[:, pl.ds((1 - c) * hr, hr)] if halves else ins[k], dst_ref=outs[k],
                send_sem=send_sems.at[k], recv_sem=recv_sems.at[k], device_id=(x, y, 1 - c), device_id_type=MESH_ID)
            cp.start()
            cps.append(cp)
        for cp in cps:
            cp.wait()

    return _pc(
        body, name=("swap_halves_" if halves else "swap_slabs_") + tag, in_specs=[ANY] * nk, out_specs=[ANY] * nk,
        out_shape=[_sds((N_CHIPS, a.shape[1] // 2 if halves else a.shape[1], a.shape[2]), a.dtype) for a in gs],
        scratch_shapes=[pltpu.SemaphoreType.DMA((nk,)), pltpu.SemaphoreType.DMA((nk,))],
    )(*gs)


def _allreduce_small(v):
    rows, cols = v.shape

    def body(v_ref, o_ref, gath, send_sems, recv_sems):
        x, y, c, others = _place()
        sib = (x, y, 1 - c)

        def slot(px, py, pc):
            return gath.at[4 * px + 2 * py + pc]

        def copy(k, block, to, src=None):
            return pltpu.make_async_remote_copy(
                src_ref=slot(*block) if src is None else src, dst_ref=slot(*block),
                send_sem=send_sems.at[k], recv_sem=recv_sems.at[k], device_id=to, device_id_type=MESH_ID)

        me = (x, y, c)
        gath[4 * x + 2 * y + c] = v_ref[...]
        first = [copy(0, me, sib, src=v_ref)]
        first += [copy(1 + r, me, (cx, cy, c), src=v_ref) for r, (cx, cy) in enumerate(others)]
        for cp in first:
            cp.start()
        passed = [copy(4 + r, (cx, cy, c), sib) for r, (cx, cy) in enumerate(others)]
        for r, (cx, cy) in enumerate(others):
            copy(1 + r, (cx, cy, c), me).wait_recv()
            passed[r].start()
        copy(0, (x, y, 1 - c), me).wait_recv()
        for r, (cx, cy) in enumerate(others):
            copy(4 + r, (cx, cy, 1 - c), me).wait_recv()
        for cp in first + passed:
            cp.wait_send()
        acc = gath[0]
        for d in range(1, N_DEV):
            acc = acc + gath[d]
        o_ref[...] = acc

    vm = pl.BlockSpec(memory_space=pltpu.VMEM)
    return _pc(
        body, name="allreduce_small", in_specs=[vm], out_specs=vm, out_shape=_sds((rows, cols), F32),
        scratch_shapes=[pltpu.VMEM((N_DEV, rows, cols), F32), pltpu.SemaphoreType.DMA((7,)), pltpu.SemaphoreType.DMA((7,))],
    )(v)


def _cols_to_chips(a):
    r, c4 = a.shape
    return a.reshape(r, N_CHIPS, c4 // N_CHIPS).transpose(1, 0, 2)


def _chips_to_cols(a):
    n, r, c = a.shape
    return a.transpose(1, 0, 2).reshape(r, n * c)


def _head_rows(a, tq):
    s = a.shape[0]
    return a[:, :HF].reshape(s // tq, tq, NP, 2).transpose(2, 0, 3, 1)


def _head_cols(a):
    s = a.shape[0]
    return a[:, :HF].reshape(s, NP, 2).transpose(1, 0, 2)


def _to_wire(gs, c_idx, tag):
    gs = [g.astype(WIRE) for g in gs]
    r1 = _swap_sibling(gs, True, tag)
    return [_pair_sum(g.reshape(N_CHIPS, 2, g.shape[1] // 2, g.shape[2]), r, c_idx) for g, r in zip(gs, r1)]


def _local_step(x, mem, tgt, sp, w_main, w_f, rest_shards, c_idx):
    s = x.shape[0]
    tq = min(TQ, s)
    b_f = jnp.pad(sp["b_forget"], ((0, 0), (0, LANES - HF)))
    proj, h, flog = _in_proj(x, sp["norm_mix_pre"], w_main, w_f)
    cf = _logf_cumsum(flog, b_f)
    qx, kx = _fox_prep(proj, _head_cols(cf))
    oa, lse, g_cw, g_kv, g_pj, g_gu, g_d = _fox_fwd(proj, qx, kx, _gather_plan(rest_shards))
    pj = g_pj.reshape(N_CHIPS, 4, D // N_CHIPS, D).transpose(1, 0, 2, 3).reshape(4, D, D)
    w = {"conv_w": _chips_to_cols(g_cw), "w_kv": _chips_to_cols(g_kv), "wpa": pj[0], "wpc": pj[1], "wpm": pj[2],
         "wout": pj[3], "w_gu": _chips_to_cols(g_gu), "w_d": g_d.reshape(FFN, D)}
    y, oc = _conv_fwd(proj, w["conv_w"], sp["conv_b"], sp["conv_ln_g"], sp["conv_ln_b"])
    mem_n, kv = _mem_kv(mem, sp["norm_mem"], w["w_kv"])
    om = _mem_attn_fwd(proj, kv)
    pa, pc, pm, merged, z, x1, h2 = _merge_out(oa, oc, om, proj, x, w["wpa"], w["wpc"], w["wpm"], w["wout"],
                                              sp["norm_mix_post"], sp["norm_ffn_pre"])
    gf, uf, act = _ffn_up(h2, w["w_gu"])
    dffn, dy, vec_f, loss_blk = _ffn_down_loss(act, w["w_d"], x1, tgt, sp["norm_ffn_post"])

    dgf, duf = _ffn_down_bwd(dffn, w["w_d"], gf, uf)
    dx1, dz, vec_n = _ffn_up_bwd(dgf, duf, w["w_gu"], x1, dy, z, sp["norm_ffn_pre"], sp["norm_mix_post"])
    dpa, dpc, dpm, dgl, doa, doc, dom, delta = _merge_bwd(dz, proj, pa, pc, pm, oa, w["wpa"], w["wpc"], w["wpm"], w["wout"])
    pj_g = jnp.stack([_wgrad(oa, dpa, "wgrad_pa"), _wgrad(oc, dpc, "wgrad_pc"), _wgrad(om, dpm, "wgrad_pm"),
                      _wgrad(merged, dz, "wgrad_out")]).reshape(4, N_CHIPS, D // N_CHIPS, D)
    early = [pj_g.transpose(1, 0, 2, 3).reshape(N_CHIPS, D, D),
             _cols_to_chips(jnp.concatenate([_wgrad(h2, dgf, "wgrad_g"), _wgrad(h2, duf, "wgrad_u")], axis=1)),
             _wgrad(act, dffn, "wgrad_d").reshape(N_CHIPS, FFN // N_CHIPS, D)]
    lse16 = lse.transpose(1, 0, 2).reshape(s, HF)
    dq, dqs, dk, dks, dv, *r2_early = _fox_bwd(proj, qx, kx, doa, _head_rows(lse16, tq), _head_rows(delta, tq),
                                              _scatter_plan(_to_wire(early, c_idx, "early")))
    over_keys = dqs[:, X_KONE:X_KONE + 2, :].transpose(2, 0, 1).reshape(s, HF)
    over_queries = dks[:, :, X_QONE:X_QONE + 2, :].transpose(1, 3, 0, 2).reshape(s, HF)
    dc = jnp.pad(over_keys - over_queries, ((0, 0), (0, LANES - HF)))
    df, db_blk = _logf_cumsum_bwd(flog, b_f, dc)
    dga, dgg, dcw, vec_c = _conv_bwd(proj, y, doc, w["conv_w"], sp["conv_ln_g"], sp["conv_ln_b"])
    dqm, dkv = _mem_attn_bwd(proj, kv, dom)
    dkv_b = dkv.astype(MXU)
    vec_m = _mem_kv_bwd(mem, sp["norm_mem"], w["w_kv"], dkv_b)
    pieces = [(dq, 0, 1), (dk, 1, 1), (dv, 2, 1), (dga, 3, 1), (dgg, 4, 1), (dqm, 5, 1), (dgl, 6, 3)]
    dx, vec_p = _in_proj_bwd(pieces, df, w_main, w_f, x, dx1, sp["norm_mix_pre"])

    dw_cols = [_wgrad(h, p, "wgrad_in_%d" % c0) for p, c0, _ in pieces]
    dwf = _wgrad(h, df, "wgrad_in_f")
    late = [_cols_to_chips(jnp.concatenate(dw_cols[:3] + [dwf[:, :HF]] + dw_cols[3:], axis=1)), _cols_to_chips(dcw),
            _cols_to_chips(_wgrad(mem_n, dkv_b, "wgrad_kv"))]
    r2_late = _run_plan(_scatter_plan(_to_wire(late, c_idx, "late")), "exchange_late")
    zero_row = jnp.zeros((1, D), F32)
    small = jnp.concatenate([
        vec_p[0:1], vec_n[1:2], vec_m[0:1], vec_c[2:3], vec_c[0:1], vec_c[1:2], vec_n[0:1], vec_f[0:1],
        jnp.pad(db_blk[0:1, :HF], ((0, 0), (0, D - HF))),
        jnp.pad(loss_blk[0:1, 0:1], ((0, 0), (0, D - 1))),
    ] + [zero_row] * (SMALL_ROWS - 10), axis=0)
    return dx, list(r2_late) + list(r2_early), small


SMALL_NAMES = ["norm_mix_pre", "norm_mix_post", "norm_mem", "conv_b", "conv_ln_g", "conv_ln_b", "norm_ffn_pre", "norm_ffn_post"]
PROJ_NAMES = ["w_proj_attn", "w_proj_conv", "w_proj_mem", "w_out"]
WEIGHT_ORDER = ["norm_mix_pre", "norm_mix_post", "norm_mem", "w_in", "b_forget", "conv_w", "conv_b", "conv_ln_g", "conv_ln_b",
                "w_kv_mem", "w_proj_attn", "w_proj_conv", "w_proj_mem", "w_out", "norm_ffn_pre", "norm_ffn_post",
                "w_gate_up", "w_down"]


def _pack_small(p):
    rows = [p[n] for n in SMALL_NAMES] + [jnp.pad(p["b_forget"], ((0, 0), (0, D - HF)))]
    return jnp.concatenate(rows + [jnp.zeros((SMALL_ROWS - len(rows), D), F32)], axis=0)


def _step(params, moms, vels, x, mem, tgt):
    c_idx = lax.axis_index("c").astype(jnp.int32).reshape(1)

    (g_in,) = _run_plan(_gather_plan([params["w_in"].astype(WIRE)]), "gather_w_in")
    w_in_full = _chips_to_cols(g_in)
    w_main = jnp.concatenate([w_in_full[:, :3 * D], w_in_full[:, 3 * D + HF:]], axis=1)
    w_f = jnp.pad(w_in_full[:, 3 * D:3 * D + HF], ((0, 0), (0, LANES - HF)))
    rest = [jnp.pad(params["conv_w"], ((0, CWP - CW), (0, 0))), params["w_kv_mem"].astype(WIRE),
            jnp.concatenate([params[n] for n in PROJ_NAMES], axis=0).astype(WIRE),
            params["w_gate_up"].astype(WIRE), params["w_down"].astype(WIRE)]

    dx, r2, small = _local_step(x, mem, tgt, params, w_main, w_f, rest, c_idx)

    r2_sib = _swap_sibling(r2, False, "all")
    full = [_chip_sum(theirs, 1 - c_idx, _chip_sum(mine, c_idx)) for mine, theirs in zip(r2, r2_sib)]
    red = [f.reshape(2 * f.shape[1], f.shape[2]) for f in full]
    pj_r = red[3].reshape(4, D // N_CHIPS, D)
    grads = {"w_in": red[0], "conv_w": red[1][:CW], "w_kv_mem": red[2], "w_gate_up": red[4], "w_down": red[5]}
    for i, n in enumerate(PROJ_NAMES):
        grads[n] = pj_r[i]

    tot = _allreduce_small(small)
    loss = tot[9, 0]
    for i, n in enumerate(SMALL_NAMES):
        grads[n] = tot[i:i + 1]
    grads["b_forget"] = tot[8:9, :HF]

    delta, new_m, new_v = {}, {}, {}
    ds, ms, vs = _adamw(_pack_small(params), tot.at[9:].set(0.0), _pack_small(moms), _pack_small(vels))
    for i, n in enumerate(SMALL_NAMES):
        delta[n], new_m[n], new_v[n] = ds[i:i + 1], ms[i:i + 1], vs[i:i + 1]
    delta["b_forget"], new_m["b_forget"], new_v["b_forget"] = ds[8:9, :HF], ms[8:9, :HF], vs[8:9, :HF]
    for n in ["w_in", "conv_w", "w_kv_mem", "w_gate_up", "w_down"] + PROJ_NAMES:
        delta[n], new_m[n], new_v[n] = _adamw(params[n], grads[n], moms[n], vels[n])
    return loss, dx, grads, delta, new_m, new_v


def kernel(x, mem, norm_mix_pre, norm_mix_post, norm_mem, w_in, b_forget, conv_w, conv_b, conv_ln_g, conv_ln_b, w_kv_mem, w_proj_attn, w_proj_conv, w_proj_mem, w_out, norm_ffn_pre, norm_ffn_post, w_gate_up, w_down, loss_target, m_norm_mix_pre, m_norm_mix_post, m_norm_mem, m_w_in, m_b_forget, m_conv_w, m_conv_b, m_conv_ln_g, m_conv_ln_b, m_w_kv_mem, m_w_proj_attn, m_w_proj_conv, m_w_proj_mem, m_w_out, m_norm_ffn_pre, m_norm_ffn_post, m_w_gate_up, m_w_down, v_norm_mix_pre, v_norm_mix_post, v_norm_mem, v_w_in, v_b_forget, v_conv_w, v_conv_b, v_conv_ln_g, v_conv_ln_b, v_w_kv_mem, v_w_proj_attn, v_w_proj_conv, v_w_proj_mem, v_w_out, v_norm_ffn_pre, v_norm_ffn_post, v_w_gate_up, v_w_down):
    local = dict(locals())
    lead = {n: local[n].shape[:-2] for n in WEIGHT_ORDER}
    two_d = lambda a: a.reshape(a.shape[-2:])
    params = {n: two_d(local[n]) for n in WEIGHT_ORDER}
    moms = {n: two_d(local["m_" + n]) for n in WEIGHT_ORDER}
    vels = {n: two_d(local["v_" + n]) for n in WEIGHT_ORDER}
    loss, dx, grads, delta, new_m, new_v = _step(params, moms, vels, two_d(x), two_d(mem), two_d(loss_target))
    outs = [loss, dx.reshape(x.shape)]
    for group in (grads, delta, new_m, new_v):
        outs += [group[n].reshape(lead[n] + group[n].shape) for n in WEIGHT_ORDER]
    return tuple(outs)
```

```python
import functools

import jax
import jax.numpy as jnp
from jax import lax
from jax.experimental import pallas as pl
from jax.experimental.pallas import tpu as pltpu

F32 = jnp.float32
MXU = jnp.bfloat16
WIRE = jnp.bfloat16

D = 1024
HF = 16
DH = 64
NP = D // 128
MEM_H = 4
MEM_DH = D // MEM_H
FFN = 2816
CW = 31
CWP = 32
HALO = 32
RMS_EPS = 1e-6
LN_EPS = 1e-5
LR, B1, B2, ADAM_EPS, WD, STEP = 0.001, 0.9, 0.999, 1e-8, 0.01, 10

N_CHIPS = 4
N_DEV = 8
LANES = 128
VMEM_LIMIT = 56 * 1024 * 1024

TM_PROJ = 512
NB_PROJ = 3
TQ = 1024
LOG2E = 1.4426950408889634
LN2 = 0.6931471805599453
QSCALE = DH ** -0.5 * LOG2E
X_BIAS = 0
X_QONE = 6
X_KONE = 8
X_ROWS = 16
TM_CONV = 256
CONV_ROWS = 128
SUB = 8
TM_ROW = 256
TM_WIDE = 4096
TS_WG = 2048
WG_CAP = 1408
SMALL_ROWS = 16


def _pc(body, **kw):
    return pl.pallas_call(body, **kw)


def _cp(n_axes):
    return pltpu.CompilerParams(dimension_semantics=("arbitrary",) * n_axes, vmem_limit_bytes=VMEM_LIMIT)


def _sds(shape, dtype):
    return jax.ShapeDtypeStruct(shape, dtype)


def _dot(a, b):
    return jnp.dot(a, b, preferred_element_type=F32)


def _dot_nt(a, b):
    return lax.dot_general(a, b, (((1,), (1,)), ((), ())), preferred_element_type=F32)


def _dot_tn(a, b):
    return lax.dot_general(a, b, (((0,), (0,)), ((), ())), preferred_element_type=F32)


def _rms(u):
    return lax.rsqrt(jnp.mean(u * u, axis=-1, keepdims=True) + RMS_EPS)


def _rms_bwd(u, r, g, dn):
    w = dn * g
    return r * w - u * (r * r * r) * jnp.mean(u * w, axis=-1, keepdims=True)


def _sigmoid(z):
    return 1.0 / (1.0 + jnp.exp(-z))


def _tile(n, cap):
    if n <= cap:
        return n
    best = None
    for t in range(LANES, cap + 1, LANES):
        if n % t == 0:
            best = t
    assert best is not None, (n, cap)
    return best


def _rowtile(rows, cols, cap_bytes=1 << 20):
    best = None
    for t in range(8, rows + 1, 8):
        if rows % t == 0 and t * cols * 4 <= cap_bytes:
            best = t
    return best if best is not None else rows


def _split3(v):
    hi = v.astype(jnp.bfloat16)
    r1 = v - hi.astype(F32)
    mid = r1.astype(jnp.bfloat16)
    lo = (r1 - mid.astype(F32)).astype(jnp.bfloat16)
    return hi, mid, lo


def _dot_exact_rhs(a01, v):
    hi, mid, lo = _split3(v)
    return _dot(a01, hi) + _dot(a01, mid) + _dot(a01, lo)


def _in_proj(x, g_pre, w_main, w_f):
    s, d = x.shape
    n = w_main.shape[1]
    tm = min(TM_PROJ, s)
    tn = n // NB_PROJ

    def body(x_ref, g_ref, w_ref, wf_ref, proj_ref, h_ref, flog_ref, hs):
        @pl.when(pl.program_id(1) == 0)
        def _():
            xv = x_ref[...]
            h = (xv * _rms(xv) * g_ref[...]).astype(MXU)
            hs[...] = h
            h_ref[...] = h
            flog_ref[...] = _dot(h, wf_ref[...])

        res = _dot(hs[...], w_ref[...])

        @pl.when(pl.program_id(1) == 0)
        def _():
            proj_ref[:, pl.ds(0, d)] = (res[:, :d] * QSCALE).astype(MXU)
            proj_ref[:, pl.ds(d, tn - d)] = res[:, d:].astype(MXU)

        @pl.when(pl.program_id(1) != 0)
        def _():
            proj_ref[...] = res.astype(MXU)

    assert tn >= d
    return _pc(
        body, name="in_proj", grid=(s // tm, NB_PROJ),
        in_specs=[pl.BlockSpec((tm, d), lambda i, j: (i, 0)), pl.BlockSpec((1, d), lambda i, j: (0, 0)),
                  pl.BlockSpec((d, tn), lambda i, j: (0, j)), pl.BlockSpec((d, LANES), lambda i, j: (0, 0))],
        out_specs=[pl.BlockSpec((tm, tn), lambda i, j: (i, j)), pl.BlockSpec((tm, d), lambda i, j: (i, 0)),
                   pl.BlockSpec((tm, LANES), lambda i, j: (i, 0))],
        out_shape=[_sds((s, n), MXU), _sds((s, d), MXU), _sds((s, LANES), F32)],
        scratch_shapes=[pltpu.VMEM((tm, d), MXU)], compiler_params=_cp(2),
    )(x, g_pre, w_main, w_f)


def _log_sigmoid(z):
    e = jnp.exp(-jnp.abs(z))
    log1p_e = jnp.where(e < 1e-3, e * (1.0 - 0.5 * e), jnp.log(1.0 + e))
    return jnp.minimum(z, 0.0) - log1p_e


def _logf_cumsum(flog, b_f):
    s = flog.shape[0]
    ch = LANES

    def body(f_ref, b_ref, c_ref):
        r = lax.broadcasted_iota(jnp.int32, (ch, ch), 0)
        q = lax.broadcasted_iota(jnp.int32, (ch, ch), 1)
        tri = jnp.where(r >= q, 1.0, 0.0).astype(jnp.bfloat16)

        def step(i, carry):
            rows = pl.ds(pl.multiple_of(i * ch, ch), ch)
            lf = _log_sigmoid(f_ref[rows, :] + b_ref[...])
            c_ref[rows, :] = _dot_exact_rhs(tri, lf) + carry
            return carry + jnp.sum(lf, axis=0, keepdims=True)

        lax.fori_loop(0, s // ch, step, jnp.zeros((1, LANES), F32))

    return _pc(body, name="logf_cumsum", out_shape=_sds((s, LANES), F32),
               compiler_params=pltpu.CompilerParams(vmem_limit_bytes=VMEM_LIMIT))(flog, b_f)


def _logf_cumsum_bwd(flog, b_f, dc):
    s = flog.shape[0]
    ch = LANES

    def body(f_ref, b_ref, dc_ref, df_ref, db_ref):
        r = lax.broadcasted_iota(jnp.int32, (ch, ch), 0)
        q = lax.broadcasted_iota(jnp.int32, (ch, ch), 1)
        tri = jnp.where(r <= q, 1.0, 0.0).astype(jnp.bfloat16)
        nch = s // ch

        def step(t, carry):
            tail, dbsum = carry
            i = nch - 1 - t
            rows = pl.ds(pl.multiple_of(i * ch, ch), ch)
            dcv = dc_ref[rows, :]
            dlf = _dot_exact_rhs(tri, dcv) + tail
            z = f_ref[rows, :] + b_ref[...]
            df = dlf * _sigmoid(-z)
            df_ref[rows, :] = df.astype(MXU)
            return tail + jnp.sum(dcv, axis=0, keepdims=True), dbsum + jnp.sum(df, axis=0, keepdims=True)

        zero = jnp.zeros((1, LANES), F32)
        _, dbsum = lax.fori_loop(0, nch, step, (zero, zero))
        db_ref[...] = jnp.broadcast_to(dbsum, db_ref.shape)

    return _pc(body, name="logf_cumsum_bwd", out_shape=[_sds((s, LANES), MXU), _sds((8, LANES), F32)],
               compiler_params=pltpu.CompilerParams(vmem_limit_bytes=VMEM_LIMIT))(flog, b_f, dc)


def _head_masks(rows):
    lane = lax.broadcasted_iota(jnp.int32, (rows, LANES), 1)
    return lane < DH, lane >= DH


def _ext_masks(rows, key_side):
    lane = lax.broadcasted_iota(jnp.int32, (rows, 2 * LANES), 1)
    ext = lane - LANES
    out = []
    for a in range(2):
        head = (lane >= a * DH) & (lane < (a + 1) * DH)
        bias = (ext >= X_BIAS + 3 * a) & (ext < X_BIAS + 3 * a + 3)
        one = ext == (X_KONE if key_side else X_QONE) + a
        out.append(head | bias | one)
    return out


def _fox_prep(proj, ccol):
    s = proj.shape[0]
    tm = min(TM_WIDE, s)

    def body(q_ref, k_ref, c_ref, qx_ref, kx_ref):
        lane = lax.broadcasted_iota(jnp.int32, (tm, LANES), 1)
        qx_ref[:, pl.ds(0, LANES)] = q_ref[...]
        qx_ref[:, pl.ds(LANES, LANES)] = jnp.where(lane < X_QONE + 2, 1.0, 0.0).astype(MXU)
        kext = jnp.where((lane >= X_KONE) & (lane < X_KONE + 2), 1.0, 0.0).astype(jnp.bfloat16)
        for a in range(2):
            terms = _split3(c_ref[0, :, a:a + 1] * (-LOG2E))
            for t, term in enumerate(terms):
                kext = jnp.where(lane == X_BIAS + 3 * a + t, term, kext)
        kx_ref[:, pl.ds(0, LANES)] = k_ref[...]
        kx_ref[:, pl.ds(LANES, LANES)] = kext.astype(MXU)

    wide = pl.BlockSpec((tm, 2 * LANES), lambda p, i: (i, p))
    return _pc(
        body, name="fox_prep", grid=(NP, s // tm),
        in_specs=[pl.BlockSpec((tm, LANES), lambda p, i: (i, p)), pl.BlockSpec((tm, LANES), lambda p, i: (i, NP + p)),
                  pl.BlockSpec((1, tm, 2), lambda p, i: (p, i, 0))],
        out_specs=[wide, wide], out_shape=[_sds((s, NP * 2 * LANES), MXU)] * 2, compiler_params=_cp(2),
    )(proj, proj, ccol)


def _fox_fwd(proj, qx, kx, plan):
    s = proj.shape[0]
    tq = min(TQ, s)
    nq = s // tq
    nx = len(plan["arrays"])

    def body(*refs):
        q_ref, k_ref, v_ref = refs[:3]
        o_ref, lse_ref = refs[3 + nx:5 + nx]
        p_id, i = pl.program_id(0), pl.program_id(1)
        _host_plan(plan, refs[3:3 + nx], refs[5 + nx:5 + 2 * nx], refs[5 + 2 * nx:], (p_id == 0) & (i == 0),
                   (p_id == NP // 2) & (i == 0), (p_id == NP - 1) & (i == nq - 1))
        qv = q_ref[...]
        qmask = _ext_masks(tq, False)
        hmask = _head_masks(tq)
        qas = [jnp.where(qmask[a], qv, jnp.zeros_like(qv)) for a in range(2)]
        row = lax.broadcasted_iota(jnp.int32, (tq, tq), 0)
        col = lax.broadcasted_iota(jnp.int32, (tq, tq), 1)

        def blk(j, carry, diag=False):
            rows = pl.ds(pl.multiple_of(j * tq, tq), tq)
            kj = k_ref[rows, :]
            vj = v_ref[rows, :]
            out = []
            for a in range(2):
                m, acc = carry[a]
                sc = _dot_nt(qas[a], kj)
                if diag:
                    sc = jnp.where(row >= col, sc, -jnp.inf)
                m_new = jnp.maximum(m, jnp.max(sc, axis=-1, keepdims=True))
                p = jnp.exp2(sc - m_new)
                va = jnp.where(hmask[a], vj, jnp.ones_like(vj))
                out.append((m_new, jnp.exp2(m - m_new) * acc + _dot(p.astype(MXU), va)))
            return tuple(out)

        init = (jnp.full((tq, 1), -jnp.inf, F32), jnp.zeros((tq, LANES), F32))
        res = blk(i, lax.fori_loop(0, i, blk, (init, init)), True)
        lane = lax.broadcasted_iota(jnp.int32, (tq, LANES), 1)
        outs, lses = [], []
        for a in range(2):
            m, acc = res[a]
            l = jnp.sum(jnp.where(lane == DH * (1 - a), acc, 0.0), axis=-1, keepdims=True)
            outs.append(acc / l)
            lses.append(m + jnp.log(l) * LOG2E)
        o_ref[...] = jnp.where(hmask[0], outs[0], outs[1]).astype(MXU)
        lane2 = lax.broadcasted_iota(jnp.int32, (tq, 2), 1)
        lse_ref[0] = jnp.where(lane2 == 0, lses[0], lses[1])

    return _pc(
        body, name="fox_fwd", grid=(NP, nq),
        in_specs=[pl.BlockSpec((tq, 2 * LANES), lambda p, i: (i, p)),
                  pl.BlockSpec((s, 2 * LANES), lambda p, i: (0, p)),
                  pl.BlockSpec((s, LANES), lambda p, i: (0, 2 * NP + p))] + [ANY] * nx,
        out_specs=[pl.BlockSpec((tq, LANES), lambda p, i: (i, p)),
                   pl.BlockSpec((1, tq, 2), lambda p, i: (p, i, 0))] + [ANY] * nx,
        out_shape=[_sds((s, D), MXU), _sds((NP, s, 2), F32)] + plan["out_shape"],
        scratch_shapes=plan["scratch"], compiler_params=_cp(2),
    )(qx, kx, proj, *plan["arrays"])


def _fox_bwd(proj, qx, kx, do, lse_row, delta_row, plan):
    s = proj.shape[0]
    tq = min(TQ, s)
    nq = s // tq
    nx = len(plan["arrays"])

    def body(*refs):
        k_ref, v_ref, q_ref, do_ref, lse_ref, dl_ref = refs[:6]
        dq_ref, dqs_ref, dk_ref, dks_ref, dv_ref = refs[6 + nx:11 + nx]
        dq_acc = refs[11 + 2 * nx]
        p_id, j = pl.program_id(0), pl.program_id(1)
        _host_plan(plan, refs[6:6 + nx], refs[11 + nx:11 + 2 * nx], refs[12 + 2 * nx:], (p_id == 0) & (j == 0),
                   (p_id == NP // 2) & (j == 0), (p_id == NP - 1) & (j == nq - 1))

        @pl.when(j == 0)
        def _():
            dq_acc[...] = jnp.zeros_like(dq_acc)

        kv = k_ref[...]
        v2 = v_ref[...]
        kmask = _ext_masks(tq, True)
        qmask = _ext_masks(tq, False)
        hmask = _head_masks(tq)
        row = lax.broadcasted_iota(jnp.int32, (tq, tq), 0)
        col = lax.broadcasted_iota(jnp.int32, (tq, tq), 1)
        carry = (jnp.zeros((tq, 2 * LANES), F32), jnp.zeros((tq, LANES), F32))
        for a in range(2):
            ka = jnp.where(kmask[a], kv, jnp.zeros_like(kv))
            va = jnp.where(hmask[a], v2, jnp.zeros_like(v2))

            def blk(i, carry, diag, a=a, ka=ka, va=va):
                dk_a, dv_a = carry
                rows = pl.ds(pl.multiple_of(i * tq, tq), tq)
                qi = q_ref[rows, :]
                doi = do_ref[rows, :]
                qa = jnp.where(qmask[a], qi, jnp.zeros_like(qi))
                doa = jnp.where(hmask[a], doi, jnp.zeros_like(doi))
                st = _dot_nt(ka, qi)
                if diag:
                    st = jnp.where(col >= row, st, -jnp.inf)
                pt = jnp.exp2(st - lse_ref[0, i, a:a + 1, :])
                dv_a = dv_a + _dot(pt.astype(MXU), doa)
                dpt = _dot_nt(va, doi)
                dsb = (pt * (dpt - dl_ref[0, i, a:a + 1, :])).astype(MXU)
                dk_a = dk_a + _dot(dsb, qa)
                dq_acc[rows, :] += _dot_tn(dsb, ka)
                return dk_a, dv_a

            carry = blk(j, carry, True)
            carry = lax.fori_loop(j + 1, nq, functools.partial(blk, diag=False), carry)
        dk_acc, dv_acc = carry
        dk_ref[...] = (dk_acc[:, :LANES] * LN2).astype(MXU)
        dks_ref[0, 0] = dk_acc[:, LANES:].T[:X_ROWS, :]
        dv_ref[...] = dv_acc.astype(MXU)

        @pl.when(j == nq - 1)
        def _():
            dq_ref[...] = (dq_acc[:, pl.ds(0, LANES)] * DH ** -0.5).astype(MXU)
            for t in range(nq):
                dqs_ref[0, :, pl.ds(t * tq, tq)] = dq_acc[pl.ds(t * tq, tq), pl.ds(LANES, LANES)].T[:X_ROWS, :]

    stat = pl.BlockSpec((1, nq, 2, tq), lambda p, j: (p, 0, 0, 0))
    whole = pl.BlockSpec((s, LANES), lambda p, j: (0, p))
    tile = pl.BlockSpec((tq, LANES), lambda p, j: (j, p))
    return _pc(
        body, name="fox_bwd", grid=(NP, nq),
        in_specs=[pl.BlockSpec((tq, 2 * LANES), lambda p, j: (j, p)),
                  pl.BlockSpec((tq, LANES), lambda p, j: (j, 2 * NP + p)),
                  pl.BlockSpec((s, 2 * LANES), lambda p, j: (0, p)),
                  whole, stat, stat] + [ANY] * nx,
        out_specs=[whole, pl.BlockSpec((1, X_ROWS, s), lambda p, j: (p, 0, 0)), tile,
                   pl.BlockSpec((1, 1, X_ROWS, tq), lambda p, j: (p, j, 0, 0)), tile] + [ANY] * nx,
        out_shape=[_sds((s, D), MXU), _sds((NP, X_ROWS, s), F32), _sds((s, D), MXU),
                   _sds((NP, nq, X_ROWS, tq), F32), _sds((s, D), MXU)] + plan["out_shape"],
        scratch_shapes=[pltpu.VMEM((s, 2 * LANES), F32)] + plan["scratch"], compiler_params=_cp(2),
    )(kx, proj, qx, do, lse_row, delta_row, *plan["arrays"])


def _glu(a, gate):
    return a.astype(F32) * _sigmoid(gate.astype(F32))


def _store_blocked(buf, row0, val):
    for c in range(D // LANES):
        buf[0, c, pl.ds(row0, val.shape[0]), :] = val[:, c * LANES:(c + 1) * LANES]


def _fill_shifted(buf):
    n = buf.shape[2] - SUB
    for r in range(1, SUB):
        buf[r, :, pl.ds(0, n), :] = buf[0, :, pl.ds(r, n), :]


def _shifted(buf, off, rows, c):
    r = off % SUB
    return buf[r, c, pl.ds(off - r, rows), :]


def _conv_fwd(proj, cw, cb, lg, lb):
    s = proj.shape[0]
    tm = min(TM_CONV, s)
    hb = tm // HALO

    rcw = min(CONV_ROWS, tm)

    def body(a_ref, g_ref, ah_ref, gh_ref, w_ref, cb_ref, lg_ref, lb_ref, y_ref, o_ref, gsh):
        i = pl.program_id(0)
        _store_blocked(gsh, 0, jnp.where(i > 0, _glu(ah_ref[...], gh_ref[...]), 0.0))
        _store_blocked(gsh, HALO, _glu(a_ref[...], g_ref[...]))
        _fill_shifted(gsh)
        for c in range(D // LANES):
            cols = pl.ds(c * LANES, LANES)
            for rc in range(tm // rcw):
                acc = jnp.broadcast_to(cb_ref[:, cols], (rcw, LANES))
                for t in range(CW):
                    acc = acc + w_ref[t:t + 1, cols] * _shifted(gsh, HALO - (CW - 1) + t + rc * rcw, rcw, c)
                y_ref[pl.ds(rc * rcw, rcw), cols] = acc
        acc = y_ref[...]
        mu = jnp.mean(acc, axis=-1, keepdims=True)
        xc = acc - mu
        r = lax.rsqrt(jnp.mean(xc * xc, axis=-1, keepdims=True) + LN_EPS)
        nrm = xc * r * lg_ref[...] + lb_ref[...]
        o_ref[...] = (nrm * _sigmoid(nrm)).astype(MXU)

    vec = pl.BlockSpec((1, D), lambda i: (0, 0))
    return _pc(
        body, name="conv_fwd", grid=(s // tm,),
        in_specs=[pl.BlockSpec((tm, D), lambda i: (i, 3)), pl.BlockSpec((tm, D), lambda i: (i, 4)),
                  pl.BlockSpec((HALO, D), lambda i: (jnp.maximum(i * hb - 1, 0), 3)),
                  pl.BlockSpec((HALO, D), lambda i: (jnp.maximum(i * hb - 1, 0), 4)),
                  pl.BlockSpec((CWP, D), lambda i: (0, 0)), vec, vec, vec],
        out_specs=[pl.BlockSpec((tm, D), lambda i: (i, 0)), pl.BlockSpec((tm, D), lambda i: (i, 0))],
        out_shape=[_sds((s, D), F32), _sds((s, D), MXU)],
        scratch_shapes=[pltpu.VMEM((SUB, D // LANES, tm + HALO, LANES), F32)], compiler_params=_cp(1),
    )(proj, proj, proj, proj, cw, cb, lg, lb)


def _conv_bwd(proj, y, do, cw, lg, lb):
    s = proj.shape[0]
    tm = min(TM_CONV, s)
    hb = tm // HALO
    nt = s // tm
    last_hblk = s // HALO - 1

    def ln_bwd(yv, dov, lgv, lbv):
        mu = jnp.mean(yv, axis=-1, keepdims=True)
        xc = yv - mu
        r = lax.rsqrt(jnp.mean(xc * xc, axis=-1, keepdims=True) + LN_EPS)
        xh = xc * r
        nrm = xh * lgv + lbv
        sg = _sigmoid(nrm)
        dn = dov.astype(F32) * (sg * (1.0 + nrm * (1.0 - sg)))
        wv = dn * lgv
        dy = r * (wv - jnp.mean(wv, axis=-1, keepdims=True) - xh * jnp.mean(wv * xh, axis=-1, keepdims=True))
        return dy, dn, xh

    rcw = min(CONV_ROWS, tm)

    def body(a_ref, g_ref, ah_ref, gh_ref, y_ref, yn_ref, do_ref, don_ref, w_ref, lg_ref, lb_ref,
             da_ref, dg_ref, dw_ref, vec_ref, gsh, dysh, dwacc):
        i = pl.program_id(0)

        @pl.when(i == 0)
        def _():
            dwacc[...] = jnp.zeros_like(dwacc)
            vec_ref[...] = jnp.zeros_like(vec_ref)

        lgv, lbv = lg_ref[...], lb_ref[...]
        _store_blocked(gsh, 0, jnp.where(i > 0, _glu(ah_ref[...], gh_ref[...]), 0.0))
        _store_blocked(gsh, HALO, _glu(a_ref[...], g_ref[...]))
        _fill_shifted(gsh)
        dy, dn, xh = ln_bwd(y_ref[...], do_ref[...], lgv, lbv)
        dyn, _, _ = ln_bwd(yn_ref[...], don_ref[...], lgv, lbv)
        _store_blocked(dysh, 0, dy)
        _store_blocked(dysh, tm, jnp.where(i < nt - 1, dyn, 0.0))
        _fill_shifted(dysh)
        vec_ref[0:1, :] += jnp.sum(dn * xh, axis=0, keepdims=True)
        vec_ref[1:2, :] += jnp.sum(dn, axis=0, keepdims=True)
        vec_ref[2:3, :] += jnp.sum(dy, axis=0, keepdims=True)
        for c in range(D // LANES):
            cols = pl.ds(c * LANES, LANES)
            for rc in range(tm // rcw):
                rows = pl.ds(rc * rcw, rcw)
                dyc = dysh[0, c, rows, :]
                dgl = jnp.zeros((rcw, LANES), F32)
                for t in range(CW):
                    dgl = dgl + w_ref[t:t + 1, cols] * _shifted(dysh, CW - 1 - t + rc * rcw, rcw, c)
                    prod = dyc * _shifted(gsh, HALO - (CW - 1) + t + rc * rcw, rcw, c)
                    dwacc[t, :, cols] += jnp.sum(prod.reshape(rcw // SUB, SUB, LANES), axis=0)
                av = a_ref[rows, cols].astype(F32)
                sgate = _sigmoid(g_ref[rows, cols].astype(F32))
                da_ref[rows, cols] = (dgl * sgate).astype(MXU)
                dg_ref[rows, cols] = (dgl * av * sgate * (1.0 - sgate)).astype(MXU)

        @pl.when(i == nt - 1)
        def _():
            dw_ref[...] = jnp.sum(dwacc[...], axis=1)

    vec = pl.BlockSpec((1, D), lambda i: (0, 0))
    cur = lambda c: pl.BlockSpec((tm, D), lambda i: (i, c))
    prv = lambda c: pl.BlockSpec((HALO, D), lambda i: (jnp.maximum(i * hb - 1, 0), c))
    nxt = pl.BlockSpec((HALO, D), lambda i: (jnp.minimum((i + 1) * hb, last_hblk), 0))
    return _pc(
        body, name="conv_bwd", grid=(nt,),
        in_specs=[cur(3), cur(4), prv(3), prv(4), cur(0), nxt, cur(0), nxt,
                  pl.BlockSpec((CWP, D), lambda i: (0, 0)), vec, vec],
        out_specs=[cur(0), cur(0), pl.BlockSpec((CWP, D), lambda i: (0, 0)), pl.BlockSpec((8, D), lambda i: (0, 0))],
        out_shape=[_sds((s, D), MXU), _sds((s, D), MXU), _sds((CWP, D), F32), _sds((8, D), F32)],
        scratch_shapes=[pltpu.VMEM((SUB, D // LANES, tm + HALO, LANES), F32)] * 2 + [pltpu.VMEM((CWP, SUB, D), F32)],
        compiler_params=_cp(1),
    )(proj, proj, proj, proj, y, y, do, do, cw, lg, lb)


def _mem_kv(mem, g_mem, w_kv):
    mm = mem.shape[0]

    def body(m_ref, g_ref, w_ref, mn_ref, kv_ref):
        mv = m_ref[...]
        mn = (mv * _rms(mv) * g_ref[...]).astype(MXU)
        mn_ref[...] = mn
        kv_ref[...] = _dot(mn, w_ref[...]).astype(MXU)

    return _pc(body, name="mem_kv", out_shape=[_sds((mm, D), MXU), _sds((mm, 2 * D), MXU)],
               compiler_params=pltpu.CompilerParams(vmem_limit_bytes=VMEM_LIMIT))(mem, g_mem, w_kv)


def _mem_kv_bwd(mem, g_mem, w_kv, dkv):
    mm = mem.shape[0]

    def body(m_ref, w_ref, dkv_ref, o_ref):
        mv = m_ref[...]
        dmn = _dot_nt(dkv_ref[...], w_ref[...])
        o_ref[...] = jnp.broadcast_to(jnp.sum(dmn * mv * _rms(mv), axis=0, keepdims=True), o_ref.shape)

    return _pc(body, name="mem_kv_bwd", out_shape=_sds((8, D), F32),
               compiler_params=pltpu.CompilerParams(vmem_limit_bytes=VMEM_LIMIT))(mem, w_kv, dkv)


def _mem_attn_fwd(proj, kv):
    s = proj.shape[0]
    mm = kv.shape[0]
    tm = min(TM_PROJ, s)
    scale = MEM_DH ** -0.5

    def body(q_ref, kv_ref, o_ref):
        for h in range(MEM_H):
            cols = pl.ds(h * MEM_DH, MEM_DH)
            qh = q_ref[:, cols] * scale
            sc = _dot_nt(qh, kv_ref[:, cols])
            m = jnp.max(sc, axis=-1, keepdims=True)
            e = jnp.exp(sc - m)
            p = e / jnp.sum(e, axis=-1, keepdims=True)
            o_ref[:, cols] = _dot(p.astype(MXU), kv_ref[:, pl.ds(D + h * MEM_DH, MEM_DH)]).astype(MXU)

    return _pc(
        body, name="mem_attn_fwd", grid=(s // tm,),
        in_specs=[pl.BlockSpec((tm, D), lambda i: (i, 5)), pl.BlockSpec((mm, 2 * D), lambda i: (0, 0))],
        out_specs=pl.BlockSpec((tm, D), lambda i: (i, 0)), out_shape=_sds((s, D), MXU), compiler_params=_cp(1),
    )(proj, kv)


def _mem_attn_bwd(proj, kv, do):
    s = proj.shape[0]
    mm = kv.shape[0]
    tm = min(TM_PROJ, s)
    scale = MEM_DH ** -0.5

    def body(q_ref, kv_ref, do_ref, dq_ref, dkv_ref):
        @pl.when(pl.program_id(0) == 0)
        def _():
            dkv_ref[...] = jnp.zeros_like(dkv_ref)

        for h in range(MEM_H):
            cols = pl.ds(h * MEM_DH, MEM_DH)
            vcols = pl.ds(D + h * MEM_DH, MEM_DH)
            qh = q_ref[:, cols]
            kh = kv_ref[:, cols] * scale
            doh = do_ref[:, cols]
            st = _dot_nt(kh, qh)
            m = jnp.max(st, axis=0, keepdims=True)
            e = jnp.exp(st - m)
            pt = e / jnp.sum(e, axis=0, keepdims=True)
            dpt = _dot_nt(kv_ref[:, vcols], doh)
            dst = pt * (dpt - jnp.sum(pt * dpt, axis=0, keepdims=True))
            dsb = dst.astype(MXU)
            dkv_ref[:, vcols] += _dot(pt.astype(MXU), doh)
            dkv_ref[:, cols] += _dot(dsb, qh) * scale
            dq_ref[:, cols] = _dot_tn(dsb, kh).astype(MXU)

    return _pc(
        body, name="mem_attn_bwd", grid=(s // tm,),
        in_specs=[pl.BlockSpec((tm, D), lambda i: (i, 5)), pl.BlockSpec((mm, 2 * D), lambda i: (0, 0)),
                  pl.BlockSpec((tm, D), lambda i: (i, 0))],
        out_specs=[pl.BlockSpec((tm, D), lambda i: (i, 0)), pl.BlockSpec((mm, 2 * D), lambda i: (0, 0))],
        out_shape=[_sds((s, D), MXU), _sds((mm, 2 * D), F32)], compiler_params=_cp(1),
    )(proj, kv, do)


def _resident(n):
    return [pltpu.VMEM((n, D, D), MXU), pltpu.SemaphoreType.DMA((n,))]


def _load_resident(hbm_refs, wbuf, sems):
    @pl.when(pl.program_id(0) == 0)
    def _():
        cps = [pltpu.make_async_copy(r, wbuf.at[k], sems.at[k]) for k, r in enumerate(hbm_refs)]
        for cp in cps:
            cp.start()
        for cp in cps:
            cp.wait()


def _merge_out(oa, oc, om, proj, x, wpa, wpc, wpm, wout, g_post, g_fpre):
    s = x.shape[0]
    tm = min(TM_ROW, s)

    def body(oa_ref, oc_ref, om_ref, gl_ref, x_ref, gp_ref, gf_ref, wpa_h, wpc_h, wpm_h, wout_h,
             pa_ref, pc_ref, pm_ref, mg_ref, z_ref, x1_ref, h2_ref, wbuf, sems):
        _load_resident([wpa_h, wpc_h, wpm_h, wout_h], wbuf, sems)
        merged = jnp.zeros((tm, D), F32)
        for b, (o_ref, p_ref) in enumerate(((oa_ref, pa_ref), (oc_ref, pc_ref), (om_ref, pm_ref))):
            pb = _dot(o_ref[...], wbuf[b])
            p_ref[...] = pb.astype(MXU)
            merged = merged + _sigmoid(gl_ref[:, pl.ds(b * D, D)].astype(F32)) * pb
        mg = merged.astype(MXU)
        mg_ref[...] = mg
        z = _dot(mg, wbuf[3])
        z_ref[...] = z
        x1 = x_ref[...] + z * _rms(z) * gp_ref[...]
        x1_ref[...] = x1
        h2_ref[...] = (x1 * _rms(x1) * gf_ref[...]).astype(MXU)

    rows = pl.BlockSpec((tm, D), lambda i: (i, 0))
    vec = pl.BlockSpec((1, D), lambda i: (0, 0))
    anyspec = pl.BlockSpec(memory_space=pl.ANY)
    return _pc(
        body, name="merge_out", grid=(s // tm,),
        in_specs=[rows, rows, rows, pl.BlockSpec((tm, 3 * D), lambda i: (i, 2)), rows, vec, vec,
                  anyspec, anyspec, anyspec, anyspec],
        out_specs=[rows] * 7,
        out_shape=[_sds((s, D), MXU)] * 4 + [_sds((s, D), F32)] * 2 + [_sds((s, D), MXU)],
        scratch_shapes=_resident(4), compiler_params=_cp(1),
    )(oa, oc, om, proj, x, g_post, g_fpre, wpa, wpc, wpm, wout)


def _ffn_up(h2, w_gu):
    s = h2.shape[0]
    tm = min(TM_PROJ, s)
    nb = 2
    bw = FFN // nb

    def body(h_ref, wg_ref, wu_ref, gf_ref, uf_ref, act_ref):
        hv = h_ref[...]
        gf = _dot(hv, wg_ref[...])
        uf = _dot(hv, wu_ref[...])
        gf_ref[...] = gf.astype(MXU)
        uf_ref[...] = uf.astype(MXU)
        act_ref[...] = (gf * _sigmoid(gf) * uf).astype(MXU)

    out = pl.BlockSpec((tm, bw), lambda i, j: (i, j))
    return _pc(
        body, name="ffn_up", grid=(s // tm, nb),
        in_specs=[pl.BlockSpec((tm, D), lambda i, j: (i, 0)), pl.BlockSpec((D, bw), lambda i, j: (0, j)),
                  pl.BlockSpec((D, bw), lambda i, j: (0, nb + j))],
        out_specs=[out, out, out], out_shape=[_sds((s, FFN), MXU)] * 3, compiler_params=_cp(2),
    )(h2, w_gu, w_gu)


def _ffn_down_loss(act, w_d, x1, tgt, g_fpost):
    s = act.shape[0]
    tm = min(TM_PROJ, s)

    def body(a_ref, w_ref, x1_ref, t_ref, g_ref, dffn_ref, dy_ref, vec_ref, loss_ref):
        @pl.when(pl.program_id(0) == 0)
        def _():
            vec_ref[...] = jnp.zeros_like(vec_ref)
            loss_ref[...] = jnp.zeros_like(loss_ref)

        ffn = _dot(a_ref[...], w_ref[...])
        r = _rms(ffn)
        gv = g_ref[...]
        e = x1_ref[...] + ffn * r * gv - t_ref[...]
        loss_ref[...] += jnp.sum(e * e) * (0.5 / D)
        dy = e * (1.0 / D)
        dy_ref[...] = dy
        vec_ref[0:1, :] += jnp.sum(dy * ffn * r, axis=0, keepdims=True)
        dffn_ref[...] = _rms_bwd(ffn, r, gv, dy).astype(MXU)

    rows = pl.BlockSpec((tm, D), lambda i: (i, 0))
    return _pc(
        body, name="ffn_down_loss", grid=(s // tm,),
        in_specs=[pl.BlockSpec((tm, FFN), lambda i: (i, 0)), pl.BlockSpec((FFN, D), lambda i: (0, 0)), rows, rows,
                  pl.BlockSpec((1, D), lambda i: (0, 0))],
        out_specs=[rows, rows, pl.BlockSpec((8, D), lambda i: (0, 0)), pl.BlockSpec((8, LANES), lambda i: (0, 0))],
        out_shape=[_sds((s, D), MXU), _sds((s, D), F32), _sds((8, D), F32), _sds((8, LANES), F32)],
        compiler_params=_cp(1),
    )(act, w_d, x1, tgt, g_fpost)


def _ffn_down_bwd(dffn, w_d, gf, uf):
    s = dffn.shape[0]
    tm = min(TM_ROW, s)

    def body(d_ref, w_ref, gf_ref, uf_ref, dgf_ref, duf_ref):
        da = _dot_nt(d_ref[...], w_ref[...])
        gf = gf_ref[...].astype(F32)
        sg = _sigmoid(gf)
        duf_ref[...] = (da * gf * sg).astype(MXU)
        dgf_ref[...] = (da * uf_ref[...].astype(F32) * (sg * (1.0 + gf * (1.0 - sg)))).astype(MXU)

    wide = pl.BlockSpec((tm, FFN), lambda i: (i, 0))
    return _pc(
        body, name="ffn_down_bwd", grid=(s // tm,),
        in_specs=[pl.BlockSpec((tm, D), lambda i: (i, 0)), pl.BlockSpec((FFN, D), lambda i: (0, 0)), wide, wide],
        out_specs=[wide, wide], out_shape=[_sds((s, FFN), MXU)] * 2, compiler_params=_cp(1),
    )(dffn, w_d, gf, uf)


def _ffn_up_bwd(dgf, duf, w_gu, x1, dy, z, g_fpre, g_post):
    s = x1.shape[0]
    tm = min(TM_ROW, s)

    def body(dgf_ref, duf_ref, w_ref, x1_ref, dy_ref, z_ref, gf_ref, gp_ref, dx1_ref, dz_ref, vec_ref):
        @pl.when(pl.program_id(0) == 0)
        def _():
            vec_ref[...] = jnp.zeros_like(vec_ref)

        dh2 = _dot_nt(dgf_ref[...], w_ref[:, pl.ds(0, FFN)]) + _dot_nt(duf_ref[...], w_ref[:, pl.ds(FFN, FFN)])
        x1 = x1_ref[...]
        r2 = _rms(x1)
        vec_ref[0:1, :] += jnp.sum(dh2 * x1 * r2, axis=0, keepdims=True)
        dx1 = dy_ref[...] + _rms_bwd(x1, r2, gf_ref[...], dh2)
        dx1_ref[...] = dx1
        z = z_ref[...]
        rz = _rms(z)
        vec_ref[1:2, :] += jnp.sum(dx1 * z * rz, axis=0, keepdims=True)
        dz_ref[...] = _rms_bwd(z, rz, gp_ref[...], dx1).astype(MXU)

    rows = pl.BlockSpec((tm, D), lambda i: (i, 0))
    wide = pl.BlockSpec((tm, FFN), lambda i: (i, 0))
    vec = pl.BlockSpec((1, D), lambda i: (0, 0))
    return _pc(
        body, name="ffn_up_bwd", grid=(s // tm,),
        in_specs=[wide, wide, pl.BlockSpec((D, 2 * FFN), lambda i: (0, 0)), rows, rows, rows, vec, vec],
        out_specs=[rows, rows, pl.BlockSpec((8, D), lambda i: (0, 0))],
        out_shape=[_sds((s, D), F32), _sds((s, D), MXU), _sds((8, D), F32)], compiler_params=_cp(1),
    )(dgf, duf, w_gu, x1, dy, z, g_fpre, g_post)


def _merge_bwd(dz, proj, pa, pc, pm, oa, wpa, wpc, wpm, wout):
    s = dz.shape[0]
    tm = min(TM_ROW, s)

    def body(dz_ref, gl_ref, pa_ref, pc_ref, pm_ref, oa_ref, wpa_h, wpc_h, wpm_h, wout_h,
             dpa_ref, dpc_ref, dpm_ref, dgl_ref, doa_ref, doc_ref, dom_ref, dl_ref, wbuf, sems):
        _load_resident([wpa_h, wpc_h, wpm_h, wout_h], wbuf, sems)
        dm = _dot_nt(dz_ref[...], wbuf[3])
        quads = ((pa_ref, dpa_ref, doa_ref), (pc_ref, dpc_ref, doc_ref), (pm_ref, dpm_ref, dom_ref))
        for b, (p_ref, dp_ref, do_ref) in enumerate(quads):
            cols = pl.ds(b * D, D)
            gt = _sigmoid(gl_ref[:, cols].astype(F32))
            dp = (dm * gt).astype(MXU)
            dp_ref[...] = dp
            dgl_ref[:, cols] = (dm * p_ref[...].astype(F32) * gt * (1.0 - gt)).astype(MXU)
            dob = _dot_nt(dp, wbuf[b]).astype(MXU)
            do_ref[...] = dob
            if b == 0:
                prod = dob.astype(F32) * oa_ref[...].astype(F32)
                d_i = lax.broadcasted_iota(jnp.int32, (D, LANES), 0)
                h_i = lax.broadcasted_iota(jnp.int32, (D, LANES), 1)
                sel = jnp.where(lax.shift_right_logical(d_i, DH.bit_length() - 1) == h_i, 1.0, 0.0).astype(jnp.bfloat16)
                dl_ref[...] = _dot_exact_rhs_t(prod, sel)

    rows = pl.BlockSpec((tm, D), lambda i: (i, 0))
    anyspec = pl.BlockSpec(memory_space=pl.ANY)
    wide = pl.BlockSpec((tm, 3 * D), lambda i: (i, 2))
    return _pc(
        body, name="merge_bwd", grid=(s // tm,),
        in_specs=[rows, wide, rows, rows, rows, rows, anyspec, anyspec, anyspec, anyspec],
        out_specs=[rows, rows, rows, pl.BlockSpec((tm, 3 * D), lambda i: (i, 0)), rows, rows, rows,
                   pl.BlockSpec((tm, LANES), lambda i: (i, 0))],
        out_shape=[_sds((s, D), MXU)] * 3 + [_sds((s, 3 * D), MXU)] + [_sds((s, D), MXU)] * 3 + [_sds((s, LANES), F32)],
        scratch_shapes=_resident(4), compiler_params=_cp(1),
    )(dz, proj, pa, pc, pm, oa, wpa, wpc, wpm, wout)


def _dot_exact_rhs_t(v, b01):
    hi, mid, lo = _split3(v)
    return _dot(hi, b01) + _dot(mid, b01) + _dot(lo, b01)


def _in_proj_bwd(pieces, df, w_main, w_f, x, dx1, g_pre, plan):
    s = x.shape[0]
    tm = min(TM_ROW, s)
    n_main = w_main.shape[1]
    np_ = len(pieces)
    nx = len(plan["arrays"])
    n_in = np_ + 6

    def body(*refs):
        p_refs = refs[:np_]
        df_ref, x_ref, dx1_ref, g_ref, w_h, wf_ref = refs[np_:n_in]
        dx_ref, vec_ref = refs[n_in + nx:n_in + nx + 2]
        wbuf, sem = refs[n_in + 2 * nx + 2:n_in + 2 * nx + 4]
        _host_plan(plan, refs[n_in:n_in + nx], refs[n_in + nx + 2:n_in + 2 * nx + 2], refs[n_in + 2 * nx + 4:],
                   pl.program_id(0) == 0, None, pl.program_id(0) == s // tm - 1)

        @pl.when(pl.program_id(0) == 0)
        def _():
            vec_ref[...] = jnp.zeros_like(vec_ref)
            cp = pltpu.make_async_copy(w_h, wbuf, sem)
            cp.start()
            cp.wait()

        dh = _dot_nt(df_ref[...], wf_ref[...])
        for p_ref, (_, c0, nc) in zip(p_refs, pieces):
            dh = dh + _dot_nt(p_ref[...], wbuf[:, pl.ds(c0 * D, nc * D)])
        xv = x_ref[...]
        r = _rms(xv)
        vec_ref[0:1, :] += jnp.sum(dh * xv * r, axis=0, keepdims=True)
        dx_ref[...] = dx1_ref[...] + _rms_bwd(xv, r, g_ref[...], dh)

    rows = pl.BlockSpec((tm, D), lambda i: (i, 0))
    p_specs = [pl.BlockSpec((tm, nc * D), lambda i: (i, 0)) for _, _, nc in pieces]
    return _pc(
        body, name="in_proj_bwd", grid=(s // tm,),
        in_specs=p_specs + [pl.BlockSpec((tm, LANES), lambda i: (i, 0)), rows, rows, pl.BlockSpec((1, D), lambda i: (0, 0)),
                            pl.BlockSpec(memory_space=pl.ANY), pl.BlockSpec((D, LANES), lambda i: (0, 0))] + [ANY] * nx,
        out_specs=[rows, pl.BlockSpec((8, D), lambda i: (0, 0))] + [ANY] * nx,
        out_shape=[_sds((s, D), F32), _sds((8, D), F32)] + plan["out_shape"],
        scratch_shapes=[pltpu.VMEM((D, n_main), MXU), pltpu.SemaphoreType.DMA] + plan["scratch"], compiler_params=_cp(1),
    )(*[p for p, _, _ in pieces], df, x, dx1, g_pre, w_main, w_f, *plan["arrays"])


def _wgrad(xa, dy, name):
    s, k = xa.shape
    n = dy.shape[1]
    ts = min(TS_WG, s)
    tk = _tile(k, WG_CAP)
    tn = _tile(n, WG_CAP)

    def body(x_ref, dy_ref, o_ref, acc):
        @pl.when(pl.program_id(2) == 0)
        def _():
            acc[...] = jnp.zeros_like(acc)

        acc[...] += _dot_tn(x_ref[...], dy_ref[...])

        @pl.when(pl.program_id(2) == s // ts - 1)
        def _():
            o_ref[...] = acc[...].astype(WIRE)

    return _pc(
        body, name=name, grid=(k // tk, n // tn, s // ts),
        in_specs=[pl.BlockSpec((ts, tk), lambda a, b, c: (c, a)), pl.BlockSpec((ts, tn), lambda a, b, c: (c, b))],
        out_specs=pl.BlockSpec((tk, tn), lambda a, b, c: (a, b)), out_shape=_sds((k, n), WIRE),
        scratch_shapes=[pltpu.VMEM((tk, tn), F32)], compiler_params=_cp(3),
    )(xa, dy)


def _pair_sum(g, r1, c_idx):
    _, _, hr, cols = g.shape
    tr = _rowtile(hr, cols)

    def body(c_ref, g_ref, r_ref, o_ref):
        o_ref[0] = (g_ref[0, 0].astype(F32) + r_ref[0].astype(F32)).astype(WIRE)

    return _pc(
        body, name="pair_sum_%dx%d" % (hr, cols), out_shape=_sds((N_CHIPS, hr, cols), WIRE),
        grid_spec=pltpu.PrefetchScalarGridSpec(
            num_scalar_prefetch=1, grid=(N_CHIPS, hr // tr),
            in_specs=[pl.BlockSpec((1, 1, tr, cols), lambda d, i, c: (d, c[0], i, 0)),
                      pl.BlockSpec((1, tr, cols), lambda d, i, c: (d, i, 0))],
            out_specs=pl.BlockSpec((1, tr, cols), lambda d, i, c: (d, i, 0))),
        compiler_params=_cp(2),
    )(c_idx, g, r1)


def _chip_sum(r2, slot, base=None):
    _, hr, cols = r2.shape
    tr = _rowtile(hr, cols)

    def body(s_ref, r_ref, *rest):
        o_ref = rest[-1]
        acc = r_ref[0].astype(F32)
        for d in range(1, N_CHIPS):
            acc = acc + r_ref[d].astype(F32)
        o_ref[0] = acc

    based = base is not None
    return _pc(
        body, name="chip_sum_%dx%d_%d" % (hr, cols, int(based)), out_shape=_sds((2, hr, cols), F32),
        grid_spec=pltpu.PrefetchScalarGridSpec(
            num_scalar_prefetch=1, grid=(hr // tr,),
            in_specs=[pl.BlockSpec((N_CHIPS, tr, cols), lambda i, s: (0, i, 0))] + ([ANY] if based else []),
            out_specs=pl.BlockSpec((1, tr, cols), lambda i, s: (s[0], i, 0))),
        input_output_aliases={2: 0} if based else {}, compiler_params=_cp(1),
    )(*((slot, r2, base) if based else (slot, r2)))


def _adamw(w, g, m, v):
    rows, cols = w.shape
    tr = _rowtile(rows, cols)
    c1 = 1.0 / (1.0 - B1 ** STEP)
    c2 = 1.0 / (1.0 - B2 ** STEP)

    def body(w_ref, g_ref, m_ref, v_ref, d_ref, mo_ref, vo_ref):
        gv = g_ref[...]
        mn = B1 * m_ref[...] + (1.0 - B1) * gv
        vn = B2 * v_ref[...] + (1.0 - B2) * (gv * gv)
        mo_ref[...] = mn
        vo_ref[...] = vn
        d_ref[...] = -LR * ((mn * c1) / (jnp.sqrt(vn * c2) + ADAM_EPS) + WD * w_ref[...])

    blk = pl.BlockSpec((tr, cols), lambda i: (i, 0))
    return _pc(
        body, name="adamw_%dx%d" % (rows, cols), grid=(rows // tr,), in_specs=[blk] * 4, out_specs=[blk] * 3,
        out_shape=[_sds((rows, cols), F32)] * 3, compiler_params=_cp(1),
    )(w, g, m, v)


MESH_ID = pl.DeviceIdType.MESH
ANY = pl.BlockSpec(memory_space=pl.ANY)


def _place():
    x, y, c = lax.axis_index("x"), lax.axis_index("y"), lax.axis_index("c")
    others = [(1 - x, y), (x, 1 - y), (1 - x, 1 - y)]
    return x, y, c, others


def _remote(src, dst, sems, idx, to):
    return pltpu.make_async_remote_copy(src_ref=src, dst_ref=dst, send_sem=sems[0].at[idx], recv_sem=sems[1].at[idx],
                                        device_id=to, device_id_type=MESH_ID)


def _gather_plan(shards):
    nk = len(shards)

    def copies(ins, outs, sems):
        x, y, c, others = _place()
        me = 2 * x + y
        sib = (x, y, 1 - c)
        local = [pltpu.make_async_copy(ins[k], outs[k].at[me], sems[2].at[k]) for k in range(nk)]
        ici, landed, fwd, fwd_landed = [], [], [], []
        for k in range(nk):
            hr = shards[k].shape[0] // 2
            for r, (cx, cy) in enumerate(others):
                mine = pl.ds(c * hr, hr)
                ici.append(_remote(ins[k].at[mine], outs[k].at[me, mine], sems, 6 * k + r, (cx, cy, c)))
                got = outs[k].at[2 * cx + cy, mine]
                landed.append(_remote(got, got, sems, 6 * k + r, (cx, cy, c)))
                fwd.append(_remote(got, got, sems, 6 * k + 3 + r, sib))
                theirs = outs[k].at[2 * cx + cy, pl.ds((1 - c) * hr, hr)]
                fwd_landed.append(_remote(theirs, theirs, sems, 6 * k + 3 + r, sib))
        return local, ici, landed, fwd, fwd_landed

    def start(ins, outs, sems):
        local, ici, _, _, _ = copies(ins, outs, sems)
        for cp in local + ici:
            cp.start()

    def forward(ins, outs, sems):
        _, _, landed, fwd, _ = copies(ins, outs, sems)
        for got, cp in zip(landed, fwd):
            got.wait_recv()
            cp.start()

    def finish(ins, outs, sems):
        local, ici, _, fwd, fwd_landed = copies(ins, outs, sems)
        for got in fwd_landed:
            got.wait_recv()
        for cp in ici + fwd:
            cp.wait_send()
        for cp in local:
            cp.wait()

    return dict(
        arrays=list(shards), out_shape=[_sds((N_CHIPS,) + a.shape, a.dtype) for a in shards],
        scratch=[pltpu.SemaphoreType.DMA((6 * nk,)), pltpu.SemaphoreType.DMA((6 * nk,)), pltpu.SemaphoreType.DMA((nk,))],
        phases=[start, forward, finish])


def _scatter_plan(ps):
    nk = len(ps)

    def copies(ins, outs, sems):
        x, y, c, others = _place()
        me = 2 * x + y
        local = [pltpu.make_async_copy(ins[k].at[me], outs[k].at[me], sems[2].at[k]) for k in range(nk)]
        ici, landed = [], []
        for k in range(nk):
            for r, (cx, cy) in enumerate(others):
                ici.append(_remote(ins[k].at[2 * cx + cy], outs[k].at[me], sems, 3 * k + r, (cx, cy, c)))
                got = outs[k].at[2 * cx + cy]
                landed.append(_remote(got, got, sems, 3 * k + r, (cx, cy, c)))
        return local, ici, landed

    def start(ins, outs, sems):
        local, ici, _ = copies(ins, outs, sems)
        for cp in local + ici:
            cp.start()

    def finish(ins, outs, sems):
        local, ici, landed = copies(ins, outs, sems)
        for got in landed:
            got.wait_recv()
        for cp in ici:
            cp.wait_send()
        for cp in local:
            cp.wait()

    return dict(
        arrays=list(ps), out_shape=[_sds(a.shape, a.dtype) for a in ps],
        scratch=[pltpu.SemaphoreType.DMA((3 * nk,)), pltpu.SemaphoreType.DMA((3 * nk,)), pltpu.SemaphoreType.DMA((nk,))],
        phases=[start, finish])


def _run_plan(plan, name):
    nk = len(plan["arrays"])

    def body(*refs):
        ins, outs, sems = refs[:nk], refs[nk:2 * nk], refs[2 * nk:]
        for phase in plan["phases"]:
            phase(ins, outs, sems)

    return _pc(body, name=name, in_specs=[ANY] * nk, out_specs=[ANY] * nk, out_shape=plan["out_shape"],
               scratch_shapes=plan["scratch"])(*plan["arrays"])


def _host_plan(plan, ins, outs, sems, first, middle, last):
    points = [first, last] if len(plan["phases"]) == 2 else [first, middle, last]
    for phase, at in zip(plan["phases"], points):
        pl.when(at)(functools.partial(phase, ins, outs, sems))


def _swap_sibling(gs, halves, tag):
    nk = len(gs)

    def body(*refs):
        ins, outs = refs[:nk], refs[nk:2 * nk]
        send_sems, recv_sems = refs[2 * nk:]
        x, y, c, _ = _place()
        cps = []
        for k in range(nk):
            hr = gs[k].shape[1] // 2
            cp = pltpu.make_async_remote_copy(
                src_ref=ins[k].at[:, pl.ds((1 - c) * hr, hr)] if halves else ins[k], dst_ref=outs[k],
                send_sem=send_sems.at[k], recv_sem=recv_sems.at[k], device_id=(x, y, 1 - c), device_id_type=MESH_ID)
            cp.start()
            cps.append(cp)
        for cp in cps:
            cp.wait()

    return _pc(
        body, name=("swap_halves_" if halves else "swap_slabs_") + tag, in_specs=[ANY] * nk, out_specs=[ANY] * nk,
        out_shape=[_sds((N_CHIPS, a.shape[1] // 2 if halves else a.shape[1], a.shape[2]), a.dtype) for a in gs],
        scratch_shapes=[pltpu.SemaphoreType.DMA((nk,)), pltpu.SemaphoreType.DMA((nk,))],
    )(*gs)


def _allreduce_small(v):
    rows, cols = v.shape

    def body(v_ref, o_ref, gath, send_sems, recv_sems):
        x, y, c, others = _place()
        sib = (x, y, 1 - c)

        def slot(px, py, pc):
            return gath.at[4 * px + 2 * py + pc]

        def copy(k, block, to, src=None):
            return pltpu.make_async_remote_copy(
                src_ref=slot(*block) if src is None else src, dst_ref=slot(*block),
                send_sem=send_sems.at[k], recv_sem=recv_sems.at[k], device_id=to, device_id_type=MESH_ID)

        me = (x, y, c)
        gath[4 * x + 2 * y + c] = v_ref[...]
        first = [copy(0, me, sib, src=v_ref)]
        first += [copy(1 + r, me, (cx, cy, c), src=v_ref) for r, (cx, cy) in enumerate(others)]
        for cp in first:
            cp.start()
        passed = [copy(4 + r, (cx, cy, c), sib) for r, (cx, cy) in enumerate(others)]
        for r, (cx, cy) in enumerate(others):
            copy(1 + r, (cx, cy, c), me).wait_recv()
            passed[r].start()
        copy(0, (x, y, 1 - c), me).wait_recv()
        for r, (cx, cy) in enumerate(others):
            copy(4 + r, (cx, cy, 1 - c), me).wait_recv()
        for cp in first + passed:
            cp.wait_send()
        acc = gath[0]
        for d in range(1, N_DEV):
            acc = acc + gath[d]
        o_ref[...] = acc

    vm = pl.BlockSpec(memory_space=pltpu.VMEM)
    return _pc(
        body, name="allreduce_small", in_specs=[vm], out_specs=vm, out_shape=_sds((rows, cols), F32),
        scratch_shapes=[pltpu.VMEM((N_DEV, rows, cols), F32), pltpu.SemaphoreType.DMA((7,)), pltpu.SemaphoreType.DMA((7,))],
    )(v)


def _cols_to_chips(a):
    r, c4 = a.shape
    return a.reshape(r, N_CHIPS, c4 // N_CHIPS).transpose(1, 0, 2)


def _chips_to_cols(a):
    n, r, c = a.shape
    return a.transpose(1, 0, 2).reshape(r, n * c)


def _head_rows(a, tq):
    s = a.shape[0]
    return a[:, :HF].reshape(s // tq, tq, NP, 2).transpose(2, 0, 3, 1)


def _head_cols(a):
    s = a.shape[0]
    return a[:, :HF].reshape(s, NP, 2).transpose(1, 0, 2)


def _to_wire(gs, c_idx, tag):
    gs = [g.astype(WIRE) for g in gs]
    r1 = _swap_sibling(gs, True, tag)
    return [_pair_sum(g.reshape(N_CHIPS, 2, g.shape[1] // 2, g.shape[2]), r, c_idx) for g, r in zip(gs, r1)]


def _local_step(x, mem, tgt, sp, w_main, w_f, rest_shards, c_idx):
    s = x.shape[0]
    tq = min(TQ, s)
    b_f = jnp.pad(sp["b_forget"], ((0, 0), (0, LANES - HF)))
    proj, h, flog = _in_proj(x, sp["norm_mix_pre"], w_main, w_f)
    cf = _logf_cumsum(flog, b_f)
    qx, kx = _fox_prep(proj, _head_cols(cf))
    oa, lse, g_cw, g_kv, g_pj, g_gu, g_d = _fox_fwd(proj, qx, kx, _gather_plan(rest_shards))
    pj = g_pj.reshape(N_CHIPS, 4, D // N_CHIPS, D).transpose(1, 0, 2, 3).reshape(4, D, D)
    w = {"conv_w": _chips_to_cols(g_cw), "w_kv": _chips_to_cols(g_kv), "wpa": pj[0], "wpc": pj[1], "wpm": pj[2],
         "wout": pj[3], "w_gu": _chips_to_cols(g_gu), "w_d": g_d.reshape(FFN, D)}
    y, oc = _conv_fwd(proj, w["conv_w"], sp["conv_b"], sp["conv_ln_g"], sp["conv_ln_b"])
    mem_n, kv = _mem_kv(mem, sp["norm_mem"], w["w_kv"])
    om = _mem_attn_fwd(proj, kv)
    pa, pc, pm, merged, z, x1, h2 = _merge_out(oa, oc, om, proj, x, w["wpa"], w["wpc"], w["wpm"], w["wout"],
                                              sp["norm_mix_post"], sp["norm_ffn_pre"])
    gf, uf, act = _ffn_up(h2, w["w_gu"])
    dffn, dy, vec_f, loss_blk = _ffn_down_loss(act, w["w_d"], x1, tgt, sp["norm_ffn_post"])

    dgf, duf = _ffn_down_bwd(dffn, w["w_d"], gf, uf)
    dx1, dz, vec_n = _ffn_up_bwd(dgf, duf, w["w_gu"], x1, dy, z, sp["norm_ffn_pre"], sp["norm_mix_post"])
    dpa, dpc, dpm, dgl, doa, doc, dom, delta = _merge_bwd(dz, proj, pa, pc, pm, oa, w["wpa"], w["wpc"], w["wpm"], w["wout"])
    pj_g = jnp.stack([_wgrad(oa, dpa, "wgrad_pa"), _wgrad(oc, dpc, "wgrad_pc"), _wgrad(om, dpm, "wgrad_pm"),
                      _wgrad(merged, dz, "wgrad_out")]).reshape(4, N_CHIPS, D // N_CHIPS, D)
    early = [pj_g.transpose(1, 0, 2, 3).reshape(N_CHIPS, D, D),
             _cols_to_chips(jnp.concatenate([_wgrad(h2, dgf, "wgrad_g"), _wgrad(h2, duf, "wgrad_u")], axis=1)),
             _wgrad(act, dffn, "wgrad_d").reshape(N_CHIPS, FFN // N_CHIPS, D)]
    lse16 = lse.transpose(1, 0, 2).reshape(s, HF)
    dq, dqs, dk, dks, dv, *r2_early = _fox_bwd(proj, qx, kx, doa, _head_rows(lse16, tq), _head_rows(delta, tq),
                                              _scatter_plan(_to_wire(early, c_idx, "early")))
    over_keys = dqs[:, X_KONE:X_KONE + 2, :].transpose(2, 0, 1).reshape(s, HF)
    over_queries = dks[:, :, X_QONE:X_QONE + 2, :].transpose(1, 3, 0, 2).reshape(s, HF)
    dc = jnp.pad(over_keys - over_queries, ((0, 0), (0, LANES - HF)))
    df, db_blk = _logf_cumsum_bwd(flog, b_f, dc)
    dga, dgg, dcw, vec_c = _conv_bwd(proj, y, doc, w["conv_w"], sp["conv_ln_g"], sp["conv_ln_b"])
    dqm, dkv = _mem_attn_bwd(proj, kv, dom)
    dkv_b = dkv.astype(MXU)
    vec_m = _mem_kv_bwd(mem, sp["norm_mem"], w["w_kv"], dkv_b)
    pieces = [(dq, 0, 1), (dk, 1, 1), (dv, 2, 1), (dga, 3, 1), (dgg, 4, 1), (dqm, 5, 1), (dgl, 6, 3)]
    dw_cols = [_wgrad(h, p, "wgrad_in_%d" % c0) for p, c0, _ in pieces]
    dwf = _wgrad(h, df, "wgrad_in_f")
    late = [_cols_to_chips(jnp.concatenate(dw_cols[:3] + [dwf[:, :HF]] + dw_cols[3:], axis=1)), _cols_to_chips(dcw),
            _cols_to_chips(_wgrad(mem_n, dkv_b, "wgrad_kv"))]
    dx, vec_p, *r2_late = _in_proj_bwd(pieces, df, w_main, w_f, x, dx1, sp["norm_mix_pre"],
                                       _scatter_plan(_to_wire(late, c_idx, "late")))
    zero_row = jnp.zeros((1, D), F32)
    small = jnp.concatenate([
        vec_p[0:1], vec_n[1:2], vec_m[0:1], vec_c[2:3], vec_c[0:1], vec_c[1:2], vec_n[0:1], vec_f[0:1],
        jnp.pad(db_blk[0:1, :HF], ((0, 0), (0, D - HF))),
        jnp.pad(loss_blk[0:1, 0:1], ((0, 0), (0, D - 1))),
    ] + [zero_row] * (SMALL_ROWS - 10), axis=0)
    return dx, list(r2_late) + list(r2_early), small


SMALL_NAMES = ["norm_mix_pre", "norm_mix_post", "norm_mem", "conv_b", "conv_ln_g", "conv_ln_b", "norm_ffn_pre", "norm_ffn_post"]
PROJ_NAMES = ["w_proj_attn", "w_proj_conv", "w_proj_mem", "w_out"]
WEIGHT_ORDER = ["norm_mix_pre", "norm_mix_post", "norm_mem", "w_in", "b_forget", "conv_w", "conv_b", "conv_ln_g", "conv_ln_b",
                "w_kv_mem", "w_proj_attn", "w_proj_conv", "w_proj_mem", "w_out", "norm_ffn_pre", "norm_ffn_post",
                "w_gate_up", "w_down"]


def _pack_small(p):
    rows = [p[n] for n in SMALL_NAMES] + [jnp.pad(p["b_forget"], ((0, 0), (0, D - HF)))]
    return jnp.concatenate(rows + [jnp.zeros((SMALL_ROWS - len(rows), D), F32)], axis=0)


def _step(params, moms, vels, x, mem, tgt):
    c_idx = lax.axis_index("c").astype(jnp.int32).reshape(1)

    (g_in,) = _run_plan(_gather_plan([params["w_in"].astype(WIRE)]), "gather_w_in")
    w_in_full = _chips_to_cols(g_in)
    w_main = jnp.concatenate([w_in_full[:, :3 * D], w_in_full[:, 3 * D + HF:]], axis=1)
    w_f = jnp.pad(w_in_full[:, 3 * D:3 * D + HF], ((0, 0), (0, LANES - HF)))
    rest = [jnp.pad(params["conv_w"], ((0, CWP - CW), (0, 0))), params["w_kv_mem"].astype(WIRE),
            jnp.concatenate([params[n] for n in PROJ_NAMES], axis=0).astype(WIRE),
            params["w_gate_up"].astype(WIRE), params["w_down"].astype(WIRE)]

    dx, r2, small = _local_step(x, mem, tgt, params, w_main, w_f, rest, c_idx)

    r2_sib = _swap_sibling(r2, False, "all")
    full = [_chip_sum(theirs, 1 - c_idx, _chip_sum(mine, c_idx)) for mine, theirs in zip(r2, r2_sib)]
    red = [f.reshape(2 * f.shape[1], f.shape[2]) for f in full]
    pj_r = red[3].reshape(4, D // N_CHIPS, D)
    grads = {"w_in": red[0], "conv_w": red[1][:CW], "w_kv_mem": red[2], "w_gate_up": red[4], "w_down": red[5]}
    for i, n in enumerate(PROJ_NAMES):
        grads[n] = pj_r[i]

    tot = _allreduce_small(small)
    loss = tot[9, 0]
    for i, n in enumerate(SMALL_NAMES):
        grads[n] = tot[i:i + 1]
    grads["b_forget"] = tot[8:9, :HF]

    delta, new_m, new_v = {}, {}, {}
    ds, ms, vs = _adamw(_pack_small(params), tot.at[9:].set(0.0), _pack_small(moms), _pack_small(vels))
    for i, n in enumerate(SMALL_NAMES):
        delta[n], new_m[n], new_v[n] = ds[i:i + 1], ms[i:i + 1], vs[i:i + 1]
    delta["b_forget"], new_m["b_forget"], new_v["b_forget"] = ds[8:9, :HF], ms[8:9, :HF], vs[8:9, :HF]
    for n in ["w_in", "conv_w", "w_kv_mem", "w_gate_up", "w_down"] + PROJ_NAMES:
        delta[n], new_m[n], new_v[n] = _adamw(params[n], grads[n], moms[n], vels[n])
    return loss, dx, grads, delta, new_m, new_v


def kernel(x, mem, norm_mix_pre, norm_mix_post, norm_mem, w_in, b_forget, conv_w, conv_b, conv_ln_g, conv_ln_b, w_kv_mem, w_proj_attn, w_proj_conv, w_proj_mem, w_out, norm_ffn_pre, norm_ffn_post, w_gate_up, w_down, loss_target, m_norm_mix_pre, m_norm_mix_post, m_norm_mem, m_w_in, m_b_forget, m_conv_w, m_conv_b, m_conv_ln_g, m_conv_ln_b, m_w_kv_mem, m_w_proj_attn, m_w_proj_conv, m_w_proj_mem, m_w_out, m_norm_ffn_pre, m_norm_ffn_post, m_w_gate_up, m_w_down, v_norm_mix_pre, v_norm_mix_post, v_norm_mem, v_w_in, v_b_forget, v_conv_w, v_conv_b, v_conv_ln_g, v_conv_ln_b, v_w_kv_mem, v_w_proj_attn, v_w_proj_conv, v_w_proj_mem, v_w_out, v_norm_ffn_pre, v_norm_ffn_post, v_w_gate_up, v_w_down):
    local = dict(locals())
    lead = {n: local[n].shape[:-2] for n in WEIGHT_ORDER}
    two_d = lambda a: a.reshape(a.shape[-2:])
    params = {n: two_d(local[n]) for n in WEIGHT_ORDER}
    moms = {n: two_d(local["m_" + n]) for n in WEIGHT_ORDER}
    vels = {n: two_d(local["v_" + n]) for n in WEIGHT_ORDER}
    loss, dx, grads, delta, new_m, new_v = _step(params, moms, vels, two_d(x), two_d(mem), two_d(loss_target))
    outs = [loss, dx.reshape(x.shape)]
    for group in (grads, delta, new_m, new_v):
        outs += [group[n].reshape(lead[n] + group[n].shape) for n in WEIGHT_ORDER]
    return tuple(outs)
```

```python
import functools

import jax
import jax.numpy as jnp
from jax import lax
from jax.experimental import pallas as pl
from jax.experimental.pallas import tpu as pltpu

F32 = jnp.float32
MXU = jnp.bfloat16
WIRE = jnp.bfloat16

D = 1024
HF = 16
DH = 64
NP = D // 128
MEM_H = 4
MEM_DH = D // MEM_H
FFN = 2816
CW = 31
CWP = 32
HALO = 32
RMS_EPS = 1e-6
LN_EPS = 1e-5
LR, B1, B2, ADAM_EPS, WD, STEP = 0.001, 0.9, 0.999, 1e-8, 0.01, 10

N_CHIPS = 4
N_DEV = 8
LANES = 128
VMEM_LIMIT = 56 * 1024 * 1024

TM_PROJ = 512
NB_PROJ = 3
TQ = 1024
LOG2E = 1.4426950408889634
LN2 = 0.6931471805599453
QSCALE = DH ** -0.5 * LOG2E
X_BIAS = 0
X_QONE = 6
X_KONE = 8
X_ROWS = 16
TM_CONV = 256
CONV_ROWS = 128
SUB = 8
TM_ROW = 256
TM_WIDE = 4096
TS_WG = 2048
WG_CAP = 1408
SMALL_ROWS = 16


def _pc(body, **kw):
    return pl.pallas_call(body, **kw)


def _cp(n_axes):
    return pltpu.CompilerParams(dimension_semantics=("arbitrary",) * n_axes, vmem_limit_bytes=VMEM_LIMIT)


def _sds(shape, dtype):
    return jax.ShapeDtypeStruct(shape, dtype)


def _dot(a, b):
    return jnp.dot(a, b, preferred_element_type=F32)


def _dot_nt(a, b):
    return lax.dot_general(a, b, (((1,), (1,)), ((), ())), preferred_element_type=F32)


def _dot_tn(a, b):
    return lax.dot_general(a, b, (((0,), (0,)), ((), ())), preferred_element_type=F32)


def _rms(u):
    return lax.rsqrt(jnp.mean(u * u, axis=-1, keepdims=True) + RMS_EPS)


def _rms_bwd(u, r, g, dn):
    w = dn * g
    return r * w - u * (r * r * r) * jnp.mean(u * w, axis=-1, keepdims=True)


def _sigmoid(z):
    return 1.0 / (1.0 + jnp.exp(-z))


def _tile(n, cap):
    if n <= cap:
        return n
    best = None
    for t in range(LANES, cap + 1, LANES):
        if n % t == 0:
            best = t
    assert best is not None, (n, cap)
    return best


def _rowtile(rows, cols, cap_bytes=1 << 20):
    best = None
    for t in range(8, rows + 1, 8):
        if rows % t == 0 and t * cols * 4 <= cap_bytes:
            best = t
    return best if best is not None else rows


def _split3(v):
    hi = v.astype(jnp.bfloat16)
    r1 = v - hi.astype(F32)
    mid = r1.astype(jnp.bfloat16)
    lo = (r1 - mid.astype(F32)).astype(jnp.bfloat16)
    return hi, mid, lo


def _dot_exact_rhs(a01, v):
    hi, mid, lo = _split3(v)
    return _dot(a01, hi) + _dot(a01, mid) + _dot(a01, lo)


def _in_proj(x, g_pre, w_main, w_f):
    s, d = x.shape
    n = w_main.shape[1]
    tm = min(TM_PROJ, s)
    tn = n // NB_PROJ

    def body(x_ref, g_ref, w_ref, wf_ref, proj_ref, h_ref, flog_ref, hs):
        @pl.when(pl.program_id(1) == 0)
        def _():
            xv = x_ref[...]
            h = (xv * _rms(xv) * g_ref[...]).astype(MXU)
            hs[...] = h
            h_ref[...] = h
            flog_ref[...] = _dot(h, wf_ref[...])

        res = _dot(hs[...], w_ref[...])

        @pl.when(pl.program_id(1) == 0)
        def _():
            proj_ref[:, pl.ds(0, d)] = (res[:, :d] * QSCALE).astype(MXU)
            proj_ref[:, pl.ds(d, tn - d)] = res[:, d:].astype(MXU)

        @pl.when(pl.program_id(1) != 0)
        def _():
            proj_ref[...] = res.astype(MXU)

    assert tn >= d
    return _pc(
        body, name="in_proj", grid=(s // tm, NB_PROJ),
        in_specs=[pl.BlockSpec((tm, d), lambda i, j: (i, 0)), pl.BlockSpec((1, d), lambda i, j: (0, 0)),
                  pl.BlockSpec((d, tn), lambda i, j: (0, j)), pl.BlockSpec((d, LANES), lambda i, j: (0, 0))],
        out_specs=[pl.BlockSpec((tm, tn), lambda i, j: (i, j)), pl.BlockSpec((tm, d), lambda i, j: (i, 0)),
                   pl.BlockSpec((tm, LANES), lambda i, j: (i, 0))],
        out_shape=[_sds((s, n), MXU), _sds((s, d), MXU), _sds((s, LANES), F32)],
        scratch_shapes=[pltpu.VMEM((tm, d), MXU)], compiler_params=_cp(2),
    )(x, g_pre, w_main, w_f)


def _log_sigmoid(z):
    e = jnp.exp(-jnp.abs(z))
    log1p_e = jnp.where(e < 1e-3, e * (1.0 - 0.5 * e), jnp.log(1.0 + e))
    return jnp.minimum(z, 0.0) - log1p_e


def _logf_cumsum(flog, b_f):
    s = flog.shape[0]
    ch = LANES

    def body(f_ref, b_ref, c_ref):
        r = lax.broadcasted_iota(jnp.int32, (ch, ch), 0)
        q = lax.broadcasted_iota(jnp.int32, (ch, ch), 1)
        tri = jnp.where(r >= q, 1.0, 0.0).astype(jnp.bfloat16)

        def step(i, carry):
            rows = pl.ds(pl.multiple_of(i * ch, ch), ch)
            lf = _log_sigmoid(f_ref[rows, :] + b_ref[...])
            c_ref[rows, :] = _dot_exact_rhs(tri, lf) + carry
            return carry + jnp.sum(lf, axis=0, keepdims=True)

        lax.fori_loop(0, s // ch, step, jnp.zeros((1, LANES), F32))

    return _pc(body, name="logf_cumsum", out_shape=_sds((s, LANES), F32),
               compiler_params=pltpu.CompilerParams(vmem_limit_bytes=VMEM_LIMIT))(flog, b_f)


def _logf_cumsum_bwd(flog, b_f, dc):
    s = flog.shape[0]
    ch = LANES

    def body(f_ref, b_ref, dc_ref, df_ref, db_ref):
        r = lax.broadcasted_iota(jnp.int32, (ch, ch), 0)
        q = lax.broadcasted_iota(jnp.int32, (ch, ch), 1)
        tri = jnp.where(r <= q, 1.0, 0.0).astype(jnp.bfloat16)
        nch = s // ch

        def step(t, carry):
            tail, dbsum = carry
            i = nch - 1 - t
            rows = pl.ds(pl.multiple_of(i * ch, ch), ch)
            dcv = dc_ref[rows, :]
            dlf = _dot_exact_rhs(tri, dcv) + tail
            z = f_ref[rows, :] + b_ref[...]
            df = dlf * _sigmoid(-z)
            df_ref[rows, :] = df.astype(MXU)
            return tail + jnp.sum(dcv, axis=0, keepdims=True), dbsum + jnp.sum(df, axis=0, keepdims=True)

        zero = jnp.zeros((1, LANES), F32)
        _, dbsum = lax.fori_loop(0, nch, step, (zero, zero))
        db_ref[...] = jnp.broadcast_to(dbsum, db_ref.shape)

    return _pc(body, name="logf_cumsum_bwd", out_shape=[_sds((s, LANES), MXU), _sds((8, LANES), F32)],
               compiler_params=pltpu.CompilerParams(vmem_limit_bytes=VMEM_LIMIT))(flog, b_f, dc)


def _head_masks(rows):
    lane = lax.broadcasted_iota(jnp.int32, (rows, LANES), 1)
    return lane < DH, lane >= DH


def _ext_masks(rows, key_side):
    lane = lax.broadcasted_iota(jnp.int32, (rows, 2 * LANES), 1)
    ext = lane - LANES
    out = []
    for a in range(2):
        head = (lane >= a * DH) & (lane < (a + 1) * DH)
        bias = (ext >= X_BIAS + 3 * a) & (ext < X_BIAS + 3 * a + 3)
        one = ext == (X_KONE if key_side else X_QONE) + a
        out.append(head | bias | one)
    return out


def _fox_prep(proj, ccol):
    s = proj.shape[0]
    tm = min(TM_WIDE, s)

    def body(q_ref, k_ref, c_ref, qx_ref, kx_ref):
        lane = lax.broadcasted_iota(jnp.int32, (tm, LANES), 1)
        qx_ref[:, pl.ds(0, LANES)] = q_ref[...]
        qx_ref[:, pl.ds(LANES, LANES)] = jnp.where(lane < X_QONE + 2, 1.0, 0.0).astype(MXU)
        kext = jnp.where((lane >= X_KONE) & (lane < X_KONE + 2), 1.0, 0.0).astype(jnp.bfloat16)
        for a in range(2):
            terms = _split3(c_ref[0, :, a:a + 1] * (-LOG2E))
            for t, term in enumerate(terms):
                kext = jnp.where(lane == X_BIAS + 3 * a + t, term, kext)
        kx_ref[:, pl.ds(0, LANES)] = k_ref[...]
        kx_ref[:, pl.ds(LANES, LANES)] = kext.astype(MXU)

    wide = pl.BlockSpec((tm, 2 * LANES), lambda p, i: (i, p))
    return _pc(
        body, name="fox_prep", grid=(NP, s // tm),
        in_specs=[pl.BlockSpec((tm, LANES), lambda p, i: (i, p)), pl.BlockSpec((tm, LANES), lambda p, i: (i, NP + p)),
                  pl.BlockSpec((1, tm, 2), lambda p, i: (p, i, 0))],
        out_specs=[wide, wide], out_shape=[_sds((s, NP * 2 * LANES), MXU)] * 2, compiler_params=_cp(2),
    )(proj, proj, ccol)


def _fox_fwd(proj, qx, kx, plan):
    s = proj.shape[0]
    tq = min(TQ, s)
    nq = s // tq
    nx = len(plan["arrays"])

    def body(*refs):
        q_ref, k_ref, v_ref = refs[:3]
        o_ref, lse_ref = refs[3 + nx:5 + nx]
        p_id, i = pl.program_id(0), pl.program_id(1)
        _host_plan(plan, refs[3:3 + nx], refs[5 + nx:5 + 2 * nx], refs[5 + 2 * nx:], (p_id == 0) & (i == 0),
                   (p_id == NP // 2) & (i == 0), (p_id == NP - 1) & (i == nq - 1))
        qv = q_ref[...]
        qmask = _ext_masks(tq, False)
        hmask = _head_masks(tq)
        qas = [jnp.where(qmask[a], qv, jnp.zeros_like(qv)) for a in range(2)]
        row = lax.broadcasted_iota(jnp.int32, (tq, tq), 0)
        col = lax.broadcasted_iota(jnp.int32, (tq, tq), 1)

        def blk(j, carry, diag=False):
            rows = pl.ds(pl.multiple_of(j * tq, tq), tq)
            kj = k_ref[rows, :]
            vj = v_ref[rows, :]
            out = []
            for a in range(2):
                m, acc = carry[a]
                sc = _dot_nt(qas[a], kj)
                if diag:
                    sc = jnp.where(row >= col, sc, -jnp.inf)
                m_new = jnp.maximum(m, jnp.max(sc, axis=-1, keepdims=True))
                p = jnp.exp2(sc - m_new)
                va = jnp.where(hmask[a], vj, jnp.ones_like(vj))
                out.append((m_new, jnp.exp2(m - m_new) * acc + _dot(p.astype(MXU), va)))
            return tuple(out)

        init = (jnp.full((tq, 1), -jnp.inf, F32), jnp.zeros((tq, LANES), F32))
        res = blk(i, lax.fori_loop(0, i, blk, (init, init)), True)
        lane = lax.broadcasted_iota(jnp.int32, (tq, LANES), 1)
        outs, lses = [], []
        for a in range(2):
            m, acc = res[a]
            l = jnp.sum(jnp.where(lane == DH * (1 - a), acc, 0.0), axis=-1, keepdims=True)
            outs.append(acc / l)
            lses.append(m + jnp.log(l) * LOG2E)
        o_ref[...] = jnp.where(hmask[0], outs[0], outs[1]).astype(MXU)
        lane2 = lax.broadcasted_iota(jnp.int32, (tq, 2), 1)
        lse_ref[0] = jnp.where(lane2 == 0, lses[0], lses[1])

    return _pc(
        body, name="fox_fwd", grid=(NP, nq),
        in_specs=[pl.BlockSpec((tq, 2 * LANES), lambda p, i: (i, p)),
                  pl.BlockSpec((s, 2 * LANES), lambda p, i: (0, p)),
                  pl.BlockSpec((s, LANES), lambda p, i: (0, 2 * NP + p))] + [ANY] * nx,
        out_specs=[pl.BlockSpec((tq, LANES), lambda p, i: (i, p)),
                   pl.BlockSpec((1, tq, 2), lambda p, i: (p, i, 0))] + [ANY] * nx,
        out_shape=[_sds((s, D), MXU), _sds((NP, s, 2), F32)] + plan["out_shape"],
        scratch_shapes=plan["scratch"], compiler_params=_cp(2),
    )(qx, kx, proj, *plan["arrays"])


def _fox_bwd(proj, qx, kx, do, lse_row, delta_row, plan):
    s = proj.shape[0]
    tq = min(TQ, s)
    nq = s // tq
    nx = len(plan["arrays"])

    def body(*refs):
        k_ref, v_ref, q_ref, do_ref, lse_ref, dl_ref = refs[:6]
        dq_ref, dqs_ref, dk_ref, dks_ref, dv_ref = refs[6 + nx:11 + nx]
        dq_acc = refs[11 + 2 * nx]
        p_id, j = pl.program_id(0), pl.program_id(1)
        _host_plan(plan, refs[6:6 + nx], refs[11 + nx:11 + 2 * nx], refs[12 + 2 * nx:], (p_id == 0) & (j == 0),
                   (p_id == NP // 2) & (j == 0), (p_id == NP - 1) & (j == nq - 1))

        @pl.when(j == 0)
        def _():
            dq_acc[...] = jnp.zeros_like(dq_acc)

        kv = k_ref[...]
        v2 = v_ref[...]
        kmask = _ext_masks(tq, True)
        qmask = _ext_masks(tq, False)
        hmask = _head_masks(tq)
        row = lax.broadcasted_iota(jnp.int32, (tq, tq), 0)
        col = lax.broadcasted_iota(jnp.int32, (tq, tq), 1)
        carry = (jnp.zeros((tq, 2 * LANES), F32), jnp.zeros((tq, LANES), F32))
        for a in range(2):
            ka = jnp.where(kmask[a], kv, jnp.zeros_like(kv))
            va = jnp.where(hmask[a], v2, jnp.zeros_like(v2))

            def blk(i, carry, diag, a=a, ka=ka, va=va):
                dk_a, dv_a = carry
                rows = pl.ds(pl.multiple_of(i * tq, tq), tq)
                qi = q_ref[rows, :]
                doi = do_ref[rows, :]
                qa = jnp.where(qmask[a], qi, jnp.zeros_like(qi))
                doa = jnp.where(hmask[a], doi, jnp.zeros_like(doi))
                st = _dot_nt(ka, qi)
                if diag:
                    st = jnp.where(col >= row, st, -jnp.inf)
                pt = jnp.exp2(st - lse_ref[0, i, a:a + 1, :])
                dv_a = dv_a + _dot(pt.astype(MXU), doa)
                dpt = _dot_nt(va, doi)
                dsb = (pt * (dpt - dl_ref[0, i, a:a + 1, :])).astype(MXU)
                dk_a = dk_a + _dot(dsb, qa)
                dq_acc[rows, :] += _dot_tn(dsb, ka)
                return dk_a, dv_a

            carry = blk(j, carry, True)
            carry = lax.fori_loop(j + 1, nq, functools.partial(blk, diag=False), carry)
        dk_acc, dv_acc = carry
        dk_ref[...] = (dk_acc[:, :LANES] * LN2).astype(MXU)
        dks_ref[0, 0] = dk_acc[:, LANES:].T[:X_ROWS, :]
        dv_ref[...] = dv_acc.astype(MXU)

        @pl.when(j == nq - 1)
        def _():
            dq_ref[...] = (dq_acc[:, pl.ds(0, LANES)] * DH ** -0.5).astype(MXU)
            for t in range(nq):
                dqs_ref[0, :, pl.ds(t * tq, tq)] = dq_acc[pl.ds(t * tq, tq), pl.ds(LANES, LANES)].T[:X_ROWS, :]

    stat = pl.BlockSpec((1, nq, 2, tq), lambda p, j: (p, 0, 0, 0))
    whole = pl.BlockSpec((s, LANES), lambda p, j: (0, p))
    tile = pl.BlockSpec((tq, LANES), lambda p, j: (j, p))
    return _pc(
        body, name="fox_bwd", grid=(NP, nq),
        in_specs=[pl.BlockSpec((tq, 2 * LANES), lambda p, j: (j, p)),
                  pl.BlockSpec((tq, LANES), lambda p, j: (j, 2 * NP + p)),
                  pl.BlockSpec((s, 2 * LANES), lambda p, j: (0, p)),
                  whole, stat, stat] + [ANY] * nx,
        out_specs=[whole, pl.BlockSpec((1, X_ROWS, s), lambda p, j: (p, 0, 0)), tile,
                   pl.BlockSpec((1, 1, X_ROWS, tq), lambda p, j: (p, j, 0, 0)), tile] + [ANY] * nx,
        out_shape=[_sds((s, D), MXU), _sds((NP, X_ROWS, s), F32), _sds((s, D), MXU),
                   _sds((NP, nq, X_ROWS, tq), F32), _sds((s, D), MXU)] + plan["out_shape"],
        scratch_shapes=[pltpu.VMEM((s, 2 * LANES), F32)] + plan["scratch"], compiler_params=_cp(2),
    )(kx, proj, qx, do, lse_row, delta_row, *plan["arrays"])


def _glu(a, gate):
    return a.astype(F32) * _sigmoid(gate.astype(F32))


def _store_blocked(buf, row0, val):
    for c in range(D // LANES):
        buf[0, c, pl.ds(row0, val.shape[0]), :] = val[:, c * LANES:(c + 1) * LANES]


def _fill_shifted(buf):
    n = buf.shape[2] - SUB
    for r in range(1, SUB):
        buf[r, :, pl.ds(0, n), :] = buf[0, :, pl.ds(r, n), :]


def _shifted(buf, off, rows, c):
    r = off % SUB
    return buf[r, c, pl.ds(off - r, rows), :]


def _conv_fwd(proj, cw, cb, lg, lb):
    s = proj.shape[0]
    tm = min(TM_CONV, s)
    hb = tm // HALO

    rcw = min(CONV_ROWS, tm)

    def body(a_ref, g_ref, ah_ref, gh_ref, w_ref, cb_ref, lg_ref, lb_ref, y_ref, o_ref, gsh):
        i = pl.program_id(0)
        _store_blocked(gsh, 0, jnp.where(i > 0, _glu(ah_ref[...], gh_ref[...]), 0.0))
        _store_blocked(gsh, HALO, _glu(a_ref[...], g_ref[...]))
        _fill_shifted(gsh)
        for c in range(D // LANES):
            cols = pl.ds(c * LANES, LANES)
            for rc in range(tm // rcw):
                acc = jnp.broadcast_to(cb_ref[:, cols], (rcw, LANES))
                for t in range(CW):
                    acc = acc + w_ref[t:t + 1, cols] * _shifted(gsh, HALO - (CW - 1) + t + rc * rcw, rcw, c)
                y_ref[pl.ds(rc * rcw, rcw), cols] = acc
        acc = y_ref[...]
        mu = jnp.mean(acc, axis=-1, keepdims=True)
        xc = acc - mu
        r = lax.rsqrt(jnp.mean(xc * xc, axis=-1, keepdims=True) + LN_EPS)
        nrm = xc * r * lg_ref[...] + lb_ref[...]
        o_ref[...] = (nrm * _sigmoid(nrm)).astype(MXU)

    vec = pl.BlockSpec((1, D), lambda i: (0, 0))
    return _pc(
        body, name="conv_fwd", grid=(s // tm,),
        in_specs=[pl.BlockSpec((tm, D), lambda i: (i, 3)), pl.BlockSpec((tm, D), lambda i: (i, 4)),
                  pl.BlockSpec((HALO, D), lambda i: (jnp.maximum(i * hb - 1, 0), 3)),
                  pl.BlockSpec((HALO, D), lambda i: (jnp.maximum(i * hb - 1, 0), 4)),
                  pl.BlockSpec((CWP, D), lambda i: (0, 0)), vec, vec, vec],
        out_specs=[pl.BlockSpec((tm, D), lambda i: (i, 0)), pl.BlockSpec((tm, D), lambda i: (i, 0))],
        out_shape=[_sds((s, D), F32), _sds((s, D), MXU)],
        scratch_shapes=[pltpu.VMEM((SUB, D // LANES, tm + HALO, LANES), F32)], compiler_params=_cp(1),
    )(proj, proj, proj, proj, cw, cb, lg, lb)


def _conv_bwd(proj, y, do, cw, lg, lb):
    s = proj.shape[0]
    tm = min(TM_CONV, s)
    hb = tm // HALO
    nt = s // tm
    last_hblk = s // HALO - 1

    def ln_bwd(yv, dov, lgv, lbv):
        mu = jnp.mean(yv, axis=-1, keepdims=True)
        xc = yv - mu
        r = lax.rsqrt(jnp.mean(xc * xc, axis=-1, keepdims=True) + LN_EPS)
        xh = xc * r
        nrm = xh * lgv + lbv
        sg = _sigmoid(nrm)
        dn = dov.astype(F32) * (sg * (1.0 + nrm * (1.0 - sg)))
        wv = dn * lgv
        dy = r * (wv - jnp.mean(wv, axis=-1, keepdims=True) - xh * jnp.mean(wv * xh, axis=-1, keepdims=True))
        return dy, dn, xh

    rcw = min(CONV_ROWS, tm)

    def body(a_ref, g_ref, ah_ref, gh_ref, y_ref, yn_ref, do_ref, don_ref, w_ref, lg_ref, lb_ref,
             da_ref, dg_ref, dw_ref, vec_ref, gsh, dysh, dwacc):
        i = pl.program_id(0)

        @pl.when(i == 0)
        def _():
            dwacc[...] = jnp.zeros_like(dwacc)
            vec_ref[...] = jnp.zeros_like(vec_ref)

        lgv, lbv = lg_ref[...], lb_ref[...]
        _store_blocked(gsh, 0, jnp.where(i > 0, _glu(ah_ref[...], gh_ref[...]), 0.0))
        _store_blocked(gsh, HALO, _glu(a_ref[...], g_ref[...]))
        _fill_shifted(gsh)
        dy, dn, xh = ln_bwd(y_ref[...], do_ref[...], lgv, lbv)
        dyn, _, _ = ln_bwd(yn_ref[...], don_ref[...], lgv, lbv)
        _store_blocked(dysh, 0, dy)
        _store_blocked(dysh, tm, jnp.where(i < nt - 1, dyn, 0.0))
        _fill_shifted(dysh)
        vec_ref[0:1, :] += jnp.sum(dn * xh, axis=0, keepdims=True)
        vec_ref[1:2, :] += jnp.sum(dn, axis=0, keepdims=True)
        vec_ref[2:3, :] += jnp.sum(dy, axis=0, keepdims=True)
        for c in range(D // LANES):
            cols = pl.ds(c * LANES, LANES)
            for rc in range(tm // rcw):
                rows = pl.ds(rc * rcw, rcw)
                dyc = dysh[0, c, rows, :]
                dgl = jnp.zeros((rcw, LANES), F32)
                for t in range(CW):
                    dgl = dgl + w_ref[t:t + 1, cols] * _shifted(dysh, CW - 1 - t + rc * rcw, rcw, c)
                    prod = dyc * _shifted(gsh, HALO - (CW - 1) + t + rc * rcw, rcw, c)
                    dwacc[t, :, cols] += jnp.sum(prod.reshape(rcw // SUB, SUB, LANES), axis=0)
                av = a_ref[rows, cols].astype(F32)
                sgate = _sigmoid(g_ref[rows, cols].astype(F32))
                da_ref[rows, cols] = (dgl * sgate).astype(MXU)
                dg_ref[rows, cols] = (dgl * av * sgate * (1.0 - sgate)).astype(MXU)

        @pl.when(i == nt - 1)
        def _():
            dw_ref[...] = jnp.sum(dwacc[...], axis=1)

    vec = pl.BlockSpec((1, D), lambda i: (0, 0))
    cur = lambda c: pl.BlockSpec((tm, D), lambda i: (i, c))
    prv = lambda c: pl.BlockSpec((HALO, D), lambda i: (jnp.maximum(i * hb - 1, 0), c))
    nxt = pl.BlockSpec((HALO, D), lambda i: (jnp.minimum((i + 1) * hb, last_hblk), 0))
    return _pc(
        body, name="conv_bwd", grid=(nt,),
        in_specs=[cur(3), cur(4), prv(3), prv(4), cur(0), nxt, cur(0), nxt,
                  pl.BlockSpec((CWP, D), lambda i: (0, 0)), vec, vec],
        out_specs=[cur(0), cur(0), pl.BlockSpec((CWP, D), lambda i: (0, 0)), pl.BlockSpec((8, D), lambda i: (0, 0))],
        out_shape=[_sds((s, D), MXU), _sds((s, D), MXU), _sds((CWP, D), F32), _sds((8, D), F32)],
        scratch_shapes=[pltpu.VMEM((SUB, D // LANES, tm + HALO, LANES), F32)] * 2 + [pltpu.VMEM((CWP, SUB, D), F32)],
        compiler_params=_cp(1),
    )(proj, proj, proj, proj, y, y, do, do, cw, lg, lb)


def _mem_kv(mem, g_mem, w_kv):
    mm = mem.shape[0]

    def body(m_ref, g_ref, w_ref, mn_ref, kv_ref):
        mv = m_ref[...]
        mn = (mv * _rms(mv) * g_ref[...]).astype(MXU)
        mn_ref[...] = mn
        kv_ref[...] = _dot(mn, w_ref[...]).astype(MXU)

    return _pc(body, name="mem_kv", out_shape=[_sds((mm, D), MXU), _sds((mm, 2 * D), MXU)],
               compiler_params=pltpu.CompilerParams(vmem_limit_bytes=VMEM_LIMIT))(mem, g_mem, w_kv)


def _mem_kv_bwd(mem, g_mem, w_kv, dkv):
    mm = mem.shape[0]

    def body(m_ref, w_ref, dkv_ref, o_ref):
        mv = m_ref[...]
        dmn = _dot_nt(dkv_ref[...], w_ref[...])
        o_ref[...] = jnp.broadcast_to(jnp.sum(dmn * mv * _rms(mv), axis=0, keepdims=True), o_ref.shape)

    return _pc(body, name="mem_kv_bwd", out_shape=_sds((8, D), F32),
               compiler_params=pltpu.CompilerParams(vmem_limit_bytes=VMEM_LIMIT))(mem, w_kv, dkv)


def _mem_attn_fwd(proj, kv):
    s = proj.shape[0]
    mm = kv.shape[0]
    tm = min(TM_PROJ, s)
    scale = MEM_DH ** -0.5

    def body(q_ref, kv_ref, o_ref):
        for h in range(MEM_H):
            cols = pl.ds(h * MEM_DH, MEM_DH)
            qh = q_ref[:, cols] * scale
            sc = _dot_nt(qh, kv_ref[:, cols])
            m = jnp.max(sc, axis=-1, keepdims=True)
            e = jnp.exp(sc - m)
            p = e / jnp.sum(e, axis=-1, keepdims=True)
            o_ref[:, cols] = _dot(p.astype(MXU), kv_ref[:, pl.ds(D + h * MEM_DH, MEM_DH)]).astype(MXU)

    return _pc(
        body, name="mem_attn_fwd", grid=(s // tm,),
        in_specs=[pl.BlockSpec((tm, D), lambda i: (i, 5)), pl.BlockSpec((mm, 2 * D), lambda i: (0, 0))],
        out_specs=pl.BlockSpec((tm, D), lambda i: (i, 0)), out_shape=_sds((s, D), MXU), compiler_params=_cp(1),
    )(proj, kv)


def _mem_attn_bwd(proj, kv, do):
    s = proj.shape[0]
    mm = kv.shape[0]
    tm = min(TM_PROJ, s)
    scale = MEM_DH ** -0.5

    def body(q_ref, kv_ref, do_ref, dq_ref, dkv_ref):
        @pl.when(pl.program_id(0) == 0)
        def _():
            dkv_ref[...] = jnp.zeros_like(dkv_ref)

        for h in range(MEM_H):
            cols = pl.ds(h * MEM_DH, MEM_DH)
            vcols = pl.ds(D + h * MEM_DH, MEM_DH)
            qh = q_ref[:, cols]
            kh = kv_ref[:, cols] * scale
            doh = do_ref[:, cols]
            st = _dot_nt(kh, qh)
            m = jnp.max(st, axis=0, keepdims=True)
            e = jnp.exp(st - m)
            pt = e / jnp.sum(e, axis=0, keepdims=True)
            dpt = _dot_nt(kv_ref[:, vcols], doh)
            dst = pt * (dpt - jnp.sum(pt * dpt, axis=0, keepdims=True))
            dsb = dst.astype(MXU)
            dkv_ref[:, vcols] += _dot(pt.astype(MXU), doh)
            dkv_ref[:, cols] += _dot(dsb, qh) * scale
            dq_ref[:, cols] = _dot_tn(dsb, kh).astype(MXU)

    return _pc(
        body, name="mem_attn_bwd", grid=(s // tm,),
        in_specs=[pl.BlockSpec((tm, D), lambda i: (i, 5)), pl.BlockSpec((mm, 2 * D), lambda i: (0, 0)),
                  pl.BlockSpec((tm, D), lambda i: (i, 0))],
        out_specs=[pl.BlockSpec((tm, D), lambda i: (i, 0)), pl.BlockSpec((mm, 2 * D), lambda i: (0, 0))],
        out_shape=[_sds((s, D), MXU), _sds((mm, 2 * D), F32)], compiler_params=_cp(1),
    )(proj, kv, do)


def _resident(n):
    return [pltpu.VMEM((n, D, D), MXU), pltpu.SemaphoreType.DMA((n,))]


def _load_resident(hbm_refs, wbuf, sems):
    @pl.when(pl.program_id(0) == 0)
    def _():
        cps = [pltpu.make_async_copy(r, wbuf.at[k], sems.at[k]) for k, r in enumerate(hbm_refs)]
        for cp in cps:
            cp.start()
        for cp in cps:
            cp.wait()


def _merge_out(oa, oc, om, proj, x, wpa, wpc, wpm, wout, g_post, g_fpre):
    s = x.shape[0]
    tm = min(TM_ROW, s)

    def body(oa_ref, oc_ref, om_ref, gl_ref, x_ref, gp_ref, gf_ref, wpa_h, wpc_h, wpm_h, wout_h,
             pa_ref, pc_ref, pm_ref, mg_ref, z_ref, x1_ref, h2_ref, wbuf, sems):
        _load_resident([wpa_h, wpc_h, wpm_h, wout_h], wbuf, sems)
        merged = jnp.zeros((tm, D), F32)
        for b, (o_ref, p_ref) in enumerate(((oa_ref, pa_ref), (oc_ref, pc_ref), (om_ref, pm_ref))):
            pb = _dot(o_ref[...], wbuf[b])
            p_ref[...] = pb.astype(MXU)
            merged = merged + _sigmoid(gl_ref[:, pl.ds(b * D, D)].astype(F32)) * pb
        mg = merged.astype(MXU)
        mg_ref[...] = mg
        z = _dot(mg, wbuf[3])
        z_ref[...] = z
        x1 = x_ref[...] + z * _rms(z) * gp_ref[...]
        x1_ref[...] = x1
        h2_ref[...] = (x1 * _rms(x1) * gf_ref[...]).astype(MXU)

    rows = pl.BlockSpec((tm, D), lambda i: (i, 0))
    vec = pl.BlockSpec((1, D), lambda i: (0, 0))
    anyspec = pl.BlockSpec(memory_space=pl.ANY)
    return _pc(
        body, name="merge_out", grid=(s // tm,),
        in_specs=[rows, rows, rows, pl.BlockSpec((tm, 3 * D), lambda i: (i, 2)), rows, vec, vec,
                  anyspec, anyspec, anyspec, anyspec],
        out_specs=[rows] * 7,
        out_shape=[_sds((s, D), MXU)] * 4 + [_sds((s, D), F32)] * 2 + [_sds((s, D), MXU)],
        scratch_shapes=_resident(4), compiler_params=_cp(1),
    )(oa, oc, om, proj, x, g_post, g_fpre, wpa, wpc, wpm, wout)


def _ffn_up(h2, w_gu):
    s = h2.shape[0]
    tm = min(TM_PROJ, s)
    nb = 2
    bw = FFN // nb

    def body(h_ref, wg_ref, wu_ref, gf_ref, uf_ref, act_ref):
        hv = h_ref[...]
        gf = _dot(hv, wg_ref[...])
        uf = _dot(hv, wu_ref[...])
        gf_ref[...] = gf.astype(MXU)
        uf_ref[...] = uf.astype(MXU)
        act_ref[...] = (gf * _sigmoid(gf) * uf).astype(MXU)

    out = pl.BlockSpec((tm, bw), lambda i, j: (i, j))
    return _pc(
        body, name="ffn_up", grid=(s // tm, nb),
        in_specs=[pl.BlockSpec((tm, D), lambda i, j: (i, 0)), pl.BlockSpec((D, bw), lambda i, j: (0, j)),
                  pl.BlockSpec((D, bw), lambda i, j: (0, nb + j))],
        out_specs=[out, out, out], out_shape=[_sds((s, FFN), MXU)] * 3, compiler_params=_cp(2),
    )(h2, w_gu, w_gu)


def _ffn_down_loss(act, w_d, x1, tgt, g_fpost):
    s = act.shape[0]
    tm = min(TM_PROJ, s)

    def body(a_ref, w_ref, x1_ref, t_ref, g_ref, dffn_ref, dy_ref, vec_ref, loss_ref):
        @pl.when(pl.program_id(0) == 0)
        def _():
            vec_ref[...] = jnp.zeros_like(vec_ref)
            loss_ref[...] = jnp.zeros_like(loss_ref)

        ffn = _dot(a_ref[...], w_ref[...])
        r = _rms(ffn)
        gv = g_ref[...]
        e = x1_ref[...] + ffn * r * gv - t_ref[...]
        loss_ref[...] += jnp.sum(e * e) * (0.5 / D)
        dy = e * (1.0 / D)
        dy_ref[...] = dy
        vec_ref[0:1, :] += jnp.sum(dy * ffn * r, axis=0, keepdims=True)
        dffn_ref[...] = _rms_bwd(ffn, r, gv, dy).astype(MXU)

    rows = pl.BlockSpec((tm, D), lambda i: (i, 0))
    return _pc(
        body, name="ffn_down_loss", grid=(s // tm,),
        in_specs=[pl.BlockSpec((tm, FFN), lambda i: (i, 0)), pl.BlockSpec((FFN, D), lambda i: (0, 0)), rows, rows,
                  pl.BlockSpec((1, D), lambda i: (0, 0))],
        out_specs=[rows, rows, pl.BlockSpec((8, D), lambda i: (0, 0)), pl.BlockSpec((8, LANES), lambda i: (0, 0))],
        out_shape=[_sds((s, D), MXU), _sds((s, D), F32), _sds((8, D), F32), _sds((8, LANES), F32)],
        compiler_params=_cp(1),
    )(act, w_d, x1, tgt, g_fpost)


def _ffn_down_bwd(dffn, w_d, gf, uf):
    s = dffn.shape[0]
    tm = min(TM_ROW, s)

    def body(d_ref, w_ref, gf_ref, uf_ref, dgf_ref, duf_ref):
        da = _dot_nt(d_ref[...], w_ref[...]).astype(MXU)
        gf = gf_ref[...]
        one = jnp.ones_like(gf)
        sg = one / (one + jnp.exp(-gf))
        silu = gf * sg
        duf_ref[...] = da * silu
        dgf_ref[...] = da * (uf_ref[...] * (sg + silu * (one - sg)))

    wide = pl.BlockSpec((tm, FFN), lambda i: (i, 0))
    return _pc(
        body, name="ffn_down_bwd", grid=(s // tm,),
        in_specs=[pl.BlockSpec((tm, D), lambda i: (i, 0)), pl.BlockSpec((FFN, D), lambda i: (0, 0)), wide, wide],
        out_specs=[wide, wide], out_shape=[_sds((s, FFN), MXU)] * 2, compiler_params=_cp(1),
    )(dffn, w_d, gf, uf)


def _ffn_up_bwd(dgf, duf, w_gu, x1, dy, z, g_fpre, g_post):
    s = x1.shape[0]
    tm = min(TM_ROW, s)

    def body(dgf_ref, duf_ref, w_ref, x1_ref, dy_ref, z_ref, gf_ref, gp_ref, dx1_ref, dz_ref, vec_ref):
        @pl.when(pl.program_id(0) == 0)
        def _():
            vec_ref[...] = jnp.zeros_like(vec_ref)

        dh2 = _dot_nt(dgf_ref[...], w_ref[:, pl.ds(0, FFN)]) + _dot_nt(duf_ref[...], w_ref[:, pl.ds(FFN, FFN)])
        x1 = x1_ref[...]
        r2 = _rms(x1)
        vec_ref[0:1, :] += jnp.sum(dh2 * x1 * r2, axis=0, keepdims=True)
        dx1 = dy_ref[...] + _rms_bwd(x1, r2, gf_ref[...], dh2)
        dx1_ref[...] = dx1
        z = z_ref[...]
        rz = _rms(z)
        vec_ref[1:2, :] += jnp.sum(dx1 * z * rz, axis=0, keepdims=True)
        dz_ref[...] = _rms_bwd(z, rz, gp_ref[...], dx1).astype(MXU)

    rows = pl.BlockSpec((tm, D), lambda i: (i, 0))
    wide = pl.BlockSpec((tm, FFN), lambda i: (i, 0))
    vec = pl.BlockSpec((1, D), lambda i: (0, 0))
    return _pc(
        body, name="ffn_up_bwd", grid=(s // tm,),
        in_specs=[wide, wide, pl.BlockSpec((D, 2 * FFN), lambda i: (0, 0)), rows, rows, rows, vec, vec],
        out_specs=[rows, rows, pl.BlockSpec((8, D), lambda i: (0, 0))],
        out_shape=[_sds((s, D), F32), _sds((s, D), MXU), _sds((8, D), F32)], compiler_params=_cp(1),
    )(dgf, duf, w_gu, x1, dy, z, g_fpre, g_post)


def _merge_bwd(dz, proj, pa, pc, pm, oa, wpa, wpc, wpm, wout):
    s = dz.shape[0]
    tm = min(TM_ROW, s)

    def body(dz_ref, gl_ref, pa_ref, pc_ref, pm_ref, oa_ref, wpa_h, wpc_h, wpm_h, wout_h,
             dpa_ref, dpc_ref, dpm_ref, dgl_ref, doa_ref, doc_ref, dom_ref, dl_ref, wbuf, sems):
        _load_resident([wpa_h, wpc_h, wpm_h, wout_h], wbuf, sems)
        dm = _dot_nt(dz_ref[...], wbuf[3])
        quads = ((pa_ref, dpa_ref, doa_ref), (pc_ref, dpc_ref, doc_ref), (pm_ref, dpm_ref, dom_ref))
        for b, (p_ref, dp_ref, do_ref) in enumerate(quads):
            cols = pl.ds(b * D, D)
            gt = _sigmoid(gl_ref[:, cols].astype(F32))
            dp = (dm * gt).astype(MXU)
            dp_ref[...] = dp
            dgl_ref[:, cols] = (dm * p_ref[...].astype(F32) * gt * (1.0 - gt)).astype(MXU)
            dob = _dot_nt(dp, wbuf[b]).astype(MXU)
            do_ref[...] = dob
            if b == 0:
                d_i = lax.broadcasted_iota(jnp.int32, (D, LANES), 0)
                h_i = lax.broadcasted_iota(jnp.int32, (D, LANES), 1)
                sel = jnp.where(lax.shift_right_logical(d_i, DH.bit_length() - 1) == h_i, 1.0, 0.0).astype(MXU)
                dl_ref[...] = _dot(dob * oa_ref[...], sel)

    rows = pl.BlockSpec((tm, D), lambda i: (i, 0))
    anyspec = pl.BlockSpec(memory_space=pl.ANY)
    wide = pl.BlockSpec((tm, 3 * D), lambda i: (i, 2))
    return _pc(
        body, name="merge_bwd", grid=(s // tm,),
        in_specs=[rows, wide, rows, rows, rows, rows, anyspec, anyspec, anyspec, anyspec],
        out_specs=[rows, rows, rows, pl.BlockSpec((tm, 3 * D), lambda i: (i, 0)), rows, rows, rows,
                   pl.BlockSpec((tm, LANES), lambda i: (i, 0))],
        out_shape=[_sds((s, D), MXU)] * 3 + [_sds((s, 3 * D), MXU)] + [_sds((s, D), MXU)] * 3 + [_sds((s, LANES), F32)],
        scratch_shapes=_resident(4), compiler_params=_cp(1),
    )(dz, proj, pa, pc, pm, oa, wpa, wpc, wpm, wout)


def _in_proj_bwd(pieces, df, w_main, w_f, x, dx1, g_pre, plan):
    s = x.shape[0]
    tm = min(TM_ROW, s)
    n_main = w_main.shape[1]
    np_ = len(pieces)
    nx = len(plan["arrays"])
    n_in = np_ + 6

    def body(*refs):
        p_refs = refs[:np_]
        df_ref, x_ref, dx1_ref, g_ref, w_h, wf_ref = refs[np_:n_in]
        dx_ref, vec_ref = refs[n_in + nx:n_in + nx + 2]
        wbuf, sem = refs[n_in + 2 * nx + 2:n_in + 2 * nx + 4]
        _host_plan(plan, refs[n_in:n_in + nx], refs[n_in + nx + 2:n_in + 2 * nx + 2], refs[n_in + 2 * nx + 4:],
                   pl.program_id(0) == 0, None, pl.program_id(0) == s // tm - 1)

        @pl.when(pl.program_id(0) == 0)
        def _():
            vec_ref[...] = jnp.zeros_like(vec_ref)
            cp = pltpu.make_async_copy(w_h, wbuf, sem)
            cp.start()
            cp.wait()

        dh = _dot_nt(df_ref[...], wf_ref[...])
        for p_ref, (_, c0, nc) in zip(p_refs, pieces):
            dh = dh + _dot_nt(p_ref[...], wbuf[:, pl.ds(c0 * D, nc * D)])
        xv = x_ref[...]
        r = _rms(xv)
        vec_ref[0:1, :] += jnp.sum(dh * xv * r, axis=0, keepdims=True)
        dx_ref[...] = dx1_ref[...] + _rms_bwd(xv, r, g_ref[...], dh)

    rows = pl.BlockSpec((tm, D), lambda i: (i, 0))
    p_specs = [pl.BlockSpec((tm, nc * D), lambda i: (i, 0)) for _, _, nc in pieces]
    return _pc(
        body, name="in_proj_bwd", grid=(s // tm,),
        in_specs=p_specs + [pl.BlockSpec((tm, LANES), lambda i: (i, 0)), rows, rows, pl.BlockSpec((1, D), lambda i: (0, 0)),
                            pl.BlockSpec(memory_space=pl.ANY), pl.BlockSpec((D, LANES), lambda i: (0, 0))] + [ANY] * nx,
        out_specs=[rows, pl.BlockSpec((8, D), lambda i: (0, 0))] + [ANY] * nx,
        out_shape=[_sds((s, D), F32), _sds((8, D), F32)] + plan["out_shape"],
        scratch_shapes=[pltpu.VMEM((D, n_main), MXU), pltpu.SemaphoreType.DMA] + plan["scratch"], compiler_params=_cp(1),
    )(*[p for p, _, _ in pieces], df, x, dx1, g_pre, w_main, w_f, *plan["arrays"])


def _wgrad(xa, dy, name):
    s, k = xa.shape
    n = dy.shape[1]
    ts = min(TS_WG, s)
    tk = _tile(k, WG_CAP)
    tn = _tile(n, WG_CAP)

    def body(x_ref, dy_ref, o_ref, acc):
        @pl.when(pl.program_id(2) == 0)
        def _():
            acc[...] = jnp.zeros_like(acc)

        acc[...] += _dot_tn(x_ref[...], dy_ref[...])

        @pl.when(pl.program_id(2) == s // ts - 1)
        def _():
            o_ref[...] = acc[...].astype(WIRE)

    return _pc(
        body, name=name, grid=(k // tk, n // tn, s // ts),
        in_specs=[pl.BlockSpec((ts, tk), lambda a, b, c: (c, a)), pl.BlockSpec((ts, tn), lambda a, b, c: (c, b))],
        out_specs=pl.BlockSpec((tk, tn), lambda a, b, c: (a, b)), out_shape=_sds((k, n), WIRE),
        scratch_shapes=[pltpu.VMEM((tk, tn), F32)], compiler_params=_cp(3),
    )(xa, dy)


def _pair_sum(g, r1, c_idx):
    _, _, hr, cols = g.shape
    tr = _rowtile(hr, cols)

    def body(c_ref, g_ref, r_ref, o_ref):
        o_ref[0] = (g_ref[0, 0].astype(F32) + r_ref[0].astype(F32)).astype(WIRE)

    return _pc(
        body, name="pair_sum_%dx%d" % (hr, cols), out_shape=_sds((N_CHIPS, hr, cols), WIRE),
        grid_spec=pltpu.PrefetchScalarGridSpec(
            num_scalar_prefetch=1, grid=(N_CHIPS, hr // tr),
            in_specs=[pl.BlockSpec((1, 1, tr, cols), lambda d, i, c: (d, c[0], i, 0)),
                      pl.BlockSpec((1, tr, cols), lambda d, i, c: (d, i, 0))],
            out_specs=pl.BlockSpec((1, tr, cols), lambda d, i, c: (d, i, 0))),
        compiler_params=_cp(2),
    )(c_idx, g, r1)


def _chip_sum(r2, slot, base=None):
    _, hr, cols = r2.shape
    tr = _rowtile(hr, cols)

    def body(s_ref, r_ref, *rest):
        o_ref = rest[-1]
        acc = r_ref[0].astype(F32)
        for d in range(1, N_CHIPS):
            acc = acc + r_ref[d].astype(F32)
        o_ref[0] = acc

    based = base is not None
    return _pc(
        body, name="chip_sum_%dx%d_%d" % (hr, cols, int(based)), out_shape=_sds((2, hr, cols), F32),
        grid_spec=pltpu.PrefetchScalarGridSpec(
            num_scalar_prefetch=1, grid=(hr // tr,),
            in_specs=[pl.BlockSpec((N_CHIPS, tr, cols), lambda i, s: (0, i, 0))] + ([ANY] if based else []),
            out_specs=pl.BlockSpec((1, tr, cols), lambda i, s: (s[0], i, 0))),
        input_output_aliases={2: 0} if based else {}, compiler_params=_cp(1),
    )(*((slot, r2, base) if based else (slot, r2)))


def _adamw(w, g, m, v):
    rows, cols = w.shape
    tr = _rowtile(rows, cols)
    c1 = 1.0 / (1.0 - B1 ** STEP)
    c2 = 1.0 / (1.0 - B2 ** STEP)

    def body(w_ref, g_ref, m_ref, v_ref, d_ref, mo_ref, vo_ref):
        gv = g_ref[...]
        mn = B1 * m_ref[...] + (1.0 - B1) * gv
        vn = B2 * v_ref[...] + (1.0 - B2) * (gv * gv)
        mo_ref[...] = mn
        vo_ref[...] = vn
        d_ref[...] = -LR * ((mn * c1) / (jnp.sqrt(vn * c2) + ADAM_EPS) + WD * w_ref[...])

    blk = pl.BlockSpec((tr, cols), lambda i: (i, 0))
    return _pc(
        body, name="adamw_%dx%d" % (rows, cols), grid=(rows // tr,), in_specs=[blk] * 4, out_specs=[blk] * 3,
        out_shape=[_sds((rows, cols), F32)] * 3, compiler_params=_cp(1),
    )(w, g, m, v)


MESH_ID = pl.DeviceIdType.MESH
ANY = pl.BlockSpec(memory_space=pl.ANY)


def _place():
    x, y, c = lax.axis_index("x"), lax.axis_index("y"), lax.axis_index("c")
    others = [(1 - x, y), (x, 1 - y), (1 - x, 1 - y)]
    return x, y, c, others


def _remote(src, dst, sems, idx, to):
    return pltpu.make_async_remote_copy(src_ref=src, dst_ref=dst, send_sem=sems[0].at[idx], recv_sem=sems[1].at[idx],
                                        device_id=to, device_id_type=MESH_ID)


def _gather_plan(shards):
    nk = len(shards)

    def copies(ins, outs, sems):
        x, y, c, others = _place()
        me = 2 * x + y
        sib = (x, y, 1 - c)
        local = [pltpu.make_async_copy(ins[k], outs[k].at[me], sems[2].at[k]) for k in range(nk)]
        ici, landed, fwd, fwd_landed = [], [], [], []
        for k in range(nk):
            hr = shards[k].shape[0] // 2
            for r, (cx, cy) in enumerate(others):
                mine = pl.ds(c * hr, hr)
                ici.append(_remote(ins[k].at[mine], outs[k].at[me, mine], sems, 6 * k + r, (cx, cy, c)))
                got = outs[k].at[2 * cx + cy, mine]
                landed.append(_remote(got, got, sems, 6 * k + r, (cx, cy, c)))
                fwd.append(_remote(got, got, sems, 6 * k + 3 + r, sib))
                theirs = outs[k].at[2 * cx + cy, pl.ds((1 - c) * hr, hr)]
                fwd_landed.append(_remote(theirs, theirs, sems, 6 * k + 3 + r, sib))
        return local, ici, landed, fwd, fwd_landed

    def start(ins, outs, sems):
        local, ici, _, _, _ = copies(ins, outs, sems)
        for cp in local + ici:
            cp.start()

    def forward(ins, outs, sems):
        _, _, landed, fwd, _ = copies(ins, outs, sems)
        for got, cp in zip(landed, fwd):
            got.wait_recv()
            cp.start()

    def finish(ins, outs, sems):
        local, ici, _, fwd, fwd_landed = copies(ins, outs, sems)
        for got in fwd_landed:
            got.wait_recv()
        for cp in ici + fwd:
            cp.wait_send()
        for cp in local:
            cp.wait()

    return dict(
        arrays=list(shards), out_shape=[_sds((N_CHIPS,) + a.shape, a.dtype) for a in shards],
        scratch=[pltpu.SemaphoreType.DMA((6 * nk,)), pltpu.SemaphoreType.DMA((6 * nk,)), pltpu.SemaphoreType.DMA((nk,))],
        phases=[start, forward, finish])


def _scatter_plan(ps):
    nk = len(ps)

    def copies(ins, outs, sems):
        x, y, c, others = _place()
        me = 2 * x + y
        local = [pltpu.make_async_copy(ins[k].at[me], outs[k].at[me], sems[2].at[k]) for k in range(nk)]
        ici, landed = [], []
        for k in range(nk):
            for r, (cx, cy) in enumerate(others):
                ici.append(_remote(ins[k].at[2 * cx + cy], outs[k].at[me], sems, 3 * k + r, (cx, cy, c)))
                got = outs[k].at[2 * cx + cy]
                landed.append(_remote(got, got, sems, 3 * k + r, (cx, cy, c)))
        return local, ici, landed

    def start(ins, outs, sems):
        local, ici, _ = copies(ins, outs, sems)
        for cp in local + ici:
            cp.start()

    def finish(ins, outs, sems):
        local, ici, landed = copies(ins, outs, sems)
        for got in landed:
            got.wait_recv()
        for cp in ici:
            cp.wait_send()
        for cp in local:
            cp.wait()

    return dict(
        arrays=list(ps), out_shape=[_sds(a.shape, a.dtype) for a in ps],
        scratch=[pltpu.SemaphoreType.DMA((3 * nk,)), pltpu.SemaphoreType.DMA((3 * nk,)), pltpu.SemaphoreType.DMA((nk,))],
        phases=[start, finish])


def _run_plan(plan, name):
    nk = len(plan["arrays"])

    def body(*refs):
        ins, outs, sems = refs[:nk], refs[nk:2 * nk], refs[2 * nk:]
        for phase in plan["phases"]:
            phase(ins, outs, sems)

    return _pc(body, name=name, in_specs=[ANY] * nk, out_specs=[ANY] * nk, out_shape=plan["out_shape"],
               scratch_shapes=plan["scratch"])(*plan["arrays"])


def _host_plan(plan, ins, outs, sems, first, middle, last):
    points = [first, last] if len(plan["phases"]) == 2 else [first, middle, last]
    for phase, at in zip(plan["phases"], points):
        pl.when(at)(functools.partial(phase, ins, outs, sems))


def _swap_sibling(gs, halves, tag):
    nk = len(gs)

    def body(*refs):
        ins, outs = refs[:nk], refs[nk:2 * nk]
        send_sems, recv_sems = refs[2 * nk:]
        x, y, c, _ = _place()
        cps = []
        for k in range(nk):
            hr = gs[k].shape[1] // 2
            cp = pltpu.make_async_remote_copy(
                src_ref=ins[k].at[:, pl.ds((1 - c) * hr, hr)] if halves else ins[k], dst_ref=outs[k],
                send_sem=send_sems.at[k], recv_sem=recv_sems.at[k], device_id=(x, y, 1 - c), device_id_type=MESH_ID)
            cp.start()
            cps.append(cp)
        for cp in cps:
            cp.wait()

    return _pc(
        body, name=("swap_halves_" if halves else "swap_slabs_") + tag, in_specs=[ANY] * nk, out_specs=[ANY] * nk,
        out_shape=[_sds((N_CHIPS, a.shape[1] // 2 if halves else a.shape[1], a.shape[2]), a.dtype) for a in gs],
        scratch_shapes=[pltpu.SemaphoreType.DMA((nk,)), pltpu.SemaphoreType.DMA((nk,))],
    )(*gs)


def _allreduce_small(v):
    rows, cols = v.shape

    def body(v_ref, o_ref, gath, send_sems, recv_sems):
        x, y, c, others = _place()
        sib = (x, y, 1 - c)

        def slot(px, py, pc):
            return gath.at[4 * px + 2 * py + pc]

        def copy(k, block, to, src=None):
            return pltpu.make_async_remote_copy(
                src_ref=slot(*block) if src is None else src, dst_ref=slot(*block),
                send_sem=send_sems.at[k], recv_sem=recv_sems.at[k], device_id=to, device_id_type=MESH_ID)

        me = (x, y, c)
        gath[4 * x + 2 * y + c] = v_ref[...]
        first = [copy(0, me, sib, src=v_ref)]
        first += [copy(1 + r, me, (cx, cy, c), src=v_ref) for r, (cx, cy) in enumerate(others)]
        for cp in first:
            cp.start()
        passed = [copy(4 + r, (cx, cy, c), sib) for r, (cx, cy) in enumerate(others)]
        for r, (cx, cy) in enumerate(others):
            copy(1 + r, (cx, cy, c), me).wait_recv()
            passed[r].start()
        copy(0, (x, y, 1 - c), me).wait_recv()
        for r, (cx, cy) in enumerate(others):
            copy(4 + r, (cx, cy, 1 - c), me).wait_recv()
        for cp in first + passed:
            cp.wait_send()
        acc = gath[0]
        for d in range(1, N_DEV):
            acc = acc + gath[d]
        o_ref[...] = acc

    vm = pl.BlockSpec(memory_space=pltpu.VMEM)
    return _pc(
        body, name="allreduce_small", in_specs=[vm], out_specs=vm, out_shape=_sds((rows, cols), F32),
        scratch_shapes=[pltpu.VMEM((N_DEV, rows, cols), F32), pltpu.SemaphoreType.DMA((7,)), pltpu.SemaphoreType.DMA((7,))],
    )(v)


def _cols_to_chips(a):
    r, c4 = a.shape
    return a.reshape(r, N_CHIPS, c4 // N_CHIPS).transpose(1, 0, 2)


def _chips_to_cols(a):
    n, r, c = a.shape
    return a.transpose(1, 0, 2).reshape(r, n * c)


def _head_rows(a, tq):
    s = a.shape[0]
    return a[:, :HF].reshape(s // tq, tq, NP, 2).transpose(2, 0, 3, 1)


def _head_cols(a):
    s = a.shape[0]
    return a[:, :HF].reshape(s, NP, 2).transpose(1, 0, 2)


def _to_wire(gs, c_idx, tag):
    gs = [g.astype(WIRE) for g in gs]
    r1 = _swap_sibling(gs, True, tag)
    return [_pair_sum(g.reshape(N_CHIPS, 2, g.shape[1] // 2, g.shape[2]), r, c_idx) for g, r in zip(gs, r1)]


def _local_step(x, mem, tgt, sp, w_main, w_f, rest_shards, c_idx):
    s = x.shape[0]
    tq = min(TQ, s)
    b_f = jnp.pad(sp["b_forget"], ((0, 0), (0, LANES - HF)))
    proj, h, flog = _in_proj(x, sp["norm_mix_pre"], w_main, w_f)
    cf = _logf_cumsum(flog, b_f)
    qx, kx = _fox_prep(proj, _head_cols(cf))
    oa, lse, g_cw, g_kv, g_pj, g_gu, g_d = _fox_fwd(proj, qx, kx, _gather_plan(rest_shards))
    pj = g_pj.reshape(N_CHIPS, 4, D // N_CHIPS, D).transpose(1, 0, 2, 3).reshape(4, D, D)
    w = {"conv_w": _chips_to_cols(g_cw), "w_kv": _chips_to_cols(g_kv), "wpa": pj[0], "wpc": pj[1], "wpm": pj[2],
         "wout": pj[3], "w_gu": _chips_to_cols(g_gu), "w_d": g_d.reshape(FFN, D)}
    y, oc = _conv_fwd(proj, w["conv_w"], sp["conv_b"], sp["conv_ln_g"], sp["conv_ln_b"])
    mem_n, kv = _mem_kv(mem, sp["norm_mem"], w["w_kv"])
    om = _mem_attn_fwd(proj, kv)
    pa, pc, pm, merged, z, x1, h2 = _merge_out(oa, oc, om, proj, x, w["wpa"], w["wpc"], w["wpm"], w["wout"],
                                              sp["norm_mix_post"], sp["norm_ffn_pre"])
    gf, uf, act = _ffn_up(h2, w["w_gu"])
    dffn, dy, vec_f, loss_blk = _ffn_down_loss(act, w["w_d"], x1, tgt, sp["norm_ffn_post"])

    dgf, duf = _ffn_down_bwd(dffn, w["w_d"], gf, uf)
    dx1, dz, vec_n = _ffn_up_bwd(dgf, duf, w["w_gu"], x1, dy, z, sp["norm_ffn_pre"], sp["norm_mix_post"])
    dpa, dpc, dpm, dgl, doa, doc, dom, delta = _merge_bwd(dz, proj, pa, pc, pm, oa, w["wpa"], w["wpc"], w["wpm"], w["wout"])
    pj_g = jnp.stack([_wgrad(oa, dpa, "wgrad_pa"), _wgrad(oc, dpc, "wgrad_pc"), _wgrad(om, dpm, "wgrad_pm"),
                      _wgrad(merged, dz, "wgrad_out")]).reshape(4, N_CHIPS, D // N_CHIPS, D)
    early = [pj_g.transpose(1, 0, 2, 3).reshape(N_CHIPS, D, D),
             _cols_to_chips(jnp.concatenate([_wgrad(h2, dgf, "wgrad_g"), _wgrad(h2, duf, "wgrad_u")], axis=1)),
             _wgrad(act, dffn, "wgrad_d").reshape(N_CHIPS, FFN // N_CHIPS, D)]
    lse16 = lse.transpose(1, 0, 2).reshape(s, HF)
    dq, dqs, dk, dks, dv, *r2_early = _fox_bwd(proj, qx, kx, doa, _head_rows(lse16, tq), _head_rows(delta, tq),
                                              _scatter_plan(_to_wire(early, c_idx, "early")))
    over_keys = dqs[:, X_KONE:X_KONE + 2, :].transpose(2, 0, 1).reshape(s, HF)
    over_queries = dks[:, :, X_QONE:X_QONE + 2, :].transpose(1, 3, 0, 2).reshape(s, HF)
    dc = jnp.pad(over_keys - over_queries, ((0, 0), (0, LANES - HF)))
    df, db_blk = _logf_cumsum_bwd(flog, b_f, dc)
    dga, dgg, dcw, vec_c = _conv_bwd(proj, y, doc, w["conv_w"], sp["conv_ln_g"], sp["conv_ln_b"])
    dqm, dkv = _mem_attn_bwd(proj, kv, dom)
    dkv_b = dkv.astype(MXU)
    vec_m = _mem_kv_bwd(mem, sp["norm_mem"], w["w_kv"], dkv_b)
    pieces = [(dq, 0, 1), (dk, 1, 1), (dv, 2, 1), (dga, 3, 1), (dgg, 4, 1), (dqm, 5, 1), (dgl, 6, 3)]
    dw_cols = [_wgrad(h, p, "wgrad_in_%d" % c0) for p, c0, _ in pieces]
    dwf = _wgrad(h, df, "wgrad_in_f")
    late = [_cols_to_chips(jnp.concatenate(dw_cols[:3] + [dwf[:, :HF]] + dw_cols[3:], axis=1)), _cols_to_chips(dcw),
            _cols_to_chips(_wgrad(mem_n, dkv_b, "wgrad_kv"))]
    dx, vec_p, *r2_late = _in_proj_bwd(pieces, df, w_main, w_f, x, dx1, sp["norm_mix_pre"],
                                       _scatter_plan(_to_wire(late, c_idx, "late")))
    zero_row = jnp.zeros((1, D), F32)
    small = jnp.concatenate([
        vec_p[0:1], vec_n[1:2], vec_m[0:1], vec_c[2:3], vec_c[0:1], vec_c[1:2], vec_n[0:1], vec_f[0:1],
        jnp.pad(db_blk[0:1, :HF], ((0, 0), (0, D - HF))),
        jnp.pad(loss_blk[0:1, 0:1], ((0, 0), (0, D - 1))),
    ] + [zero_row] * (SMALL_ROWS - 10), axis=0)
    return dx, list(r2_late) + list(r2_early), small


SMALL_NAMES = ["norm_mix_pre", "norm_mix_post", "norm_mem", "conv_b", "conv_ln_g", "conv_ln_b", "norm_ffn_pre", "norm_ffn_post"]
PROJ_NAMES = ["w_proj_attn", "w_proj_conv", "w_proj_mem", "w_out"]
WEIGHT_ORDER = ["norm_mix_pre", "norm_mix_post", "norm_mem", "w_in", "b_forget", "conv_w", "conv_b", "conv_ln_g", "conv_ln_b",
                "w_kv_mem", "w_proj_attn", "w_proj_conv", "w_proj_mem", "w_out", "norm_ffn_pre", "norm_ffn_post",
                "w_gate_up", "w_down"]


def _pack_small(p):
    rows = [p[n] for n in SMALL_NAMES] + [jnp.pad(p["b_forget"], ((0, 0), (0, D - HF)))]
    return jnp.concatenate(rows + [jnp.zeros((SMALL_ROWS - len(rows), D), F32)], axis=0)


def _step(params, moms, vels, x, mem, tgt):
    c_idx = lax.axis_index("c").astype(jnp.int32).reshape(1)

    (g_in,) = _run_plan(_gather_plan([params["w_in"].astype(WIRE)]), "gather_w_in")
    w_in_full = _chips_to_cols(g_in)
    w_main = jnp.concatenate([w_in_full[:, :3 * D], w_in_full[:, 3 * D + HF:]], axis=1)
    w_f = jnp.pad(w_in_full[:, 3 * D:3 * D + HF], ((0, 0), (0, LANES - HF)))
    rest = [jnp.pad(params["conv_w"], ((0, CWP - CW), (0, 0))), params["w_kv_mem"].astype(WIRE),
            jnp.concatenate([params[n] for n in PROJ_NAMES], axis=0).astype(WIRE),
            params["w_gate_up"].astype(WIRE), params["w_down"].astype(WIRE)]

    dx, r2, small = _local_step(x, mem, tgt, params, w_main, w_f, rest, c_idx)

    r2_sib = _swap_sibling(r2, False, "all")
    full = [_chip_sum(theirs, 1 - c_idx, _chip_sum(mine, c_idx)) for mine, theirs in zip(r2, r2_sib)]
    red = [f.reshape(2 * f.shape[1], f.shape[2]) for f in full]
    pj_r = red[3].reshape(4, D // N_CHIPS, D)
    grads = {"w_in": red[0], "conv_w": red[1][:CW], "w_kv_mem": red[2], "w_gate_up": red[4], "w_down": red[5]}
    for i, n in enumerate(PROJ_NAMES):
        grads[n] = pj_r[i]

    tot = _allreduce_small(small)
    loss = tot[9, 0]
    for i, n in enumerate(SMALL_NAMES):
        grads[n] = tot[i:i + 1]
    grads["b_forget"] = tot[8:9, :HF]

    delta, new_m, new_v = {}, {}, {}
    ds, ms, vs = _adamw(_pack_small(params), tot.at[9:].set(0.0), _pack_small(moms), _pack_small(vels))
    for i, n in enumerate(SMALL_NAMES):
        delta[n], new_m[n], new_v[n] = ds[i:i + 1], ms[i:i + 1], vs[i:i + 1]
    delta["b_forget"], new_m["b_forget"], new_v["b_forget"] = ds[8:9, :HF], ms[8:9, :HF], vs[8:9, :HF]
    for n in ["w_in", "conv_w", "w_kv_mem", "w_gate_up", "w_down"] + PROJ_NAMES:
        delta[n], new_m[n], new_v[n] = _adamw(params[n], grads[n], moms[n], vels[n])
    return loss, dx, grads, delta, new_m, new_v


def kernel(x, mem, norm_mix_pre, norm_mix_post, norm_mem, w_in, b_forget, conv_w, conv_b, conv_ln_g, conv_ln_b, w_kv_mem, w_proj_attn, w_proj_conv, w_proj_mem, w_out, norm_ffn_pre, norm_ffn_post, w_gate_up, w_down, loss_target, m_norm_mix_pre, m_norm_mix_post, m_norm_mem, m_w_in, m_b_forget, m_conv_w, m_conv_b, m_conv_ln_g, m_conv_ln_b, m_w_kv_mem, m_w_proj_attn, m_w_proj_conv, m_w_proj_mem, m_w_out, m_norm_ffn_pre, m_norm_ffn_post, m_w_gate_up, m_w_down, v_norm_mix_pre, v_norm_mix_post, v_norm_mem, v_w_in, v_b_forget, v_conv_w, v_conv_b, v_conv_ln_g, v_conv_ln_b, v_w_kv_mem, v_w_proj_attn, v_w_proj_conv, v_w_proj_mem, v_w_out, v_norm_ffn_pre, v_norm_ffn_post, v_w_gate_up, v_w_down):
    local = dict(locals())
    lead = {n: local[n].shape[:-2] for n in WEIGHT_ORDER}
    two_d = lambda a: a.reshape(a.shape[-2:])
    params = {n: two_d(local[n]) for n in WEIGHT_ORDER}
    moms = {n: two_d(local["m_" + n]) for n in WEIGHT_ORDER}
    vels = {n: two_d(local["v_" + n]) for n in WEIGHT_ORDER}
    loss, dx, grads, delta, new_m, new_v = _step(params, moms, vels, two_d(x), two_d(mem), two_d(loss_target))
    outs = [loss, dx.reshape(x.shape)]
    for group in (grads, delta, new_m, new_v):
        outs += [group[n].reshape(lead[n] + group[n].shape) for n in WEIGHT_ORDER]
    return tuple(outs)
```

```python
import functools

import jax
import jax.numpy as jnp
from jax import lax
from jax.experimental import pallas as pl
from jax.experimental.pallas import tpu as pltpu

F32 = jnp.float32
MXU = jnp.bfloat16
WIRE = jnp.bfloat16

D = 1024
HF = 16
DH = 64
NP = D // 128
MEM_H = 4
MEM_DH = D // MEM_H
FFN = 2816
CW = 31
CWP = 32
HALO = 32
RMS_EPS = 1e-6
LN_EPS = 1e-5
LR, B1, B2, ADAM_EPS, WD, STEP = 0.001, 0.9, 0.999, 1e-8, 0.01, 10

N_CHIPS = 4
N_DEV = 8
LANES = 128
VMEM_LIMIT = 56 * 1024 * 1024

TM_PROJ = 512
NB_PROJ = 3
TQ = 1024
LOG2E = 1.4426950408889634
LN2 = 0.6931471805599453
QSCALE = DH ** -0.5 * LOG2E
X_BIAS = 0
X_QONE = 6
X_KONE = 8
X_ROWS = 16
TM_CONV = 256
CONV_ROWS = 128
SUB = 8
TM_ROW = 256
TM_WIDE = 4096
TS_WG = 2048
WG_CAP = 1408
SMALL_ROWS = 16


def _pc(body, **kw):
    return pl.pallas_call(body, **kw)


def _cp(n_axes):
    return pltpu.CompilerParams(dimension_semantics=("arbitrary",) * n_axes, vmem_limit_bytes=VMEM_LIMIT)


def _sds(shape, dtype):
    return jax.ShapeDtypeStruct(shape, dtype)


def _dot(a, b):
    return jnp.dot(a, b, preferred_element_type=F32)


def _dot_nt(a, b):
    return lax.dot_general(a, b, (((1,), (1,)), ((), ())), preferred_element_type=F32)


def _dot_tn(a, b):
    return lax.dot_general(a, b, (((0,), (0,)), ((), ())), preferred_element_type=F32)


def _rms(u):
    return lax.rsqrt(jnp.mean(u * u, axis=-1, keepdims=True) + RMS_EPS)


def _rms_bwd(u, r, g, dn):
    w = dn * g
    return r * w - u * (r * r * r) * jnp.mean(u * w, axis=-1, keepdims=True)


def _sigmoid(z):
    return 1.0 / (1.0 + jnp.exp(-z))


def _tile(n, cap):
    if n <= cap:
        return n
    best = None
    for t in range(LANES, cap + 1, LANES):
        if n % t == 0:
            best = t
    assert best is not None, (n, cap)
    return best


def _rowtile(rows, cols, cap_bytes=1 << 20):
    best = None
    for t in range(8, rows + 1, 8):
        if rows % t == 0 and t * cols * 4 <= cap_bytes:
            best = t
    return best if best is not None else rows


def _split3(v):
    hi = v.astype(jnp.bfloat16)
    r1 = v - hi.astype(F32)
    mid = r1.astype(jnp.bfloat16)
    lo = (r1 - mid.astype(F32)).astype(jnp.bfloat16)
    return hi, mid, lo


def _dot_exact_rhs(a01, v):
    hi, mid, lo = _split3(v)
    return _dot(a01, hi) + _dot(a01, mid) + _dot(a01, lo)


def _in_proj(x, g_pre, w_main, w_f):
    s, d = x.shape
    n = w_main.shape[1]
    tm = min(TM_PROJ, s)
    tn = n // NB_PROJ

    def body(x_ref, g_ref, w_ref, wf_ref, proj_ref, h_ref, flog_ref, hs):
        @pl.when(pl.program_id(1) == 0)
        def _():
            xv = x_ref[...]
            h = (xv * _rms(xv) * g_ref[...]).astype(MXU)
            hs[...] = h
            h_ref[...] = h
            flog_ref[...] = _dot(h, wf_ref[...])

        res = _dot(hs[...], w_ref[...])

        @pl.when(pl.program_id(1) == 0)
        def _():
            proj_ref[:, pl.ds(0, d)] = (res[:, :d] * QSCALE).astype(MXU)
            proj_ref[:, pl.ds(d, tn - d)] = res[:, d:].astype(MXU)

        @pl.when(pl.program_id(1) != 0)
        def _():
            proj_ref[...] = res.astype(MXU)

    assert tn >= d
    return _pc(
        body, name="in_proj", grid=(s // tm, NB_PROJ),
        in_specs=[pl.BlockSpec((tm, d), lambda i, j: (i, 0)), pl.BlockSpec((1, d), lambda i, j: (0, 0)),
                  pl.BlockSpec((d, tn), lambda i, j: (0, j)), pl.BlockSpec((d, LANES), lambda i, j: (0, 0))],
        out_specs=[pl.BlockSpec((tm, tn), lambda i, j: (i, j)), pl.BlockSpec((tm, d), lambda i, j: (i, 0)),
                   pl.BlockSpec((tm, LANES), lambda i, j: (i, 0))],
        out_shape=[_sds((s, n), MXU), _sds((s, d), MXU), _sds((s, LANES), F32)],
        scratch_shapes=[pltpu.VMEM((tm, d), MXU)], compiler_params=_cp(2),
    )(x, g_pre, w_main, w_f)


def _log_sigmoid(z):
    e = jnp.exp(-jnp.abs(z))
    log1p_e = jnp.where(e < 1e-3, e * (1.0 - 0.5 * e), jnp.log(1.0 + e))
    return jnp.minimum(z, 0.0) - log1p_e


def _logf_cumsum(flog, b_f):
    s = flog.shape[0]
    ch = LANES

    def body(f_ref, b_ref, c_ref):
        r = lax.broadcasted_iota(jnp.int32, (ch, ch), 0)
        q = lax.broadcasted_iota(jnp.int32, (ch, ch), 1)
        tri = jnp.where(r >= q, 1.0, 0.0).astype(jnp.bfloat16)

        def step(i, carry):
            rows = pl.ds(pl.multiple_of(i * ch, ch), ch)
            lf = _log_sigmoid(f_ref[rows, :] + b_ref[...])
            c_ref[rows, :] = _dot_exact_rhs(tri, lf) + carry
            return carry + jnp.sum(lf, axis=0, keepdims=True)

        lax.fori_loop(0, s // ch, step, jnp.zeros((1, LANES), F32))

    return _pc(body, name="logf_cumsum", out_shape=_sds((s, LANES), F32),
               compiler_params=pltpu.CompilerParams(vmem_limit_bytes=VMEM_LIMIT))(flog, b_f)


def _logf_cumsum_bwd(flog, b_f, dc):
    s = flog.shape[0]
    ch = LANES

    def body(f_ref, b_ref, dc_ref, df_ref, db_ref):
        r = lax.broadcasted_iota(jnp.int32, (ch, ch), 0)
        q = lax.broadcasted_iota(jnp.int32, (ch, ch), 1)
        tri = jnp.where(r <= q, 1.0, 0.0).astype(jnp.bfloat16)
        nch = s // ch

        def step(t, carry):
            tail, dbsum = carry
            i = nch - 1 - t
            rows = pl.ds(pl.multiple_of(i * ch, ch), ch)
            dcv = dc_ref[rows, :]
            dlf = _dot_exact_rhs(tri, dcv) + tail
            z = f_ref[rows, :] + b_ref[...]
            df = dlf * _sigmoid(-z)
            df_ref[rows, :] = df.astype(MXU)
            return tail + jnp.sum(dcv, axis=0, keepdims=True), dbsum + jnp.sum(df, axis=0, keepdims=True)

        zero = jnp.zeros((1, LANES), F32)
        _, dbsum = lax.fori_loop(0, nch, step, (zero, zero))
        db_ref[...] = jnp.broadcast_to(dbsum, db_ref.shape)

    return _pc(body, name="logf_cumsum_bwd", out_shape=[_sds((s, LANES), MXU), _sds((8, LANES), F32)],
               compiler_params=pltpu.CompilerParams(vmem_limit_bytes=VMEM_LIMIT))(flog, b_f, dc)


def _head_masks(rows):
    lane = lax.broadcasted_iota(jnp.int32, (rows, LANES), 1)
    return lane < DH, lane >= DH


def _ext_masks(rows, key_side):
    lane = lax.broadcasted_iota(jnp.int32, (rows, 2 * LANES), 1)
    ext = lane - LANES
    out = []
    for a in range(2):
        head = (lane >= a * DH) & (lane < (a + 1) * DH)
        bias = (ext >= X_BIAS + 3 * a) & (ext < X_BIAS + 3 * a + 3)
        one = ext == (X_KONE if key_side else X_QONE) + a
        out.append(head | bias | one)
    return out


def _fox_prep(proj, ccol):
    s = proj.shape[0]
    tm = min(TM_WIDE, s)

    def body(q_ref, k_ref, c_ref, qx_ref, kx_ref):
        lane = lax.broadcasted_iota(jnp.int32, (tm, LANES), 1)
        qx_ref[:, pl.ds(0, LANES)] = q_ref[...]
        qx_ref[:, pl.ds(LANES, LANES)] = jnp.where(lane < X_QONE + 2, 1.0, 0.0).astype(MXU)
        kext = jnp.where((lane >= X_KONE) & (lane < X_KONE + 2), 1.0, 0.0).astype(jnp.bfloat16)
        for a in range(2):
            terms = _split3(c_ref[0, :, a:a + 1] * (-LOG2E))
            for t, term in enumerate(terms):
                kext = jnp.where(lane == X_BIAS + 3 * a + t, term, kext)
        kx_ref[:, pl.ds(0, LANES)] = k_ref[...]
        kx_ref[:, pl.ds(LANES, LANES)] = kext.astype(MXU)

    wide = pl.BlockSpec((tm, 2 * LANES), lambda p, i: (i, p))
    return _pc(
        body, name="fox_prep", grid=(NP, s // tm),
        in_specs=[pl.BlockSpec((tm, LANES), lambda p, i: (i, p)), pl.BlockSpec((tm, LANES), lambda p, i: (i, NP + p)),
                  pl.BlockSpec((1, tm, 2), lambda p, i: (p, i, 0))],
        out_specs=[wide, wide], out_shape=[_sds((s, NP * 2 * LANES), MXU)] * 2, compiler_params=_cp(2),
    )(proj, proj, ccol)


def _fox_fwd(proj, qx, kx, plan):
    s = proj.shape[0]
    tq = min(TQ, s)
    nq = s // tq
    nx = len(plan["arrays"])

    def body(*refs):
        q_ref, k_ref, v_ref = refs[:3]
        o_ref, lse_ref = refs[3 + nx:5 + nx]
        p_id, i = pl.program_id(0), pl.program_id(1)
        _host_plan(plan, refs[3:3 + nx], refs[5 + nx:5 + 2 * nx], refs[5 + 2 * nx:], (p_id == 0) & (i == 0),
                   (p_id == NP // 2) & (i == 0), (p_id == NP - 1) & (i == nq - 1))
        qv = q_ref[...]
        qmask = _ext_masks(tq, False)
        hmask = _head_masks(tq)
        qas = [jnp.where(qmask[a], qv, jnp.zeros_like(qv)) for a in range(2)]
        row = lax.broadcasted_iota(jnp.int32, (tq, tq), 0)
        col = lax.broadcasted_iota(jnp.int32, (tq, tq), 1)

        def blk(j, carry, diag=False):
            rows = pl.ds(pl.multiple_of(j * tq, tq), tq)
            kj = k_ref[rows, :]
            vj = v_ref[rows, :]
            out = []
            for a in range(2):
                m, acc = carry[a]
                sc = _dot_nt(qas[a], kj)
                if diag:
                    sc = jnp.where(row >= col, sc, -jnp.inf)
                m_new = jnp.maximum(m, jnp.max(sc, axis=-1, keepdims=True))
                p = jnp.exp2(sc - m_new)
                va = jnp.where(hmask[a], vj, jnp.ones_like(vj))
                out.append((m_new, jnp.exp2(m - m_new) * acc + _dot(p.astype(MXU), va)))
            return tuple(out)

        init = (jnp.full((tq, 1), -jnp.inf, F32), jnp.zeros((tq, LANES), F32))
        res = blk(i, lax.fori_loop(0, i, blk, (init, init)), True)
        lane = lax.broadcasted_iota(jnp.int32, (tq, LANES), 1)
        outs, lses = [], []
        for a in range(2):
            m, acc = res[a]
            l = jnp.sum(jnp.where(lane == DH * (1 - a), acc, 0.0), axis=-1, keepdims=True)
            outs.append(acc / l)
            lses.append(m + jnp.log(l) * LOG2E)
        o_ref[...] = jnp.where(hmask[0], outs[0], outs[1]).astype(MXU)
        lane2 = lax.broadcasted_iota(jnp.int32, (tq, 2), 1)
        lse_ref[0] = jnp.where(lane2 == 0, lses[0], lses[1])

    return _pc(
        body, name="fox_fwd", grid=(NP, nq),
        in_specs=[pl.BlockSpec((tq, 2 * LANES), lambda p, i: (i, p)),
                  pl.BlockSpec((s, 2 * LANES), lambda p, i: (0, p)),
                  pl.BlockSpec((s, LANES), lambda p, i: (0, 2 * NP + p))] + [ANY] * nx,
        out_specs=[pl.BlockSpec((tq, LANES), lambda p, i: (i, p)),
                   pl.BlockSpec((1, tq, 2), lambda p, i: (p, i, 0))] + [ANY] * nx,
        out_shape=[_sds((s, D), MXU), _sds((NP, s, 2), F32)] + plan["out_shape"],
        scratch_shapes=plan["scratch"], compiler_params=_cp(2),
    )(qx, kx, proj, *plan["arrays"])


def _fox_bwd(proj, qx, kx, do, lse_row, delta_row, plan):
    s = proj.shape[0]
    tq = min(TQ, s)
    nq = s // tq
    nx = len(plan["arrays"])

    def body(*refs):
        k_ref, v_ref, q_ref, do_ref, lse_ref, dl_ref = refs[:6]
        dq_ref, dqs_ref, dk_ref, dks_ref, dv_ref = refs[6 + nx:11 + nx]
        dq_acc = refs[11 + 2 * nx]
        p_id, j = pl.program_id(0), pl.program_id(1)
        _host_plan(plan, refs[6:6 + nx], refs[11 + nx:11 + 2 * nx], refs[12 + 2 * nx:], (p_id == 0) & (j == 0),
                   (p_id == NP // 2) & (j == 0), (p_id == NP - 1) & (j == nq - 1))

        @pl.when(j == 0)
        def _():
            dq_acc[...] = jnp.zeros_like(dq_acc)

        kv = k_ref[...]
        v2 = v_ref[...]
        kmask = _ext_masks(tq, True)
        qmask = _ext_masks(tq, False)
        hmask = _head_masks(tq)
        row = lax.broadcasted_iota(jnp.int32, (tq, tq), 0)
        col = lax.broadcasted_iota(jnp.int32, (tq, tq), 1)
        carry = (jnp.zeros((tq, 2 * LANES), F32), jnp.zeros((tq, LANES), F32))
        for a in range(2):
            ka = jnp.where(kmask[a], kv, jnp.zeros_like(kv))
            va = jnp.where(hmask[a], v2, jnp.zeros_like(v2))

            def blk(i, carry, diag, a=a, ka=ka, va=va):
                dk_a, dv_a = carry
                rows = pl.ds(pl.multiple_of(i * tq, tq), tq)
                qi = q_ref[rows, :]
                doi = do_ref[rows, :]
                qa = jnp.where(qmask[a], qi, jnp.zeros_like(qi))
                doa = jnp.where(hmask[a], doi, jnp.zeros_like(doi))
                st = _dot_nt(ka, qi)
                if diag:
                    st = jnp.where(col >= row, st, -jnp.inf)
                pt = jnp.exp2(st - lse_ref[0, i, a:a + 1, :])
                dv_a = dv_a + _dot(pt.astype(MXU), doa)
                dpt = _dot_nt(va, doi)
                dsb = (pt * (dpt - dl_ref[0, i, a:a + 1, :])).astype(MXU)
                dk_a = dk_a + _dot(dsb, qa)
                dq_acc[rows, :] += _dot_tn(dsb, ka)
                return dk_a, dv_a

            carry = blk(j, carry, True)
            carry = lax.fori_loop(j + 1, nq, functools.partial(blk, diag=False), carry)
        dk_acc, dv_acc = carry
        dk_ref[...] = (dk_acc[:, :LANES] * LN2).astype(MXU)
        dks_ref[0, 0] = dk_acc[:, LANES:].T[:X_ROWS, :]
        dv_ref[...] = dv_acc.astype(MXU)

        @pl.when(j == nq - 1)
        def _():
            dq_ref[...] = (dq_acc[:, pl.ds(0, LANES)] * DH ** -0.5).astype(MXU)
            for t in range(nq):
                dqs_ref[0, :, pl.ds(t * tq, tq)] = dq_acc[pl.ds(t * tq, tq), pl.ds(LANES, LANES)].T[:X_ROWS, :]

    stat = pl.BlockSpec((1, nq, 2, tq), lambda p, j: (p, 0, 0, 0))
    whole = pl.BlockSpec((s, LANES), lambda p, j: (0, p))
    tile = pl.BlockSpec((tq, LANES), lambda p, j: (j, p))
    return _pc(
        body, name="fox_bwd", grid=(NP, nq),
        in_specs=[pl.BlockSpec((tq, 2 * LANES), lambda p, j: (j, p)),
                  pl.BlockSpec((tq, LANES), lambda p, j: (j, 2 * NP + p)),
                  pl.BlockSpec((s, 2 * LANES), lambda p, j: (0, p)),
                  whole, stat, stat] + [ANY] * nx,
        out_specs=[whole, pl.BlockSpec((1, X_ROWS, s), lambda p, j: (p, 0, 0)), tile,
                   pl.BlockSpec((1, 1, X_ROWS, tq), lambda p, j: (p, j, 0, 0)), tile] + [ANY] * nx,
        out_shape=[_sds((s, D), MXU), _sds((NP, X_ROWS, s), F32), _sds((s, D), MXU),
                   _sds((NP, nq, X_ROWS, tq), F32), _sds((s, D), MXU)] + plan["out_shape"],
        scratch_shapes=[pltpu.VMEM((s, 2 * LANES), F32)] + plan["scratch"], compiler_params=_cp(2),
    )(kx, proj, qx, do, lse_row, delta_row, *plan["arrays"])


def _glu(a, gate):
    return a.astype(F32) * _sigmoid(gate.astype(F32))


def _store_blocked(buf, row0, val):
    for c in range(D // LANES):
        buf[0, c, pl.ds(row0, val.shape[0]), :] = val[:, c * LANES:(c + 1) * LANES]


def _fill_shifted(buf):
    n = buf.shape[2] - SUB
    for r in range(1, SUB):
        buf[r, :, pl.ds(0, n), :] = buf[0, :, pl.ds(r, n), :]


def _shifted(buf, off, rows, c):
    r = off % SUB
    return buf[r, c, pl.ds(off - r, rows), :]


def _conv_fwd(proj, cw, cb, lg, lb):
    s = proj.shape[0]
    tm = min(TM_CONV, s)
    hb = tm // HALO

    rcw = min(CONV_ROWS, tm)

    def body(a_ref, g_ref, ah_ref, gh_ref, w_ref, cb_ref, lg_ref, lb_ref, y_ref, o_ref, gsh):
        i = pl.program_id(0)
        _store_blocked(gsh, 0, jnp.where(i > 0, _glu(ah_ref[...], gh_ref[...]), 0.0))
        _store_blocked(gsh, HALO, _glu(a_ref[...], g_ref[...]))
        _fill_shifted(gsh)
        for c in range(D // LANES):
            cols = pl.ds(c * LANES, LANES)
            for rc in range(tm // rcw):
                acc = jnp.broadcast_to(cb_ref[:, cols], (rcw, LANES))
                for t in range(CW):
                    acc = acc + w_ref[t:t + 1, cols] * _shifted(gsh, HALO - (CW - 1) + t + rc * rcw, rcw, c)
                y_ref[pl.ds(rc * rcw, rcw), cols] = acc
        acc = y_ref[...]
        mu = jnp.mean(acc, axis=-1, keepdims=True)
        xc = acc - mu
        r = lax.rsqrt(jnp.mean(xc * xc, axis=-1, keepdims=True) + LN_EPS)
        nrm = xc * r * lg_ref[...] + lb_ref[...]
        o_ref[...] = (nrm * _sigmoid(nrm)).astype(MXU)

    vec = pl.BlockSpec((1, D), lambda i: (0, 0))
    return _pc(
        body, name="conv_fwd", grid=(s // tm,),
        in_specs=[pl.BlockSpec((tm, D), lambda i: (i, 3)), pl.BlockSpec((tm, D), lambda i: (i, 4)),
                  pl.BlockSpec((HALO, D), lambda i: (jnp.maximum(i * hb - 1, 0), 3)),
                  pl.BlockSpec((HALO, D), lambda i: (jnp.maximum(i * hb - 1, 0), 4)),
                  pl.BlockSpec((CWP, D), lambda i: (0, 0)), vec, vec, vec],
        out_specs=[pl.BlockSpec((tm, D), lambda i: (i, 0)), pl.BlockSpec((tm, D), lambda i: (i, 0))],
        out_shape=[_sds((s, D), F32), _sds((s, D), MXU)],
        scratch_shapes=[pltpu.VMEM((SUB, D // LANES, tm + HALO, LANES), F32)], compiler_params=_cp(1),
    )(proj, proj, proj, proj, cw, cb, lg, lb)


def _conv_bwd(proj, y, do, cw, lg, lb):
    s = proj.shape[0]
    tm = min(TM_CONV, s)
    hb = tm // HALO
    nt = s // tm
    last_hblk = s // HALO - 1

    def ln_bwd(yv, dov, lgv, lbv):
        mu = jnp.mean(yv, axis=-1, keepdims=True)
        xc = yv - mu
        r = lax.rsqrt(jnp.mean(xc * xc, axis=-1, keepdims=True) + LN_EPS)
        xh = xc * r
        nrm = xh * lgv + lbv
        sg = _sigmoid(nrm)
        dn = dov.astype(F32) * (sg * (1.0 + nrm * (1.0 - sg)))
        wv = dn * lgv
        dy = r * (wv - jnp.mean(wv, axis=-1, keepdims=True) - xh * jnp.mean(wv * xh, axis=-1, keepdims=True))
        return dy, dn, xh

    rcw = min(CONV_ROWS, tm)

    def body(a_ref, g_ref, ah_ref, gh_ref, y_ref, yn_ref, do_ref, don_ref, w_ref, lg_ref, lb_ref,
             da_ref, dg_ref, dw_ref, vec_ref, gsh, dysh, dwacc):
        i = pl.program_id(0)

        @pl.when(i == 0)
        def _():
            dwacc[...] = jnp.zeros_like(dwacc)
            vec_ref[...] = jnp.zeros_like(vec_ref)

        lgv, lbv = lg_ref[...], lb_ref[...]
        _store_blocked(gsh, 0, jnp.where(i > 0, _glu(ah_ref[...], gh_ref[...]), 0.0))
        _store_blocked(gsh, HALO, _glu(a_ref[...], g_ref[...]))
        _fill_shifted(gsh)
        dy, dn, xh = ln_bwd(y_ref[...], do_ref[...], lgv, lbv)
        dyn, _, _ = ln_bwd(yn_ref[...], don_ref[...], lgv, lbv)
        _store_blocked(dysh, 0, dy)
        _store_blocked(dysh, tm, jnp.where(i < nt - 1, dyn, 0.0))
        _fill_shifted(dysh)
        vec_ref[0:1, :] += jnp.sum(dn * xh, axis=0, keepdims=True)
        vec_ref[1:2, :] += jnp.sum(dn, axis=0, keepdims=True)
        vec_ref[2:3, :] += jnp.sum(dy, axis=0, keepdims=True)
        for c in range(D // LANES):
            cols = pl.ds(c * LANES, LANES)
            for rc in range(tm // rcw):
                rows = pl.ds(rc * rcw, rcw)
                dyc = dysh[0, c, rows, :]
                dgl = jnp.zeros((rcw, LANES), F32)
                for t in range(CW):
                    dgl = dgl + w_ref[t:t + 1, cols] * _shifted(dysh, CW - 1 - t + rc * rcw, rcw, c)
                    prod = dyc * _shifted(gsh, HALO - (CW - 1) + t + rc * rcw, rcw, c)
                    dwacc[t, :, cols] += jnp.sum(prod.reshape(rcw // SUB, SUB, LANES), axis=0)
                av = a_ref[rows, cols].astype(F32)
                sgate = _sigmoid(g_ref[rows, cols].astype(F32))
                da_ref[rows, cols] = (dgl * sgate).astype(MXU)
                dg_ref[rows, cols] = (dgl * av * sgate * (1.0 - sgate)).astype(MXU)

        @pl.when(i == nt - 1)
        def _():
            dw_ref[...] = jnp.sum(dwacc[...], axis=1)

    vec = pl.BlockSpec((1, D), lambda i: (0, 0))
    cur = lambda c: pl.BlockSpec((tm, D), lambda i: (i, c))
    prv = lambda c: pl.BlockSpec((HALO, D), lambda i: (jnp.maximum(i * hb - 1, 0), c))
    nxt = pl.BlockSpec((HALO, D), lambda i: (jnp.minimum((i + 1) * hb, last_hblk), 0))
    return _pc(
        body, name="conv_bwd", grid=(nt,),
        in_specs=[cur(3), cur(4), prv(3), prv(4), cur(0), nxt, cur(0), nxt,
                  pl.BlockSpec((CWP, D), lambda i: (0, 0)), vec, vec],
        out_specs=[cur(0), cur(0), pl.BlockSpec((CWP, D), lambda i: (0, 0)), pl.BlockSpec((8, D), lambda i: (0, 0))],
        out_shape=[_sds((s, D), MXU), _sds((s, D), MXU), _sds((CWP, D), F32), _sds((8, D), F32)],
        scratch_shapes=[pltpu.VMEM((SUB, D // LANES, tm + HALO, LANES), F32)] * 2 + [pltpu.VMEM((CWP, SUB, D), F32)],
        compiler_params=_cp(1),
    )(proj, proj, proj, proj, y, y, do, do, cw, lg, lb)


def _mem_kv(mem, g_mem, w_kv):
    mm = mem.shape[0]

    def body(m_ref, g_ref, w_ref, mn_ref, kv_ref):
        mv = m_ref[...]
        mn = (mv * _rms(mv) * g_ref[...]).astype(MXU)
        mn_ref[...] = mn
        kv_ref[...] = _dot(mn, w_ref[...]).astype(MXU)

    return _pc(body, name="mem_kv", out_shape=[_sds((mm, D), MXU), _sds((mm, 2 * D), MXU)],
               compiler_params=pltpu.CompilerParams(vmem_limit_bytes=VMEM_LIMIT))(mem, g_mem, w_kv)


def _mem_kv_bwd(mem, g_mem, w_kv, dkv):
    mm = mem.shape[0]

    def body(m_ref, w_ref, dkv_ref, o_ref):
        mv = m_ref[...]
        dmn = _dot_nt(dkv_ref[...], w_ref[...])
        o_ref[...] = jnp.broadcast_to(jnp.sum(dmn * mv * _rms(mv), axis=0, keepdims=True), o_ref.shape)

    return _pc(body, name="mem_kv_bwd", out_shape=_sds((8, D), F32),
               compiler_params=pltpu.CompilerParams(vmem_limit_bytes=VMEM_LIMIT))(mem, w_kv, dkv)


def _mem_attn_fwd(proj, kv):
    s = proj.shape[0]
    mm = kv.shape[0]
    tm = min(TM_PROJ, s)
    scale = MEM_DH ** -0.5

    def body(q_ref, kv_ref, o_ref):
        for h in range(MEM_H):
            cols = pl.ds(h * MEM_DH, MEM_DH)
            qh = q_ref[:, cols] * scale
            sc = _dot_nt(qh, kv_ref[:, cols])
            m = jnp.max(sc, axis=-1, keepdims=True)
            e = jnp.exp(sc - m)
            p = e / jnp.sum(e, axis=-1, keepdims=True)
            o_ref[:, cols] = _dot(p.astype(MXU), kv_ref[:, pl.ds(D + h * MEM_DH, MEM_DH)]).astype(MXU)

    return _pc(
        body, name="mem_attn_fwd", grid=(s // tm,),
        in_specs=[pl.BlockSpec((tm, D), lambda i: (i, 5)), pl.BlockSpec((mm, 2 * D), lambda i: (0, 0))],
        out_specs=pl.BlockSpec((tm, D), lambda i: (i, 0)), out_shape=_sds((s, D), MXU), compiler_params=_cp(1),
    )(proj, kv)


def _mem_attn_bwd(proj, kv, do):
    s = proj.shape[0]
    mm = kv.shape[0]
    tm = min(TM_PROJ, s)
    scale = MEM_DH ** -0.5

    def body(q_ref, kv_ref, do_ref, dq_ref, dkv_ref):
        @pl.when(pl.program_id(0) == 0)
        def _():
            dkv_ref[...] = jnp.zeros_like(dkv_ref)

        for h in range(MEM_H):
            cols = pl.ds(h * MEM_DH, MEM_DH)
            vcols = pl.ds(D + h * MEM_DH, MEM_DH)
            qh = q_ref[:, cols]
            kh = kv_ref[:, cols] * scale
            doh = do_ref[:, cols]
            st = _dot_nt(kh, qh)
            m = jnp.max(st, axis=0, keepdims=True)
            e = jnp.exp(st - m)
            pt = e / jnp.sum(e, axis=0, keepdims=True)
            dpt = _dot_nt(kv_ref[:, vcols], doh)
            dst = pt * (dpt - jnp.sum(pt * dpt, axis=0, keepdims=True))
            dsb = dst.astype(MXU)
            dkv_ref[:, vcols] += _dot(pt.astype(MXU), doh)
            dkv_ref[:, cols] += _dot(dsb, qh) * scale
            dq_ref[:, cols] = _dot_tn(dsb, kh).astype(MXU)

    return _pc(
        body, name="mem_attn_bwd", grid=(s // tm,),
        in_specs=[pl.BlockSpec((tm, D), lambda i: (i, 5)), pl.BlockSpec((mm, 2 * D), lambda i: (0, 0)),
                  pl.BlockSpec((tm, D), lambda i: (i, 0))],
        out_specs=[pl.BlockSpec((tm, D), lambda i: (i, 0)), pl.BlockSpec((mm, 2 * D), lambda i: (0, 0))],
        out_shape=[_sds((s, D), MXU), _sds((mm, 2 * D), F32)], compiler_params=_cp(1),
    )(proj, kv, do)


def _resident(n):
    return [pltpu.VMEM((n, D, D), MXU), pltpu.SemaphoreType.DMA((n,))]


def _load_resident(hbm_refs, wbuf, sems):
    @pl.when(pl.program_id(0) == 0)
    def _():
        cps = [pltpu.make_async_copy(r, wbuf.at[k], sems.at[k]) for k, r in enumerate(hbm_refs)]
        for cp in cps:
            cp.start()
        for cp in cps:
            cp.wait()


def _merge_out(oa, oc, om, proj, x, wpa, wpc, wpm, wout, g_post, g_fpre):
    s = x.shape[0]
    tm = min(TM_ROW, s)

    def body(oa_ref, oc_ref, om_ref, gl_ref, x_ref, gp_ref, gf_ref, wpa_h, wpc_h, wpm_h, wout_h,
             pa_ref, pc_ref, pm_ref, mg_ref, z_ref, x1_ref, h2_ref, wbuf, sems):
        _load_resident([wpa_h, wpc_h, wpm_h, wout_h], wbuf, sems)
        merged = jnp.zeros((tm, D), F32)
        for b, (o_ref, p_ref) in enumerate(((oa_ref, pa_ref), (oc_ref, pc_ref), (om_ref, pm_ref))):
            pb = _dot(o_ref[...], wbuf[b])
            p_ref[...] = pb.astype(MXU)
            merged = merged + _sigmoid(gl_ref[:, pl.ds(b * D, D)].astype(F32)) * pb
        mg = merged.astype(MXU)
        mg_ref[...] = mg
        z = _dot(mg, wbuf[3])
        z_ref[...] = z
        x1 = x_ref[...] + z * _rms(z) * gp_ref[...]
        x1_ref[...] = x1
        h2_ref[...] = (x1 * _rms(x1) * gf_ref[...]).astype(MXU)

    rows = pl.BlockSpec((tm, D), lambda i: (i, 0))
    vec = pl.BlockSpec((1, D), lambda i: (0, 0))
    anyspec = pl.BlockSpec(memory_space=pl.ANY)
    return _pc(
        body, name="merge_out", grid=(s // tm,),
        in_specs=[rows, rows, rows, pl.BlockSpec((tm, 3 * D), lambda i: (i, 2)), rows, vec, vec,
                  anyspec, anyspec, anyspec, anyspec],
        out_specs=[rows] * 7,
        out_shape=[_sds((s, D), MXU)] * 4 + [_sds((s, D), F32)] * 2 + [_sds((s, D), MXU)],
        scratch_shapes=_resident(4), compiler_params=_cp(1),
    )(oa, oc, om, proj, x, g_post, g_fpre, wpa, wpc, wpm, wout)


def _ffn_up(h2, w_gu):
    s = h2.shape[0]
    tm = min(TM_PROJ, s)
    nb = 2
    bw = FFN // nb

    def body(h_ref, wg_ref, wu_ref, gf_ref, uf_ref, act_ref):
        hv = h_ref[...]
        gf = _dot(hv, wg_ref[...])
        uf = _dot(hv, wu_ref[...])
        gf_ref[...] = gf.astype(MXU)
        uf_ref[...] = uf.astype(MXU)
        act_ref[...] = (gf * _sigmoid(gf) * uf).astype(MXU)

    out = pl.BlockSpec((tm, bw), lambda i, j: (i, j))
    return _pc(
        body, name="ffn_up", grid=(s // tm, nb),
        in_specs=[pl.BlockSpec((tm, D), lambda i, j: (i, 0)), pl.BlockSpec((D, bw), lambda i, j: (0, j)),
                  pl.BlockSpec((D, bw), lambda i, j: (0, nb + j))],
        out_specs=[out, out, out], out_shape=[_sds((s, FFN), MXU)] * 3, compiler_params=_cp(2),
    )(h2, w_gu, w_gu)


def _ffn_down_loss(act, w_d, x1, tgt, g_fpost):
    s = act.shape[0]
    tm = min(TM_PROJ, s)

    def body(a_ref, w_ref, x1_ref, t_ref, g_ref, dffn_ref, dy_ref, vec_ref, loss_ref):
        @pl.when(pl.program_id(0) == 0)
        def _():
            vec_ref[...] = jnp.zeros_like(vec_ref)
            loss_ref[...] = jnp.zeros_like(loss_ref)

        ffn = _dot(a_ref[...], w_ref[...])
        r = _rms(ffn)
        gv = g_ref[...]
        e = x1_ref[...] + ffn * r * gv - t_ref[...]
        loss_ref[...] += jnp.sum(e * e) * (0.5 / D)
        dy = e * (1.0 / D)
        dy_ref[...] = dy
        vec_ref[0:1, :] += jnp.sum(dy * ffn * r, axis=0, keepdims=True)
        dffn_ref[...] = _rms_bwd(ffn, r, gv, dy).astype(MXU)

    rows = pl.BlockSpec((tm, D), lambda i: (i, 0))
    return _pc(
        body, name="ffn_down_loss", grid=(s // tm,),
        in_specs=[pl.BlockSpec((tm, FFN), lambda i: (i, 0)), pl.BlockSpec((FFN, D), lambda i: (0, 0)), rows, rows,
                  pl.BlockSpec((1, D), lambda i: (0, 0))],
        out_specs=[rows, rows, pl.BlockSpec((8, D), lambda i: (0, 0)), pl.BlockSpec((8, LANES), lambda i: (0, 0))],
        out_shape=[_sds((s, D), MXU), _sds((s, D), F32), _sds((8, D), F32), _sds((8, LANES), F32)],
        compiler_params=_cp(1),
    )(act, w_d, x1, tgt, g_fpost)


def _ffn_down_bwd(dffn, w_d, gf, uf):
    s = dffn.shape[0]
    tm = min(TM_ROW, s)

    def body(d_ref, w_ref, gf_ref, uf_ref, dgf_ref, duf_ref):
        da = _dot_nt(d_ref[...], w_ref[...]).astype(MXU)
        gf = gf_ref[...]
        one = jnp.ones_like(gf)
        sg = one / (one + jnp.exp(-gf))
        silu = gf * sg
        duf_ref[...] = da * silu
        dgf_ref[...] = da * (uf_ref[...] * (sg + silu * (one - sg)))

    wide = pl.BlockSpec((tm, FFN), lambda i: (i, 0))
    return _pc(
        body, name="ffn_down_bwd", grid=(s // tm,),
        in_specs=[pl.BlockSpec((tm, D), lambda i: (i, 0)), pl.BlockSpec((FFN, D), lambda i: (0, 0)), wide, wide],
        out_specs=[wide, wide], out_shape=[_sds((s, FFN), MXU)] * 2, compiler_params=_cp(1),
    )(dffn, w_d, gf, uf)


def _ffn_up_bwd(dgf, duf, w_gu, x1, dy, z, g_fpre, g_post):
    s = x1.shape[0]
    tm = min(TM_ROW, s)

    def body(dgf_ref, duf_ref, w_ref, x1_ref, dy_ref, z_ref, gf_ref, gp_ref, dx1_ref, dz_ref, vec_ref):
        @pl.when(pl.program_id(0) == 0)
        def _():
            vec_ref[...] = jnp.zeros_like(vec_ref)

        dh2 = _dot_nt(dgf_ref[...], w_ref[:, pl.ds(0, FFN)]) + _dot_nt(duf_ref[...], w_ref[:, pl.ds(FFN, FFN)])
        x1 = x1_ref[...]
        r2 = _rms(x1)
        vec_ref[0:1, :] += jnp.sum(dh2 * x1 * r2, axis=0, keepdims=True)
        dx1 = dy_ref[...] + _rms_bwd(x1, r2, gf_ref[...], dh2)
        dx1_ref[...] = dx1
        z = z_ref[...]
        rz = _rms(z)
        vec_ref[1:2, :] += jnp.sum(dx1 * z * rz, axis=0, keepdims=True)
        dz_ref[...] = _rms_bwd(z, rz, gp_ref[...], dx1).astype(MXU)

    rows = pl.BlockSpec((tm, D), lambda i: (i, 0))
    wide = pl.BlockSpec((tm, FFN), lambda i: (i, 0))
    vec = pl.BlockSpec((1, D), lambda i: (0, 0))
    return _pc(
        body, name="ffn_up_bwd", grid=(s // tm,),
        in_specs=[wide, wide, pl.BlockSpec((D, 2 * FFN), lambda i: (0, 0)), rows, rows, rows, vec, vec],
        out_specs=[rows, rows, pl.BlockSpec((8, D), lambda i: (0, 0))],
        out_shape=[_sds((s, D), F32), _sds((s, D), MXU), _sds((8, D), F32)], compiler_params=_cp(1),
    )(dgf, duf, w_gu, x1, dy, z, g_fpre, g_post)


def _merge_bwd(dz, proj, pa, pc, pm, oa, wpa, wpc, wpm, wout):
    s = dz.shape[0]
    tm = min(TM_ROW, s)

    def body(dz_ref, gl_ref, pa_ref, pc_ref, pm_ref, oa_ref, wpa_h, wpc_h, wpm_h, wout_h,
             dpa_ref, dpc_ref, dpm_ref, dgl_ref, doa_ref, doc_ref, dom_ref, dl_ref, wbuf, sems):
        _load_resident([wpa_h, wpc_h, wpm_h, wout_h], wbuf, sems)
        dm = _dot_nt(dz_ref[...], wbuf[3])
        quads = ((pa_ref, dpa_ref, doa_ref), (pc_ref, dpc_ref, doc_ref), (pm_ref, dpm_ref, dom_ref))
        for b, (p_ref, dp_ref, do_ref) in enumerate(quads):
            cols = pl.ds(b * D, D)
            gt = _sigmoid(gl_ref[:, cols].astype(F32))
            dp = (dm * gt).astype(MXU)
            dp_ref[...] = dp
            dgl_ref[:, cols] = (dm * p_ref[...].astype(F32) * gt * (1.0 - gt)).astype(MXU)
            dob = _dot_nt(dp, wbuf[b]).astype(MXU)
            do_ref[...] = dob
            if b == 0:
                d_i = lax.broadcasted_iota(jnp.int32, (D, LANES), 0)
                h_i = lax.broadcasted_iota(jnp.int32, (D, LANES), 1)
                sel = jnp.where(lax.shift_right_logical(d_i, DH.bit_length() - 1) == h_i, 1.0, 0.0).astype(MXU)
                dl_ref[...] = _dot(dob * oa_ref[...], sel)

    rows = pl.BlockSpec((tm, D), lambda i: (i, 0))
    anyspec = pl.BlockSpec(memory_space=pl.ANY)
    wide = pl.BlockSpec((tm, 3 * D), lambda i: (i, 2))
    return _pc(
        body, name="merge_bwd", grid=(s // tm,),
        in_specs=[rows, wide, rows, rows, rows, rows, anyspec, anyspec, anyspec, anyspec],
        out_specs=[rows, rows, rows, pl.BlockSpec((tm, 3 * D), lambda i: (i, 0)), rows, rows, rows,
                   pl.BlockSpec((tm, LANES), lambda i: (i, 0))],
        out_shape=[_sds((s, D), MXU)] * 3 + [_sds((s, 3 * D), MXU)] + [_sds((s, D), MXU)] * 3 + [_sds((s, LANES), F32)],
        scratch_shapes=_resident(4), compiler_params=_cp(1),
    )(dz, proj, pa, pc, pm, oa, wpa, wpc, wpm, wout)


def _in_proj_bwd(pieces, df, w_main, w_f, x, dx1, g_pre, plan):
    s = x.shape[0]
    tm = min(TM_ROW, s)
    n_main = w_main.shape[1]
    np_ = len(pieces)
    nx = len(plan["arrays"])
    n_in = np_ + 6

    def body(*refs):
        p_refs = refs[:np_]
        df_ref, x_ref, dx1_ref, g_ref, w_h, wf_ref = refs[np_:n_in]
        dx_ref, vec_ref = refs[n_in + nx:n_in + nx + 2]
        wbuf, sem = refs[n_in + 2 * nx + 2:n_in + 2 * nx + 4]
        _host_plan(plan, refs[n_in:n_in + nx], refs[n_in + nx + 2:n_in + 2 * nx + 2], refs[n_in + 2 * nx + 4:],
                   pl.program_id(0) == 0, None, pl.program_id(0) == s // tm - 1)

        @pl.when(pl.program_id(0) == 0)
        def _():
            vec_ref[...] = jnp.zeros_like(vec_ref)
            cp = pltpu.make_async_copy(w_h, wbuf, sem)
            cp.start()
            cp.wait()

        dh = _dot_nt(df_ref[...], wf_ref[...])
        for p_ref, (_, c0, nc) in zip(p_refs, pieces):
            dh = dh + _dot_nt(p_ref[...], wbuf[:, pl.ds(c0 * D, nc * D)])
        xv = x_ref[...]
        r = _rms(xv)
        vec_ref[0:1, :] += jnp.sum(dh * xv * r, axis=0, keepdims=True)
        dx_ref[...] = dx1_ref[...] + _rms_bwd(xv, r, g_ref[...], dh)

    rows = pl.BlockSpec((tm, D), lambda i: (i, 0))
    p_specs = [pl.BlockSpec((tm, nc * D), lambda i: (i, 0)) for _, _, nc in pieces]
    return _pc(
        body, name="in_proj_bwd", grid=(s // tm,),
        in_specs=p_specs + [pl.BlockSpec((tm, LANES), lambda i: (i, 0)), rows, rows, pl.BlockSpec((1, D), lambda i: (0, 0)),
                            pl.BlockSpec(memory_space=pl.ANY), pl.BlockSpec((D, LANES), lambda i: (0, 0))] + [ANY] * nx,
        out_specs=[rows, pl.BlockSpec((8, D), lambda i: (0, 0))] + [ANY] * nx,
        out_shape=[_sds((s, D), F32), _sds((8, D), F32)] + plan["out_shape"],
        scratch_shapes=[pltpu.VMEM((D, n_main), MXU), pltpu.SemaphoreType.DMA] + plan["scratch"], compiler_params=_cp(1),
    )(*[p for p, _, _ in pieces], df, x, dx1, g_pre, w_main, w_f, *plan["arrays"])


def _wgrad(xa, dy, name):
    s, k = xa.shape
    n = dy.shape[1]
    ts = min(TS_WG, s)
    tk = _tile(k, WG_CAP)
    tn = _tile(n, WG_CAP)

    def body(x_ref, dy_ref, o_ref, acc):
        @pl.when(pl.program_id(2) == 0)
        def _():
            acc[...] = jnp.zeros_like(acc)

        acc[...] += _dot_tn(x_ref[...], dy_ref[...])

        @pl.when(pl.program_id(2) == s // ts - 1)
        def _():
            o_ref[...] = acc[...].astype(WIRE)

    return _pc(
        body, name=name, grid=(k // tk, n // tn, s // ts),
        in_specs=[pl.BlockSpec((ts, tk), lambda a, b, c: (c, a)), pl.BlockSpec((ts, tn), lambda a, b, c: (c, b))],
        out_specs=pl.BlockSpec((tk, tn), lambda a, b, c: (a, b)), out_shape=_sds((k, n), WIRE),
        scratch_shapes=[pltpu.VMEM((tk, tn), F32)], compiler_params=_cp(3),
    )(xa, dy)


def _pair_sum(g, r1, c_idx):
    _, _, hr, cols = g.shape
    tr = _rowtile(hr, cols)

    def body(c_ref, g_ref, r_ref, o_ref):
        o_ref[0] = (g_ref[0, 0].astype(F32) + r_ref[0].astype(F32)).astype(WIRE)

    return _pc(
        body, name="pair_sum_%dx%d" % (hr, cols), out_shape=_sds((N_CHIPS, hr, cols), WIRE),
        grid_spec=pltpu.PrefetchScalarGridSpec(
            num_scalar_prefetch=1, grid=(N_CHIPS, hr // tr),
            in_specs=[pl.BlockSpec((1, 1, tr, cols), lambda d, i, c: (d, c[0], i, 0)),
                      pl.BlockSpec((1, tr, cols), lambda d, i, c: (d, i, 0))],
            out_specs=pl.BlockSpec((1, tr, cols), lambda d, i, c: (d, i, 0))),
        compiler_params=_cp(2),
    )(c_idx, g, r1)


def _chip_sum(r2, slot, base=None):
    _, hr, cols = r2.shape
    tr = _rowtile(hr, cols)

    def body(s_ref, r_ref, *rest):
        o_ref = rest[-1]
        acc = r_ref[0].astype(F32)
        for d in range(1, N_CHIPS):
            acc = acc + r_ref[d].astype(F32)
        o_ref[0] = acc

    based = base is not None
    return _pc(
        body, name="chip_sum_%dx%d_%d" % (hr, cols, int(based)), out_shape=_sds((2, hr, cols), F32),
        grid_spec=pltpu.PrefetchScalarGridSpec(
            num_scalar_prefetch=1, grid=(hr // tr,),
            in_specs=[pl.BlockSpec((N_CHIPS, tr, cols), lambda i, s: (0, i, 0))] + ([ANY] if based else []),
            out_specs=pl.BlockSpec((1, tr, cols), lambda i, s: (s[0], i, 0))),
        input_output_aliases={2: 0} if based else {}, compiler_params=_cp(1),
    )(*((slot, r2, base) if based else (slot, r2)))


def _adamw(w, g, m, v):
    rows, cols = w.shape
    tr = _rowtile(rows, cols)
    c1 = 1.0 / (1.0 - B1 ** STEP)
    c2 = 1.0 / (1.0 - B2 ** STEP)

    def body(w_ref, g_ref, m_ref, v_ref, d_ref, mo_ref, vo_ref):
        gv = g_ref[...]
        mn = B1 * m_ref[...] + (1.0 - B1) * gv
        vn = B2 * v_ref[...] + (1.0 - B2) * (gv * gv)
        mo_ref[...] = mn
        vo_ref[...] = vn
        d_ref[...] = -LR * ((mn * c1) / (jnp.sqrt(vn * c2) + ADAM_EPS) + WD * w_ref[...])

    blk = pl.BlockSpec((tr, cols), lambda i: (i, 0))
    return _pc(
        body, name="adamw_%dx%d" % (rows, cols), grid=(rows // tr,), in_specs=[blk] * 4, out_specs=[blk] * 3,
        out_shape=[_sds((rows, cols), F32)] * 3, compiler_params=_cp(1),
    )(w, g, m, v)


MESH_ID = pl.DeviceIdType.MESH
ANY = pl.BlockSpec(memory_space=pl.ANY)


def _place():
    x, y, c = lax.axis_index("x"), lax.axis_index("y"), lax.axis_index("c")
    others = [(1 - x, y), (x, 1 - y), (1 - x, 1 - y)]
    return x, y, c, others


def _remote(src, dst, sems, idx, to):
    return pltpu.make_async_remote_copy(src_ref=src, dst_ref=dst, send_sem=sems[0].at[idx], recv_sem=sems[1].at[idx],
                                        device_id=to, device_id_type=MESH_ID)


def _gather_plan(shards):
    nk = len(shards)

    def copies(ins, outs, sems):
        x, y, c, others = _place()
        me = 2 * x + y
        sib = (x, y, 1 - c)
        local = [pltpu.make_async_copy(ins[k], outs[k].at[me], sems[2].at[k]) for k in range(nk)]
        ici, landed, fwd, fwd_landed = [], [], [], []
        for k in range(nk):
            hr = shards[k].shape[0] // 2
            for r, (cx, cy) in enumerate(others):
                mine = pl.ds(c * hr, hr)
                ici.append(_remote(ins[k].at[mine], outs[k].at[me, mine], sems, 6 * k + r, (cx, cy, c)))
                got = outs[k].at[2 * cx + cy, mine]
                landed.append(_remote(got, got, sems, 6 * k + r, (cx, cy, c)))
                fwd.append(_remote(got, got, sems, 6 * k + 3 + r, sib))
                theirs = outs[k].at[2 * cx + cy, pl.ds((1 - c) * hr, hr)]
                fwd_landed.append(_remote(theirs, theirs, sems, 6 * k + 3 + r, sib))
        return local, ici, landed, fwd, fwd_landed

    def start(ins, outs, sems):
        local, ici, _, _, _ = copies(ins, outs, sems)
        for cp in local + ici:
            cp.start()

    def forward(ins, outs, sems):
        _, _, landed, fwd, _ = copies(ins, outs, sems)
        for got, cp in zip(landed, fwd):
            got.wait_recv()
            cp.start()

    def finish(ins, outs, sems):
        local, ici, _, fwd, fwd_landed = copies(ins, outs, sems)
        for got in fwd_landed:
            got.wait_recv()
        for cp in ici + fwd:
            cp.wait_send()
        for cp in local:
            cp.wait()

    return dict(
        arrays=list(shards), out_shape=[_sds((N_CHIPS,) + a.shape, a.dtype) for a in shards],
        scratch=[pltpu.SemaphoreType.DMA((6 * nk,)), pltpu.SemaphoreType.DMA((6 * nk,)), pltpu.SemaphoreType.DMA((nk,))],
        phases=[start, forward, finish])


def _scatter_plan(ps):
    nk = len(ps)

    def copies(ins, outs, sems):
        x, y, c, others = _place()
        me = 2 * x + y
        local = [pltpu.make_async_copy(ins[k].at[me], outs[k].at[me], sems[2].at[k]) for k in range(nk)]
        ici, landed = [], []
        for k in range(nk):
            for r, (cx, cy) in enumerate(others):
                ici.append(_remote(ins[k].at[2 * cx + cy], outs[k].at[me], sems, 3 * k + r, (cx, cy, c)))
                got = outs[k].at[2 * cx + cy]
                landed.append(_remote(got, got, sems, 3 * k + r, (cx, cy, c)))
        return local, ici, landed

    def start(ins, outs, sems):
        local, ici, _ = copies(ins, outs, sems)
        for cp in local + ici:
            cp.start()

    def finish(ins, outs, sems):
        local, ici, landed = copies(ins, outs, sems)
        for got in landed:
            got.wait_recv()
        for cp in ici:
            cp.wait_send()
        for cp in local:
            cp.wait()

    return dict(
        arrays=list(ps), out_shape=[_sds(a.shape, a.dtype) for a in ps],
        scratch=[pltpu.SemaphoreType.DMA((3 * nk,)), pltpu.SemaphoreType.DMA((3 * nk,)), pltpu.SemaphoreType.DMA((nk,))],
        phases=[start, finish])


def _swap_plan(arrs):
    nk = len(arrs)

    def copies(ins, outs, sems):
        x, y, c, _ = _place()
        return [_remote(ins[k], outs[k], sems, k, (x, y, 1 - c)) for k in range(nk)]

    def start(ins, outs, sems):
        for cp in copies(ins, outs, sems):
            cp.start()

    def finish(ins, outs, sems):
        for cp in copies(ins, outs, sems):
            cp.wait()

    return dict(arrays=list(arrs), out_shape=[_sds(a.shape, a.dtype) for a in arrs],
                scratch=[pltpu.SemaphoreType.DMA((nk,)), pltpu.SemaphoreType.DMA((nk,))], phases=[start, finish])


def _join_plans(a, b):
    na, sa = len(a["arrays"]), len(a["scratch"])

    def phase(k):
        def run(ins, outs, sems):
            a["phases"][k](ins[:na], outs[:na], sems[:sa])
            b["phases"][k](ins[na:], outs[na:], sems[sa:])
        return run

    return dict(arrays=a["arrays"] + b["arrays"], out_shape=a["out_shape"] + b["out_shape"],
                scratch=a["scratch"] + b["scratch"], phases=[phase(0), phase(1)])


def _run_plan(plan, name):
    nk = len(plan["arrays"])

    def body(*refs):
        ins, outs, sems = refs[:nk], refs[nk:2 * nk], refs[2 * nk:]
        for phase in plan["phases"]:
            phase(ins, outs, sems)

    return _pc(body, name=name, in_specs=[ANY] * nk, out_specs=[ANY] * nk, out_shape=plan["out_shape"],
               scratch_shapes=plan["scratch"])(*plan["arrays"])


def _host_plan(plan, ins, outs, sems, first, middle, last):
    points = [first, last] if len(plan["phases"]) == 2 else [first, middle, last]
    for phase, at in zip(plan["phases"], points):
        pl.when(at)(functools.partial(phase, ins, outs, sems))


def _swap_sibling(gs, halves, tag):
    nk = len(gs)

    def body(*refs):
        ins, outs = refs[:nk], refs[nk:2 * nk]
        send_sems, recv_sems = refs[2 * nk:]
        x, y, c, _ = _place()
        cps = []
        for k in range(nk):
            hr = gs[k].shape[1] // 2
            cp = pltpu.make_async_remote_copy(
                src_ref=ins[k].at[:, pl.ds((1 - c) * hr, hr)] if halves else ins[k], dst_ref=outs[k],
                send_sem=send_sems.at[k], recv_sem=recv_sems.at[k], device_id=(x, y, 1 - c), device_id_type=MESH_ID)
            cp.start()
            cps.append(cp)
        for cp in cps:
            cp.wait()

    return _pc(
        body, name=("swap_halves_" if halves else "swap_slabs_") + tag, in_specs=[ANY] * nk, out_specs=[ANY] * nk,
        out_shape=[_sds((N_CHIPS, a.shape[1] // 2 if halves else a.shape[1], a.shape[2]), a.dtype) for a in gs],
        scratch_shapes=[pltpu.SemaphoreType.DMA((nk,)), pltpu.SemaphoreType.DMA((nk,))],
    )(*gs)


def _allreduce_small(v):
    rows, cols = v.shape

    def body(v_ref, o_ref, gath, send_sems, recv_sems):
        x, y, c, others = _place()
        sib = (x, y, 1 - c)

        def slot(px, py, pc):
            return gath.at[4 * px + 2 * py + pc]

        def copy(k, block, to, src=None):
            return pltpu.make_async_remote_copy(
                src_ref=slot(*block) if src is None else src, dst_ref=slot(*block),
                send_sem=send_sems.at[k], recv_sem=recv_sems.at[k], device_id=to, device_id_type=MESH_ID)

        me = (x, y, c)
        gath[4 * x + 2 * y + c] = v_ref[...]
        first = [copy(0, me, sib, src=v_ref)]
        first += [copy(1 + r, me, (cx, cy, c), src=v_ref) for r, (cx, cy) in enumerate(others)]
        for cp in first:
            cp.start()
        passed = [copy(4 + r, (cx, cy, c), sib) for r, (cx, cy) in enumerate(others)]
        for r, (cx, cy) in enumerate(others):
            copy(1 + r, (cx, cy, c), me).wait_recv()
            passed[r].start()
        copy(0, (x, y, 1 - c), me).wait_recv()
        for r, (cx, cy) in enumerate(others):
            copy(4 + r, (cx, cy, 1 - c), me).wait_recv()
        for cp in first + passed:
            cp.wait_send()
        acc = gath[0]
        for d in range(1, N_DEV):
            acc = acc + gath[d]
        o_ref[...] = acc

    vm = pl.BlockSpec(memory_space=pltpu.VMEM)
    return _pc(
        body, name="allreduce_small", in_specs=[vm], out_specs=vm, out_shape=_sds((rows, cols), F32),
        scratch_shapes=[pltpu.VMEM((N_DEV, rows, cols), F32), pltpu.SemaphoreType.DMA((7,)), pltpu.SemaphoreType.DMA((7,))],
    )(v)


def _cols_to_chips(a):
    r, c4 = a.shape
    return a.reshape(r, N_CHIPS, c4 // N_CHIPS).transpose(1, 0, 2)


def _chips_to_cols(a):
    n, r, c = a.shape
    return a.transpose(1, 0, 2).reshape(r, n * c)


def _head_rows(a, tq):
    s = a.shape[0]
    return a[:, :HF].reshape(s // tq, tq, NP, 2).transpose(2, 0, 3, 1)


def _head_cols(a):
    s = a.shape[0]
    return a[:, :HF].reshape(s, NP, 2).transpose(1, 0, 2)


def _to_wire(gs, c_idx, tag):
    gs = [g.astype(WIRE) for g in gs]
    r1 = _swap_sibling(gs, True, tag)
    return [_pair_sum(g.reshape(N_CHIPS, 2, g.shape[1] // 2, g.shape[2]), r, c_idx) for g, r in zip(gs, r1)]


def _local_step(x, mem, tgt, sp, w_main, w_f, rest_shards, c_idx):
    s = x.shape[0]
    tq = min(TQ, s)
    b_f = jnp.pad(sp["b_forget"], ((0, 0), (0, LANES - HF)))
    proj, h, flog = _in_proj(x, sp["norm_mix_pre"], w_main, w_f)
    cf = _logf_cumsum(flog, b_f)
    qx, kx = _fox_prep(proj, _head_cols(cf))
    oa, lse, g_cw, g_kv, g_pj, g_gu, g_d = _fox_fwd(proj, qx, kx, _gather_plan(rest_shards))
    pj = g_pj.reshape(N_CHIPS, 4, D // N_CHIPS, D).transpose(1, 0, 2, 3).reshape(4, D, D)
    w = {"conv_w": _chips_to_cols(g_cw), "w_kv": _chips_to_cols(g_kv), "wpa": pj[0], "wpc": pj[1], "wpm": pj[2],
         "wout": pj[3], "w_gu": _chips_to_cols(g_gu), "w_d": g_d.reshape(FFN, D)}
    y, oc = _conv_fwd(proj, w["conv_w"], sp["conv_b"], sp["conv_ln_g"], sp["conv_ln_b"])
    mem_n, kv = _mem_kv(mem, sp["norm_mem"], w["w_kv"])
    om = _mem_attn_fwd(proj, kv)
    pa, pc, pm, merged, z, x1, h2 = _merge_out(oa, oc, om, proj, x, w["wpa"], w["wpc"], w["wpm"], w["wout"],
                                              sp["norm_mix_post"], sp["norm_ffn_pre"])
    gf, uf, act = _ffn_up(h2, w["w_gu"])
    dffn, dy, vec_f, loss_blk = _ffn_down_loss(act, w["w_d"], x1, tgt, sp["norm_ffn_post"])

    dgf, duf = _ffn_down_bwd(dffn, w["w_d"], gf, uf)
    dx1, dz, vec_n = _ffn_up_bwd(dgf, duf, w["w_gu"], x1, dy, z, sp["norm_ffn_pre"], sp["norm_mix_post"])
    dpa, dpc, dpm, dgl, doa, doc, dom, delta = _merge_bwd(dz, proj, pa, pc, pm, oa, w["wpa"], w["wpc"], w["wpm"], w["wout"])
    pj_g = jnp.stack([_wgrad(oa, dpa, "wgrad_pa"), _wgrad(oc, dpc, "wgrad_pc"), _wgrad(om, dpm, "wgrad_pm"),
                      _wgrad(merged, dz, "wgrad_out")]).reshape(4, N_CHIPS, D // N_CHIPS, D)
    early = [pj_g.transpose(1, 0, 2, 3).reshape(N_CHIPS, D, D),
             _cols_to_chips(jnp.concatenate([_wgrad(h2, dgf, "wgrad_g"), _wgrad(h2, duf, "wgrad_u")], axis=1)),
             _wgrad(act, dffn, "wgrad_d").reshape(N_CHIPS, FFN // N_CHIPS, D)]
    lse16 = lse.transpose(1, 0, 2).reshape(s, HF)
    dq, dqs, dk, dks, dv, *r2_early = _fox_bwd(proj, qx, kx, doa, _head_rows(lse16, tq), _head_rows(delta, tq),
                                              _scatter_plan(_to_wire(early, c_idx, "early")))
    over_keys = dqs[:, X_KONE:X_KONE + 2, :].transpose(2, 0, 1).reshape(s, HF)
    over_queries = dks[:, :, X_QONE:X_QONE + 2, :].transpose(1, 3, 0, 2).reshape(s, HF)
    dc = jnp.pad(over_keys - over_queries, ((0, 0), (0, LANES - HF)))
    df, db_blk = _logf_cumsum_bwd(flog, b_f, dc)
    dga, dgg, dcw, vec_c = _conv_bwd(proj, y, doc, w["conv_w"], sp["conv_ln_g"], sp["conv_ln_b"])
    dqm, dkv = _mem_attn_bwd(proj, kv, dom)
    dkv_b = dkv.astype(MXU)
    vec_m = _mem_kv_bwd(mem, sp["norm_mem"], w["w_kv"], dkv_b)
    pieces = [(dq, 0, 1), (dk, 1, 1), (dv, 2, 1), (dga, 3, 1), (dgg, 4, 1), (dqm, 5, 1), (dgl, 6, 3)]
    dw_cols = [_wgrad(h, p, "wgrad_in_%d" % c0) for p, c0, _ in pieces]
    dwf = _wgrad(h, df, "wgrad_in_f")
    late = [_cols_to_chips(jnp.concatenate(dw_cols[:3] + [dwf[:, :HF]] + dw_cols[3:], axis=1)), _cols_to_chips(dcw),
            _cols_to_chips(_wgrad(mem_n, dkv_b, "wgrad_kv"))]
    dx, vec_p, *landed = _in_proj_bwd(
        pieces, df, w_main, w_f, x, dx1, sp["norm_mix_pre"],
        _join_plans(_scatter_plan(_to_wire(late, c_idx, "late")), _swap_plan(r2_early)))
    r2_late, sib_early = landed[:len(late)], landed[len(late):]
    zero_row = jnp.zeros((1, D), F32)
    small = jnp.concatenate([
        vec_p[0:1], vec_n[1:2], vec_m[0:1], vec_c[2:3], vec_c[0:1], vec_c[1:2], vec_n[0:1], vec_f[0:1],
        jnp.pad(db_blk[0:1, :HF], ((0, 0), (0, D - HF))),
        jnp.pad(loss_blk[0:1, 0:1], ((0, 0), (0, D - 1))),
    ] + [zero_row] * (SMALL_ROWS - 10), axis=0)
    return dx, list(r2_late) + list(r2_early), list(sib_early), small


SMALL_NAMES = ["norm_mix_pre", "norm_mix_post", "norm_mem", "conv_b", "conv_ln_g", "conv_ln_b", "norm_ffn_pre", "norm_ffn_post"]
PROJ_NAMES = ["w_proj_attn", "w_proj_conv", "w_proj_mem", "w_out"]
WEIGHT_ORDER = ["norm_mix_pre", "norm_mix_post", "norm_mem", "w_in", "b_forget", "conv_w", "conv_b", "conv_ln_g", "conv_ln_b",
                "w_kv_mem", "w_proj_attn", "w_proj_conv", "w_proj_mem", "w_out", "norm_ffn_pre", "norm_ffn_post",
                "w_gate_up", "w_down"]


def _pack_small(p):
    rows = [p[n] for n in SMALL_NAMES] + [jnp.pad(p["b_forget"], ((0, 0), (0, D - HF)))]
    return jnp.concatenate(rows + [jnp.zeros((SMALL_ROWS - len(rows), D), F32)], axis=0)


def _step(params, moms, vels, x, mem, tgt):
    c_idx = lax.axis_index("c").astype(jnp.int32).reshape(1)

    (g_in,) = _run_plan(_gather_plan([params["w_in"].astype(WIRE)]), "gather_w_in")
    w_in_full = _chips_to_cols(g_in)
    w_main = jnp.concatenate([w_in_full[:, :3 * D], w_in_full[:, 3 * D + HF:]], axis=1)
    w_f = jnp.pad(w_in_full[:, 3 * D:3 * D + HF], ((0, 0), (0, LANES - HF)))
    rest = [jnp.pad(params["conv_w"], ((0, CWP - CW), (0, 0))), params["w_kv_mem"].astype(WIRE),
            jnp.concatenate([params[n] for n in PROJ_NAMES], axis=0).astype(WIRE),
            params["w_gate_up"].astype(WIRE), params["w_down"].astype(WIRE)]

    dx, r2, sib_early, small = _local_step(x, mem, tgt, params, w_main, w_f, rest, c_idx)

    n_late = len(r2) - len(sib_early)
    r2_sib = list(_swap_sibling(r2[:n_late], False, "late")) + sib_early
    full = [_chip_sum(theirs, 1 - c_idx, _chip_sum(mine, c_idx)) for mine, theirs in zip(r2, r2_sib)]
    red = [f.reshape(2 * f.shape[1], f.shape[2]) for f in full]
    pj_r = red[3].reshape(4, D // N_CHIPS, D)
    grads = {"w_in": red[0], "conv_w": red[1][:CW], "w_kv_mem": red[2], "w_gate_up": red[4], "w_down": red[5]}
    for i, n in enumerate(PROJ_NAMES):
        grads[n] = pj_r[i]

    tot = _allreduce_small(small)
    loss = tot[9, 0]
    for i, n in enumerate(SMALL_NAMES):
        grads[n] = tot[i:i + 1]
    grads["b_forget"] = tot[8:9, :HF]

    delta, new_m, new_v = {}, {}, {}
    ds, ms, vs = _adamw(_pack_small(params), tot.at[9:].set(0.0), _pack_small(moms), _pack_small(vels))
    for i, n in enumerate(SMALL_NAMES):
        delta[n], new_m[n], new_v[n] = ds[i:i + 1], ms[i:i + 1], vs[i:i + 1]
    delta["b_forget"], new_m["b_forget"], new_v["b_forget"] = ds[8:9, :HF], ms[8:9, :HF], vs[8:9, :HF]
    for n in ["w_in", "conv_w", "w_kv_mem", "w_gate_up", "w_down"] + PROJ_NAMES:
        delta[n], new_m[n], new_v[n] = _adamw(params[n], grads[n], moms[n], vels[n])
    return loss, dx, grads, delta, new_m, new_v


def kernel(x, mem, norm_mix_pre, norm_mix_post, norm_mem, w_in, b_forget, conv_w, conv_b, conv_ln_g, conv_ln_b, w_kv_mem, w_proj_attn, w_proj_conv, w_proj_mem, w_out, norm_ffn_pre, norm_ffn_post, w_gate_up, w_down, loss_target, m_norm_mix_pre, m_norm_mix_post, m_norm_mem, m_w_in, m_b_forget, m_conv_w, m_conv_b, m_conv_ln_g, m_conv_ln_b, m_w_kv_mem, m_w_proj_attn, m_w_proj_conv, m_w_proj_mem, m_w_out, m_norm_ffn_pre, m_norm_ffn_post, m_w_gate_up, m_w_down, v_norm_mix_pre, v_norm_mix_post, v_norm_mem, v_w_in, v_b_forget, v_conv_w, v_conv_b, v_conv_ln_g, v_conv_ln_b, v_w_kv_mem, v_w_proj_attn, v_w_proj_conv, v_w_proj_mem, v_w_out, v_norm_ffn_pre, v_norm_ffn_post, v_w_gate_up, v_w_down):
    local = dict(locals())
    lead = {n: local[n].shape[:-2] for n in WEIGHT_ORDER}
    two_d = lambda a: a.reshape(a.shape[-2:])
    params = {n: two_d(local[n]) for n in WEIGHT_ORDER}
    moms = {n: two_d(local["m_" + n]) for n in WEIGHT_ORDER}
    vels = {n: two_d(local["v_" + n]) for n in WEIGHT_ORDER}
    loss, dx, grads, delta, new_m, new_v = _step(params, moms, vels, two_d(x), two_d(mem), two_d(loss_target))
    outs = [loss, dx.reshape(x.shape)]
    for group in (grads, delta, new_m, new_v):
        outs += [group[n].reshape(lead[n] + group[n].shape) for n in WEIGHT_ORDER]
    return tuple(outs)
```
